```python
import jax
import jax.numpy as jnp
from jax import lax
import numpy as np

D_MODEL = 1024
BATCH = 4
SEQ = 4096
DEPTH = 2

GRID_W = 64
CTX_LEN = 256
EPS = 1e-6
ROPE_THETA = 10000.0
N_MOD = 6
GROUP_W = D_MODEL // 4
CONV_W = GROUP_W
CONV_KSIZE = 31
GLA_HEADS = GROUP_W // 64
GLA_DV = 64
GLA_DK = 32
GLA_RANK = 16
GLA_TAU = 16.0
ATT_HEADS = GROUP_W // 64
ATT_KV_HEADS = ATT_HEADS // 2
ATT_DH = 64
RET_HEADS = GROUP_W // 64
RET_DV = 64
RET_DK = 32
CHUNK = 64
Q_BLOCK = 128
D_FF = 4 * D_MODEL
IN_SPLITS = (CONV_W, CONV_W,
             GLA_HEADS * GLA_DK, GLA_HEADS * GLA_DK, GLA_HEADS * GLA_DV, GLA_HEADS * GLA_DV, GLA_RANK,
             ATT_HEADS * ATT_DH, ATT_KV_HEADS * ATT_DH, ATT_KV_HEADS * ATT_DH,
             RET_HEADS * RET_DK, RET_HEADS * RET_DK, RET_HEADS * RET_DV, RET_HEADS * RET_DV)
P_IN = sum(IN_SPLITS)
MIX_W = CONV_W + GLA_HEADS * GLA_DV + ATT_HEADS * ATT_DH + RET_HEADS * RET_DV

kernel_name = 'hybrid_parallel_groups_prefix_ctx_dit'


def rms_norm(x, g):
    xf = x.astype(jnp.float32)
    y = xf * lax.rsqrt(jnp.mean(xf * xf, axis=-1, keepdims=True) + EPS)
    return (y * g.astype(jnp.float32)).astype(x.dtype)


def layer_norm(x, g, b):
    xf = x.astype(jnp.float32)
    mu = jnp.mean(xf, axis=-1, keepdims=True)
    xc = xf - mu
    y = xc * lax.rsqrt(jnp.mean(xc * xc, axis=-1, keepdims=True) + EPS)
    return (y * g.astype(jnp.float32) + b.astype(jnp.float32)).astype(x.dtype)


def split_cols(h):
    out, start = [], 0
    for size in IN_SPLITS:
        out.append(h[..., start:start + size])
        start += size
    return out


def split_heads(t, n_heads):
    B, T, W = t.shape
    return t.reshape(B, T, n_heads, W // n_heads).transpose(0, 2, 1, 3)


def merge_heads(t):
    B, H, T, d = t.shape
    return t.transpose(0, 2, 1, 3).reshape(B, T, H * d)


def axial_rope_tables(row_ids, col_ids, head_dim):
    n_ax = head_dim // 4
    inv = ROPE_THETA ** (-jnp.arange(n_ax, dtype=jnp.float32) / n_ax)
    ang = jnp.concatenate([row_ids.astype(jnp.float32)[:, None] * inv,
                           col_ids.astype(jnp.float32)[:, None] * inv], axis=-1)
    return jnp.cos(ang), jnp.sin(ang)


def apply_rope(x, cos, sin):
    d2 = x.shape[-1] // 2
    x1, x2 = x[..., :d2], x[..., d2:]
    return jnp.concatenate([x1 * cos - x2 * sin, x2 * cos + x1 * sin], axis=-1).astype(x.dtype)


def chunked_gated_scan(q, k, v, log_a, s0):
    out_dtype = v.dtype
    B, H, T, dk = q.shape
    dv = v.shape[-1]
    n = T // CHUNK
    qc = q.astype(jnp.float32).reshape(B, H, n, CHUNK, dk)
    kc = k.astype(jnp.float32).reshape(B, H, n, CHUNK, dk)
    vc = v.astype(jnp.float32).reshape(B, H, n, CHUNK, dv)
    b = jnp.cumsum(log_a.astype(jnp.float32).reshape(B, H, n, CHUNK, dk), axis=3)
    b_last = b[:, :, :, -1:, :]
    q_dec = qc * jnp.exp(b)
    k_inv = kc * jnp.exp(-b)
    k_end = kc * jnp.exp(b_last - b)
    lower_tri = jnp.tril(jnp.ones((CHUNK, CHUNK), dtype=bool))
    scores = jnp.where(lower_tri, jnp.einsum('bhncd,bhnsd->bhncs', q_dec, k_inv), 0.0)
    o_intra = jnp.einsum('bhncs,bhnse->bhnce', scores, vc)
    kv = jnp.einsum('bhncd,bhnce->bhnde', k_end, vc)
    decay = jnp.exp(b_last[:, :, :, 0, :])

    def step(s, inp):
        qd_i, kv_i, dec_i = inp
        o_i = jnp.einsum('bhcd,bhde->bhce', qd_i, s)
        return dec_i[..., None] * s + kv_i, o_i

    xs = (jnp.moveaxis(q_dec, 2, 0), jnp.moveaxis(kv, 2, 0), jnp.moveaxis(decay, 2, 0))
    s_final, o_inter = lax.scan(step, s0, xs)
    o = o_intra + jnp.moveaxis(o_inter, 0, 2)
    return o.reshape(B, H, T, dv).astype(out_dtype), s_final


def bidirectional_scan(lat, ctx, need_ctx):
    q_l, k_l, v_l, af_l, ab_l = lat
    q_c, k_c, v_c, af_c, ab_c = ctx
    B, H, _, dk = q_l.shape
    dv = v_l.shape[-1]
    s0 = jnp.zeros((B, H, dk, dv), jnp.float32)
    flip = lambda t: jnp.flip(t, axis=2)
    o_cf, s_cf = chunked_gated_scan(q_c, k_c, v_c, af_c, s0)
    o_lf, _ = chunked_gated_scan(q_l, k_l, v_l, af_l, s_cf)
    o_cb, s_cb = chunked_gated_scan(flip(q_c), flip(k_c), flip(v_c), flip(ab_c), s0)
    o_lb, _ = chunked_gated_scan(flip(q_l), flip(k_l), flip(v_l), flip(ab_l), s_cb)
    o_l = o_lf + flip(o_lb)
    o_c = (o_cf + flip(o_cb)) if need_ctx else None
    return o_l, o_c


def conformer_conv(a, g, w_dw, b_dw, ln_g, ln_b, w_pw, b_pw):
    u = a * jax.nn.sigmoid(g)
    pad = (CONV_KSIZE - 1) // 2
    y = lax.conv_general_dilated(u, w_dw[:, None, :].astype(u.dtype), (1,), [(pad, pad)],
                                 dimension_numbers=('NWC', 'WIO', 'NWC'),
                                 feature_group_count=u.shape[-1]) + b_dw
    y = layer_norm(y, ln_g, ln_b)
    return jax.nn.silu(y) @ w_pw + b_pw


def conv_mixer(a_l, g_l, a_c, g_c, w_dw, b_dw, ln_g, ln_b, w_pw, b_pw, need_ctx):
    y_l = conformer_conv(a_l, g_l, w_dw, b_dw, ln_g, ln_b, w_pw, b_pw)
    y_c = conformer_conv(a_c, g_c, w_dw, b_dw, ln_g, ln_b, w_pw, b_pw) if need_ctx else None
    return y_l, y_c


def gla_mixer(p_l, p_c, w_a_f, b_a_f, w_a_b, b_a_b, norm_g, need_ctx):
    def prep(q, k, v, z):
        q = split_heads(q, GLA_HEADS) * GLA_DK ** -0.5
        k = split_heads(k, GLA_HEADS)
        v = split_heads(v, GLA_HEADS)
        la_f = split_heads(jax.nn.log_sigmoid(z @ w_a_f + b_a_f) / GLA_TAU, GLA_HEADS)
        la_b = split_heads(jax.nn.log_sigmoid(z @ w_a_b + b_a_b) / GLA_TAU, GLA_HEADS)
        return q, k, v, la_f, la_b

    q_l, k_l, v_l, r_l, z_l = p_l
    q_c, k_c, v_c, r_c, z_c = p_c
    o_l, o_c = bidirectional_scan(prep(q_l, k_l, v_l, z_l), prep(q_c, k_c, v_c, z_c), need_ctx)
    g = norm_g[:, None, :]
    y_l = merge_heads(rms_norm(o_l, g)) * jax.nn.silu(r_l)
    y_c = merge_heads(rms_norm(o_c, g)) * jax.nn.silu(r_c) if need_ctx else None
    return y_l, y_c


def retention_mixer(p_l, p_c, log_gamma, norm_g, cos, sin, need_ctx):
    def prep(q, k, v, rope):
        q = split_heads(q, RET_HEADS)
        k = split_heads(k, RET_HEADS) * RET_DK ** -0.5
        v = split_heads(v, RET_HEADS)
        if rope:
            q = apply_rope(q, cos, sin)
            k = apply_rope(k, cos, sin)
        la = jnp.broadcast_to(log_gamma[None, :, None, None], q.shape)
        return q, k, v, la, la

    q_l, k_l, v_l, g_l = p_l
    q_c, k_c, v_c, g_c = p_c
    o_l, o_c = bidirectional_scan(prep(q_l, k_l, v_l, True), prep(q_c, k_c, v_c, False), need_ctx)
    g = norm_g[:, None, :]
    y_l = merge_heads(rms_norm(o_l, g)) * jax.nn.silu(g_l)
    y_c = merge_heads(rms_norm(o_c, g)) * jax.nn.silu(g_c) if need_ctx else None
    return y_l, y_c


def gqa_attend(q, k, v):
    s = jnp.einsum('bkgqd,bksd->bkgqs', q, k, preferred_element_type=jnp.float32) * ATT_DH ** -0.5
    p = jax.nn.softmax(s, axis=-1).astype(v.dtype)
    return jnp.einsum('bkgqs,bksd->bkgqd', p, v)


def attention_mixer(p_l, p_c, qn_g, kn_g, cos, sin, need_ctx):
    G = ATT_HEADS // ATT_KV_HEADS

    def prep(q, k, v):
        q = rms_norm(split_heads(q, ATT_HEADS), qn_g)
        k = rms_norm(split_heads(k, ATT_KV_HEADS), kn_g)
        return q, k, split_heads(v, ATT_KV_HEADS)

    q_l, k_l, v_l = prep(*p_l)
    q_l = apply_rope(q_l, cos, sin)
    k_l = apply_rope(k_l, cos, sin)
    q_c, k_c, v_c = prep(*p_c)
    keys = jnp.concatenate([k_c, k_l], axis=2)
    vals = jnp.concatenate([v_c, v_l], axis=2)
    B, _, T, _ = q_l.shape
    nblk = T // Q_BLOCK
    qb = q_l.reshape(B, ATT_KV_HEADS, G, nblk, Q_BLOCK, ATT_DH).transpose(3, 0, 1, 2, 4, 5)
    ob = lax.map(lambda qi: gqa_attend(qi, keys, vals), qb)
    o_l = ob.transpose(1, 2, 3, 0, 4, 5).reshape(B, ATT_HEADS, T, ATT_DH)
    y_l = merge_heads(o_l)
    y_c = None
    if need_ctx:
        Tc = q_c.shape[2]
        o_c = gqa_attend(q_c.reshape(B, ATT_KV_HEADS, G, Tc, ATT_DH), k_c, v_c)
        y_c = merge_heads(o_c.reshape(B, ATT_HEADS, Tc, ATT_DH))
    return y_l, y_c


def sq_relu_mlp(h, w_up, w_down):
    return jnp.square(jax.nn.relu(h @ w_up)) @ w_down


def setup_inputs(seed: int = 0) -> dict:
    key = jax.random.key(seed)
    ks = jax.random.split(key, 32)
    nrm = lambda i, shape, scale: jax.random.normal(ks[i], shape, jnp.float32) * scale
    L = DEPTH
    return {
        'x': nrm(0, (BATCH, SEQ, D_MODEL), 1.0),
        'c': nrm(1, (BATCH, D_MODEL), 1.0),
        'ctx': nrm(2, (BATCH, CTX_LEN, D_MODEL), 1.0),
        'c_ctx': nrm(3, (D_MODEL,), 1.0),
        'w_mod': nrm(4, (L, D_MODEL, N_MOD * D_MODEL), 0.5 * D_MODEL ** -0.5),
        'b_mod': nrm(5, (L, N_MOD * D_MODEL), 0.02),
        'norm1_g': 1.0 + nrm(6, (L, D_MODEL), 0.02),
        'norm2_g': 1.0 + nrm(7, (L, D_MODEL), 0.02),
        'w_in': nrm(8, (L, D_MODEL, P_IN), D_MODEL ** -0.5),
        'conv_w_dw': nrm(9, (L, CONV_KSIZE, CONV_W), CONV_KSIZE ** -0.5),
        'conv_b_dw': nrm(10, (L, CONV_W), 0.02),
        'conv_ln_g': 1.0 + nrm(11, (L, CONV_W), 0.02),
        'conv_ln_b': nrm(12, (L, CONV_W), 0.02),
        'conv_w_pw': nrm(13, (L, CONV_W, CONV_W), CONV_W ** -0.5),
        'conv_b_pw': nrm(14, (L, CONV_W), 0.02),
        'gla_w_a_f': nrm(15, (L, GLA_RANK, GLA_HEADS * GLA_DK), GLA_RANK ** -0.5),
        'gla_b_a_f': nrm(16, (L, GLA_HEADS * GLA_DK), 0.02),
        'gla_w_a_b': nrm(17, (L, GLA_RANK, GLA_HEADS * GLA_DK), GLA_RANK ** -0.5),
        'gla_b_a_b': nrm(18, (L, GLA_HEADS * GLA_DK), 0.02),
        'gla_norm_g': 1.0 + nrm(19, (L, GLA_HEADS, GLA_DV), 0.02),
        'att_q_norm_g': 1.0 + nrm(20, (L, ATT_DH), 0.02),
        'att_k_norm_g': 1.0 + nrm(21, (L, ATT_DH), 0.02),
        'ret_norm_g': 1.0 + nrm(22, (L, RET_HEADS, RET_DV), 0.02),
        'w_out': nrm(23, (L, MIX_W, D_MODEL), MIX_W ** -0.5),
        'w_up': nrm(24, (L, D_MODEL, D_FF), D_MODEL ** -0.5),
        'w_down': nrm(25, (L, D_FF, D_MODEL), D_FF ** -0.5),
        'final_norm_g': 1.0 + nrm(26, (D_MODEL,), 0.02),
    }


def reference(x, c, ctx, c_ctx, w_mod, b_mod, norm1_g, norm2_g, w_in,
              conv_w_dw, conv_b_dw, conv_ln_g, conv_ln_b, conv_w_pw, conv_b_pw,
              gla_w_a_f, gla_b_a_f, gla_w_a_b, gla_b_a_b, gla_norm_g,
              att_q_norm_g, att_k_norm_g, ret_norm_g,
              w_out, w_up, w_down, final_norm_g):
    B, T, _ = x.shape
    rows = T // GRID_W
    row_ids = jnp.repeat(jnp.arange(rows), GRID_W)
    col_ids = jnp.tile(jnp.arange(GRID_W), rows)
    cos_att, sin_att = axial_rope_tables(row_ids, col_ids, ATT_DH)
    cos_ret, sin_ret = axial_rope_tables(row_ids, col_ids, RET_DK)
    log_gamma = jnp.log1p(-jnp.exp2(-5.0 - jnp.arange(RET_HEADS, dtype=jnp.float32)))
    silu_c = jax.nn.silu(c)
    silu_cc = jax.nn.silu(c_ctx)
    xc = ctx
    for l in range(DEPTH):
        need_ctx = l < DEPTH - 1
        mod_l = (silu_c @ w_mod[l] + b_mod[l])[:, None, :]
        mod_c = (silu_cc @ w_mod[l] + b_mod[l])[None, None, :]
        sh1_l, sc1_l, g1_l, sh2_l, sc2_l, g2_l = jnp.split(mod_l, N_MOD, axis=-1)
        sh1_c, sc1_c, g1_c, sh2_c, sc2_c, g2_c = jnp.split(mod_c, N_MOD, axis=-1)

        h_l = rms_norm(x, norm1_g[l]) * (1.0 + sc1_l) + sh1_l
        h_c = rms_norm(xc, norm1_g[l]) * (1.0 + sc1_c) + sh1_c
        pl = split_cols(h_l @ w_in[l])
        pc = split_cols(h_c @ w_in[l])
        y_conv_l, y_conv_c = conv_mixer(pl[0], pl[1], pc[0], pc[1], conv_w_dw[l], conv_b_dw[l],
                                        conv_ln_g[l], conv_ln_b[l], conv_w_pw[l], conv_b_pw[l],
                                        need_ctx)
        y_gla_l, y_gla_c = gla_mixer(pl[2:7], pc[2:7], gla_w_a_f[l], gla_b_a_f[l],
                                     gla_w_a_b[l], gla_b_a_b[l], gla_norm_g[l], need_ctx)
        y_att_l, y_att_c = attention_mixer(pl[7:10], pc[7:10], att_q_norm_g[l], att_k_norm_g[l],
                                           cos_att, sin_att, need_ctx)
        y_ret_l, y_ret_c = retention_mixer(pl[10:14], pc[10:14], log_gamma, ret_norm_g[l],
                                           cos_ret, sin_ret, need_ctx)
        o_l = jnp.concatenate([y_conv_l, y_gla_l, y_att_l, y_ret_l], axis=-1) @ w_out[l]
        x = x + g1_l * o_l
        h2_l = rms_norm(x, norm2_g[l]) * (1.0 + sc2_l) + sh2_l
        x = x + g2_l * sq_relu_mlp(h2_l, w_up[l], w_down[l])

        if need_ctx:
            o_c = jnp.concatenate([y_conv_c, y_gla_c, y_att_c, y_ret_c], axis=-1) @ w_out[l]
            xc = xc + g1_c * o_c
            h2_c = rms_norm(xc, norm2_g[l]) * (1.0 + sc2_c) + sh2_c
            xc = xc + g2_c * sq_relu_mlp(h2_c, w_up[l], w_down[l])
    return rms_norm(x, final_norm_g)
```

```python
import functools

import jax
import jax.numpy as jnp
from jax import lax
from jax.experimental import pallas as pl
from jax.experimental.pallas import tpu as pltpu

F32 = jnp.float32
BF16 = jnp.bfloat16

EPS = 1e-6
GRID_W = 64
ROPE_THETA = 10000.0
N_MOD = 6
CONV_KSIZE = 31
CONV_PAD = (CONV_KSIZE - 1) // 2
HALO = 16
GLA_TAU = 16.0
GLA_RANK = 16
CHUNK = 64
N_HEADS = 4
KV_HEADS = 2
HEAD_DV = 64
SCAN_DK = 32
GROUP_W = 256
ROW_TILE = 256
LANES = 128
V7X_VMEM_BYTES = 64 * 1024 * 1024
VMEM_LIMIT = V7X_VMEM_BYTES * 3 // 4

W_CONV = 2 * GROUP_W
W_GLA = 2 * N_HEADS * SCAN_DK + 2 * GROUP_W + LANES
W_ATT = GROUP_W + 2 * KV_HEADS * HEAD_DV
W_RET = 2 * N_HEADS * SCAN_DK + 2 * GROUP_W
W_IN_PAD = W_CONV + W_GLA + W_ATT + W_RET


def _params(semantics):
    return pltpu.CompilerParams(dimension_semantics=semantics, vmem_limit_bytes=VMEM_LIMIT)


def _silu(x):
    return x * jax.nn.sigmoid(x)


def _rms(x):
    return x * lax.rsqrt(jnp.mean(x * x, axis=-1, keepdims=True) + EPS)


def _group_sumsq(t, ones_blockdiag):
    t2 = t * t
    hi = t2.astype(BF16)
    lo = (t2 - hi.astype(F32)).astype(BF16)
    return (jnp.dot(hi, ones_blockdiag, preferred_element_type=F32)
            + jnp.dot(lo, ones_blockdiag, preferred_element_type=F32))


def _mod_body(cc_ref, w_ref, b_ref, o_ref):
    s = _silu(cc_ref[...])
    o_ref[0] = jnp.dot(s.astype(BF16), w_ref[0].astype(BF16),
                       preferred_element_type=F32) + b_ref[0]


def _modulation(cc, w_mod, b_mod):
    depth, d, n = w_mod.shape
    tn = 1024
    return pl.pallas_call(
        _mod_body,
        grid=(depth, n // tn),
        in_specs=[pl.BlockSpec((8, d), lambda l, i: (0, 0)),
                  pl.BlockSpec((1, d, tn), lambda l, i: (l, 0, i)),
                  pl.BlockSpec((1, 1, tn), lambda l, i: (l, 0, i))],
        out_specs=pl.BlockSpec((1, 8, tn), lambda l, i: (l, 0, i)),
        out_shape=jax.ShapeDtypeStruct((depth, 8, n), F32),
        compiler_params=_params(("parallel", "parallel")),
        name="modulation",
    )(cc, w_mod, b_mod.reshape(depth, 1, n))


def _inproj_body(x_ref, mod_ref, g_ref, w_ref, conv_ref, gla_ref, att_ref, ret_ref):
    d = x_ref.shape[2]
    shift = mod_ref[0, :, 0:d]
    scale = mod_ref[0, :, d:2 * d]
    h = (_rms(x_ref[0]) * g_ref[...]) * (1.0 + scale) + shift
    hb = h.astype(BF16)
    off = 0
    for ref in (conv_ref, gla_ref, att_ref, ret_ref):
        w = ref.shape[2]
        ref[0] = jnp.dot(hb, w_ref[:, off:off + w], preferred_element_type=F32)
        off += w


def _in_proj(x_all, mod, norm_g, w_in_pad, n_ctx_tiles):
    bsz, t_all, d = x_all.shape
    tm = ROW_TILE
    ctx_row = bsz
    widths = (W_CONV, W_GLA, W_ATT, W_RET)
    return pl.pallas_call(
        _inproj_body,
        grid=(bsz, t_all // tm),
        in_specs=[pl.BlockSpec((1, tm, d), lambda b, j: (b, j, 0)),
                  pl.BlockSpec((1, 1, N_MOD * d),
                               lambda b, j: (jnp.where(j < n_ctx_tiles, ctx_row, b), 0, 0)),
                  pl.BlockSpec((1, d), lambda b, j: (0, 0)),
                  pl.BlockSpec((d, W_IN_PAD), lambda b, j: (0, 0))],
        out_specs=[pl.BlockSpec((1, tm, w), lambda b, j: (b, j, 0)) for w in widths],
        out_shape=[jax.ShapeDtypeStruct((bsz, t_all, w), F32) for w in widths],
        compiler_params=_params(("parallel", "parallel")),
        name="in_proj",
    )(x_all, mod, norm_g, w_in_pad)


def _conv_body(n_ctx_tiles, n_tiles, tile0, main_ref, prev_ref, next_ref, wdw_ref, bdw_ref,
               lng_ref, lnb_ref, wpw_ref, bpw_ref, o_ref, ubuf, sbuf):
    tt = main_ref.shape[1]
    c = o_ref.shape[2]
    j = pl.program_id(1) + tile0

    def glu(blk):
        return blk[:, :c] * jax.nn.sigmoid(blk[:, c:])

    has_prev = jnp.logical_and(j != 0, j != n_ctx_tiles)
    has_next = jnp.logical_and(j != n_ctx_tiles - 1, j != n_tiles - 1)

    @pl.when(has_prev)
    def _():
        ubuf[0:HALO, :] = glu(prev_ref[0])

    @pl.when(jnp.logical_not(has_prev))
    def _():
        ubuf[0:HALO, :] = jnp.zeros((HALO, c), F32)

    ubuf[HALO:HALO + tt, :] = glu(main_ref[0])

    @pl.when(has_next)
    def _():
        ubuf[HALO + tt:, :] = glu(next_ref[0])

    @pl.when(jnp.logical_not(has_next))
    def _():
        ubuf[HALO + tt:, :] = jnp.zeros((HALO, c), F32)

    rows = 64
    for r in range(0, tt, rows):
        acc = jnp.zeros((rows, c), F32)
        for k in range(CONV_KSIZE):
            lo = r + k + HALO - CONV_PAD
            acc = acc + ubuf[lo:lo + rows, :] * wdw_ref[k:k + 1, :]
        y = acc + bdw_ref[...]
        yc = y - jnp.mean(y, axis=-1, keepdims=True)
        yn = yc * lax.rsqrt(jnp.mean(yc * yc, axis=-1, keepdims=True) + EPS)
        yn = yn * lng_ref[...] + lnb_ref[...]
        sbuf[r:r + rows, :] = _silu(yn).astype(BF16)
    o_ref[0] = jnp.dot(sbuf[...], wpw_ref[...], preferred_element_type=F32) + bpw_ref[...]


def _conv_mixer(conv_in, w_dw, b_dw, ln_g, ln_b, w_pw, b_pw, n_ctx_tiles, with_ctx):
    bsz, t_all, _ = conv_in.shape
    tt = ROW_TILE
    c = GROUP_W
    n_tiles = t_all // tt
    tile0 = 0 if with_ctx else n_ctx_tiles
    per = tt // HALO
    n_halo = t_all // HALO
    row = lambda b, j: (0, 0)
    return pl.pallas_call(
        functools.partial(_conv_body, n_ctx_tiles, n_tiles, tile0),
        grid=(bsz, n_tiles - tile0),
        in_specs=[pl.BlockSpec((1, tt, 2 * c), lambda b, j: (b, j + tile0, 0)),
                  pl.BlockSpec((1, HALO, 2 * c),
                               lambda b, j: (b, jnp.maximum((j + tile0) * per - 1, 0), 0)),
                  pl.BlockSpec((1, HALO, 2 * c),
                               lambda b, j: (b, jnp.minimum((j + tile0 + 1) * per, n_halo - 1), 0)),
                  pl.BlockSpec((CONV_KSIZE, c), row),
                  pl.BlockSpec((1, c), row), pl.BlockSpec((1, c), row), pl.BlockSpec((1, c), row),
                  pl.BlockSpec((c, c), row), pl.BlockSpec((1, c), row)],
        out_specs=pl.BlockSpec((1, tt, c), lambda b, j: (b, j, 0)),
        out_shape=jax.ShapeDtypeStruct((bsz, t_all - tile0 * tt, c), F32),
        scratch_shapes=[pltpu.VMEM((tt + 2 * HALO, c), F32), pltpu.VMEM((tt, c), BF16)],
        compiler_params=_params(("parallel", "parallel")),
        name="conv_mixer",
    )(conv_in, conv_in, conv_in, w_dw, b_dw, ln_g, ln_b, w_pw, b_pw)


def _chunk_cumsum(la, reverse):
    n = la.shape[0]
    pos = lax.broadcasted_iota(jnp.int32, la.shape, 0) % CHUNK
    x = la
    s = 1
    while s < CHUNK:
        if reverse:
            x = x + jnp.where(pos < CHUNK - s, pltpu.roll(x, n - s, 0), 0.0)
        else:
            x = x + jnp.where(pos >= s, pltpu.roll(x, s, 0), 0.0)
        s *= 2
    return x


def _scan_chunk(q, k, v, b, st_ref, reverse, head_mask, value_mask, tri):
    nt_dims = (((1,), (1,)), ((), ()))
    b_tot = b[0:1, :] if reverse else b[CHUNK - 1:CHUNK, :]
    q_dec = (q * jnp.exp(b)).astype(BF16)
    k_inv = (k * jnp.exp(-b)).astype(BF16)
    k_end = (k * jnp.exp(b_tot - b)).astype(BF16)
    vb = v.astype(BF16)
    zero = jnp.zeros((), BF16)
    k_blk = jnp.where(head_mask, jnp.concatenate([k_inv] * N_HEADS, axis=0), zero)
    scores = lax.dot_general(q_dec, k_blk, nt_dims, preferred_element_type=F32)
    scores = jnp.where(tri, scores, 0.0).astype(BF16)
    v_blk = jnp.where(value_mask, jnp.concatenate([vb] * N_HEADS, axis=0), zero)
    o = jnp.dot(scores, v_blk, preferred_element_type=F32)
    st = st_ref[...]
    o = o + lax.dot_general(q_dec, st.astype(BF16), nt_dims, preferred_element_type=F32)
    kv_t = lax.dot_general(vb, k_end, (((0,), (0,)), ((), ())), preferred_element_type=F32)
    st_ref[...] = st * jnp.exp(b_tot) + jnp.where(head_mask, kv_t, 0.0)
    return o


def _scan_body(mode, *refs):
    if mode == "gla":
        (qf, kf, vf, zf, qb, kb, vb, zb, waf, baf, wab, bab, of_ref, ob_ref, stf, stb) = refs
    else:
        (qf, kf, vf, qb, kb, vb, lg, cosf, sinf, cosb, sinb, of_ref, ob_ref, stf, stb) = refs

    @pl.when(pl.program_id(1) == 0)
    def _():
        stf[...] = jnp.zeros(stf.shape, F32)
        stb[...] = jnp.zeros(stb.shape, F32)

    tb = qf.shape[1]
    dk_all = qf.shape[2]
    dv_all = vf.shape[2]
    scale = SCAN_DK ** -0.5

    r = lax.broadcasted_iota(jnp.int32, (dv_all, dk_all), 0)
    c = lax.broadcasted_iota(jnp.int32, (dv_all, dk_all), 1)
    head_mask = (r // HEAD_DV) == (c // SCAN_DK)
    r = lax.broadcasted_iota(jnp.int32, (dv_all, dv_all), 0)
    c = lax.broadcasted_iota(jnp.int32, (dv_all, dv_all), 1)
    value_mask = (r // HEAD_DV) == (c // HEAD_DV)
    r = lax.broadcasted_iota(jnp.int32, (CHUNK, dv_all), 0)
    c = lax.broadcasted_iota(jnp.int32, (CHUNK, dv_all), 1) % CHUNK
    tri_f = c <= r
    tri_b = c >= r

    if mode == "gla":
        def log_decay(z_ref, w_ref, bias_ref):
            pre = jnp.dot(z_ref[0].astype(BF16), w_ref[...], preferred_element_type=F32) + bias_ref[...]
            log_sig = jnp.minimum(pre, 0.0) - jnp.log1p(jnp.exp(-jnp.abs(pre)))
            return log_sig / GLA_TAU

        q_f, k_f, la_f = qf[0] * scale, kf[0], log_decay(zf, waf, baf)
        q_b, k_b, la_b = qb[0] * scale, kb[0], log_decay(zb, wab, bab)
    else:
        lane = lax.broadcasted_iota(jnp.int32, (tb, dk_all), 1)
        first_half = (lane % SCAN_DK) < SCAN_DK // 2

        def rope(t, cos_ref, sin_ref):
            rot = jnp.where(first_half, pltpu.roll(t, dk_all - SCAN_DK // 2, 1),
                            pltpu.roll(t, SCAN_DK // 2, 1))
            return t * cos_ref[...] + rot * sin_ref[...]

        q_f, k_f = rope(qf[0], cosf, sinf), rope(kf[0] * scale, cosf, sinf)
        q_b, k_b = rope(qb[0], cosb, sinb), rope(kb[0] * scale, cosb, sinb)
        la_f = la_b = jnp.broadcast_to(lg[...], (tb, dk_all))

    b_f = _chunk_cumsum(la_f, False)
    b_b = _chunk_cumsum(la_b, True)
    v_f = vf[0]
    v_b = vb[0]
    n_chunks = tb // CHUNK
    for i in range(n_chunks):
        lo = i * CHUNK
        of_ref[0, lo:lo + CHUNK, :] = _scan_chunk(
            q_f[lo:lo + CHUNK], k_f[lo:lo + CHUNK], v_f[lo:lo + CHUNK], b_f[lo:lo + CHUNK],
            stf, False, head_mask, value_mask, tri_f)
        lo = (n_chunks - 1 - i) * CHUNK
        ob_ref[0, lo:lo + CHUNK, :] = _scan_chunk(
            q_b[lo:lo + CHUNK], k_b[lo:lo + CHUNK], v_b[lo:lo + CHUNK], b_b[lo:lo + CHUNK],
            stb, True, head_mask, value_mask, tri_b)


def _scan_mixer(mode, src, n_ctx_tiles, extras):
    bsz, t_all, _ = src.shape
    tb = ROW_TILE
    n_tiles = t_all // tb
    dk_all = N_HEADS * SCAN_DK
    dv_all = GROUP_W

    def bwd(j):
        return jnp.where(j < n_ctx_tiles, n_ctx_tiles - 1 - j, n_tiles - 1 - (j - n_ctx_tiles))

    def col_specs(tile_of, with_z):
        specs = [pl.BlockSpec((1, tb, dk_all), lambda b, j: (b, tile_of(j), 0)),
                 pl.BlockSpec((1, tb, dk_all), lambda b, j: (b, tile_of(j), 1)),
                 pl.BlockSpec((1, tb, dv_all), lambda b, j: (b, tile_of(j), 1))]
        if with_z:
            z_blk = (2 * dk_all + 2 * dv_all) // LANES
            specs.append(pl.BlockSpec((1, tb, LANES), lambda b, j: (b, tile_of(j), z_blk)))
        return specs

    fwd = lambda j: j
    row = lambda b, j: (0, 0)
    if mode == "gla":
        w_a_f, b_a_f, w_a_b, b_a_b = extras
        in_specs = (col_specs(fwd, True) + col_specs(bwd, True)
                    + [pl.BlockSpec((LANES, dk_all), row), pl.BlockSpec((1, dk_all), row),
                       pl.BlockSpec((LANES, dk_all), row), pl.BlockSpec((1, dk_all), row)])
        args = (src,) * 8 + (w_a_f, b_a_f, w_a_b, b_a_b)
    else:
        log_gamma, cos, sin = extras
        in_specs = (col_specs(fwd, False) + col_specs(bwd, False)
                    + [pl.BlockSpec((1, dk_all), row),
                       pl.BlockSpec((tb, dk_all), lambda b, j: (j, 0)),
                       pl.BlockSpec((tb, dk_all), lambda b, j: (j, 0)),
                       pl.BlockSpec((tb, dk_all), lambda b, j: (bwd(j), 0)),
                       pl.BlockSpec((tb, dk_all), lambda b, j: (bwd(j), 0))])
        args = (src,) * 6 + (log_gamma, cos, sin, cos, sin)
    return pl.pallas_call(
        functools.partial(_scan_body, mode),
        grid=(bsz, n_tiles),
        in_specs=in_specs,
        out_specs=[pl.BlockSpec((1, tb, dv_all), lambda b, j: (b, j, 0)),
                   pl.BlockSpec((1, tb, dv_all), lambda b, j: (b, bwd(j), 0))],
        out_shape=[jax.ShapeDtypeStruct((bsz, t_all, dv_all), F32)] * 2,
        scratch_shapes=[pltpu.VMEM((dv_all, dk_all), F32), pltpu.VMEM((dv_all, dk_all), F32)],
        compiler_params=_params(("parallel", "arbitrary")),
        name="scan_" + mode,
    )(*args)


def _attprep_body(x_ref, cos_ref, sin_ref, qg_ref, kg_ref, ones_ref, q_out, k_out, vt_out):
    x = x_ref[0]
    tm = x.shape[0]
    wq = N_HEADS * HEAD_DV
    wk = KV_HEADS * HEAD_DV
    q = x[:, :wq]
    k = x[:, wq:wq + wk]
    v = x[:, wq + wk:]

    def norm_rope(t, gain, w):
        ss = _group_sumsq(t, ones_ref[:w, :w])
        t = t * lax.rsqrt(ss * (1.0 / HEAD_DV) + EPS) * gain
        lane = lax.broadcasted_iota(jnp.int32, t.shape, 1)
        first_half = (lane % HEAD_DV) < HEAD_DV // 2
        rot = jnp.where(first_half, pltpu.roll(t, w - HEAD_DV // 2, 1), pltpu.roll(t, HEAD_DV // 2, 1))
        return t * cos_ref[:, :w] + rot * sin_ref[:, :w]

    qn = (norm_rope(q, qg_ref[...], wq) * HEAD_DV ** -0.5).astype(BF16)
    kn = norm_rope(k, kg_ref[...], wk).astype(BF16)
    for h in range(N_HEADS):
        q_out[0, h] = qn[:, h * HEAD_DV:(h + 1) * HEAD_DV]
    for h in range(KV_HEADS):
        k_out[0, h] = kn[:, h * HEAD_DV:(h + 1) * HEAD_DV]
    vt = v.T.astype(BF16)
    for h in range(KV_HEADS):
        vt_out[0, h, 0] = vt[h * HEAD_DV:(h + 1) * HEAD_DV, :]


def _att_prep(att_in, cos, sin, q_gain, k_gain, ones_blockdiag):
    bsz, t_all, w = att_in.shape
    tm = ROW_TILE
    n_tiles = t_all // tm
    wq = N_HEADS * HEAD_DV
    row = lambda b, j: (0, 0)
    return pl.pallas_call(
        _attprep_body,
        grid=(bsz, n_tiles),
        in_specs=[pl.BlockSpec((1, tm, w), lambda b, j: (b, j, 0)),
                  pl.BlockSpec((tm, wq), lambda b, j: (j, 0)),
                  pl.BlockSpec((tm, wq), lambda b, j: (j, 0)),
                  pl.BlockSpec((1, wq), row), pl.BlockSpec((1, KV_HEADS * HEAD_DV), row),
                  pl.BlockSpec((wq, wq), row)],
        out_specs=[pl.BlockSpec((1, N_HEADS, tm, HEAD_DV), lambda b, j: (b, 0, j, 0)),
                   pl.BlockSpec((1, KV_HEADS, tm, HEAD_DV), lambda b, j: (b, 0, j, 0)),
                   pl.BlockSpec((1, KV_HEADS, 1, HEAD_DV, tm), lambda b, j: (b, 0, j, 0, 0))],
        out_shape=[jax.ShapeDtypeStruct((bsz, N_HEADS, t_all, HEAD_DV), BF16),
                   jax.ShapeDtypeStruct((bsz, KV_HEADS, t_all, HEAD_DV), BF16),
                   jax.ShapeDtypeStruct((bsz, KV_HEADS, n_tiles, HEAD_DV, tm), BF16)],
        compiler_params=_params(("parallel", "parallel")),
        name="att_prep",
    )(att_in, cos, sin, q_gain, k_gain, ones_blockdiag)


def _att_body(n_ctx_tiles, n_tiles, tile0, q_ref, k_ref, vt_ref, o_ref):
    group = q_ref.shape[1]
    tq = q_ref.shape[2]
    tk = vt_ref.shape[4]
    nq = group * tq
    q2 = q_ref[0].reshape(nq, HEAD_DV)
    qi = pl.program_id(2) + tile0
    n_kv = jnp.where(qi < n_ctx_tiles, n_ctx_tiles, n_tiles)

    def step(i, carry):
        m, l, acc = carry
        kb = k_ref[0, 0, pl.ds(pl.multiple_of(i * tk, tk), tk), :]
        s = lax.dot_general(kb, q2, (((1,), (1,)), ((), ())), preferred_element_type=F32)
        m_new = jnp.maximum(m, jnp.max(s, axis=0, keepdims=True))
        alpha = jnp.exp(m - m_new)
        p = jnp.exp(s - m_new)
        l = alpha * l + jnp.sum(p, axis=0, keepdims=True)
        acc = alpha * acc + jnp.dot(vt_ref[0, 0, i], p.astype(BF16), preferred_element_type=F32)
        return m_new, l, acc

    init = (jnp.full((1, nq), -1e30, F32), jnp.zeros((1, nq), F32), jnp.zeros((HEAD_DV, nq), F32))
    _, l, acc = lax.fori_loop(0, n_kv, step, init)
    o_t = acc / l
    stacked = jnp.concatenate([o_t[:, g * tq:(g + 1) * tq] for g in range(group)], axis=0)
    o_ref[0] = stacked.T


def _attention(q, k, vt, n_ctx_tiles, with_ctx):
    bsz, _, t_all, _ = q.shape
    tq = ROW_TILE
    n_tiles = t_all // tq
    tile0 = 0 if with_ctx else n_ctx_tiles
    group = N_HEADS // KV_HEADS
    return pl.pallas_call(
        functools.partial(_att_body, n_ctx_tiles, n_tiles, tile0),
        grid=(bsz, KV_HEADS, n_tiles - tile0),
        in_specs=[pl.BlockSpec((1, group, tq, HEAD_DV), lambda b, h, i: (b, h, i + tile0, 0)),
                  pl.BlockSpec((1, 1, t_all, HEAD_DV), lambda b, h, i: (b, h, 0, 0)),
                  pl.BlockSpec((1, 1, n_tiles, HEAD_DV, tq), lambda b, h, i: (b, h, 0, 0, 0))],
        out_specs=pl.BlockSpec((1, tq, group * HEAD_DV), lambda b, h, i: (b, i, h)),
        out_shape=jax.ShapeDtypeStruct((bsz, t_all - tile0 * tq, GROUP_W), F32),
        compiler_params=_params(("parallel", "parallel", "arbitrary")),
        name="attention",
    )(q, k, vt)


def _post_body(final, yconv, gla_f, gla_b, gla_gate, yatt, ret_f, ret_b, ret_gate, gla_gain, ret_gain,
               ones_ref, x_ref, mod_ref, n2g_ref, wout_ref, wup_ref, wdown_ref, fng_ref, o_ref):
    d = x_ref.shape[2]
    tm = x_ref.shape[1]
    d_ff = wup_ref.shape[1]

    def finish(of_ref, ob_ref, gate_ref, gain_ref):
        o = of_ref[0] + ob_ref[0]
        ss = _group_sumsq(o, ones_ref[...])
        y = o * lax.rsqrt(ss * (1.0 / HEAD_DV) + EPS) * gain_ref[...]
        return (y * _silu(gate_ref[0])).astype(BF16)

    mixed = jnp.concatenate([yconv[0].astype(BF16), finish(gla_f, gla_b, gla_gate, gla_gain),
                             yatt[0].astype(BF16), finish(ret_f, ret_b, ret_gate, ret_gain)], axis=1)
    o = jnp.dot(mixed, wout_ref[...], preferred_element_type=F32)
    gate1 = mod_ref[0, :, 2 * d:3 * d]
    shift2 = mod_ref[0, :, 3 * d:4 * d]
    scale2 = mod_ref[0, :, 4 * d:5 * d]
    gate2 = mod_ref[0, :, 5 * d:6 * d]
    x_mid = x_ref[0] + gate1 * o
    h2 = ((_rms(x_mid) * n2g_ref[...]) * (1.0 + scale2) + shift2).astype(BF16)
    acc = jnp.zeros((tm, d), F32)
    tf = 1024
    for f in range(0, d_ff, tf):
        u = jnp.dot(h2, wup_ref[:, f:f + tf], preferred_element_type=F32)
        a = jnp.square(jnp.maximum(u, 0.0)).astype(BF16)
        acc = acc + jnp.dot(a, wdown_ref[f:f + tf, :], preferred_element_type=F32)
    x_out = x_mid + gate2 * acc
    if final:
        x_out = _rms(x_out) * fng_ref[...]
    o_ref[0] = x_out


def _post(final, yconv, gla_of, gla_ob, gla_in, yatt, ret_of, ret_ob, ret_in, gla_gain, ret_gain,
          ones_blockdiag, x_all, mod, norm2_g, w_out, w_up, w_down, final_g, n_ctx_tiles, with_ctx):
    bsz, t_all, d = x_all.shape
    tm = ROW_TILE
    ctx_row = bsz
    tile0 = 0 if with_ctx else n_ctx_tiles
    n_out = t_all // tm - tile0
    c = GROUP_W
    gate_blk = (2 * N_HEADS * SCAN_DK + c) // c
    full = lambda b, j: (b, j + tile0, 0)
    own = lambda b, j: (b, j, 0)
    row = lambda b, j: (0, 0)
    resident = dict(pipeline_mode=pl.Buffered(1))
    return pl.pallas_call(
        functools.partial(_post_body, final),
        grid=(bsz, n_out),
        in_specs=[pl.BlockSpec((1, tm, c), own),
                  pl.BlockSpec((1, tm, c), full), pl.BlockSpec((1, tm, c), full),
                  pl.BlockSpec((1, tm, c), lambda b, j: (b, j + tile0, gate_blk)),
                  pl.BlockSpec((1, tm, c), own),
                  pl.BlockSpec((1, tm, c), full), pl.BlockSpec((1, tm, c), full),
                  pl.BlockSpec((1, tm, c), lambda b, j: (b, j + tile0, gate_blk)),
                  pl.BlockSpec((1, c), row), pl.BlockSpec((1, c), row),
                  pl.BlockSpec((c, c), row),
                  pl.BlockSpec((1, tm, d), full),
                  pl.BlockSpec((1, 1, N_MOD * d),
                               lambda b, j: (jnp.where(j + tile0 < n_ctx_tiles, ctx_row, b), 0, 0)),
                  pl.BlockSpec((1, d), row),
                  pl.BlockSpec(w_out.shape, row, **resident),
                  pl.BlockSpec(w_up.shape, row, **resident),
                  pl.BlockSpec(w_down.shape, row, **resident),
                  pl.BlockSpec((1, d), row)],
        out_specs=pl.BlockSpec((1, tm, d), own),
        out_shape=jax.ShapeDtypeStruct((bsz, n_out * tm, d), F32),
        compiler_params=_params(("parallel", "parallel")),
        name="post",
    )(yconv, gla_of, gla_ob, gla_in, yatt, ret_of, ret_ob, ret_in, gla_gain, ret_gain,
      ones_blockdiag, x_all, mod, norm2_g, w_out, w_up, w_down, final_g)


def _rope_table(t_lat, n_ctx, head_dim, width):
    n_ax = head_dim // 4
    inv = ROPE_THETA ** (-jnp.arange(n_ax, dtype=F32) / n_ax)
    pos = jnp.arange(t_lat)
    ang = jnp.concatenate([(pos // GRID_W).astype(F32)[:, None] * inv,
                           (pos % GRID_W).astype(F32)[:, None] * inv], axis=-1)
    cos, sin = jnp.cos(ang), jnp.sin(ang)
    cos = jnp.concatenate([jnp.ones((n_ctx, head_dim // 2), F32), cos], axis=0)
    sin = jnp.concatenate([jnp.zeros((n_ctx, head_dim // 2), F32), sin], axis=0)
    reps = width // head_dim
    return (jnp.tile(jnp.concatenate([cos, cos], axis=-1), (1, reps)),
            jnp.tile(jnp.concatenate([-sin, sin], axis=-1), (1, reps)))


def kernel(x, c, ctx, c_ctx, w_mod, b_mod, norm1_g, norm2_g, w_in, conv_w_dw, conv_b_dw, conv_ln_g,
           conv_ln_b, conv_w_pw, conv_b_pw, gla_w_a_f, gla_b_a_f, gla_w_a_b, gla_b_a_b, gla_norm_g,
           att_q_norm_g, att_k_norm_g, ret_norm_g, w_out, w_up, w_down, final_norm_g):
    bsz, t_lat, d = x.shape
    n_ctx = ctx.shape[1]
    depth = w_mod.shape[0]
    assert bsz + 1 <= 8 and n_ctx % ROW_TILE == 0 and t_lat % ROW_TILE == 0
    n_ctx_tiles = n_ctx // ROW_TILE
    dk_all = N_HEADS * SCAN_DK

    cc = jnp.concatenate([c, c_ctx[None, :], jnp.zeros((8 - bsz - 1, d), F32)], axis=0)
    mods = _modulation(cc, w_mod, b_mod)

    z_off = W_CONV + W_GLA - LANES
    w_in_pad = jnp.concatenate(
        [w_in[:, :, :z_off + GLA_RANK], jnp.zeros((depth, d, LANES - GLA_RANK), F32),
         w_in[:, :, z_off + GLA_RANK:]], axis=-1).astype(BF16)
    pad_rank = lambda w: jnp.concatenate(
        [w, jnp.zeros((depth, LANES - GLA_RANK, dk_all), F32)], axis=1).astype(BF16)
    w_a_f, w_a_b = pad_rank(gla_w_a_f), pad_rank(gla_w_a_b)
    w_pw, w_out_b, w_up_b, w_down_b = (w.astype(BF16) for w in (conv_w_pw, w_out, w_up, w_down))

    cos_att, sin_att = _rope_table(t_lat, n_ctx, HEAD_DV, N_HEADS * HEAD_DV)
    cos_ret, sin_ret = _rope_table(t_lat, n_ctx, SCAN_DK, dk_all)
    log_gamma = jnp.log1p(-jnp.exp2(-5.0 - jnp.arange(N_HEADS, dtype=F32)))
    log_gamma = jnp.repeat(log_gamma, SCAN_DK)[None, :]
    gid = jnp.arange(GROUP_W) // HEAD_DV
    ones_blockdiag = (gid[:, None] == gid[None, :]).astype(BF16)
    row = lambda v: v.reshape(1, -1)

    x_all = jnp.concatenate([ctx, x], axis=1)
    for l in range(depth):
        with_ctx = l < depth - 1
        final = l == depth - 1
        mod = mods[l].reshape(8, 1, N_MOD * d)
        conv_in, gla_in, att_in, ret_in = _in_proj(x_all, mod, row(norm1_g[l]), w_in_pad[l], n_ctx_tiles)
        y_conv = _conv_mixer(conv_in, conv_w_dw[l], row(conv_b_dw[l]), row(conv_ln_g[l]),
                             row(conv_ln_b[l]), w_pw[l], row(conv_b_pw[l]), n_ctx_tiles, with_ctx)
        gla_of, gla_ob = _scan_mixer("gla", gla_in, n_ctx_tiles,
                                     (w_a_f[l], row(gla_b_a_f[l]), w_a_b[l], row(gla_b_a_b[l])))
        ret_of, ret_ob = _scan_mixer("ret", ret_in, n_ctx_tiles, (log_gamma, cos_ret, sin_ret))
        q, k, vt = _att_prep(att_in, cos_att, sin_att, row(jnp.tile(att_q_norm_g[l], N_HEADS)),
                             row(jnp.tile(att_k_norm_g[l], KV_HEADS)), ones_blockdiag)
        y_att = _attention(q, k, vt, n_ctx_tiles, with_ctx)
        x_all = _post(final, y_conv, gla_of, gla_ob, gla_in, y_att, ret_of, ret_ob, ret_in,
                      row(gla_norm_g[l]), row(ret_norm_g[l]), ones_blockdiag, x_all, mod,
                      row(norm2_g[l]), w_out_b[l], w_up_b[l], w_down_b[l], row(final_norm_g),
                      n_ctx_tiles, with_ctx)
    return x_all
```

```python
import functools

import jax
import jax.numpy as jnp
from jax import lax
from jax.experimental import pallas as pl
from jax.experimental.pallas import tpu as pltpu

F32 = jnp.float32
BF16 = jnp.bfloat16

EPS = 1e-6
GRID_W = 64
ROPE_THETA = 10000.0
N_MOD = 6
CONV_KSIZE = 31
CONV_PAD = (CONV_KSIZE - 1) // 2
HALO = 16
GLA_TAU = 16.0
GLA_RANK = 16
CHUNK = 64
N_HEADS = 4
KV_HEADS = 2
HEAD_DV = 64
SCAN_DK = 32
GROUP_W = 256
ROW_TILE = 256
LANES = 128
BF16_SUBLANES = 16
VT_ROWS = HEAD_DV + BF16_SUBLANES
LOG2_E = 1.4426950408889634
V7X_VMEM_BYTES = 64 * 1024 * 1024
VMEM_LIMIT = V7X_VMEM_BYTES * 3 // 4

W_CONV = 2 * GROUP_W
W_GLA = 2 * N_HEADS * SCAN_DK + 2 * GROUP_W + LANES
W_ATT = GROUP_W + 2 * KV_HEADS * HEAD_DV
W_RET = 2 * N_HEADS * SCAN_DK + 2 * GROUP_W
W_IN_PAD = W_CONV + W_GLA + W_ATT + W_RET


def _params(semantics):
    return pltpu.CompilerParams(dimension_semantics=semantics, vmem_limit_bytes=VMEM_LIMIT)


def _silu(x):
    return x * jax.nn.sigmoid(x)


def _rms(x):
    return x * lax.rsqrt(jnp.mean(x * x, axis=-1, keepdims=True) + EPS)


def _group_sumsq(t, ones_blockdiag):
    t2 = t * t
    hi = t2.astype(BF16)
    lo = (t2 - hi.astype(F32)).astype(BF16)
    return (jnp.dot(hi, ones_blockdiag, preferred_element_type=F32)
            + jnp.dot(lo, ones_blockdiag, preferred_element_type=F32))


def _mod_body(cc_ref, w_ref, b_ref, o_ref):
    s = _silu(cc_ref[...])
    o_ref[0] = jnp.dot(s.astype(BF16), w_ref[0].astype(BF16),
                       preferred_element_type=F32) + b_ref[0]


def _modulation(cc, w_mod, b_mod):
    depth, d, n = w_mod.shape
    tn = 1024
    return pl.pallas_call(
        _mod_body,
        grid=(depth, n // tn),
        in_specs=[pl.BlockSpec((8, d), lambda l, i: (0, 0)),
                  pl.BlockSpec((1, d, tn), lambda l, i: (l, 0, i)),
                  pl.BlockSpec((1, 1, tn), lambda l, i: (l, 0, i))],
        out_specs=pl.BlockSpec((1, 8, tn), lambda l, i: (l, 0, i)),
        out_shape=jax.ShapeDtypeStruct((depth, 8, n), F32),
        compiler_params=_params(("parallel", "parallel")),
        name="modulation",
    )(cc, w_mod, b_mod.reshape(depth, 1, n))


def _inproj_body(x_ref, mod_ref, g_ref, w_ref, conv_ref, gla_ref, att_ref, ret_ref):
    d = x_ref.shape[2]
    shift = mod_ref[0, :, 0:d]
    scale = mod_ref[0, :, d:2 * d]
    h = (_rms(x_ref[0]) * g_ref[...]) * (1.0 + scale) + shift
    hb = h.astype(BF16)
    off = 0
    for ref in (conv_ref, gla_ref, att_ref, ret_ref):
        w = ref.shape[2]
        ref[0] = jnp.dot(hb, w_ref[:, off:off + w], preferred_element_type=F32)
        off += w


def _in_proj(x_all, mod, norm_g, w_in_pad, n_ctx_tiles):
    bsz, t_all, d = x_all.shape
    tm = ROW_TILE
    ctx_row = bsz
    widths = (W_CONV, W_GLA, W_ATT, W_RET)
    return pl.pallas_call(
        _inproj_body,
        grid=(bsz, t_all // tm),
        in_specs=[pl.BlockSpec((1, tm, d), lambda b, j: (b, j, 0)),
                  pl.BlockSpec((1, 1, N_MOD * d),
                               lambda b, j: (jnp.where(j < n_ctx_tiles, ctx_row, b), 0, 0)),
                  pl.BlockSpec((1, d), lambda b, j: (0, 0)),
                  pl.BlockSpec((d, W_IN_PAD), lambda b, j: (0, 0))],
        out_specs=[pl.BlockSpec((1, tm, w), lambda b, j: (b, j, 0)) for w in widths],
        out_shape=[jax.ShapeDtypeStruct((bsz, t_all, w), F32) for w in widths],
        compiler_params=_params(("parallel", "parallel")),
        name="in_proj",
    )(x_all, mod, norm_g, w_in_pad)


def _conv_body(n_ctx_tiles, n_tiles, tile0, main_ref, prev_ref, next_ref, wdw_ref, bdw_ref,
               lng_ref, lnb_ref, wpw_ref, bpw_ref, o_ref, ubuf, sbuf):
    tt = main_ref.shape[1]
    c = o_ref.shape[2]
    j = pl.program_id(1) + tile0

    def glu(blk):
        return blk[:, :c] * jax.nn.sigmoid(blk[:, c:])

    has_prev = jnp.logical_and(j != 0, j != n_ctx_tiles)
    has_next = jnp.logical_and(j != n_ctx_tiles - 1, j != n_tiles - 1)

    @pl.when(has_prev)
    def _():
        ubuf[0:HALO, :] = glu(prev_ref[0])

    @pl.when(jnp.logical_not(has_prev))
    def _():
        ubuf[0:HALO, :] = jnp.zeros((HALO, c), F32)

    ubuf[HALO:HALO + tt, :] = glu(main_ref[0])

    @pl.when(has_next)
    def _():
        ubuf[HALO + tt:, :] = glu(next_ref[0])

    @pl.when(jnp.logical_not(has_next))
    def _():
        ubuf[HALO + tt:, :] = jnp.zeros((HALO, c), F32)

    rows = 64
    for r in range(0, tt, rows):
        acc = jnp.zeros((rows, c), F32)
        for k in range(CONV_KSIZE):
            lo = r + k + HALO - CONV_PAD
            acc = acc + ubuf[lo:lo + rows, :] * wdw_ref[k:k + 1, :]
        y = acc + bdw_ref[...]
        yc = y - jnp.mean(y, axis=-1, keepdims=True)
        yn = yc * lax.rsqrt(jnp.mean(yc * yc, axis=-1, keepdims=True) + EPS)
        yn = yn * lng_ref[...] + lnb_ref[...]
        sbuf[r:r + rows, :] = _silu(yn).astype(BF16)
    o_ref[0] = jnp.dot(sbuf[...], wpw_ref[...], preferred_element_type=F32) + bpw_ref[...]


def _conv_mixer(conv_in, w_dw, b_dw, ln_g, ln_b, w_pw, b_pw, n_ctx_tiles, with_ctx):
    bsz, t_all, _ = conv_in.shape
    tt = ROW_TILE
    c = GROUP_W
    n_tiles = t_all // tt
    tile0 = 0 if with_ctx else n_ctx_tiles
    per = tt // HALO
    n_halo = t_all // HALO
    row = lambda b, j: (0, 0)
    return pl.pallas_call(
        functools.partial(_conv_body, n_ctx_tiles, n_tiles, tile0),
        grid=(bsz, n_tiles - tile0),
        in_specs=[pl.BlockSpec((1, tt, 2 * c), lambda b, j: (b, j + tile0, 0)),
                  pl.BlockSpec((1, HALO, 2 * c),
                               lambda b, j: (b, jnp.maximum((j + tile0) * per - 1, 0), 0)),
                  pl.BlockSpec((1, HALO, 2 * c),
                               lambda b, j: (b, jnp.minimum((j + tile0 + 1) * per, n_halo - 1), 0)),
                  pl.BlockSpec((CONV_KSIZE, c), row),
                  pl.BlockSpec((1, c), row), pl.BlockSpec((1, c), row), pl.BlockSpec((1, c), row),
                  pl.BlockSpec((c, c), row), pl.BlockSpec((1, c), row)],
        out_specs=pl.BlockSpec((1, tt, c), lambda b, j: (b, j, 0)),
        out_shape=jax.ShapeDtypeStruct((bsz, t_all - tile0 * tt, c), F32),
        scratch_shapes=[pltpu.VMEM((tt + 2 * HALO, c), F32), pltpu.VMEM((tt, c), BF16)],
        compiler_params=_params(("parallel", "parallel")),
        name="conv_mixer",
    )(conv_in, conv_in, conv_in, w_dw, b_dw, ln_g, ln_b, w_pw, b_pw)


def _chunk_cumsum(la, reverse):
    n = la.shape[0]
    pos = lax.broadcasted_iota(jnp.int32, la.shape, 0) % CHUNK
    x = la
    s = 1
    while s < CHUNK:
        if reverse:
            x = x + jnp.where(pos < CHUNK - s, pltpu.roll(x, n - s, 0), 0.0)
        else:
            x = x + jnp.where(pos >= s, pltpu.roll(x, s, 0), 0.0)
        s *= 2
    return x


def _scan_chunk(q, k, v, b, st_ref, reverse, head_mask, value_mask, tri):
    nt_dims = (((1,), (1,)), ((), ()))
    b_tot = b[0:1, :] if reverse else b[CHUNK - 1:CHUNK, :]
    q_dec = (q * jnp.exp(b)).astype(BF16)
    k_inv = (k * jnp.exp(-b)).astype(BF16)
    k_end = (k * jnp.exp(b_tot - b)).astype(BF16)
    vb = v.astype(BF16)
    zero = jnp.zeros((), BF16)
    k_blk = jnp.where(head_mask, jnp.concatenate([k_inv] * N_HEADS, axis=0), zero)
    scores = lax.dot_general(q_dec, k_blk, nt_dims, preferred_element_type=F32)
    scores = jnp.where(tri, scores, 0.0).astype(BF16)
    v_blk = jnp.where(value_mask, jnp.concatenate([vb] * N_HEADS, axis=0), zero)
    o = jnp.dot(scores, v_blk, preferred_element_type=F32)
    st = st_ref[...]
    o = o + lax.dot_general(q_dec, st.astype(BF16), nt_dims, preferred_element_type=F32)
    kv_t = lax.dot_general(vb, k_end, (((0,), (0,)), ((), ())), preferred_element_type=F32)
    st_ref[...] = st * jnp.exp(b_tot) + jnp.where(head_mask, kv_t, 0.0)
    return o


def _scan_body(mode, *refs):
    if mode == "gla":
        (qf, kf, vf, zf, qb, kb, vb, zb, waf, baf, wab, bab, of_ref, ob_ref, stf, stb) = refs
    else:
        (qf, kf, vf, qb, kb, vb, lg, cosf, sinf, cosb, sinb, of_ref, ob_ref, stf, stb) = refs

    @pl.when(pl.program_id(1) == 0)
    def _():
        stf[...] = jnp.zeros(stf.shape, F32)
        stb[...] = jnp.zeros(stb.shape, F32)

    tb = qf.shape[1]
    dk_all = qf.shape[2]
    dv_all = vf.shape[2]
    scale = SCAN_DK ** -0.5

    r = lax.broadcasted_iota(jnp.int32, (dv_all, dk_all), 0)
    c = lax.broadcasted_iota(jnp.int32, (dv_all, dk_all), 1)
    head_mask = (r // HEAD_DV) == (c // SCAN_DK)
    r = lax.broadcasted_iota(jnp.int32, (dv_all, dv_all), 0)
    c = lax.broadcasted_iota(jnp.int32, (dv_all, dv_all), 1)
    value_mask = (r // HEAD_DV) == (c // HEAD_DV)
    r = lax.broadcasted_iota(jnp.int32, (CHUNK, dv_all), 0)
    c = lax.broadcasted_iota(jnp.int32, (CHUNK, dv_all), 1) % CHUNK
    tri_f = c <= r
    tri_b = c >= r

    if mode == "gla":
        def log_decay(z_ref, w_ref, bias_ref):
            pre = jnp.dot(z_ref[0].astype(BF16), w_ref[...], preferred_element_type=F32) + bias_ref[...]
            log_sig = jnp.minimum(pre, 0.0) - jnp.log1p(jnp.exp(-jnp.abs(pre)))
            return log_sig / GLA_TAU

        q_f, k_f, la_f = qf[0] * scale, kf[0], log_decay(zf, waf, baf)
        q_b, k_b, la_b = qb[0] * scale, kb[0], log_decay(zb, wab, bab)
    else:
        lane = lax.broadcasted_iota(jnp.int32, (tb, dk_all), 1)
        first_half = (lane % SCAN_DK) < SCAN_DK // 2

        def rope(t, cos_ref, sin_ref):
            rot = jnp.where(first_half, pltpu.roll(t, dk_all - SCAN_DK // 2, 1),
                            pltpu.roll(t, SCAN_DK // 2, 1))
            return t * cos_ref[...] + rot * sin_ref[...]

        q_f, k_f = rope(qf[0], cosf, sinf), rope(kf[0] * scale, cosf, sinf)
        q_b, k_b = rope(qb[0], cosb, sinb), rope(kb[0] * scale, cosb, sinb)
        la_f = la_b = jnp.broadcast_to(lg[...], (tb, dk_all))

    b_f = _chunk_cumsum(la_f, False)
    b_b = _chunk_cumsum(la_b, True)
    v_f = vf[0]
    v_b = vb[0]
    n_chunks = tb // CHUNK
    for i in range(n_chunks):
        lo = i * CHUNK
        of_ref[0, lo:lo + CHUNK, :] = _scan_chunk(
            q_f[lo:lo + CHUNK], k_f[lo:lo + CHUNK], v_f[lo:lo + CHUNK], b_f[lo:lo + CHUNK],
            stf, False, head_mask, value_mask, tri_f)
        lo = (n_chunks - 1 - i) * CHUNK
        ob_ref[0, lo:lo + CHUNK, :] = _scan_chunk(
            q_b[lo:lo + CHUNK], k_b[lo:lo + CHUNK], v_b[lo:lo + CHUNK], b_b[lo:lo + CHUNK],
            stb, True, head_mask, value_mask, tri_b)


def _scan_mixer(mode, src, n_ctx_tiles, extras):
    bsz, t_all, _ = src.shape
    tb = ROW_TILE
    n_tiles = t_all // tb
    dk_all = N_HEADS * SCAN_DK
    dv_all = GROUP_W

    def bwd(j):
        return jnp.where(j < n_ctx_tiles, n_ctx_tiles - 1 - j, n_tiles - 1 - (j - n_ctx_tiles))

    def col_specs(tile_of, with_z):
        specs = [pl.BlockSpec((1, tb, dk_all), lambda b, j: (b, tile_of(j), 0)),
                 pl.BlockSpec((1, tb, dk_all), lambda b, j: (b, tile_of(j), 1)),
                 pl.BlockSpec((1, tb, dv_all), lambda b, j: (b, tile_of(j), 1))]
        if with_z:
            z_blk = (2 * dk_all + 2 * dv_all) // LANES
            specs.append(pl.BlockSpec((1, tb, LANES), lambda b, j: (b, tile_of(j), z_blk)))
        return specs

    fwd = lambda j: j
    row = lambda b, j: (0, 0)
    if mode == "gla":
        w_a_f, b_a_f, w_a_b, b_a_b = extras
        in_specs = (col_specs(fwd, True) + col_specs(bwd, True)
                    + [pl.BlockSpec((LANES, dk_all), row), pl.BlockSpec((1, dk_all), row),
                       pl.BlockSpec((LANES, dk_all), row), pl.BlockSpec((1, dk_all), row)])
        args = (src,) * 8 + (w_a_f, b_a_f, w_a_b, b_a_b)
    else:
        log_gamma, cos, sin = extras
        in_specs = (col_specs(fwd, False) + col_specs(bwd, False)
                    + [pl.BlockSpec((1, dk_all), row),
                       pl.BlockSpec((tb, dk_all), lambda b, j: (j, 0)),
                       pl.BlockSpec((tb, dk_all), lambda b, j: (j, 0)),
                       pl.BlockSpec((tb, dk_all), lambda b, j: (bwd(j), 0)),
                       pl.BlockSpec((tb, dk_all), lambda b, j: (bwd(j), 0))])
        args = (src,) * 6 + (log_gamma, cos, sin, cos, sin)
    return pl.pallas_call(
        functools.partial(_scan_body, mode),
        grid=(bsz, n_tiles),
        in_specs=in_specs,
        out_specs=[pl.BlockSpec((1, tb, dv_all), lambda b, j: (b, j, 0)),
                   pl.BlockSpec((1, tb, dv_all), lambda b, j: (b, bwd(j), 0))],
        out_shape=[jax.ShapeDtypeStruct((bsz, t_all, dv_all), F32)] * 2,
        scratch_shapes=[pltpu.VMEM((dv_all, dk_all), F32), pltpu.VMEM((dv_all, dk_all), F32)],
        compiler_params=_params(("parallel", "arbitrary")),
        name="scan_" + mode,
    )(*args)


def _attprep_body(x_ref, cos_ref, sin_ref, qg_ref, kg_ref, ones_ref, q_out, k_out, vt_out):
    x = x_ref[0]
    tm = x.shape[0]
    wq = N_HEADS * HEAD_DV
    wk = KV_HEADS * HEAD_DV
    q = x[:, :wq]
    k = x[:, wq:wq + wk]
    v = x[:, wq + wk:]

    def norm_rope(t, gain, w):
        ss = _group_sumsq(t, ones_ref[:w, :w])
        t = t * lax.rsqrt(ss * (1.0 / HEAD_DV) + EPS) * gain
        lane = lax.broadcasted_iota(jnp.int32, t.shape, 1)
        first_half = (lane % HEAD_DV) < HEAD_DV // 2
        rot = jnp.where(first_half, pltpu.roll(t, w - HEAD_DV // 2, 1), pltpu.roll(t, HEAD_DV // 2, 1))
        return t * cos_ref[:, :w] + rot * sin_ref[:, :w]

    qn = (norm_rope(q, qg_ref[...], wq) * (HEAD_DV ** -0.5 * LOG2_E)).astype(BF16)
    kn = norm_rope(k, kg_ref[...], wk).astype(BF16)
    for h in range(N_HEADS):
        q_out[0, h] = qn[:, h * HEAD_DV:(h + 1) * HEAD_DV]
    for h in range(KV_HEADS):
        k_out[0, h] = kn[:, h * HEAD_DV:(h + 1) * HEAD_DV]
    vt = v.T.astype(BF16)
    ones = jnp.ones((VT_ROWS - HEAD_DV, tm), BF16)
    for h in range(KV_HEADS):
        vt_out[0, h, 0, 0:HEAD_DV, :] = vt[h * HEAD_DV:(h + 1) * HEAD_DV, :]
        vt_out[0, h, 0, HEAD_DV:, :] = ones


def _att_prep(att_in, cos, sin, q_gain, k_gain, ones_blockdiag):
    bsz, t_all, w = att_in.shape
    tm = ROW_TILE
    n_tiles = t_all // tm
    wq = N_HEADS * HEAD_DV
    row = lambda b, j: (0, 0)
    return pl.pallas_call(
        _attprep_body,
        grid=(bsz, n_tiles),
        in_specs=[pl.BlockSpec((1, tm, w), lambda b, j: (b, j, 0)),
                  pl.BlockSpec((tm, wq), lambda b, j: (j, 0)),
                  pl.BlockSpec((tm, wq), lambda b, j: (j, 0)),
                  pl.BlockSpec((1, wq), row), pl.BlockSpec((1, KV_HEADS * HEAD_DV), row),
                  pl.BlockSpec((wq, wq), row)],
        out_specs=[pl.BlockSpec((1, N_HEADS, tm, HEAD_DV), lambda b, j: (b, 0, j, 0)),
                   pl.BlockSpec((1, KV_HEADS, tm, HEAD_DV), lambda b, j: (b, 0, j, 0)),
                   pl.BlockSpec((1, KV_HEADS, 1, VT_ROWS, tm), lambda b, j: (b, 0, j, 0, 0))],
        out_shape=[jax.ShapeDtypeStruct((bsz, N_HEADS, t_all, HEAD_DV), BF16),
                   jax.ShapeDtypeStruct((bsz, KV_HEADS, t_all, HEAD_DV), BF16),
                   jax.ShapeDtypeStruct((bsz, KV_HEADS, n_tiles, VT_ROWS, tm), BF16)],
        compiler_params=_params(("parallel", "parallel")),
        name="att_prep",
    )(att_in, cos, sin, q_gain, k_gain, ones_blockdiag)


def _att_body(n_ctx_tiles, n_tiles, tile0, q_ref, k_ref, vt_ref, o_ref):
    group = q_ref.shape[1]
    tq = q_ref.shape[2]
    tk = vt_ref.shape[4]
    nq = group * tq

    def attend(n_blocks):
        q2 = q_ref[0].reshape(nq, HEAD_DV)

        def scores(i):
            kb = k_ref[0, 0, i * tk:(i + 1) * tk, :]
            return lax.dot_general(kb, q2, (((1,), (1,)), ((), ())), preferred_element_type=F32)

        m = acc = None
        s_next = scores(0)
        for i in range(n_blocks):
            s = s_next
            if i + 1 < n_blocks:
                s_next = scores(i + 1)
            blk_max = jnp.max(s, axis=0, keepdims=True)
            m_new = blk_max if i == 0 else jnp.maximum(m, blk_max)
            p = jnp.exp2(s - m_new).astype(BF16)
            pv = jnp.dot(vt_ref[0, 0, i], p, preferred_element_type=F32)
            acc = pv if i == 0 else jnp.exp2(m - m_new) * acc + pv
            m = m_new
        o_t = acc[:HEAD_DV] / acc[HEAD_DV:HEAD_DV + 1]
        stacked = jnp.concatenate([o_t[:, g * tq:(g + 1) * tq] for g in range(group)], axis=0)
        o_ref[0] = stacked.T

    qi = pl.program_id(2) + tile0
    if tile0 < n_ctx_tiles:
        pl.when(qi < n_ctx_tiles)(lambda: attend(n_ctx_tiles))
    pl.when(qi >= n_ctx_tiles)(lambda: attend(n_tiles))


def _attention(q, k, vt, n_ctx_tiles, with_ctx):
    bsz, _, t_all, _ = q.shape
    tq = ROW_TILE
    n_tiles = t_all // tq
    tile0 = 0 if with_ctx else n_ctx_tiles
    group = N_HEADS // KV_HEADS
    return pl.pallas_call(
        functools.partial(_att_body, n_ctx_tiles, n_tiles, tile0),
        grid=(bsz, KV_HEADS, n_tiles - tile0),
        in_specs=[pl.BlockSpec((1, group, tq, HEAD_DV), lambda b, h, i: (b, h, i + tile0, 0)),
                  pl.BlockSpec((1, 1, t_all, HEAD_DV), lambda b, h, i: (b, h, 0, 0)),
                  pl.BlockSpec((1, 1, n_tiles, VT_ROWS, tq), lambda b, h, i: (b, h, 0, 0, 0))],
        out_specs=pl.BlockSpec((1, tq, group * HEAD_DV), lambda b, h, i: (b, i, h)),
        out_shape=jax.ShapeDtypeStruct((bsz, t_all - tile0 * tq, GROUP_W), F32),
        compiler_params=_params(("parallel", "parallel", "arbitrary")),
        name="attention",
    )(q, k, vt)


def _post_body(final, yconv, gla_f, gla_b, gla_gate, yatt, ret_f, ret_b, ret_gate, gla_gain, ret_gain,
               ones_ref, x_ref, mod_ref, n2g_ref, wout_ref, wup_ref, wdown_ref, fng_ref, o_ref):
    d = x_ref.shape[2]
    tm = x_ref.shape[1]
    d_ff = wup_ref.shape[1]

    def finish(of_ref, ob_ref, gate_ref, gain_ref):
        o = of_ref[0] + ob_ref[0]
        ss = _group_sumsq(o, ones_ref[...])
        y = o * lax.rsqrt(ss * (1.0 / HEAD_DV) + EPS) * gain_ref[...]
        return (y * _silu(gate_ref[0])).astype(BF16)

    mixed = jnp.concatenate([yconv[0].astype(BF16), finish(gla_f, gla_b, gla_gate, gla_gain),
                             yatt[0].astype(BF16), finish(ret_f, ret_b, ret_gate, ret_gain)], axis=1)
    o = jnp.dot(mixed, wout_ref[...], preferred_element_type=F32)
    gate1 = mod_ref[0, :, 2 * d:3 * d]
    shift2 = mod_ref[0, :, 3 * d:4 * d]
    scale2 = mod_ref[0, :, 4 * d:5 * d]
    gate2 = mod_ref[0, :, 5 * d:6 * d]
    x_mid = x_ref[0] + gate1 * o
    h2 = ((_rms(x_mid) * n2g_ref[...]) * (1.0 + scale2) + shift2).astype(BF16)
    acc = jnp.zeros((tm, d), F32)
    tf = 1024
    for f in range(0, d_ff, tf):
        u = jnp.dot(h2, wup_ref[:, f:f + tf], preferred_element_type=F32)
        a = jnp.square(jnp.maximum(u, 0.0)).astype(BF16)
        acc = acc + jnp.dot(a, wdown_ref[f:f + tf, :], preferred_element_type=F32)
    x_out = x_mid + gate2 * acc
    if final:
        x_out = _rms(x_out) * fng_ref[...]
    o_ref[0] = x_out


def _post(final, yconv, gla_of, gla_ob, gla_in, yatt, ret_of, ret_ob, ret_in, gla_gain, ret_gain,
          ones_blockdiag, x_all, mod, norm2_g, w_out, w_up, w_down, final_g, n_ctx_tiles, with_ctx):
    bsz, t_all, d = x_all.shape
    tm = ROW_TILE
    ctx_row = bsz
    tile0 = 0 if with_ctx else n_ctx_tiles
    n_out = t_all // tm - tile0
    c = GROUP_W
    gate_blk = (2 * N_HEADS * SCAN_DK + c) // c
    full = lambda b, j: (b, j + tile0, 0)
    own = lambda b, j: (b, j, 0)
    row = lambda b, j: (0, 0)
    resident = dict(pipeline_mode=pl.Buffered(1))
    return pl.pallas_call(
        functools.partial(_post_body, final),
        grid=(bsz, n_out),
        in_specs=[pl.BlockSpec((1, tm, c), own),
                  pl.BlockSpec((1, tm, c), full), pl.BlockSpec((1, tm, c), full),
                  pl.BlockSpec((1, tm, c), lambda b, j: (b, j + tile0, gate_blk)),
                  pl.BlockSpec((1, tm, c), own),
                  pl.BlockSpec((1, tm, c), full), pl.BlockSpec((1, tm, c), full),
                  pl.BlockSpec((1, tm, c), lambda b, j: (b, j + tile0, gate_blk)),
                  pl.BlockSpec((1, c), row), pl.BlockSpec((1, c), row),
                  pl.BlockSpec((c, c), row),
                  pl.BlockSpec((1, tm, d), full),
                  pl.BlockSpec((1, 1, N_MOD * d),
                               lambda b, j: (jnp.where(j + tile0 < n_ctx_tiles, ctx_row, b), 0, 0)),
                  pl.BlockSpec((1, d), row),
                  pl.BlockSpec(w_out.shape, row, **resident),
                  pl.BlockSpec(w_up.shape, row, **resident),
                  pl.BlockSpec(w_down.shape, row, **resident),
                  pl.BlockSpec((1, d), row)],
        out_specs=pl.BlockSpec((1, tm, d), own),
        out_shape=jax.ShapeDtypeStruct((bsz, n_out * tm, d), F32),
        compiler_params=_params(("parallel", "parallel")),
        name="post",
    )(yconv, gla_of, gla_ob, gla_in, yatt, ret_of, ret_ob, ret_in, gla_gain, ret_gain,
      ones_blockdiag, x_all, mod, norm2_g, w_out, w_up, w_down, final_g)


def _rope_table(t_lat, n_ctx, head_dim, width):
    n_ax = head_dim // 4
    inv = ROPE_THETA ** (-jnp.arange(n_ax, dtype=F32) / n_ax)
    pos = jnp.arange(t_lat)
    ang = jnp.concatenate([(pos // GRID_W).astype(F32)[:, None] * inv,
                           (pos % GRID_W).astype(F32)[:, None] * inv], axis=-1)
    cos, sin = jnp.cos(ang), jnp.sin(ang)
    cos = jnp.concatenate([jnp.ones((n_ctx, head_dim // 2), F32), cos], axis=0)
    sin = jnp.concatenate([jnp.zeros((n_ctx, head_dim // 2), F32), sin], axis=0)
    reps = width // head_dim
    return (jnp.tile(jnp.concatenate([cos, cos], axis=-1), (1, reps)),
            jnp.tile(jnp.concatenate([-sin, sin], axis=-1), (1, reps)))


def kernel(x, c, ctx, c_ctx, w_mod, b_mod, norm1_g, norm2_g, w_in, conv_w_dw, conv_b_dw, conv_ln_g,
           conv_ln_b, conv_w_pw, conv_b_pw, gla_w_a_f, gla_b_a_f, gla_w_a_b, gla_b_a_b, gla_norm_g,
           att_q_norm_g, att_k_norm_g, ret_norm_g, w_out, w_up, w_down, final_norm_g):
    bsz, t_lat, d = x.shape
    n_ctx = ctx.shape[1]
    depth = w_mod.shape[0]
    assert bsz + 1 <= 8 and n_ctx % ROW_TILE == 0 and t_lat % ROW_TILE == 0
    n_ctx_tiles = n_ctx // ROW_TILE
    dk_all = N_HEADS * SCAN_DK

    cc = jnp.concatenate([c, c_ctx[None, :], jnp.zeros((8 - bsz - 1, d), F32)], axis=0)
    mods = _modulation(cc, w_mod, b_mod)

    z_off = W_CONV + W_GLA - LANES
    w_in_pad = jnp.concatenate(
        [w_in[:, :, :z_off + GLA_RANK], jnp.zeros((depth, d, LANES - GLA_RANK), F32),
         w_in[:, :, z_off + GLA_RANK:]], axis=-1).astype(BF16)
    pad_rank = lambda w: jnp.concatenate(
        [w, jnp.zeros((depth, LANES - GLA_RANK, dk_all), F32)], axis=1).astype(BF16)
    w_a_f, w_a_b = pad_rank(gla_w_a_f), pad_rank(gla_w_a_b)
    w_pw, w_out_b, w_up_b, w_down_b = (w.astype(BF16) for w in (conv_w_pw, w_out, w_up, w_down))

    cos_att, sin_att = _rope_table(t_lat, n_ctx, HEAD_DV, N_HEADS * HEAD_DV)
    cos_ret, sin_ret = _rope_table(t_lat, n_ctx, SCAN_DK, dk_all)
    log_gamma = jnp.log1p(-jnp.exp2(-5.0 - jnp.arange(N_HEADS, dtype=F32)))
    log_gamma = jnp.repeat(log_gamma, SCAN_DK)[None, :]
    gid = jnp.arange(GROUP_W) // HEAD_DV
    ones_blockdiag = (gid[:, None] == gid[None, :]).astype(BF16)
    row = lambda v: v.reshape(1, -1)

    x_all = jnp.concatenate([ctx, x], axis=1)
    for l in range(depth):
        with_ctx = l < depth - 1
        final = l == depth - 1
        mod = mods[l].reshape(8, 1, N_MOD * d)
        conv_in, gla_in, att_in, ret_in = _in_proj(x_all, mod, row(norm1_g[l]), w_in_pad[l], n_ctx_tiles)
        y_conv = _conv_mixer(conv_in, conv_w_dw[l], row(conv_b_dw[l]), row(conv_ln_g[l]),
                             row(conv_ln_b[l]), w_pw[l], row(conv_b_pw[l]), n_ctx_tiles, with_ctx)
        gla_of, gla_ob = _scan_mixer("gla", gla_in, n_ctx_tiles,
                                     (w_a_f[l], row(gla_b_a_f[l]), w_a_b[l], row(gla_b_a_b[l])))
        ret_of, ret_ob = _scan_mixer("ret", ret_in, n_ctx_tiles, (log_gamma, cos_ret, sin_ret))
        q, k, vt = _att_prep(att_in, cos_att, sin_att, row(jnp.tile(att_q_norm_g[l], N_HEADS)),
                             row(jnp.tile(att_k_norm_g[l], KV_HEADS)), ones_blockdiag)
        y_att = _attention(q, k, vt, n_ctx_tiles, with_ctx)
        x_all = _post(final, y_conv, gla_of, gla_ob, gla_in, y_att, ret_of, ret_ob, ret_in,
                      row(gla_norm_g[l]), row(ret_norm_g[l]), ones_blockdiag, x_all, mod,
                      row(norm2_g[l]), w_out_b[l], w_up_b[l], w_down_b[l], row(final_norm_g),
                      n_ctx_tiles, with_ctx)
    return x_all
```

```python
import functools

import jax
import jax.numpy as jnp
from jax import lax
from jax.experimental import pallas as pl
from jax.experimental.pallas import tpu as pltpu

F32 = jnp.float32
BF16 = jnp.bfloat16

EPS = 1e-6
GRID_W = 64
ROPE_THETA = 10000.0
N_MOD = 6
CONV_KSIZE = 31
CONV_PAD = (CONV_KSIZE - 1) // 2
HALO = 16
GLA_TAU = 16.0
GLA_RANK = 16
CHUNK = 64
N_HEADS = 4
KV_HEADS = 2
HEAD_DV = 64
SCAN_DK = 32
GROUP_W = 256
ROW_TILE = 256
LANES = 128
SUBLANES = 8
BF16_SUBLANES = 16
VT_ROWS = HEAD_DV + BF16_SUBLANES
LOG2_E = 1.4426950408889634
V7X_VMEM_BYTES = 64 * 1024 * 1024
VMEM_LIMIT = V7X_VMEM_BYTES * 3 // 4

W_CONV = 2 * GROUP_W
W_GLA = 2 * N_HEADS * SCAN_DK + 2 * GROUP_W + LANES
W_ATT = GROUP_W + 2 * KV_HEADS * HEAD_DV
W_RET = 2 * N_HEADS * SCAN_DK + 2 * GROUP_W
W_IN_PAD = W_CONV + W_GLA + W_ATT + W_RET


def _params(semantics):
    return pltpu.CompilerParams(dimension_semantics=semantics, vmem_limit_bytes=VMEM_LIMIT)


def _silu(x):
    return x * jax.nn.sigmoid(x)


def _rms(x):
    return x * lax.rsqrt(jnp.mean(x * x, axis=-1, keepdims=True) + EPS)


def _group_sumsq(t, ones_blockdiag):
    t2 = t * t
    hi = t2.astype(BF16)
    lo = (t2 - hi.astype(F32)).astype(BF16)
    return (jnp.dot(hi, ones_blockdiag, preferred_element_type=F32)
            + jnp.dot(lo, ones_blockdiag, preferred_element_type=F32))


def _mod_body(cc_ref, w_ref, b_ref, o_ref):
    s = _silu(cc_ref[...])
    o_ref[0] = jnp.dot(s.astype(BF16), w_ref[0].astype(BF16),
                       preferred_element_type=F32) + b_ref[0]


def _modulation(cc, w_mod, b_mod):
    depth, d, n = w_mod.shape
    tn = 1024
    return pl.pallas_call(
        _mod_body,
        grid=(depth, n // tn),
        in_specs=[pl.BlockSpec((8, d), lambda l, i: (0, 0)),
                  pl.BlockSpec((1, d, tn), lambda l, i: (l, 0, i)),
                  pl.BlockSpec((1, 1, tn), lambda l, i: (l, 0, i))],
        out_specs=pl.BlockSpec((1, 8, tn), lambda l, i: (l, 0, i)),
        out_shape=jax.ShapeDtypeStruct((depth, 8, n), F32),
        compiler_params=_params(("parallel", "parallel")),
        name="modulation",
    )(cc, w_mod, b_mod.reshape(depth, 1, n))


def _inproj_body(x_ref, mod_ref, g_ref, w_ref, conv_ref, gla_ref, att_ref, ret_ref):
    d = x_ref.shape[2]
    shift = mod_ref[0, :, 0:d]
    scale = mod_ref[0, :, d:2 * d]
    h = (_rms(x_ref[0]) * g_ref[...]) * (1.0 + scale) + shift
    hb = h.astype(BF16)
    off = 0
    for ref in (conv_ref, gla_ref, att_ref, ret_ref):
        w = ref.shape[2]
        ref[0] = jnp.dot(hb, w_ref[:, off:off + w], preferred_element_type=F32)
        off += w


def _in_proj(x_all, mod, norm_g, w_in_pad, n_ctx_tiles):
    bsz, t_all, d = x_all.shape
    tm = ROW_TILE
    ctx_row = bsz
    widths = (W_CONV, W_GLA, W_ATT, W_RET)
    return pl.pallas_call(
        _inproj_body,
        grid=(bsz, t_all // tm),
        in_specs=[pl.BlockSpec((1, tm, d), lambda b, j: (b, j, 0)),
                  pl.BlockSpec((1, 1, N_MOD * d),
                               lambda b, j: (jnp.where(j < n_ctx_tiles, ctx_row, b), 0, 0)),
                  pl.BlockSpec((1, d), lambda b, j: (0, 0)),
                  pl.BlockSpec((d, W_IN_PAD), lambda b, j: (0, 0))],
        out_specs=[pl.BlockSpec((1, tm, w), lambda b, j: (b, j, 0)) for w in widths],
        out_shape=[jax.ShapeDtypeStruct((bsz, t_all, w), F32) for w in widths],
        compiler_params=_params(("parallel", "parallel")),
        name="in_proj",
    )(x_all, mod, norm_g, w_in_pad)


def _conv_body(n_ctx_tiles, n_tiles, tile0, main_ref, prev_ref, next_ref, wdw_ref, bdw_ref,
               lng_ref, lnb_ref, wpw_ref, bpw_ref, o_ref, ubuf, shifted, sbuf):
    tt = main_ref.shape[1]
    c = o_ref.shape[2]
    j = pl.program_id(1) + tile0

    def glu(blk):
        return blk[:, :c] * jax.nn.sigmoid(blk[:, c:])

    has_prev = jnp.logical_and(j != 0, j != n_ctx_tiles)
    has_next = jnp.logical_and(j != n_ctx_tiles - 1, j != n_tiles - 1)

    @pl.when(has_prev)
    def _():
        ubuf[0:HALO, :] = glu(prev_ref[0])

    @pl.when(jnp.logical_not(has_prev))
    def _():
        ubuf[0:HALO, :] = jnp.zeros((HALO, c), F32)

    ubuf[HALO:HALO + tt, :] = glu(main_ref[0])

    @pl.when(has_next)
    def _():
        ubuf[HALO + tt:, :] = glu(next_ref[0])

    @pl.when(jnp.logical_not(has_next))
    def _():
        ubuf[HALO + tt:, :] = jnp.zeros((HALO, c), F32)

    span = shifted.shape[1]
    for s in range(1, SUBLANES):
        shifted[s - 1] = ubuf[s:s + span, :]

    rows = 64
    for r in range(0, tt, rows):
        acc = jnp.zeros((rows, c), F32)
        for k in range(CONV_KSIZE):
            off = k + HALO - CONV_PAD
            lo = r + off - off % SUBLANES
            if off % SUBLANES == 0:
                tap = ubuf[lo:lo + rows, :]
            else:
                tap = shifted[off % SUBLANES - 1, lo:lo + rows, :]
            acc = acc + tap * wdw_ref[k:k + 1, :]
        y = acc + bdw_ref[...]
        yc = y - jnp.mean(y, axis=-1, keepdims=True)
        yn = yc * lax.rsqrt(jnp.mean(yc * yc, axis=-1, keepdims=True) + EPS)
        yn = yn * lng_ref[...] + lnb_ref[...]
        sbuf[r:r + rows, :] = _silu(yn).astype(BF16)
    o_ref[0] = jnp.dot(sbuf[...], wpw_ref[...], preferred_element_type=F32) + bpw_ref[...]


def _conv_mixer(conv_in, w_dw, b_dw, ln_g, ln_b, w_pw, b_pw, n_ctx_tiles, with_ctx):
    bsz, t_all, _ = conv_in.shape
    tt = ROW_TILE
    c = GROUP_W
    n_tiles = t_all // tt
    tile0 = 0 if with_ctx else n_ctx_tiles
    per = tt // HALO
    n_halo = t_all // HALO
    row = lambda b, j: (0, 0)
    return pl.pallas_call(
        functools.partial(_conv_body, n_ctx_tiles, n_tiles, tile0),
        grid=(bsz, n_tiles - tile0),
        in_specs=[pl.BlockSpec((1, tt, 2 * c), lambda b, j: (b, j + tile0, 0)),
                  pl.BlockSpec((1, HALO, 2 * c),
                               lambda b, j: (b, jnp.maximum((j + tile0) * per - 1, 0), 0)),
                  pl.BlockSpec((1, HALO, 2 * c),
                               lambda b, j: (b, jnp.minimum((j + tile0 + 1) * per, n_halo - 1), 0)),
                  pl.BlockSpec((CONV_KSIZE, c), row),
                  pl.BlockSpec((1, c), row), pl.BlockSpec((1, c), row), pl.BlockSpec((1, c), row),
                  pl.BlockSpec((c, c), row), pl.BlockSpec((1, c), row)],
        out_specs=pl.BlockSpec((1, tt, c), lambda b, j: (b, j, 0)),
        out_shape=jax.ShapeDtypeStruct((bsz, t_all - tile0 * tt, c), F32),
        scratch_shapes=[pltpu.VMEM((tt + 2 * HALO, c), F32),
                        pltpu.VMEM((SUBLANES - 1, tt + 2 * HALO - SUBLANES, c), F32),
                        pltpu.VMEM((tt, c), BF16)],
        compiler_params=_params(("parallel", "parallel")),
        name="conv_mixer",
    )(conv_in, conv_in, conv_in, w_dw, b_dw, ln_g, ln_b, w_pw, b_pw)


def _chunk_cumsum(la, tri_ref):
    w = la.shape[1]
    hi = la.astype(BF16)
    lo = (la - hi.astype(F32)).astype(BF16)
    both = jnp.dot(tri_ref[...], jnp.concatenate([hi, lo], axis=1), preferred_element_type=F32)
    return both[:, :w] + both[:, w:]


def _decay_factors(b, reverse):
    b_tot = b[0:1, :] if reverse else b[CHUNK - 1:CHUNK, :]
    return jnp.exp(b), jnp.exp(-b), jnp.exp(b_tot - b), jnp.exp(b_tot)


def _scan_chunk_head(q, k, v, factors, st, key_mask, value_mask, head_mask):
    nt_dims = (((1,), (1,)), ((), ()))
    e_q, e_inv, e_end, dec = factors
    q_dec = (q * e_q).astype(BF16)
    k_inv = (k * e_inv).astype(BF16)
    k_end = (k * e_end).astype(BF16)
    vb = v.astype(BF16)
    k_blk = jnp.concatenate([k_inv] * N_HEADS, axis=0) * key_mask
    scores = lax.dot_general(q_dec, k_blk, nt_dims, preferred_element_type=F32)
    o_inter = lax.dot_general(q_dec, st.astype(BF16), nt_dims, preferred_element_type=F32)
    kv_t = lax.dot_general(vb, k_end, (((0,), (0,)), ((), ())), preferred_element_type=F32)
    st = st * dec + jnp.where(head_mask, kv_t, 0.0)
    v_blk = jnp.concatenate([vb] * N_HEADS, axis=0) * value_mask
    return (scores, v_blk, o_inter), st


def _scan_chunk_tail(pending, tri):
    scores, v_blk, o_inter = pending
    scores = jnp.where(tri, scores, 0.0).astype(BF16)
    return jnp.dot(scores, v_blk, preferred_element_type=F32) + o_inter


def _scan_body(gf, gb, rf, rb, waf, baf, wab, bab, lg, cosf, sinf, cosb, sinb,
               key_mask_ref, value_mask_ref, cum_f_ref, cum_b_ref,
               gof, gob, rof, rob, st_gf, st_gb, st_rf, st_rb):
    states = (st_gf, st_gb, st_rf, st_rb)

    @pl.when(pl.program_id(1) == 0)
    def _():
        for st in states:
            st[...] = jnp.zeros(st.shape, F32)

    tb = gf.shape[1]
    dk_all = N_HEADS * SCAN_DK
    dv_all = GROUP_W
    scale = SCAN_DK ** -0.5
    key_mask = key_mask_ref[...]
    value_mask = value_mask_ref[...]
    r = lax.broadcasted_iota(jnp.int32, (dv_all, dk_all), 0)
    c = lax.broadcasted_iota(jnp.int32, (dv_all, dk_all), 1)
    head_mask = (r // HEAD_DV) == (c // SCAN_DK)
    r = lax.broadcasted_iota(jnp.int32, (CHUNK, dv_all), 0)
    c = lax.broadcasted_iota(jnp.int32, (CHUNK, dv_all), 1) % CHUNK
    tri_f = c <= r
    tri_b = c >= r

    def qkv(ref):
        return (ref[0, :, 0:dk_all], ref[0, :, dk_all:2 * dk_all],
                ref[0, :, 2 * dk_all:2 * dk_all + dv_all])

    def gla_cum_decay(ref, w_ref, bias_ref, cum_ref):
        z = ref[0, :, 2 * dk_all + 2 * dv_all:]
        pre = jnp.dot(z.astype(BF16), w_ref[...], preferred_element_type=F32) + bias_ref[...]
        log_sig = jnp.minimum(pre, 0.0) - jnp.log1p(jnp.exp(-jnp.abs(pre)))
        return _chunk_cumsum(log_sig / GLA_TAU, cum_ref)

    q_gf, k_gf, v_gf = qkv(gf)
    q_gb, k_gb, v_gb = qkv(gb)
    b_gf = gla_cum_decay(gf, waf, baf, cum_f_ref)
    b_gb = gla_cum_decay(gb, wab, bab, cum_b_ref)

    lane = lax.broadcasted_iota(jnp.int32, (tb, dk_all), 1)
    first_half = (lane % SCAN_DK) < SCAN_DK // 2

    def rope(t, cos_ref, sin_ref):
        rot = jnp.where(first_half, pltpu.roll(t, dk_all - SCAN_DK // 2, 1),
                        pltpu.roll(t, SCAN_DK // 2, 1))
        return t * cos_ref[...] + rot * sin_ref[...]

    q_rf, k_rf, v_rf = qkv(rf)
    q_rb, k_rb, v_rb = qkv(rb)
    q_rf, k_rf = rope(q_rf, cosf, sinf), rope(k_rf * scale, cosf, sinf)
    q_rb, k_rb = rope(q_rb, cosb, sinb), rope(k_rb * scale, cosb, sinb)
    pos = lax.broadcasted_iota(jnp.int32, (CHUNK, dk_all), 0).astype(F32)
    ret_fac_f = _decay_factors((pos + 1.0) * lg[...], False)
    ret_fac_b = _decay_factors((CHUNK - pos) * lg[...], True)

    chains = (
        (q_gf * scale, k_gf, v_gf, lambda lo: _decay_factors(b_gf[lo:lo + CHUNK], False), False, gof, tri_f),
        (q_gb * scale, k_gb, v_gb, lambda lo: _decay_factors(b_gb[lo:lo + CHUNK], True), True, gob, tri_b),
        (q_rf, k_rf, v_rf, lambda lo: ret_fac_f, False, rof, tri_f),
        (q_rb, k_rb, v_rb, lambda lo: ret_fac_b, True, rob, tri_b),
    )
    st_vals = [st[...] for st in states]
    n_chunks = tb // CHUNK

    def heads(i):
        pending = []
        for n, (q, k, v, factors, reverse, _, _) in enumerate(chains):
            lo = (n_chunks - 1 - i if reverse else i) * CHUNK
            part, st_vals[n] = _scan_chunk_head(
                q[lo:lo + CHUNK], k[lo:lo + CHUNK], v[lo:lo + CHUNK], factors(lo), st_vals[n],
                key_mask, value_mask, head_mask)
            pending.append((lo, part))
        return pending

    pending = heads(0)
    for i in range(n_chunks):
        following = heads(i + 1) if i + 1 < n_chunks else None
        for (lo, part), (_, _, _, _, _, out_ref, tri) in zip(pending, chains):
            out_ref[0, lo:lo + CHUNK, :] = _scan_chunk_tail(part, tri)
        pending = following
    for st, val in zip(states, st_vals):
        st[...] = val


def _scan_mixers(gla_in, ret_in, n_ctx_tiles, gla_decay, log_gamma, cos, sin, key_mask, value_mask):
    bsz, t_all, _ = gla_in.shape
    tb = ROW_TILE
    n_tiles = t_all // tb
    dk_all = N_HEADS * SCAN_DK
    dv_all = GROUP_W

    def bwd(j):
        return jnp.where(j < n_ctx_tiles, n_ctx_tiles - 1 - j, n_tiles - 1 - (j - n_ctx_tiles))

    pos = jnp.arange(tb)
    same_chunk = (pos[:, None] // CHUNK) == (pos[None, :] // CHUNK)
    cum_f = (same_chunk & (pos[None, :] <= pos[:, None])).astype(BF16)
    cum_b = (same_chunk & (pos[None, :] >= pos[:, None])).astype(BF16)

    row = lambda b, j: (0, 0)
    fwd_blk = lambda b, j: (b, j, 0)
    bwd_blk = lambda b, j: (b, bwd(j), 0)
    w_a_f, b_a_f, w_a_b, b_a_b = gla_decay
    in_specs = [pl.BlockSpec((1, tb, W_GLA), fwd_blk), pl.BlockSpec((1, tb, W_GLA), bwd_blk),
                pl.BlockSpec((1, tb, W_RET), fwd_blk), pl.BlockSpec((1, tb, W_RET), bwd_blk),
                pl.BlockSpec((LANES, dk_all), row), pl.BlockSpec((1, dk_all), row),
                pl.BlockSpec((LANES, dk_all), row), pl.BlockSpec((1, dk_all), row),
                pl.BlockSpec((1, dk_all), row),
                pl.BlockSpec((tb, dk_all), lambda b, j: (j, 0)),
                pl.BlockSpec((tb, dk_all), lambda b, j: (j, 0)),
                pl.BlockSpec((tb, dk_all), lambda b, j: (bwd(j), 0)),
                pl.BlockSpec((tb, dk_all), lambda b, j: (bwd(j), 0)),
                pl.BlockSpec((dv_all, dk_all), row), pl.BlockSpec((dv_all, dv_all), row),
                pl.BlockSpec((tb, tb), row), pl.BlockSpec((tb, tb), row)]
    return pl.pallas_call(
        _scan_body,
        grid=(bsz, n_tiles),
        in_specs=in_specs,
        out_specs=[pl.BlockSpec((1, tb, dv_all), fwd_blk), pl.BlockSpec((1, tb, dv_all), bwd_blk)] * 2,
        out_shape=[jax.ShapeDtypeStruct((bsz, t_all, dv_all), F32)] * 4,
        scratch_shapes=[pltpu.VMEM((dv_all, dk_all), F32)] * 4,
        compiler_params=_params(("parallel", "arbitrary")),
        name="scans",
    )(gla_in, gla_in, ret_in, ret_in, w_a_f, b_a_f, w_a_b, b_a_b, log_gamma, cos, sin, cos, sin,
      key_mask, value_mask, cum_f, cum_b)


def _attprep_body(x_ref, cos_ref, sin_ref, qg_ref, kg_ref, ones_ref, q_out, k_out, vt_out):
    x = x_ref[0]
    tm = x.shape[0]
    wq = N_HEADS * HEAD_DV
    wk = KV_HEADS * HEAD_DV
    q = x[:, :wq]
    k = x[:, wq:wq + wk]
    v = x[:, wq + wk:]

    def norm_rope(t, gain, w):
        ss = _group_sumsq(t, ones_ref[:w, :w])
        t = t * lax.rsqrt(ss * (1.0 / HEAD_DV) + EPS) * gain
        lane = lax.broadcasted_iota(jnp.int32, t.shape, 1)
        first_half = (lane % HEAD_DV) < HEAD_DV // 2
        rot = jnp.where(first_half, pltpu.roll(t, w - HEAD_DV // 2, 1), pltpu.roll(t, HEAD_DV // 2, 1))
        return t * cos_ref[:, :w] + rot * sin_ref[:, :w]

    qn = (norm_rope(q, qg_ref[...], wq) * (HEAD_DV ** -0.5 * LOG2_E)).astype(BF16)
    kn = norm_rope(k, kg_ref[...], wk).astype(BF16)
    for h in range(N_HEADS):
        q_out[0, h] = qn[:, h * HEAD_DV:(h + 1) * HEAD_DV]
    for h in range(KV_HEADS):
        k_out[0, h] = kn[:, h * HEAD_DV:(h + 1) * HEAD_DV]
    vt = v.T.astype(BF16)
    ones = jnp.ones((VT_ROWS - HEAD_DV, tm), BF16)
    for h in range(KV_HEADS):
        vt_out[0, h, 0, 0:HEAD_DV, :] = vt[h * HEAD_DV:(h + 1) * HEAD_DV, :]
        vt_out[0, h, 0, HEAD_DV:, :] = ones


def _att_prep(att_in, cos, sin, q_gain, k_gain, ones_blockdiag):
    bsz, t_all, w = att_in.shape
    tm = ROW_TILE
    n_tiles = t_all // tm
    wq = N_HEADS * HEAD_DV
    row = lambda b, j: (0, 0)
    return pl.pallas_call(
        _attprep_body,
        grid=(bsz, n_tiles),
        in_specs=[pl.BlockSpec((1, tm, w), lambda b, j: (b, j, 0)),
                  pl.BlockSpec((tm, wq), lambda b, j: (j, 0)),
                  pl.BlockSpec((tm, wq), lambda b, j: (j, 0)),
                  pl.BlockSpec((1, wq), row), pl.BlockSpec((1, KV_HEADS * HEAD_DV), row),
                  pl.BlockSpec((wq, wq), row)],
        out_specs=[pl.BlockSpec((1, N_HEADS, tm, HEAD_DV), lambda b, j: (b, 0, j, 0)),
                   pl.BlockSpec((1, KV_HEADS, tm, HEAD_DV), lambda b, j: (b, 0, j, 0)),
                   pl.BlockSpec((1, KV_HEADS, 1, VT_ROWS, tm), lambda b, j: (b, 0, j, 0, 0))],
        out_shape=[jax.ShapeDtypeStruct((bsz, N_HEADS, t_all, HEAD_DV), BF16),
                   jax.ShapeDtypeStruct((bsz, KV_HEADS, t_all, HEAD_DV), BF16),
                   jax.ShapeDtypeStruct((bsz, KV_HEADS, n_tiles, VT_ROWS, tm), BF16)],
        compiler_params=_params(("parallel", "parallel")),
        name="att_prep",
    )(att_in, cos, sin, q_gain, k_gain, ones_blockdiag)


def _att_body(n_ctx_tiles, n_tiles, tile0, q_ref, k_ref, vt_ref, o_ref):
    group = q_ref.shape[1]
    tq = q_ref.shape[2]
    tk = vt_ref.shape[4]
    nq = group * tq

    def attend(n_blocks):
        q2 = q_ref[0].reshape(nq, HEAD_DV)

        def scores(i):
            kb = k_ref[0, 0, i * tk:(i + 1) * tk, :]
            return lax.dot_general(kb, q2, (((1,), (1,)), ((), ())), preferred_element_type=F32)

        m = acc = None
        s_next = scores(0)
        for i in range(n_blocks):
            s = s_next
            if i + 1 < n_blocks:
                s_next = scores(i + 1)
            blk_max = jnp.max(s, axis=0, keepdims=True)
            m_new = blk_max if i == 0 else jnp.maximum(m, blk_max)
            p = jnp.exp2(s - m_new).astype(BF16)
            pv = jnp.dot(vt_ref[0, 0, i], p, preferred_element_type=F32)
            acc = pv if i == 0 else jnp.exp2(m - m_new) * acc + pv
            m = m_new
        o_t = acc[:HEAD_DV] / acc[HEAD_DV:HEAD_DV + 1]
        stacked = jnp.concatenate([o_t[:, g * tq:(g + 1) * tq] for g in range(group)], axis=0)
        o_ref[0] = stacked.T

    qi = pl.program_id(2) + tile0
    if tile0 < n_ctx_tiles:
        pl.when(qi < n_ctx_tiles)(lambda: attend(n_ctx_tiles))
    pl.when(qi >= n_ctx_tiles)(lambda: attend(n_tiles))


def _attention(q, k, vt, n_ctx_tiles, with_ctx):
    bsz, _, t_all, _ = q.shape
    tq = ROW_TILE
    n_tiles = t_all // tq
    tile0 = 0 if with_ctx else n_ctx_tiles
    group = N_HEADS // KV_HEADS
    return pl.pallas_call(
        functools.partial(_att_body, n_ctx_tiles, n_tiles, tile0),
        grid=(bsz, KV_HEADS, n_tiles - tile0),
        in_specs=[pl.BlockSpec((1, group, tq, HEAD_DV), lambda b, h, i: (b, h, i + tile0, 0)),
                  pl.BlockSpec((1, 1, t_all, HEAD_DV), lambda b, h, i: (b, h, 0, 0)),
                  pl.BlockSpec((1, 1, n_tiles, VT_ROWS, tq), lambda b, h, i: (b, h, 0, 0, 0))],
        out_specs=pl.BlockSpec((1, tq, group * HEAD_DV), lambda b, h, i: (b, i, h)),
        out_shape=jax.ShapeDtypeStruct((bsz, t_all - tile0 * tq, GROUP_W), F32),
        compiler_params=_params(("parallel", "parallel", "arbitrary")),
        name="attention",
    )(q, k, vt)


def _post_body(final, yconv, gla_f, gla_b, gla_gate, yatt, ret_f, ret_b, ret_gate, gla_gain, ret_gain,
               ones_ref, x_ref, mod_ref, n2g_ref, wout_ref, wup_ref, wdown_ref, fng_ref, o_ref):
    d = x_ref.shape[2]
    tm = x_ref.shape[1]
    d_ff = wup_ref.shape[1]

    def finish(of_ref, ob_ref, gate_ref, gain_ref):
        o = of_ref[0] + ob_ref[0]
        ss = _group_sumsq(o, ones_ref[...])
        y = o * lax.rsqrt(ss * (1.0 / HEAD_DV) + EPS) * gain_ref[...]
        return (y * _silu(gate_ref[0])).astype(BF16)

    mixed = jnp.concatenate([yconv[0].astype(BF16), finish(gla_f, gla_b, gla_gate, gla_gain),
                             yatt[0].astype(BF16), finish(ret_f, ret_b, ret_gate, ret_gain)], axis=1)
    o = jnp.dot(mixed, wout_ref[...], preferred_element_type=F32)
    gate1 = mod_ref[0, :, 2 * d:3 * d]
    shift2 = mod_ref[0, :, 3 * d:4 * d]
    scale2 = mod_ref[0, :, 4 * d:5 * d]
    gate2 = mod_ref[0, :, 5 * d:6 * d]
    x_mid = x_ref[0] + gate1 * o
    h2 = ((_rms(x_mid) * n2g_ref[...]) * (1.0 + scale2) + shift2).astype(BF16)
    acc = jnp.zeros((tm, d), F32)
    tf = 1024
    for f in range(0, d_ff, tf):
        u = jnp.dot(h2, wup_ref[:, f:f + tf], preferred_element_type=F32)
        a = jnp.square(jnp.maximum(u, 0.0)).astype(BF16)
        acc = acc + jnp.dot(a, wdown_ref[f:f + tf, :], preferred_element_type=F32)
    x_out = x_mid + gate2 * acc
    if final:
        x_out = _rms(x_out) * fng_ref[...]
    o_ref[0] = x_out


def _post(final, yconv, gla_of, gla_ob, gla_in, yatt, ret_of, ret_ob, ret_in, gla_gain, ret_gain,
          ones_blockdiag, x_all, mod, norm2_g, w_out, w_up, w_down, final_g, n_ctx_tiles, with_ctx):
    bsz, t_all, d = x_all.shape
    tm = ROW_TILE
    ctx_row = bsz
    tile0 = 0 if with_ctx else n_ctx_tiles
    n_out = t_all // tm - tile0
    c = GROUP_W
    gate_blk = (2 * N_HEADS * SCAN_DK + c) // c
    full = lambda b, j: (b, j + tile0, 0)
    own = lambda b, j: (b, j, 0)
    row = lambda b, j: (0, 0)
    resident = dict(pipeline_mode=pl.Buffered(1))
    return pl.pallas_call(
        functools.partial(_post_body, final),
        grid=(bsz, n_out),
        in_specs=[pl.BlockSpec((1, tm, c), own),
                  pl.BlockSpec((1, tm, c), full), pl.BlockSpec((1, tm, c), full),
                  pl.BlockSpec((1, tm, c), lambda b, j: (b, j + tile0, gate_blk)),
                  pl.BlockSpec((1, tm, c), own),
                  pl.BlockSpec((1, tm, c), full), pl.BlockSpec((1, tm, c), full),
                  pl.BlockSpec((1, tm, c), lambda b, j: (b, j + tile0, gate_blk)),
                  pl.BlockSpec((1, c), row), pl.BlockSpec((1, c), row),
                  pl.BlockSpec((c, c), row),
                  pl.BlockSpec((1, tm, d), full),
                  pl.BlockSpec((1, 1, N_MOD * d),
                               lambda b, j: (jnp.where(j + tile0 < n_ctx_tiles, ctx_row, b), 0, 0)),
                  pl.BlockSpec((1, d), row),
                  pl.BlockSpec(w_out.shape, row, **resident),
                  pl.BlockSpec(w_up.shape, row, **resident),
                  pl.BlockSpec(w_down.shape, row, **resident),
                  pl.BlockSpec((1, d), row)],
        out_specs=pl.BlockSpec((1, tm, d), own),
        out_shape=jax.ShapeDtypeStruct((bsz, n_out * tm, d), F32),
        compiler_params=_params(("parallel", "parallel")),
        name="post",
    )(yconv, gla_of, gla_ob, gla_in, yatt, ret_of, ret_ob, ret_in, gla_gain, ret_gain,
      ones_blockdiag, x_all, mod, norm2_g, w_out, w_up, w_down, final_g)


def _rope_table(t_lat, n_ctx, head_dim, width):
    n_ax = head_dim // 4
    inv = ROPE_THETA ** (-jnp.arange(n_ax, dtype=F32) / n_ax)
    pos = jnp.arange(t_lat)
    ang = jnp.concatenate([(pos // GRID_W).astype(F32)[:, None] * inv,
                           (pos % GRID_W).astype(F32)[:, None] * inv], axis=-1)
    cos, sin = jnp.cos(ang), jnp.sin(ang)
    cos = jnp.concatenate([jnp.ones((n_ctx, head_dim // 2), F32), cos], axis=0)
    sin = jnp.concatenate([jnp.zeros((n_ctx, head_dim // 2), F32), sin], axis=0)
    reps = width // head_dim
    return (jnp.tile(jnp.concatenate([cos, cos], axis=-1), (1, reps)),
            jnp.tile(jnp.concatenate([-sin, sin], axis=-1), (1, reps)))


def kernel(x, c, ctx, c_ctx, w_mod, b_mod, norm1_g, norm2_g, w_in, conv_w_dw, conv_b_dw, conv_ln_g,
           conv_ln_b, conv_w_pw, conv_b_pw, gla_w_a_f, gla_b_a_f, gla_w_a_b, gla_b_a_b, gla_norm_g,
           att_q_norm_g, att_k_norm_g, ret_norm_g, w_out, w_up, w_down, final_norm_g):
    bsz, t_lat, d = x.shape
    n_ctx = ctx.shape[1]
    depth = w_mod.shape[0]
    assert bsz + 1 <= 8 and n_ctx % ROW_TILE == 0 and t_lat % ROW_TILE == 0
    n_ctx_tiles = n_ctx // ROW_TILE
    dk_all = N_HEADS * SCAN_DK

    cc = jnp.concatenate([c, c_ctx[None, :], jnp.zeros((8 - bsz - 1, d), F32)], axis=0)
    mods = _modulation(cc, w_mod, b_mod)

    z_off = W_CONV + W_GLA - LANES
    w_in_pad = jnp.concatenate(
        [w_in[:, :, :z_off + GLA_RANK], jnp.zeros((depth, d, LANES - GLA_RANK), F32),
         w_in[:, :, z_off + GLA_RANK:]], axis=-1).astype(BF16)
    pad_rank = lambda w: jnp.concatenate(
        [w, jnp.zeros((depth, LANES - GLA_RANK, dk_all), F32)], axis=1).astype(BF16)
    w_a_f, w_a_b = pad_rank(gla_w_a_f), pad_rank(gla_w_a_b)
    w_pw, w_out_b, w_up_b, w_down_b = (w.astype(BF16) for w in (conv_w_pw, w_out, w_up, w_down))

    cos_att, sin_att = _rope_table(t_lat, n_ctx, HEAD_DV, N_HEADS * HEAD_DV)
    cos_ret, sin_ret = _rope_table(t_lat, n_ctx, SCAN_DK, dk_all)
    log_gamma = jnp.log1p(-jnp.exp2(-5.0 - jnp.arange(N_HEADS, dtype=F32)))
    log_gamma = jnp.repeat(log_gamma, SCAN_DK)[None, :]
    gid = jnp.arange(GROUP_W) // HEAD_DV
    ones_blockdiag = (gid[:, None] == gid[None, :]).astype(BF16)
    key_mask = (gid[:, None] == (jnp.arange(dk_all) // SCAN_DK)[None, :]).astype(BF16)
    row = lambda v: v.reshape(1, -1)

    x_all = jnp.concatenate([ctx, x], axis=1)
    for l in range(depth):
        with_ctx = l < depth - 1
        final = l == depth - 1
        mod = mods[l].reshape(8, 1, N_MOD * d)
        conv_in, gla_in, att_in, ret_in = _in_proj(x_all, mod, row(norm1_g[l]), w_in_pad[l], n_ctx_tiles)
        y_conv = _conv_mixer(conv_in, conv_w_dw[l], row(conv_b_dw[l]), row(conv_ln_g[l]),
                             row(conv_ln_b[l]), w_pw[l], row(conv_b_pw[l]), n_ctx_tiles, with_ctx)
        gla_of, gla_ob, ret_of, ret_ob = _scan_mixers(
            gla_in, ret_in, n_ctx_tiles, (w_a_f[l], row(gla_b_a_f[l]), w_a_b[l], row(gla_b_a_b[l])),
            log_gamma, cos_ret, sin_ret, key_mask, ones_blockdiag)
        q, k, vt = _att_prep(att_in, cos_att, sin_att, row(jnp.tile(att_q_norm_g[l], N_HEADS)),
                             row(jnp.tile(att_k_norm_g[l], KV_HEADS)), ones_blockdiag)
        y_att = _attention(q, k, vt, n_ctx_tiles, with_ctx)
        x_all = _post(final, y_conv, gla_of, gla_ob, gla_in, y_att, ret_of, ret_ob, ret_in,
                      row(gla_norm_g[l]), row(ret_norm_g[l]), ones_blockdiag, x_all, mod,
                      row(norm2_g[l]), w_out_b[l], w_up_b[l], w_down_b[l], row(final_norm_g),
                      n_ctx_tiles, with_ctx)
    return x_all
```

```python
import functools

import numpy as np
import jax
import jax.numpy as jnp
from jax import lax
from jax.experimental import pallas as pl
from jax.experimental.pallas import tpu as pltpu

F32 = jnp.float32
BF16 = jnp.bfloat16

EPS = 1e-6
GRID_W = 64
ROPE_THETA = 10000.0
N_MOD = 6
CONV_KSIZE = 31
CONV_PAD = (CONV_KSIZE - 1) // 2
HALO = 16
GLA_TAU = 16.0
GLA_RANK = 16
CHUNK = 64
N_HEADS = 4
KV_HEADS = 2
HEAD_DV = 64
SCAN_DK = 32
GROUP_W = 256
ROW_TILE = 256
LANES = 128
SUBLANES = 8
BF16_SUBLANES = 16
VT_ROWS = HEAD_DV + BF16_SUBLANES
LOG2_E = 1.4426950408889634
V7X_VMEM_BYTES = 64 * 1024 * 1024
VMEM_LIMIT = V7X_VMEM_BYTES * 3 // 4

W_CONV = 2 * GROUP_W
W_GLA = 2 * N_HEADS * SCAN_DK + 2 * GROUP_W + LANES
W_ATT = GROUP_W + 2 * KV_HEADS * HEAD_DV
W_RET = 2 * N_HEADS * SCAN_DK + 2 * GROUP_W
W_IN_A = W_CONV + W_GLA - LANES
W_IN_B = W_ATT + W_RET


def _params(semantics):
    return pltpu.CompilerParams(dimension_semantics=semantics, vmem_limit_bytes=VMEM_LIMIT)


def _layer_spec(arr, layer):
    rest = arr.shape[1:]
    return pl.BlockSpec((None,) + rest, lambda *_: (layer,) + (0,) * len(rest))


def _const_spec(arr):
    return pl.BlockSpec(arr.shape, lambda *_: (0,) * arr.ndim)


def _silu(x):
    return x * jax.nn.sigmoid(x)


def _rms(x):
    return x * lax.rsqrt(jnp.mean(x * x, axis=-1, keepdims=True) + EPS)


def _group_sumsq(t, ones_blockdiag):
    t2 = t * t
    hi = t2.astype(BF16)
    lo = (t2 - hi.astype(F32)).astype(BF16)
    return (jnp.dot(hi, ones_blockdiag, preferred_element_type=F32)
            + jnp.dot(lo, ones_blockdiag, preferred_element_type=F32))


def _load_rows(split, is_ctx, lat_ref, ctx_ref, buf):
    if not split:
        return lat_ref[0]

    @pl.when(is_ctx)
    def _():
        buf[...] = ctx_ref[0]

    @pl.when(jnp.logical_not(is_ctx))
    def _():
        buf[...] = lat_ref[0]

    return buf[...]


def _row_specs(split, x_lat, x_ctx, tm, n_ctx_tiles, tile0):
    d = x_lat.shape[2]
    if not split:
        return [pl.BlockSpec((1, tm, d), lambda b, j: (b, j + tile0, 0))], [x_lat]
    return ([pl.BlockSpec((1, tm, d), lambda b, j: (b, jnp.maximum(j - n_ctx_tiles, 0), 0)),
             pl.BlockSpec((1, tm, d), lambda b, j: (b, jnp.minimum(j, n_ctx_tiles - 1), 0))],
            [x_lat, x_ctx])


def _mod_body(cc_ref, w_ref, b_ref, o_ref):
    s = _silu(cc_ref[...])
    o_ref[0] = jnp.dot(s.astype(BF16), w_ref[0].astype(BF16),
                       preferred_element_type=F32) + b_ref[0]


def _modulation(cc, w_mod, b_mod):
    depth, d, n = w_mod.shape
    tn = 1024
    return pl.pallas_call(
        _mod_body,
        grid=(depth, n // tn),
        in_specs=[pl.BlockSpec((8, d), lambda l, i: (0, 0)),
                  pl.BlockSpec((1, d, tn), lambda l, i: (l, 0, i)),
                  pl.BlockSpec((1, 1, tn), lambda l, i: (l, 0, i))],
        out_specs=pl.BlockSpec((1, 8, tn), lambda l, i: (l, 0, i)),
        out_shape=jax.ShapeDtypeStruct((depth, 8, n), F32),
        compiler_params=_params(("parallel", "parallel")),
        name="modulation",
    )(cc, w_mod, b_mod.reshape(depth, 1, n))


def _inproj_body(split, n_ctx_tiles, *refs):
    n_src = 2 if split else 1
    lat_ref = refs[0]
    ctx_ref = refs[1] if split else None
    (mod_ref, g_ref, wa_ref, wz_ref, wb_ref, cos_ref, sin_ref, qg_ref, kg_ref, ones_ref,
     conv_ref, gla_ref, ret_ref, q_out, k_out, vt_out) = refs[n_src:n_src + 16]
    xbuf = refs[n_src + 16] if split else None
    tm, d = lat_ref.shape[1], lat_ref.shape[2]

    x = _load_rows(split, pl.program_id(1) < n_ctx_tiles, lat_ref, ctx_ref, xbuf)
    shift = mod_ref[:, 0:d]
    scale = mod_ref[:, d:2 * d]
    hb = ((_rms(x) * g_ref[...]) * (1.0 + scale) + shift).astype(BF16)

    def proj(w_ref, lo, width):
        return jnp.dot(hb, w_ref[:, lo:lo + width], preferred_element_type=F32)

    wq = N_HEADS * HEAD_DV
    wk = KV_HEADS * HEAD_DV
    att = proj(wb_ref, 0, W_ATT)
    q = att[:, :wq]
    k = att[:, wq:wq + wk]
    v = att[:, wq + wk:]
    ss_q = _group_sumsq(q, ones_ref[...])
    ss_k = _group_sumsq(k, ones_ref[:wk, :wk])
    conv_ref[0] = proj(wa_ref, 0, W_CONV)
    gla_ref[0, :, 0:W_GLA - LANES] = proj(wa_ref, W_CONV, W_GLA - LANES)
    gla_ref[0, :, W_GLA - LANES:] = proj(wz_ref, 0, LANES)
    ret_ref[0] = proj(wb_ref, W_ATT, W_RET)

    def norm_rope(t, ss, gain, w):
        t = t * lax.rsqrt(ss * (1.0 / HEAD_DV) + EPS) * gain
        lane = lax.broadcasted_iota(jnp.int32, t.shape, 1)
        first_half = (lane % HEAD_DV) < HEAD_DV // 2
        rot = jnp.where(first_half, pltpu.roll(t, w - HEAD_DV // 2, 1), pltpu.roll(t, HEAD_DV // 2, 1))
        return t * cos_ref[:, :w] + rot * sin_ref[:, :w]

    qn = norm_rope(q, ss_q, qg_ref[...], wq) * (HEAD_DV ** -0.5 * LOG2_E)
    kn = norm_rope(k, ss_k, kg_ref[...], wk)
    low = lax.broadcasted_iota(jnp.int32, (tm, LANES), 1) < HEAD_DV
    pair0 = qn[:, :LANES]
    pair1 = qn[:, LANES:]
    q_out[0, 0] = jnp.where(low, pair0, 0.0).astype(BF16)
    q_out[0, 1] = jnp.where(low, pltpu.roll(pair0, HEAD_DV, 1), 0.0).astype(BF16)
    q_out[0, 2] = jnp.where(low, 0.0, pltpu.roll(pair1, HEAD_DV, 1)).astype(BF16)
    q_out[0, 3] = jnp.where(low, 0.0, pair1).astype(BF16)
    k_out[0] = kn.astype(BF16)
    vt = v.T.astype(BF16)
    ones = jnp.ones((VT_ROWS - HEAD_DV, tm), BF16)
    for h in range(KV_HEADS):
        vt_out[0, h, 0, 0:HEAD_DV, :] = vt[h * HEAD_DV:(h + 1) * HEAD_DV, :]
        vt_out[0, h, 0, HEAD_DV:, :] = ones


def _in_proj(layer, x_lat, x_ctx, mods, norm1_g, w_a, w_z, w_b, cos, sin, q_gain, k_gain,
             ones_blockdiag, n_ctx_tiles):
    split = x_ctx is not None
    bsz, _, d = x_lat.shape
    tm = ROW_TILE
    t_all = x_lat.shape[1] + (x_ctx.shape[1] if split else 0)
    n_tiles = t_all // tm
    ctx_row = bsz
    wq = N_HEADS * HEAD_DV
    row_specs, row_args = _row_specs(split, x_lat, x_ctx, tm, n_ctx_tiles, 0)
    tile = lambda b, j: (b, j, 0)
    return pl.pallas_call(
        functools.partial(_inproj_body, split, n_ctx_tiles),
        grid=(bsz, n_tiles),
        in_specs=row_specs + [
            pl.BlockSpec((None, None, 1, N_MOD * d),
                         lambda b, j: (layer, jnp.where(j < n_ctx_tiles, ctx_row, b), 0, 0)),
            _layer_spec(norm1_g, layer), _layer_spec(w_a, layer), _layer_spec(w_z, layer),
            _layer_spec(w_b, layer),
            pl.BlockSpec((tm, wq), lambda b, j: (j, 0)), pl.BlockSpec((tm, wq), lambda b, j: (j, 0)),
            _layer_spec(q_gain, layer), _layer_spec(k_gain, layer), _const_spec(ones_blockdiag)],
        out_specs=[pl.BlockSpec((1, tm, W_CONV), tile), pl.BlockSpec((1, tm, W_GLA), tile),
                   pl.BlockSpec((1, tm, W_RET), tile),
                   pl.BlockSpec((1, N_HEADS, tm, LANES), lambda b, j: (b, 0, j, 0)),
                   pl.BlockSpec((1, tm, LANES), tile),
                   pl.BlockSpec((1, KV_HEADS, 1, VT_ROWS, tm), lambda b, j: (b, 0, j, 0, 0))],
        out_shape=[jax.ShapeDtypeStruct((bsz, t_all, W_CONV), F32),
                   jax.ShapeDtypeStruct((bsz, t_all, W_GLA), F32),
                   jax.ShapeDtypeStruct((bsz, t_all, W_RET), F32),
                   jax.ShapeDtypeStruct((bsz, N_HEADS, t_all, LANES), BF16),
                   jax.ShapeDtypeStruct((bsz, t_all, LANES), BF16),
                   jax.ShapeDtypeStruct((bsz, KV_HEADS, n_tiles, VT_ROWS, tm), BF16)],
        scratch_shapes=[pltpu.VMEM((tm, d), F32)] if split else [],
        compiler_params=_params(("parallel", "parallel")),
        name="in_proj",
    )(*row_args, mods, norm1_g, w_a, w_z, w_b, cos, sin, q_gain, k_gain, ones_blockdiag)


def _conv_body(n_ctx_tiles, n_tiles, tile0, main_ref, prev_ref, next_ref, wdw_ref, bdw_ref,
               lng_ref, lnb_ref, wpw_ref, bpw_ref, o_ref, ubuf, shifted, sbuf):
    tt = main_ref.shape[1]
    c = o_ref.shape[2]
    j = pl.program_id(1) + tile0

    def glu(blk):
        return blk[:, :c] * jax.nn.sigmoid(blk[:, c:])

    has_prev = jnp.logical_and(j != 0, j != n_ctx_tiles)
    has_next = jnp.logical_and(j != n_ctx_tiles - 1, j != n_tiles - 1)

    @pl.when(has_prev)
    def _():
        ubuf[0:HALO, :] = glu(prev_ref[0])

    @pl.when(jnp.logical_not(has_prev))
    def _():
        ubuf[0:HALO, :] = jnp.zeros((HALO, c), F32)

    ubuf[HALO:HALO + tt, :] = glu(main_ref[0])

    @pl.when(has_next)
    def _():
        ubuf[HALO + tt:, :] = glu(next_ref[0])

    @pl.when(jnp.logical_not(has_next))
    def _():
        ubuf[HALO + tt:, :] = jnp.zeros((HALO, c), F32)

    span = shifted.shape[1]
    for s in range(1, SUBLANES):
        shifted[s - 1] = ubuf[s:s + span, :]

    rows = 64
    for r in range(0, tt, rows):
        acc = jnp.zeros((rows, c), F32)
        for k in range(CONV_KSIZE):
            off = k + HALO - CONV_PAD
            lo = r + off - off % SUBLANES
            if off % SUBLANES == 0:
                tap = ubuf[lo:lo + rows, :]
            else:
                tap = shifted[off % SUBLANES - 1, lo:lo + rows, :]
            acc = acc + tap * wdw_ref[k:k + 1, :]
        y = acc + bdw_ref[...]
        yc = y - jnp.mean(y, axis=-1, keepdims=True)
        yn = yc * lax.rsqrt(jnp.mean(yc * yc, axis=-1, keepdims=True) + EPS)
        yn = yn * lng_ref[...] + lnb_ref[...]
        sbuf[r:r + rows, :] = _silu(yn).astype(BF16)
    o_ref[0] = jnp.dot(sbuf[...], wpw_ref[...], preferred_element_type=F32) + bpw_ref[...]


def _conv_mixer(layer, conv_in, w_dw, b_dw, ln_g, ln_b, w_pw, b_pw, n_ctx_tiles, with_ctx):
    bsz, t_all, _ = conv_in.shape
    tt = ROW_TILE
    c = GROUP_W
    n_tiles = t_all // tt
    tile0 = 0 if with_ctx else n_ctx_tiles
    per = tt // HALO
    n_halo = t_all // HALO
    return pl.pallas_call(
        functools.partial(_conv_body, n_ctx_tiles, n_tiles, tile0),
        grid=(bsz, n_tiles - tile0),
        in_specs=[pl.BlockSpec((1, tt, 2 * c), lambda b, j: (b, j + tile0, 0)),
                  pl.BlockSpec((1, HALO, 2 * c),
                               lambda b, j: (b, jnp.maximum((j + tile0) * per - 1, 0), 0)),
                  pl.BlockSpec((1, HALO, 2 * c),
                               lambda b, j: (b, jnp.minimum((j + tile0 + 1) * per, n_halo - 1), 0)),
                  _layer_spec(w_dw, layer), _layer_spec(b_dw, layer), _layer_spec(ln_g, layer),
                  _layer_spec(ln_b, layer), _layer_spec(w_pw, layer), _layer_spec(b_pw, layer)],
        out_specs=pl.BlockSpec((1, tt, c), lambda b, j: (b, j, 0)),
        out_shape=jax.ShapeDtypeStruct((bsz, t_all - tile0 * tt, c), F32),
        scratch_shapes=[pltpu.VMEM((tt + 2 * HALO, c), F32),
                        pltpu.VMEM((SUBLANES - 1, tt + 2 * HALO - SUBLANES, c), F32),
                        pltpu.VMEM((tt, c), BF16)],
        compiler_params=_params(("parallel", "parallel")),
        name="conv_mixer",
    )(conv_in, conv_in, conv_in, w_dw, b_dw, ln_g, ln_b, w_pw, b_pw)


def _chunk_cumsum(la, tri_ref):
    w = la.shape[1]
    hi = la.astype(BF16)
    lo = (la - hi.astype(F32)).astype(BF16)
    both = jnp.dot(tri_ref[...], jnp.concatenate([hi, lo], axis=1), preferred_element_type=F32)
    return both[:, :w] + both[:, w:]


def _decay_factors(b, reverse):
    b_tot = b[0:1, :] if reverse else b[CHUNK - 1:CHUNK, :]
    return jnp.exp(b), jnp.exp(-b), jnp.exp(b_tot - b), jnp.exp(b_tot)


def _scan_chunk_head(q, k, v, factors, st, key_mask, value_mask, head_mask):
    nt_dims = (((1,), (1,)), ((), ()))
    e_q, e_inv, e_end, dec = factors
    q_dec = (q * e_q).astype(BF16)
    k_inv = (k * e_inv).astype(BF16)
    k_end = (k * e_end).astype(BF16)
    vb = v.astype(BF16)
    k_blk = jnp.concatenate([k_inv] * N_HEADS, axis=0) * key_mask
    scores = lax.dot_general(q_dec, k_blk, nt_dims, preferred_element_type=F32)
    o_inter = lax.dot_general(q_dec, st.astype(BF16), nt_dims, preferred_element_type=F32)
    kv_t = lax.dot_general(vb, k_end, (((0,), (0,)), ((), ())), preferred_element_type=F32)
    st = st * dec + jnp.where(head_mask, kv_t, 0.0)
    v_blk = jnp.concatenate([vb] * N_HEADS, axis=0) * value_mask
    return (scores, v_blk, o_inter), st


def _scan_chunk_tail(pending, tri):
    scores, v_blk, o_inter = pending
    scores = jnp.where(tri, scores, 0.0).astype(BF16)
    return jnp.dot(scores, v_blk, preferred_element_type=F32) + o_inter


def _scan_body(gf, gb, rf, rb, waf, baf, wab, bab, lg, cosf, sinf, cosb, sinb,
               key_mask_ref, value_mask_ref, cum_f_ref, cum_b_ref,
               gof, gob, rof, rob, st_gf, st_gb, st_rf, st_rb):
    states = (st_gf, st_gb, st_rf, st_rb)

    @pl.when(pl.program_id(1) == 0)
    def _():
        for st in states:
            st[...] = jnp.zeros(st.shape, F32)

    tb = gf.shape[1]
    dk_all = N_HEADS * SCAN_DK
    dv_all = GROUP_W
    scale = SCAN_DK ** -0.5
    key_mask = key_mask_ref[...]
    value_mask = value_mask_ref[...]
    r = lax.broadcasted_iota(jnp.int32, (dv_all, dk_all), 0)
    c = lax.broadcasted_iota(jnp.int32, (dv_all, dk_all), 1)
    head_mask = (r // HEAD_DV) == (c // SCAN_DK)
    r = lax.broadcasted_iota(jnp.int32, (CHUNK, dv_all), 0)
    c = lax.broadcasted_iota(jnp.int32, (CHUNK, dv_all), 1) % CHUNK
    tri_f = c <= r
    tri_b = c >= r

    def qkv(ref):
        return (ref[0, :, 0:dk_all], ref[0, :, dk_all:2 * dk_all],
                ref[0, :, 2 * dk_all:2 * dk_all + dv_all])

    def gla_cum_decay(ref, w_ref, bias_ref, cum_ref):
        z = ref[0, :, 2 * dk_all + 2 * dv_all:]
        pre = jnp.dot(z.astype(BF16), w_ref[...], preferred_element_type=F32) + bias_ref[...]
        log_sig = jnp.minimum(pre, 0.0) - jnp.log1p(jnp.exp(-jnp.abs(pre)))
        return _chunk_cumsum(log_sig / GLA_TAU, cum_ref)

    q_gf, k_gf, v_gf = qkv(gf)
    q_gb, k_gb, v_gb = qkv(gb)
    b_gf = gla_cum_decay(gf, waf, baf, cum_f_ref)
    b_gb = gla_cum_decay(gb, wab, bab, cum_b_ref)

    lane = lax.broadcasted_iota(jnp.int32, (tb, dk_all), 1)
    first_half = (lane % SCAN_DK) < SCAN_DK // 2

    def rope(t, cos_ref, sin_ref):
        rot = jnp.where(first_half, pltpu.roll(t, dk_all - SCAN_DK // 2, 1),
                        pltpu.roll(t, SCAN_DK // 2, 1))
        return t * cos_ref[...] + rot * sin_ref[...]

    q_rf, k_rf, v_rf = qkv(rf)
    q_rb, k_rb, v_rb = qkv(rb)
    q_rf, k_rf = rope(q_rf, cosf, sinf), rope(k_rf * scale, cosf, sinf)
    q_rb, k_rb = rope(q_rb, cosb, sinb), rope(k_rb * scale, cosb, sinb)
    pos = lax.broadcasted_iota(jnp.int32, (CHUNK, dk_all), 0).astype(F32)
    ret_fac_f = _decay_factors((pos + 1.0) * lg[...], False)
    ret_fac_b = _decay_factors((CHUNK - pos) * lg[...], True)

    chains = (
        (q_gf * scale, k_gf, v_gf, lambda lo: _decay_factors(b_gf[lo:lo + CHUNK], False), False, gof, tri_f),
        (q_gb * scale, k_gb, v_gb, lambda lo: _decay_factors(b_gb[lo:lo + CHUNK], True), True, gob, tri_b),
        (q_rf, k_rf, v_rf, lambda lo: ret_fac_f, False, rof, tri_f),
        (q_rb, k_rb, v_rb, lambda lo: ret_fac_b, True, rob, tri_b),
    )
    st_vals = [st[...] for st in states]
    n_chunks = tb // CHUNK

    def heads(i):
        pending = []
        for n, (q, k, v, factors, reverse, _, _) in enumerate(chains):
            lo = (n_chunks - 1 - i if reverse else i) * CHUNK
            part, st_vals[n] = _scan_chunk_head(
                q[lo:lo + CHUNK], k[lo:lo + CHUNK], v[lo:lo + CHUNK], factors(lo), st_vals[n],
                key_mask, value_mask, head_mask)
            pending.append((lo, part))
        return pending

    pending = heads(0)
    for i in range(n_chunks):
        following = heads(i + 1) if i + 1 < n_chunks else None
        for (lo, part), (_, _, _, _, _, out_ref, tri) in zip(pending, chains):
            out_ref[0, lo:lo + CHUNK, :] = _scan_chunk_tail(part, tri)
        pending = following
    for st, val in zip(states, st_vals):
        st[...] = val


def _scan_mixers(layer, gla_in, ret_in, n_ctx_tiles, w_a_f, b_a_f, w_a_b, b_a_b, log_gamma, cos, sin,
                 key_mask, value_mask, cum_f, cum_b):
    bsz, t_all, _ = gla_in.shape
    tb = ROW_TILE
    n_tiles = t_all // tb
    dk_all = N_HEADS * SCAN_DK
    dv_all = GROUP_W

    def bwd(j):
        return jnp.where(j < n_ctx_tiles, n_ctx_tiles - 1 - j, n_tiles - 1 - (j - n_ctx_tiles))

    fwd_blk = lambda b, j: (b, j, 0)
    bwd_blk = lambda b, j: (b, bwd(j), 0)
    in_specs = [pl.BlockSpec((1, tb, W_GLA), fwd_blk), pl.BlockSpec((1, tb, W_GLA), bwd_blk),
                pl.BlockSpec((1, tb, W_RET), fwd_blk), pl.BlockSpec((1, tb, W_RET), bwd_blk),
                _layer_spec(w_a_f, layer), _layer_spec(b_a_f, layer),
                _layer_spec(w_a_b, layer), _layer_spec(b_a_b, layer),
                _const_spec(log_gamma),
                pl.BlockSpec((tb, dk_all), lambda b, j: (j, 0)),
                pl.BlockSpec((tb, dk_all), lambda b, j: (j, 0)),
                pl.BlockSpec((tb, dk_all), lambda b, j: (bwd(j), 0)),
                pl.BlockSpec((tb, dk_all), lambda b, j: (bwd(j), 0)),
                _const_spec(key_mask), _const_spec(value_mask), _const_spec(cum_f), _const_spec(cum_b)]
    return pl.pallas_call(
        _scan_body,
        grid=(bsz, n_tiles),
        in_specs=in_specs,
        out_specs=[pl.BlockSpec((1, tb, dv_all), fwd_blk), pl.BlockSpec((1, tb, dv_all), bwd_blk)] * 2,
        out_shape=[jax.ShapeDtypeStruct((bsz, t_all, dv_all), F32)] * 4,
        scratch_shapes=[pltpu.VMEM((dv_all, dk_all), F32)] * 4,
        compiler_params=_params(("parallel", "arbitrary")),
        name="scans",
    )(gla_in, gla_in, ret_in, ret_in, w_a_f, b_a_f, w_a_b, b_a_b, log_gamma, cos, sin, cos, sin,
      key_mask, value_mask, cum_f, cum_b)


def _att_body(n_ctx_tiles, n_tiles, tile0, q_ref, k_ref, vt_ref, o_ref):
    group = q_ref.shape[1]
    tq = q_ref.shape[2]
    tk = vt_ref.shape[4]
    nq = group * tq

    def attend(n_blocks):
        q2 = q_ref[0].reshape(nq, LANES)

        def scores(i):
            kb = k_ref[0, i * tk:(i + 1) * tk, :]
            return lax.dot_general(kb, q2, (((1,), (1,)), ((), ())), preferred_element_type=F32)

        m = acc = None
        s_next = scores(0)
        for i in range(n_blocks):
            s = s_next
            if i + 1 < n_blocks:
                s_next = scores(i + 1)
            blk_max = jnp.max(s, axis=0, keepdims=True)
            m_new = blk_max if i == 0 else jnp.maximum(m, blk_max)
            p = jnp.exp2(s - m_new).astype(BF16)
            pv = jnp.dot(vt_ref[0, 0, i], p, preferred_element_type=F32)
            acc = pv if i == 0 else jnp.exp2(m - m_new) * acc + pv
            m = m_new
        o_t = acc[:HEAD_DV] / acc[HEAD_DV:HEAD_DV + 1]
        stacked = jnp.concatenate([o_t[:, g * tq:(g + 1) * tq] for g in range(group)], axis=0)
        o_ref[0] = stacked.T

    qi = pl.program_id(2) + tile0
    if tile0 < n_ctx_tiles:
        pl.when(qi < n_ctx_tiles)(lambda: attend(n_ctx_tiles))
    pl.when(qi >= n_ctx_tiles)(lambda: attend(n_tiles))


def _attention(q, k, vt, n_ctx_tiles, with_ctx):
    bsz, _, t_all, _ = q.shape
    tq = ROW_TILE
    n_tiles = t_all // tq
    tile0 = 0 if with_ctx else n_ctx_tiles
    group = N_HEADS // KV_HEADS
    return pl.pallas_call(
        functools.partial(_att_body, n_ctx_tiles, n_tiles, tile0),
        grid=(bsz, KV_HEADS, n_tiles - tile0),
        in_specs=[pl.BlockSpec((1, group, tq, LANES), lambda b, h, i: (b, h, i + tile0, 0)),
                  pl.BlockSpec((1, t_all, LANES), lambda b, h, i: (b, 0, 0)),
                  pl.BlockSpec((1, 1, n_tiles, VT_ROWS, tq), lambda b, h, i: (b, h, 0, 0, 0))],
        out_specs=pl.BlockSpec((1, tq, group * HEAD_DV), lambda b, h, i: (b, i, h)),
        out_shape=jax.ShapeDtypeStruct((bsz, t_all - tile0 * tq, GROUP_W), F32),
        compiler_params=_params(("parallel", "parallel", "arbitrary")),
        name="attention",
    )(q, k, vt)


def _post_body(final, split, n_ctx_tiles, tile0, *refs):
    (yconv, gla_f, gla_b, gla_gate, yatt, ret_f, ret_b, ret_gate, gla_gain, ret_gain, ones_ref) = refs[:11]
    n_src = 2 if split else 1
    lat_ref = refs[11]
    ctx_ref = refs[12] if split else None
    mod_ref, n2g_ref, wout_ref, wup_ref, wdown_ref, fng_ref, o_ref = refs[11 + n_src:18 + n_src]
    xbuf = refs[18 + n_src] if split else None
    tm, d = lat_ref.shape[1], lat_ref.shape[2]
    d_ff = wup_ref.shape[1]

    def finish(of_ref, ob_ref, gate_ref, gain_ref):
        o = of_ref[0] + ob_ref[0]
        ss = _group_sumsq(o, ones_ref[...])
        y = o * lax.rsqrt(ss * (1.0 / HEAD_DV) + EPS) * gain_ref[...]
        return (y * _silu(gate_ref[0])).astype(BF16)

    mixed = jnp.concatenate([yconv[0].astype(BF16), finish(gla_f, gla_b, gla_gate, gla_gain),
                             yatt[0].astype(BF16), finish(ret_f, ret_b, ret_gate, ret_gain)], axis=1)
    o = jnp.dot(mixed, wout_ref[...], preferred_element_type=F32)
    gate1 = mod_ref[:, 2 * d:3 * d]
    shift2 = mod_ref[:, 3 * d:4 * d]
    scale2 = mod_ref[:, 4 * d:5 * d]
    gate2 = mod_ref[:, 5 * d:6 * d]
    x = _load_rows(split, pl.program_id(1) + tile0 < n_ctx_tiles, lat_ref, ctx_ref, xbuf)
    x_mid = x + gate1 * o
    h2 = ((_rms(x_mid) * n2g_ref[...]) * (1.0 + scale2) + shift2).astype(BF16)
    acc = jnp.zeros((tm, d), F32)
    tf = 1024
    for f in range(0, d_ff, tf):
        u = jnp.dot(h2, wup_ref[:, f:f + tf], preferred_element_type=F32)
        a = jnp.square(jnp.maximum(u, 0.0)).astype(BF16)
        acc = acc + jnp.dot(a, wdown_ref[f:f + tf, :], preferred_element_type=F32)
    x_out = x_mid + gate2 * acc
    if final:
        x_out = _rms(x_out) * fng_ref[...]
    o_ref[0] = x_out


def _post(layer, final, yconv, gla_of, gla_ob, gla_in, yatt, ret_of, ret_ob, ret_in, gla_gain, ret_gain,
          ones_blockdiag, x_lat, x_ctx, mods, norm2_g, w_out, w_up, w_down, final_g, n_ctx_tiles,
          with_ctx):
    split = x_ctx is not None
    bsz, _, d = x_lat.shape
    tm = ROW_TILE
    t_all = gla_of.shape[1]
    ctx_row = bsz
    tile0 = 0 if with_ctx else n_ctx_tiles
    n_out = t_all // tm - tile0
    c = GROUP_W
    gate_blk = (2 * N_HEADS * SCAN_DK + c) // c
    full = lambda b, j: (b, j + tile0, 0)
    own = lambda b, j: (b, j, 0)
    assert not split or tile0 == 0
    row_specs, row_args = _row_specs(split, x_lat, x_ctx, tm, n_ctx_tiles, tile0)

    def resident(arr):
        rest = arr.shape[1:]
        return pl.BlockSpec((None,) + rest, lambda *_: (layer,) + (0,) * len(rest),
                            pipeline_mode=pl.Buffered(1))

    return pl.pallas_call(
        functools.partial(_post_body, final, split, n_ctx_tiles, tile0),
        grid=(bsz, n_out),
        in_specs=[pl.BlockSpec((1, tm, c), own),
                  pl.BlockSpec((1, tm, c), full), pl.BlockSpec((1, tm, c), full),
                  pl.BlockSpec((1, tm, c), lambda b, j: (b, j + tile0, gate_blk)),
                  pl.BlockSpec((1, tm, c), own),
                  pl.BlockSpec((1, tm, c), full), pl.BlockSpec((1, tm, c), full),
                  pl.BlockSpec((1, tm, c), lambda b, j: (b, j + tile0, gate_blk)),
                  _layer_spec(gla_gain, layer), _layer_spec(ret_gain, layer),
                  _const_spec(ones_blockdiag)] + row_specs + [
                  pl.BlockSpec((None, None, 1, N_MOD * d),
                               lambda b, j: (layer, jnp.where(j + tile0 < n_ctx_tiles, ctx_row, b), 0, 0)),
                  _layer_spec(norm2_g, layer),
                  resident(w_out), resident(w_up), resident(w_down),
                  _const_spec(final_g)],
        out_specs=pl.BlockSpec((1, tm, d), own),
        out_shape=jax.ShapeDtypeStruct((bsz, n_out * tm, d), F32),
        scratch_shapes=[pltpu.VMEM((tm, d), F32)] if split else [],
        compiler_params=_params(("parallel", "parallel")),
        name="post",
    )(yconv, gla_of, gla_ob, gla_in, yatt, ret_of, ret_ob, ret_in, gla_gain, ret_gain,
      ones_blockdiag, *row_args, mods, norm2_g, w_out, w_up, w_down, final_g)


def _rope_table(t_lat, n_ctx, head_dim, width):
    n_ax = head_dim // 4
    inv = np.float32(ROPE_THETA) ** (-np.arange(n_ax, dtype=np.float32) / np.float32(n_ax))
    pos = np.arange(t_lat)
    ang = np.concatenate([(pos // GRID_W).astype(np.float32)[:, None] * inv,
                          (pos % GRID_W).astype(np.float32)[:, None] * inv], axis=-1).astype(np.float32)
    cos, sin = np.cos(ang), np.sin(ang)
    cos = np.concatenate([np.ones((n_ctx, head_dim // 2), np.float32), cos], axis=0)
    sin = np.concatenate([np.zeros((n_ctx, head_dim // 2), np.float32), sin], axis=0)
    reps = width // head_dim
    return (jnp.asarray(np.tile(np.concatenate([cos, cos], axis=-1), (1, reps)), F32),
            jnp.asarray(np.tile(np.concatenate([-sin, sin], axis=-1), (1, reps)), F32))


def kernel(x, c, ctx, c_ctx, w_mod, b_mod, norm1_g, norm2_g, w_in, conv_w_dw, conv_b_dw, conv_ln_g,
           conv_ln_b, conv_w_pw, conv_b_pw, gla_w_a_f, gla_b_a_f, gla_w_a_b, gla_b_a_b, gla_norm_g,
           att_q_norm_g, att_k_norm_g, ret_norm_g, w_out, w_up, w_down, final_norm_g):
    bsz, t_lat, d = x.shape
    n_ctx = ctx.shape[1]
    depth = w_mod.shape[0]
    assert bsz + 1 <= 8 and n_ctx % ROW_TILE == 0 and t_lat % ROW_TILE == 0
    n_ctx_tiles = n_ctx // ROW_TILE
    dk_all = N_HEADS * SCAN_DK

    cc = jnp.concatenate([c, c_ctx[None, :], jnp.zeros((8 - bsz - 1, d), F32)], axis=0)
    mods = _modulation(cc, w_mod, b_mod)
    mods = mods.reshape(depth, 8, 1, N_MOD * d)

    stack_rows = lambda v: v.reshape(depth, 1, -1)
    w_a = w_in[:, :, :W_IN_A].astype(BF16)
    w_z = jnp.pad(w_in[:, :, W_IN_A:W_IN_A + GLA_RANK], ((0, 0), (0, 0), (0, LANES - GLA_RANK))).astype(BF16)
    w_b = w_in[:, :, W_IN_A + GLA_RANK:].astype(BF16)
    pad_rank = lambda w: jnp.pad(w, ((0, 0), (0, LANES - GLA_RANK), (0, 0))).astype(BF16)
    w_a_f, w_a_b = pad_rank(gla_w_a_f), pad_rank(gla_w_a_b)
    w_pw, w_out_b, w_up_b, w_down_b = (w.astype(BF16) for w in (conv_w_pw, w_out, w_up, w_down))
    q_gain = stack_rows(jnp.tile(att_q_norm_g, (1, N_HEADS)))
    k_gain = stack_rows(jnp.tile(att_k_norm_g, (1, KV_HEADS)))

    cos_att, sin_att = _rope_table(t_lat, n_ctx, HEAD_DV, N_HEADS * HEAD_DV)
    cos_ret, sin_ret = _rope_table(t_lat, n_ctx, SCAN_DK, dk_all)
    gamma = 1.0 - np.exp2(-5.0 - np.arange(N_HEADS, dtype=np.float64))
    log_gamma = jnp.asarray(np.repeat(np.log(gamma), SCAN_DK)[None, :], F32)
    gid = np.arange(GROUP_W) // HEAD_DV
    ones_blockdiag = jnp.asarray(gid[:, None] == gid[None, :], BF16)
    key_mask = jnp.asarray(gid[:, None] == (np.arange(dk_all) // SCAN_DK)[None, :], BF16)
    pos = np.arange(ROW_TILE)
    same_chunk = (pos[:, None] // CHUNK) == (pos[None, :] // CHUNK)
    cum_f = jnp.asarray(same_chunk & (pos[None, :] <= pos[:, None]), BF16)
    cum_b = jnp.asarray(same_chunk & (pos[None, :] >= pos[:, None]), BF16)

    x_lat, x_ctx = x, ctx
    for l in range(depth):
        with_ctx = l < depth - 1
        final = l == depth - 1
        conv_in, gla_in, ret_in, q, k, vt = _in_proj(
            l, x_lat, x_ctx, mods, stack_rows(norm1_g), w_a, w_z, w_b, cos_att, sin_att, q_gain, k_gain,
            ones_blockdiag, n_ctx_tiles)
        y_conv = _conv_mixer(l, conv_in, conv_w_dw, stack_rows(conv_b_dw), stack_rows(conv_ln_g),
                             stack_rows(conv_ln_b), w_pw, stack_rows(conv_b_pw), n_ctx_tiles, with_ctx)
        gla_of, gla_ob, ret_of, ret_ob = _scan_mixers(
            l, gla_in, ret_in, n_ctx_tiles, w_a_f, stack_rows(gla_b_a_f), w_a_b, stack_rows(gla_b_a_b),
            log_gamma, cos_ret, sin_ret, key_mask, ones_blockdiag, cum_f, cum_b)
        y_att = _attention(q, k, vt, n_ctx_tiles, with_ctx)
        x_lat = _post(l, final, y_conv, gla_of, gla_ob, gla_in, y_att, ret_of, ret_ob, ret_in,
                      stack_rows(gla_norm_g), stack_rows(ret_norm_g), ones_blockdiag, x_lat, x_ctx, mods,
                      stack_rows(norm2_g), w_out_b, w_up_b, w_down_b, final_norm_g.reshape(1, -1),
                      n_ctx_tiles, with_ctx)
        x_ctx = None
    return x_lat
```

```python
import functools

import numpy as np
import jax
import jax.numpy as jnp
from jax import lax
from jax.experimental import pallas as pl
from jax.experimental.pallas import tpu as pltpu

F32 = jnp.float32
BF16 = jnp.bfloat16

EPS = 1e-6
GRID_W = 64
ROPE_THETA = 10000.0
N_MOD = 6
CONV_KSIZE = 31
CONV_PAD = (CONV_KSIZE - 1) // 2
HALO = 16
GLA_TAU = 16.0
GLA_RANK = 16
CHUNK = 64
N_HEADS = 4
KV_HEADS = 2
HEAD_DV = 64
SCAN_DK = 32
GROUP_W = 256
ROW_TILE = 256
LANES = 128
SUBLANES = 8
BF16_SUBLANES = 16
VT_ROWS = HEAD_DV + BF16_SUBLANES
LOG2_E = 1.4426950408889634
SCORE_LOOKAHEAD = 2
V7X_VMEM_BYTES = 64 * 1024 * 1024
VMEM_LIMIT = V7X_VMEM_BYTES * 3 // 4

W_CONV = 2 * GROUP_W
W_GLA = 2 * N_HEADS * SCAN_DK + 2 * GROUP_W + LANES
W_ATT = GROUP_W + 2 * KV_HEADS * HEAD_DV
W_RET = 2 * N_HEADS * SCAN_DK + 2 * GROUP_W
W_IN_A = W_CONV + W_GLA - LANES
W_IN_B = W_ATT + W_RET


def _params(semantics):
    return pltpu.CompilerParams(dimension_semantics=semantics, vmem_limit_bytes=VMEM_LIMIT)


def _layer_spec(arr, layer):
    rest = arr.shape[1:]
    return pl.BlockSpec((None,) + rest, lambda *_: (layer,) + (0,) * len(rest))


def _const_spec(arr):
    return pl.BlockSpec(arr.shape, lambda *_: (0,) * arr.ndim)


def _silu(x):
    return x * jax.nn.sigmoid(x)


def _rms(x):
    return x * lax.rsqrt(jnp.mean(x * x, axis=-1, keepdims=True) + EPS)


def _group_sumsq(t, ones_blockdiag):
    t2 = t * t
    hi = t2.astype(BF16)
    lo = (t2 - hi.astype(F32)).astype(BF16)
    return (jnp.dot(hi, ones_blockdiag, preferred_element_type=F32)
            + jnp.dot(lo, ones_blockdiag, preferred_element_type=F32))


def _load_rows(split, is_ctx, lat_ref, ctx_ref, buf):
    if not split:
        return lat_ref[0]

    @pl.when(is_ctx)
    def _():
        buf[...] = ctx_ref[0]

    @pl.when(jnp.logical_not(is_ctx))
    def _():
        buf[...] = lat_ref[0]

    return buf[...]


def _row_specs(split, x_lat, x_ctx, tm, n_ctx_tiles, tile0):
    d = x_lat.shape[2]
    if not split:
        return [pl.BlockSpec((1, tm, d), lambda b, j: (b, j + tile0, 0))], [x_lat]
    return ([pl.BlockSpec((1, tm, d), lambda b, j: (b, jnp.maximum(j - n_ctx_tiles, 0), 0)),
             pl.BlockSpec((1, tm, d), lambda b, j: (b, jnp.minimum(j, n_ctx_tiles - 1), 0))],
            [x_lat, x_ctx])


def _mod_body(cc_ref, w_ref, b_ref, o_ref):
    s = _silu(cc_ref[...])
    o_ref[0] = jnp.dot(s.astype(BF16), w_ref[0].astype(BF16),
                       preferred_element_type=F32) + b_ref[0]


def _modulation(cc, w_mod, b_mod):
    depth, d, n = w_mod.shape
    tn = 1024
    return pl.pallas_call(
        _mod_body,
        grid=(depth, n // tn),
        in_specs=[pl.BlockSpec((8, d), lambda l, i: (0, 0)),
                  pl.BlockSpec((1, d, tn), lambda l, i: (l, 0, i)),
                  pl.BlockSpec((1, 1, tn), lambda l, i: (l, 0, i))],
        out_specs=pl.BlockSpec((1, 8, tn), lambda l, i: (l, 0, i)),
        out_shape=jax.ShapeDtypeStruct((depth, 8, n), F32),
        compiler_params=_params(("parallel", "parallel")),
        name="modulation",
    )(cc, w_mod, b_mod.reshape(depth, 1, n))


def _inproj_body(split, n_ctx_tiles, *refs):
    n_src = 2 if split else 1
    lat_ref = refs[0]
    ctx_ref = refs[1] if split else None
    (mod_ref, g_ref, wa_ref, wz_ref, wb_ref, cos_ref, sin_ref, qg_ref, kg_ref, ones_ref,
     conv_ref, gla_ref, ret_ref, q_out, k_out, vt_out) = refs[n_src:n_src + 16]
    xbuf = refs[n_src + 16] if split else None
    tm, d = lat_ref.shape[1], lat_ref.shape[2]

    x = _load_rows(split, pl.program_id(1) < n_ctx_tiles, lat_ref, ctx_ref, xbuf)
    shift = mod_ref[:, 0:d]
    scale = mod_ref[:, d:2 * d]
    hb = ((_rms(x) * g_ref[...]) * (1.0 + scale) + shift).astype(BF16)

    def proj(w_ref, lo, width):
        return jnp.dot(hb, w_ref[:, lo:lo + width], preferred_element_type=F32)

    wq = N_HEADS * HEAD_DV
    wk = KV_HEADS * HEAD_DV
    att = proj(wb_ref, 0, W_ATT)
    q = att[:, :wq]
    k = att[:, wq:wq + wk]
    v = att[:, wq + wk:]
    ss_q = _group_sumsq(q, ones_ref[...])
    ss_k = _group_sumsq(k, ones_ref[:wk, :wk])
    conv_ref[0] = proj(wa_ref, 0, W_CONV)
    gla_ref[0, :, 0:W_GLA - LANES] = proj(wa_ref, W_CONV, W_GLA - LANES)
    gla_ref[0, :, W_GLA - LANES:] = proj(wz_ref, 0, LANES)
    ret_ref[0] = proj(wb_ref, W_ATT, W_RET)

    def norm_rope(t, ss, gain, w):
        t = t * lax.rsqrt(ss * (1.0 / HEAD_DV) + EPS) * gain
        lane = lax.broadcasted_iota(jnp.int32, t.shape, 1)
        first_half = (lane % HEAD_DV) < HEAD_DV // 2
        rot = jnp.where(first_half, pltpu.roll(t, w - HEAD_DV // 2, 1), pltpu.roll(t, HEAD_DV // 2, 1))
        return t * cos_ref[:, :w] + rot * sin_ref[:, :w]

    qn = norm_rope(q, ss_q, qg_ref[...], wq) * (HEAD_DV ** -0.5 * LOG2_E)
    kn = norm_rope(k, ss_k, kg_ref[...], wk)
    low = lax.broadcasted_iota(jnp.int32, (tm, LANES), 1) < HEAD_DV
    pair0 = qn[:, :LANES]
    pair1 = qn[:, LANES:]
    q_out[0, 0] = jnp.where(low, pair0, 0.0).astype(BF16)
    q_out[0, 1] = jnp.where(low, pltpu.roll(pair0, HEAD_DV, 1), 0.0).astype(BF16)
    q_out[0, 2] = jnp.where(low, 0.0, pltpu.roll(pair1, HEAD_DV, 1)).astype(BF16)
    q_out[0, 3] = jnp.where(low, 0.0, pair1).astype(BF16)
    k_out[0] = kn.astype(BF16)
    vt = v.T.astype(BF16)
    ones = jnp.ones((VT_ROWS - HEAD_DV, tm), BF16)
    for h in range(KV_HEADS):
        vt_out[0, h, 0, 0:HEAD_DV, :] = vt[h * HEAD_DV:(h + 1) * HEAD_DV, :]
        vt_out[0, h, 0, HEAD_DV:, :] = ones


def _in_proj(layer, x_lat, x_ctx, mods, norm1_g, w_a, w_z, w_b, cos, sin, q_gain, k_gain,
             ones_blockdiag, n_ctx_tiles):
    split = x_ctx is not None
    bsz, _, d = x_lat.shape
    tm = ROW_TILE
    t_all = x_lat.shape[1] + (x_ctx.shape[1] if split else 0)
    n_tiles = t_all // tm
    ctx_row = bsz
    wq = N_HEADS * HEAD_DV
    row_specs, row_args = _row_specs(split, x_lat, x_ctx, tm, n_ctx_tiles, 0)
    tile = lambda b, j: (b, j, 0)
    return pl.pallas_call(
        functools.partial(_inproj_body, split, n_ctx_tiles),
        grid=(bsz, n_tiles),
        in_specs=row_specs + [
            pl.BlockSpec((None, None, 1, N_MOD * d),
                         lambda b, j: (layer, jnp.where(j < n_ctx_tiles, ctx_row, b), 0, 0)),
            _layer_spec(norm1_g, layer), _layer_spec(w_a, layer), _layer_spec(w_z, layer),
            _layer_spec(w_b, layer),
            pl.BlockSpec((tm, wq), lambda b, j: (j, 0)), pl.BlockSpec((tm, wq), lambda b, j: (j, 0)),
            _layer_spec(q_gain, layer), _layer_spec(k_gain, layer), _const_spec(ones_blockdiag)],
        out_specs=[pl.BlockSpec((1, tm, W_CONV), tile), pl.BlockSpec((1, tm, W_GLA), tile),
                   pl.BlockSpec((1, tm, W_RET), tile),
                   pl.BlockSpec((1, N_HEADS, tm, LANES), lambda b, j: (b, 0, j, 0)),
                   pl.BlockSpec((1, tm, LANES), tile),
                   pl.BlockSpec((1, KV_HEADS, 1, VT_ROWS, tm), lambda b, j: (b, 0, j, 0, 0))],
        out_shape=[jax.ShapeDtypeStruct((bsz, t_all, W_CONV), F32),
                   jax.ShapeDtypeStruct((bsz, t_all, W_GLA), F32),
                   jax.ShapeDtypeStruct((bsz, t_all, W_RET), F32),
                   jax.ShapeDtypeStruct((bsz, N_HEADS, t_all, LANES), BF16),
                   jax.ShapeDtypeStruct((bsz, t_all, LANES), BF16),
                   jax.ShapeDtypeStruct((bsz, KV_HEADS, n_tiles, VT_ROWS, tm), BF16)],
        scratch_shapes=[pltpu.VMEM((tm, d), F32)] if split else [],
        compiler_params=_params(("parallel", "parallel")),
        name="in_proj",
    )(*row_args, mods, norm1_g, w_a, w_z, w_b, cos, sin, q_gain, k_gain, ones_blockdiag)


def _conv_body(n_ctx_tiles, n_tiles, tile0, main_ref, prev_ref, next_ref, wdw_ref, bdw_ref,
               lng_ref, lnb_ref, wpw_ref, bpw_ref, o_ref, ubuf, shifted, sbuf):
    tt = main_ref.shape[1]
    c = o_ref.shape[2]
    j = pl.program_id(1) + tile0

    def glu(blk):
        return blk[:, :c] * jax.nn.sigmoid(blk[:, c:])

    has_prev = jnp.logical_and(j != 0, j != n_ctx_tiles)
    has_next = jnp.logical_and(j != n_ctx_tiles - 1, j != n_tiles - 1)

    @pl.when(has_prev)
    def _():
        ubuf[0:HALO, :] = glu(prev_ref[0])

    @pl.when(jnp.logical_not(has_prev))
    def _():
        ubuf[0:HALO, :] = jnp.zeros((HALO, c), F32)

    ubuf[HALO:HALO + tt, :] = glu(main_ref[0])

    @pl.when(has_next)
    def _():
        ubuf[HALO + tt:, :] = glu(next_ref[0])

    @pl.when(jnp.logical_not(has_next))
    def _():
        ubuf[HALO + tt:, :] = jnp.zeros((HALO, c), F32)

    span = shifted.shape[1]
    for s in range(1, SUBLANES):
        shifted[s - 1] = ubuf[s:s + span, :]

    rows = 64
    for r in range(0, tt, rows):
        acc = jnp.zeros((rows, c), F32)
        for k in range(CONV_KSIZE):
            off = k + HALO - CONV_PAD
            lo = r + off - off % SUBLANES
            if off % SUBLANES == 0:
                tap = ubuf[lo:lo + rows, :]
            else:
                tap = shifted[off % SUBLANES - 1, lo:lo + rows, :]
            acc = acc + tap * wdw_ref[k:k + 1, :]
        y = acc + bdw_ref[...]
        yc = y - jnp.mean(y, axis=-1, keepdims=True)
        yn = yc * lax.rsqrt(jnp.mean(yc * yc, axis=-1, keepdims=True) + EPS)
        yn = yn * lng_ref[...] + lnb_ref[...]
        sbuf[r:r + rows, :] = _silu(yn).astype(BF16)
    o_ref[0] = jnp.dot(sbuf[...], wpw_ref[...], preferred_element_type=F32) + bpw_ref[...]


def _conv_mixer(layer, conv_in, w_dw, b_dw, ln_g, ln_b, w_pw, b_pw, n_ctx_tiles, with_ctx):
    bsz, t_all, _ = conv_in.shape
    tt = ROW_TILE
    c = GROUP_W
    n_tiles = t_all // tt
    tile0 = 0 if with_ctx else n_ctx_tiles
    per = tt // HALO
    n_halo = t_all // HALO
    return pl.pallas_call(
        functools.partial(_conv_body, n_ctx_tiles, n_tiles, tile0),
        grid=(bsz, n_tiles - tile0),
        in_specs=[pl.BlockSpec((1, tt, 2 * c), lambda b, j: (b, j + tile0, 0)),
                  pl.BlockSpec((1, HALO, 2 * c),
                               lambda b, j: (b, jnp.maximum((j + tile0) * per - 1, 0), 0)),
                  pl.BlockSpec((1, HALO, 2 * c),
                               lambda b, j: (b, jnp.minimum((j + tile0 + 1) * per, n_halo - 1), 0)),
                  _layer_spec(w_dw, layer), _layer_spec(b_dw, layer), _layer_spec(ln_g, layer),
                  _layer_spec(ln_b, layer), _layer_spec(w_pw, layer), _layer_spec(b_pw, layer)],
        out_specs=pl.BlockSpec((1, tt, c), lambda b, j: (b, j, 0)),
        out_shape=jax.ShapeDtypeStruct((bsz, t_all - tile0 * tt, c), F32),
        scratch_shapes=[pltpu.VMEM((tt + 2 * HALO, c), F32),
                        pltpu.VMEM((SUBLANES - 1, tt + 2 * HALO - SUBLANES, c), F32),
                        pltpu.VMEM((tt, c), BF16)],
        compiler_params=_params(("parallel", "parallel")),
        name="conv_mixer",
    )(conv_in, conv_in, conv_in, w_dw, b_dw, ln_g, ln_b, w_pw, b_pw)


def _chunk_cumsum(la, tri_ref):
    w = la.shape[1]
    hi = la.astype(BF16)
    lo = (la - hi.astype(F32)).astype(BF16)
    both = jnp.dot(tri_ref[...], jnp.concatenate([hi, lo], axis=1), preferred_element_type=F32)
    return both[:, :w] + both[:, w:]


def _decay_factors(b, reverse):
    b_tot = b[0:1, :] if reverse else b[CHUNK - 1:CHUNK, :]
    return jnp.exp(b), jnp.exp(-b), jnp.exp(b_tot - b), jnp.exp(b_tot)


def _scan_chunk_head(q, k, v, factors, st, key_mask, value_mask, head_mask):
    nt_dims = (((1,), (1,)), ((), ()))
    e_q, e_inv, e_end, dec = factors
    q_dec = (q * e_q).astype(BF16)
    k_inv = (k * e_inv).astype(BF16)
    k_end = (k * e_end).astype(BF16)
    vb = v.astype(BF16)
    k_blk = jnp.concatenate([k_inv] * N_HEADS, axis=0) * key_mask
    scores = lax.dot_general(q_dec, k_blk, nt_dims, preferred_element_type=F32)
    o_inter = lax.dot_general(q_dec, st.astype(BF16), nt_dims, preferred_element_type=F32)
    kv_t = lax.dot_general(vb, k_end, (((0,), (0,)), ((), ())), preferred_element_type=F32)
    st = st * dec + jnp.where(head_mask, kv_t, 0.0)
    v_blk = jnp.concatenate([vb] * N_HEADS, axis=0) * value_mask
    return (scores, v_blk, o_inter), st


def _scan_chunk_tail(pending, tri):
    scores, v_blk, o_inter = pending
    scores = jnp.where(tri, scores, 0.0).astype(BF16)
    return jnp.dot(scores, v_blk, preferred_element_type=F32) + o_inter


def _scan_body(gf, gb, rf, rb, waf, baf, wab, bab, lg, cosf, sinf, cosb, sinb,
               key_mask_ref, value_mask_ref, cum_f_ref, cum_b_ref,
               gof, gob, rof, rob, st_gf, st_gb, st_rf, st_rb):
    states = (st_gf, st_gb, st_rf, st_rb)

    @pl.when(pl.program_id(1) == 0)
    def _():
        for st in states:
            st[...] = jnp.zeros(st.shape, F32)

    tb = gf.shape[1]
    dk_all = N_HEADS * SCAN_DK
    dv_all = GROUP_W
    scale = SCAN_DK ** -0.5
    key_mask = key_mask_ref[...]
    value_mask = value_mask_ref[...]
    r = lax.broadcasted_iota(jnp.int32, (dv_all, dk_all), 0)
    c = lax.broadcasted_iota(jnp.int32, (dv_all, dk_all), 1)
    head_mask = (r // HEAD_DV) == (c // SCAN_DK)
    r = lax.broadcasted_iota(jnp.int32, (CHUNK, dv_all), 0)
    c = lax.broadcasted_iota(jnp.int32, (CHUNK, dv_all), 1) % CHUNK
    tri_f = c <= r
    tri_b = c >= r

    def qkv(ref):
        return (ref[0, :, 0:dk_all], ref[0, :, dk_all:2 * dk_all],
                ref[0, :, 2 * dk_all:2 * dk_all + dv_all])

    def gla_cum_decay(ref, w_ref, bias_ref, cum_ref):
        z = ref[0, :, 2 * dk_all + 2 * dv_all:]
        pre = jnp.dot(z.astype(BF16), w_ref[...], preferred_element_type=F32) + bias_ref[...]
        log_sig = jnp.minimum(pre, 0.0) - jnp.log1p(jnp.exp(-jnp.abs(pre)))
        return _chunk_cumsum(log_sig / GLA_TAU, cum_ref)

    q_gf, k_gf, v_gf = qkv(gf)
    q_gb, k_gb, v_gb = qkv(gb)
    b_gf = gla_cum_decay(gf, waf, baf, cum_f_ref)
    b_gb = gla_cum_decay(gb, wab, bab, cum_b_ref)

    lane = lax.broadcasted_iota(jnp.int32, (tb, dk_all), 1)
    first_half = (lane % SCAN_DK) < SCAN_DK // 2

    def rope(t, cos_ref, sin_ref):
        rot = jnp.where(first_half, pltpu.roll(t, dk_all - SCAN_DK // 2, 1),
                        pltpu.roll(t, SCAN_DK // 2, 1))
        return t * cos_ref[...] + rot * sin_ref[...]

    q_rf, k_rf, v_rf = qkv(rf)
    q_rb, k_rb, v_rb = qkv(rb)
    q_rf, k_rf = rope(q_rf, cosf, sinf), rope(k_rf * scale, cosf, sinf)
    q_rb, k_rb = rope(q_rb, cosb, sinb), rope(k_rb * scale, cosb, sinb)
    pos = lax.broadcasted_iota(jnp.int32, (CHUNK, dk_all), 0).astype(F32)
    ret_fac_f = _decay_factors((pos + 1.0) * lg[...], False)
    ret_fac_b = _decay_factors((CHUNK - pos) * lg[...], True)

    chains = (
        (q_gf * scale, k_gf, v_gf, lambda lo: _decay_factors(b_gf[lo:lo + CHUNK], False), False, gof, tri_f),
        (q_gb * scale, k_gb, v_gb, lambda lo: _decay_factors(b_gb[lo:lo + CHUNK], True), True, gob, tri_b),
        (q_rf, k_rf, v_rf, lambda lo: ret_fac_f, False, rof, tri_f),
        (q_rb, k_rb, v_rb, lambda lo: ret_fac_b, True, rob, tri_b),
    )
    st_vals = [st[...] for st in states]
    n_chunks = tb // CHUNK

    def heads(i):
        pending = []
        for n, (q, k, v, factors, reverse, _, _) in enumerate(chains):
            lo = (n_chunks - 1 - i if reverse else i) * CHUNK
            part, st_vals[n] = _scan_chunk_head(
                q[lo:lo + CHUNK], k[lo:lo + CHUNK], v[lo:lo + CHUNK], factors(lo), st_vals[n],
                key_mask, value_mask, head_mask)
            pending.append((lo, part))
        return pending

    pending = heads(0)
    for i in range(n_chunks):
        following = heads(i + 1) if i + 1 < n_chunks else None
        for (lo, part), (_, _, _, _, _, out_ref, tri) in zip(pending, chains):
            out_ref[0, lo:lo + CHUNK, :] = _scan_chunk_tail(part, tri)
        pending = following
    for st, val in zip(states, st_vals):
        st[...] = val


def _scan_mixers(layer, gla_in, ret_in, n_ctx_tiles, w_a_f, b_a_f, w_a_b, b_a_b, log_gamma, cos, sin,
                 key_mask, value_mask, cum_f, cum_b):
    bsz, t_all, _ = gla_in.shape
    tb = ROW_TILE
    n_tiles = t_all // tb
    dk_all = N_HEADS * SCAN_DK
    dv_all = GROUP_W

    def bwd(j):
        return jnp.where(j < n_ctx_tiles, n_ctx_tiles - 1 - j, n_tiles - 1 - (j - n_ctx_tiles))

    fwd_blk = lambda b, j: (b, j, 0)
    bwd_blk = lambda b, j: (b, bwd(j), 0)
    in_specs = [pl.BlockSpec((1, tb, W_GLA), fwd_blk), pl.BlockSpec((1, tb, W_GLA), bwd_blk),
                pl.BlockSpec((1, tb, W_RET), fwd_blk), pl.BlockSpec((1, tb, W_RET), bwd_blk),
                _layer_spec(w_a_f, layer), _layer_spec(b_a_f, layer),
                _layer_spec(w_a_b, layer), _layer_spec(b_a_b, layer),
                _const_spec(log_gamma),
                pl.BlockSpec((tb, dk_all), lambda b, j: (j, 0)),
                pl.BlockSpec((tb, dk_all), lambda b, j: (j, 0)),
                pl.BlockSpec((tb, dk_all), lambda b, j: (bwd(j), 0)),
                pl.BlockSpec((tb, dk_all), lambda b, j: (bwd(j), 0)),
                _const_spec(key_mask), _const_spec(value_mask), _const_spec(cum_f), _const_spec(cum_b)]
    return pl.pallas_call(
        _scan_body,
        grid=(bsz, n_tiles),
        in_specs=in_specs,
        out_specs=[pl.BlockSpec((1, tb, dv_all), fwd_blk), pl.BlockSpec((1, tb, dv_all), bwd_blk)] * 2,
        out_shape=[jax.ShapeDtypeStruct((bsz, t_all, dv_all), F32)] * 4,
        scratch_shapes=[pltpu.VMEM((dv_all, dk_all), F32)] * 4,
        compiler_params=_params(("parallel", "arbitrary")),
        name="scans",
    )(gla_in, gla_in, ret_in, ret_in, w_a_f, b_a_f, w_a_b, b_a_b, log_gamma, cos, sin, cos, sin,
      key_mask, value_mask, cum_f, cum_b)


def _att_body(n_ctx_tiles, n_tiles, tile0, q_ref, k_ref, vt_ref, o_ref):
    group = q_ref.shape[1]
    tq = q_ref.shape[2]
    tk = vt_ref.shape[4]
    nq = group * tq

    def attend(n_blocks):
        q2 = q_ref[0].reshape(nq, LANES)

        def scores(i):
            kb = k_ref[0, i * tk:(i + 1) * tk, :]
            return lax.dot_general(kb, q2, (((1,), (1,)), ((), ())), preferred_element_type=F32)

        m = acc = None
        ahead = [scores(i) for i in range(min(SCORE_LOOKAHEAD, n_blocks))]
        for i in range(n_blocks):
            s = ahead.pop(0)
            if i + SCORE_LOOKAHEAD < n_blocks:
                ahead.append(scores(i + SCORE_LOOKAHEAD))
            blk_max = jnp.max(s, axis=0, keepdims=True)
            m_new = blk_max if i == 0 else jnp.maximum(m, blk_max)
            p = jnp.exp2(s - m_new).astype(BF16)
            pv = jnp.dot(vt_ref[0, 0, i], p, preferred_element_type=F32)
            acc = pv if i == 0 else jnp.exp2(m - m_new) * acc + pv
            m = m_new
        o_t = acc[:HEAD_DV] / acc[HEAD_DV:HEAD_DV + 1]
        stacked = jnp.concatenate([o_t[:, g * tq:(g + 1) * tq] for g in range(group)], axis=0)
        o_ref[0] = stacked.T

    qi = pl.program_id(2) + tile0
    if tile0 < n_ctx_tiles:
        pl.when(qi < n_ctx_tiles)(lambda: attend(n_ctx_tiles))
    pl.when(qi >= n_ctx_tiles)(lambda: attend(n_tiles))


def _attention(q, k, vt, n_ctx_tiles, with_ctx):
    bsz, _, t_all, _ = q.shape
    tq = ROW_TILE
    n_tiles = t_all // tq
    tile0 = 0 if with_ctx else n_ctx_tiles
    group = N_HEADS // KV_HEADS
    return pl.pallas_call(
        functools.partial(_att_body, n_ctx_tiles, n_tiles, tile0),
        grid=(bsz, KV_HEADS, n_tiles - tile0),
        in_specs=[pl.BlockSpec((1, group, tq, LANES), lambda b, h, i: (b, h, i + tile0, 0)),
                  pl.BlockSpec((1, t_all, LANES), lambda b, h, i: (b, 0, 0)),
                  pl.BlockSpec((1, 1, n_tiles, VT_ROWS, tq), lambda b, h, i: (b, h, 0, 0, 0))],
        out_specs=pl.BlockSpec((1, tq, group * HEAD_DV), lambda b, h, i: (b, i, h)),
        out_shape=jax.ShapeDtypeStruct((bsz, t_all - tile0 * tq, GROUP_W), F32),
        compiler_params=_params(("parallel", "parallel", "arbitrary")),
        name="attention",
    )(q, k, vt)


def _post_body(final, split, n_ctx_tiles, tile0, *refs):
    (yconv, gla_f, gla_b, gla_gate, yatt, ret_f, ret_b, ret_gate, gla_gain, ret_gain, ones_ref) = refs[:11]
    n_src = 2 if split else 1
    lat_ref = refs[11]
    ctx_ref = refs[12] if split else None
    mod_ref, n2g_ref, wout_ref, wup_ref, wdown_ref, fng_ref, o_ref = refs[11 + n_src:18 + n_src]
    xbuf = refs[18 + n_src] if split else None
    tm, d = lat_ref.shape[1], lat_ref.shape[2]
    d_ff = wup_ref.shape[1]

    def finish(of_ref, ob_ref, gate_ref, gain_ref):
        o = of_ref[0] + ob_ref[0]
        ss = _group_sumsq(o, ones_ref[...])
        y = o * lax.rsqrt(ss * (1.0 / HEAD_DV) + EPS) * gain_ref[...]
        return (y * _silu(gate_ref[0])).astype(BF16)

    mixed = jnp.concatenate([yconv[0].astype(BF16), finish(gla_f, gla_b, gla_gate, gla_gain),
                             yatt[0].astype(BF16), finish(ret_f, ret_b, ret_gate, ret_gain)], axis=1)
    o = jnp.dot(mixed, wout_ref[...], preferred_element_type=F32)
    gate1 = mod_ref[:, 2 * d:3 * d]
    shift2 = mod_ref[:, 3 * d:4 * d]
    scale2 = mod_ref[:, 4 * d:5 * d]
    gate2 = mod_ref[:, 5 * d:6 * d]
    x = _load_rows(split, pl.program_id(1) + tile0 < n_ctx_tiles, lat_ref, ctx_ref, xbuf)
    x_mid = x + gate1 * o
    h2 = ((_rms(x_mid) * n2g_ref[...]) * (1.0 + scale2) + shift2).astype(BF16)
    acc = jnp.zeros((tm, d), F32)
    tf = 1024
    for f in range(0, d_ff, tf):
        u = jnp.dot(h2, wup_ref[:, f:f + tf], preferred_element_type=F32)
        a = jnp.square(jnp.maximum(u, 0.0)).astype(BF16)
        acc = acc + jnp.dot(a, wdown_ref[f:f + tf, :], preferred_element_type=F32)
    x_out = x_mid + gate2 * acc
    if final:
        x_out = _rms(x_out) * fng_ref[...]
    o_ref[0] = x_out


def _post(layer, final, yconv, gla_of, gla_ob, gla_in, yatt, ret_of, ret_ob, ret_in, gla_gain, ret_gain,
          ones_blockdiag, x_lat, x_ctx, mods, norm2_g, w_out, w_up, w_down, final_g, n_ctx_tiles,
          with_ctx):
    split = x_ctx is not None
    bsz, _, d = x_lat.shape
    tm = ROW_TILE
    t_all = gla_of.shape[1]
    ctx_row = bsz
    tile0 = 0 if with_ctx else n_ctx_tiles
    n_out = t_all // tm - tile0
    c = GROUP_W
    gate_blk = (2 * N_HEADS * SCAN_DK + c) // c
    full = lambda b, j: (b, j + tile0, 0)
    own = lambda b, j: (b, j, 0)
    assert not split or tile0 == 0
    row_specs, row_args = _row_specs(split, x_lat, x_ctx, tm, n_ctx_tiles, tile0)

    def resident(arr):
        rest = arr.shape[1:]
        return pl.BlockSpec((None,) + rest, lambda *_: (layer,) + (0,) * len(rest),
                            pipeline_mode=pl.Buffered(1))

    return pl.pallas_call(
        functools.partial(_post_body, final, split, n_ctx_tiles, tile0),
        grid=(bsz, n_out),
        in_specs=[pl.BlockSpec((1, tm, c), own),
                  pl.BlockSpec((1, tm, c), full), pl.BlockSpec((1, tm, c), full),
                  pl.BlockSpec((1, tm, c), lambda b, j: (b, j + tile0, gate_blk)),
                  pl.BlockSpec((1, tm, c), own),
                  pl.BlockSpec((1, tm, c), full), pl.BlockSpec((1, tm, c), full),
                  pl.BlockSpec((1, tm, c), lambda b, j: (b, j + tile0, gate_blk)),
                  _layer_spec(gla_gain, layer), _layer_spec(ret_gain, layer),
                  _const_spec(ones_blockdiag)] + row_specs + [
                  pl.BlockSpec((None, None, 1, N_MOD * d),
                               lambda b, j: (layer, jnp.where(j + tile0 < n_ctx_tiles, ctx_row, b), 0, 0)),
                  _layer_spec(norm2_g, layer),
                  resident(w_out), resident(w_up), resident(w_down),
                  _const_spec(final_g)],
        out_specs=pl.BlockSpec((1, tm, d), own),
        out_shape=jax.ShapeDtypeStruct((bsz, n_out * tm, d), F32),
        scratch_shapes=[pltpu.VMEM((tm, d), F32)] if split else [],
        compiler_params=_params(("parallel", "parallel")),
        name="post",
    )(yconv, gla_of, gla_ob, gla_in, yatt, ret_of, ret_ob, ret_in, gla_gain, ret_gain,
      ones_blockdiag, *row_args, mods, norm2_g, w_out, w_up, w_down, final_g)


def _rope_table(t_lat, n_ctx, head_dim, width):
    n_ax = head_dim // 4
    inv = np.float32(ROPE_THETA) ** (-np.arange(n_ax, dtype=np.float32) / np.float32(n_ax))
    pos = np.arange(t_lat)
    ang = np.concatenate([(pos // GRID_W).astype(np.float32)[:, None] * inv,
                          (pos % GRID_W).astype(np.float32)[:, None] * inv], axis=-1).astype(np.float32)
    cos, sin = np.cos(ang), np.sin(ang)
    cos = np.concatenate([np.ones((n_ctx, head_dim // 2), np.float32), cos], axis=0)
    sin = np.concatenate([np.zeros((n_ctx, head_dim // 2), np.float32), sin], axis=0)
    reps = width // head_dim
    return (jnp.asarray(np.tile(np.concatenate([cos, cos], axis=-1), (1, reps)), F32),
            jnp.asarray(np.tile(np.concatenate([-sin, sin], axis=-1), (1, reps)), F32))


def kernel(x, c, ctx, c_ctx, w_mod, b_mod, norm1_g, norm2_g, w_in, conv_w_dw, conv_b_dw, conv_ln_g,
           conv_ln_b, conv_w_pw, conv_b_pw, gla_w_a_f, gla_b_a_f, gla_w_a_b, gla_b_a_b, gla_norm_g,
           att_q_norm_g, att_k_norm_g, ret_norm_g, w_out, w_up, w_down, final_norm_g):
    bsz, t_lat, d = x.shape
    n_ctx = ctx.shape[1]
    depth = w_mod.shape[0]
    assert bsz + 1 <= 8 and n_ctx % ROW_TILE == 0 and t_lat % ROW_TILE == 0
    n_ctx_tiles = n_ctx // ROW_TILE
    dk_all = N_HEADS * SCAN_DK

    cc = jnp.concatenate([c, c_ctx[None, :], jnp.zeros((8 - bsz - 1, d), F32)], axis=0)
    mods = _modulation(cc, w_mod, b_mod)
    mods = mods.reshape(depth, 8, 1, N_MOD * d)

    stack_rows = lambda v: v.reshape(depth, 1, -1)
    w_a = w_in[:, :, :W_IN_A].astype(BF16)
    w_z = jnp.pad(w_in[:, :, W_IN_A:W_IN_A + GLA_RANK], ((0, 0), (0, 0), (0, LANES - GLA_RANK))).astype(BF16)
    w_b = w_in[:, :, W_IN_A + GLA_RANK:].astype(BF16)
    pad_rank = lambda w: jnp.pad(w, ((0, 0), (0, LANES - GLA_RANK), (0, 0))).astype(BF16)
    w_a_f, w_a_b = pad_rank(gla_w_a_f), pad_rank(gla_w_a_b)
    w_pw, w_out_b, w_up_b, w_down_b = (w.astype(BF16) for w in (conv_w_pw, w_out, w_up, w_down))
    q_gain = stack_rows(jnp.tile(att_q_norm_g, (1, N_HEADS)))
    k_gain = stack_rows(jnp.tile(att_k_norm_g, (1, KV_HEADS)))

    cos_att, sin_att = _rope_table(t_lat, n_ctx, HEAD_DV, N_HEADS * HEAD_DV)
    cos_ret, sin_ret = _rope_table(t_lat, n_ctx, SCAN_DK, dk_all)
    gamma = 1.0 - np.exp2(-5.0 - np.arange(N_HEADS, dtype=np.float64))
    log_gamma = jnp.asarray(np.repeat(np.log(gamma), SCAN_DK)[None, :], F32)
    gid = np.arange(GROUP_W) // HEAD_DV
    ones_blockdiag = jnp.asarray(gid[:, None] == gid[None, :], BF16)
    key_mask = jnp.asarray(gid[:, None] == (np.arange(dk_all) // SCAN_DK)[None, :], BF16)
    pos = np.arange(ROW_TILE)
    same_chunk = (pos[:, None] // CHUNK) == (pos[None, :] // CHUNK)
    cum_f = jnp.asarray(same_chunk & (pos[None, :] <= pos[:, None]), BF16)
    cum_b = jnp.asarray(same_chunk & (pos[None, :] >= pos[:, None]), BF16)

    x_lat, x_ctx = x, ctx
    for l in range(depth):
        with_ctx = l < depth - 1
        final = l == depth - 1
        conv_in, gla_in, ret_in, q, k, vt = _in_proj(
            l, x_lat, x_ctx, mods, stack_rows(norm1_g), w_a, w_z, w_b, cos_att, sin_att, q_gain, k_gain,
            ones_blockdiag, n_ctx_tiles)
        y_conv = _conv_mixer(l, conv_in, conv_w_dw, stack_rows(conv_b_dw), stack_rows(conv_ln_g),
                             stack_rows(conv_ln_b), w_pw, stack_rows(conv_b_pw), n_ctx_tiles, with_ctx)
        gla_of, gla_ob, ret_of, ret_ob = _scan_mixers(
            l, gla_in, ret_in, n_ctx_tiles, w_a_f, stack_rows(gla_b_a_f), w_a_b, stack_rows(gla_b_a_b),
            log_gamma, cos_ret, sin_ret, key_mask, ones_blockdiag, cum_f, cum_b)
        y_att = _attention(q, k, vt, n_ctx_tiles, with_ctx)
        x_lat = _post(l, final, y_conv, gla_of, gla_ob, gla_in, y_att, ret_of, ret_ob, ret_in,
                      stack_rows(gla_norm_g), stack_rows(ret_norm_g), ones_blockdiag, x_lat, x_ctx, mods,
                      stack_rows(norm2_g), w_out_b, w_up_b, w_down_b, final_norm_g.reshape(1, -1),
                      n_ctx_tiles, with_ctx)
        x_ctx = None
    return x_lat
```

```python
import functools

import numpy as np
import jax
import jax.numpy as jnp
from jax import lax
from jax.experimental import pallas as pl
from jax.experimental.pallas import tpu as pltpu

F32 = jnp.float32
BF16 = jnp.bfloat16

EPS = 1e-6
GRID_W = 64
ROPE_THETA = 10000.0
N_MOD = 6
CONV_KSIZE = 31
CONV_PAD = (CONV_KSIZE - 1) // 2
HALO = 16
GLA_TAU = 16.0
GLA_RANK = 16
CHUNK = 64
N_HEADS = 4
KV_HEADS = 2
HEAD_DV = 64
SCAN_DK = 32
GROUP_W = 256
ROW_TILE = 256
LANES = 128
SUBLANES = 8
BF16_SUBLANES = 16
VT_ROWS = HEAD_DV + BF16_SUBLANES
LOG2_E = 1.4426950408889634
SCAN_BATCH = 4
SCORE_LOOKAHEAD = 2
V7X_VMEM_BYTES = 64 * 1024 * 1024
VMEM_LIMIT = V7X_VMEM_BYTES * 3 // 4

W_CONV = 2 * GROUP_W
W_GLA = 2 * N_HEADS * SCAN_DK + 2 * GROUP_W + LANES
W_ATT = GROUP_W + 2 * KV_HEADS * HEAD_DV
W_RET = 2 * N_HEADS * SCAN_DK + 2 * GROUP_W
W_IN_A = W_CONV + W_GLA - LANES
W_IN_B = W_ATT + W_RET


def _params(semantics):
    return pltpu.CompilerParams(dimension_semantics=semantics, vmem_limit_bytes=VMEM_LIMIT)


def _layer_spec(arr, layer):
    rest = arr.shape[1:]
    return pl.BlockSpec((None,) + rest, lambda *_: (layer,) + (0,) * len(rest))


def _const_spec(arr):
    return pl.BlockSpec(arr.shape, lambda *_: (0,) * arr.ndim)


def _silu(x):
    return x * jax.nn.sigmoid(x)


def _rms(x):
    return x * lax.rsqrt(jnp.mean(x * x, axis=-1, keepdims=True) + EPS)


def _group_sumsq(t, ones_blockdiag):
    t2 = t * t
    hi = t2.astype(BF16)
    lo = (t2 - hi.astype(F32)).astype(BF16)
    return (jnp.dot(hi, ones_blockdiag, preferred_element_type=F32)
            + jnp.dot(lo, ones_blockdiag, preferred_element_type=F32))


def _load_rows(split, is_ctx, lat_ref, ctx_ref, buf):
    if not split:
        return lat_ref[0]

    @pl.when(is_ctx)
    def _():
        buf[...] = ctx_ref[0]

    @pl.when(jnp.logical_not(is_ctx))
    def _():
        buf[...] = lat_ref[0]

    return buf[...]


def _row_specs(split, x_lat, x_ctx, tm, n_ctx_tiles, tile0):
    d = x_lat.shape[2]
    if not split:
        return [pl.BlockSpec((1, tm, d), lambda b, j: (b, j + tile0, 0))], [x_lat]
    return ([pl.BlockSpec((1, tm, d), lambda b, j: (b, jnp.maximum(j - n_ctx_tiles, 0), 0)),
             pl.BlockSpec((1, tm, d), lambda b, j: (b, jnp.minimum(j, n_ctx_tiles - 1), 0))],
            [x_lat, x_ctx])


def _mod_body(cc_ref, w_ref, b_ref, o_ref):
    s = _silu(cc_ref[...])
    o_ref[0] = jnp.dot(s.astype(BF16), w_ref[0].astype(BF16),
                       preferred_element_type=F32) + b_ref[0]


def _modulation(cc, w_mod, b_mod):
    depth, d, n = w_mod.shape
    tn = 1024
    return pl.pallas_call(
        _mod_body,
        grid=(depth, n // tn),
        in_specs=[pl.BlockSpec((8, d), lambda l, i: (0, 0)),
                  pl.BlockSpec((1, d, tn), lambda l, i: (l, 0, i)),
                  pl.BlockSpec((1, 1, tn), lambda l, i: (l, 0, i))],
        out_specs=pl.BlockSpec((1, 8, tn), lambda l, i: (l, 0, i)),
        out_shape=jax.ShapeDtypeStruct((depth, 8, n), F32),
        compiler_params=_params(("parallel", "parallel")),
        name="modulation",
    )(cc, w_mod, b_mod.reshape(depth, 1, n))


def _inproj_body(split, n_ctx_tiles, *refs):
    n_src = 2 if split else 1
    lat_ref = refs[0]
    ctx_ref = refs[1] if split else None
    (mod_ref, g_ref, wa_ref, wz_ref, wb_ref, cos_ref, sin_ref, qg_ref, kg_ref, ones_ref,
     conv_ref, gla_ref, ret_ref, q_out, k_out, vt_out) = refs[n_src:n_src + 16]
    xbuf = refs[n_src + 16] if split else None
    tm, d = lat_ref.shape[1], lat_ref.shape[2]

    x = _load_rows(split, pl.program_id(1) < n_ctx_tiles, lat_ref, ctx_ref, xbuf)
    shift = mod_ref[:, 0:d]
    scale = mod_ref[:, d:2 * d]
    hb = ((_rms(x) * g_ref[...]) * (1.0 + scale) + shift).astype(BF16)

    def proj(w_ref, lo, width):
        return jnp.dot(hb, w_ref[:, lo:lo + width], preferred_element_type=F32)

    wq = N_HEADS * HEAD_DV
    wk = KV_HEADS * HEAD_DV
    att = proj(wb_ref, 0, W_ATT)
    q = att[:, :wq]
    k = att[:, wq:wq + wk]
    v = att[:, wq + wk:]
    ss_q = _group_sumsq(q, ones_ref[...])
    ss_k = _group_sumsq(k, ones_ref[:wk, :wk])
    gla_ref[0, :, W_GLA - LANES:] = proj(wz_ref, 0, LANES)
    ret_ref[0] = proj(wb_ref, W_ATT, W_RET)
    gla_ref[0, :, 0:W_GLA - LANES] = proj(wa_ref, W_CONV, W_GLA - LANES)
    conv_ref[0] = proj(wa_ref, 0, W_CONV)

    def norm_rope(t, ss, gain, w):
        t = t * lax.rsqrt(ss * (1.0 / HEAD_DV) + EPS) * gain
        lane = lax.broadcasted_iota(jnp.int32, t.shape, 1)
        first_half = (lane % HEAD_DV) < HEAD_DV // 2
        rot = jnp.where(first_half, pltpu.roll(t, w - HEAD_DV // 2, 1), pltpu.roll(t, HEAD_DV // 2, 1))
        return t * cos_ref[:, :w] + rot * sin_ref[:, :w]

    qn = norm_rope(q, ss_q, qg_ref[...], wq) * (HEAD_DV ** -0.5 * LOG2_E)
    kn = norm_rope(k, ss_k, kg_ref[...], wk)
    low = lax.broadcasted_iota(jnp.int32, (tm, LANES), 1) < HEAD_DV
    pair0 = qn[:, :LANES]
    pair1 = qn[:, LANES:]
    q_out[0, 0] = jnp.where(low, pair0, 0.0).astype(BF16)
    q_out[0, 1] = jnp.where(low, pltpu.roll(pair0, HEAD_DV, 1), 0.0).astype(BF16)
    q_out[0, 2] = jnp.where(low, 0.0, pltpu.roll(pair1, HEAD_DV, 1)).astype(BF16)
    q_out[0, 3] = jnp.where(low, 0.0, pair1).astype(BF16)
    k_out[0] = kn.astype(BF16)
    vt = v.T.astype(BF16)
    ones = jnp.ones((VT_ROWS - HEAD_DV, tm), BF16)
    for h in range(KV_HEADS):
        vt_out[0, h, 0, 0:HEAD_DV, :] = vt[h * HEAD_DV:(h + 1) * HEAD_DV, :]
        vt_out[0, h, 0, HEAD_DV:, :] = ones


def _in_proj(layer, x_lat, x_ctx, mods, norm1_g, w_a, w_z, w_b, cos, sin, q_gain, k_gain,
             ones_blockdiag, n_ctx_tiles):
    split = x_ctx is not None
    bsz, _, d = x_lat.shape
    tm = ROW_TILE
    t_all = x_lat.shape[1] + (x_ctx.shape[1] if split else 0)
    n_tiles = t_all // tm
    ctx_row = bsz
    wq = N_HEADS * HEAD_DV
    row_specs, row_args = _row_specs(split, x_lat, x_ctx, tm, n_ctx_tiles, 0)
    tile = lambda b, j: (b, j, 0)
    return pl.pallas_call(
        functools.partial(_inproj_body, split, n_ctx_tiles),
        grid=(bsz, n_tiles),
        in_specs=row_specs + [
            pl.BlockSpec((None, None, 1, N_MOD * d),
                         lambda b, j: (layer, jnp.where(j < n_ctx_tiles, ctx_row, b), 0, 0)),
            _layer_spec(norm1_g, layer), _layer_spec(w_a, layer), _layer_spec(w_z, layer),
            _layer_spec(w_b, layer),
            pl.BlockSpec((tm, wq), lambda b, j: (j, 0)), pl.BlockSpec((tm, wq), lambda b, j: (j, 0)),
            _layer_spec(q_gain, layer), _layer_spec(k_gain, layer), _const_spec(ones_blockdiag)],
        out_specs=[pl.BlockSpec((1, tm, W_CONV), tile), pl.BlockSpec((1, tm, W_GLA), tile),
                   pl.BlockSpec((1, tm, W_RET), tile),
                   pl.BlockSpec((1, N_HEADS, tm, LANES), lambda b, j: (b, 0, j, 0)),
                   pl.BlockSpec((1, tm, LANES), tile),
                   pl.BlockSpec((1, KV_HEADS, 1, VT_ROWS, tm), lambda b, j: (b, 0, j, 0, 0))],
        out_shape=[jax.ShapeDtypeStruct((bsz, t_all, W_CONV), F32),
                   jax.ShapeDtypeStruct((bsz, t_all, W_GLA), F32),
                   jax.ShapeDtypeStruct((bsz, t_all, W_RET), F32),
                   jax.ShapeDtypeStruct((bsz, N_HEADS, t_all, LANES), BF16),
                   jax.ShapeDtypeStruct((bsz, t_all, LANES), BF16),
                   jax.ShapeDtypeStruct((bsz, KV_HEADS, n_tiles, VT_ROWS, tm), BF16)],
        scratch_shapes=[pltpu.VMEM((tm, d), F32)] if split else [],
        compiler_params=_params(("parallel", "parallel")),
        name="in_proj",
    )(*row_args, mods, norm1_g, w_a, w_z, w_b, cos, sin, q_gain, k_gain, ones_blockdiag)


def _conv_body(n_ctx_tiles, n_tiles, tile0, main_ref, prev_ref, next_ref, wdw_ref, bdw_ref,
               lng_ref, lnb_ref, wpw_ref, bpw_ref, o_ref, ubuf, shifted, sbuf):
    tt = main_ref.shape[1]
    c = o_ref.shape[2]
    j = pl.program_id(1) + tile0

    def glu(blk):
        return blk[:, :c] * jax.nn.sigmoid(blk[:, c:])

    has_prev = jnp.logical_and(j != 0, j != n_ctx_tiles)
    has_next = jnp.logical_and(j != n_ctx_tiles - 1, j != n_tiles - 1)

    @pl.when(has_prev)
    def _():
        ubuf[0:HALO, :] = glu(prev_ref[0])

    @pl.when(jnp.logical_not(has_prev))
    def _():
        ubuf[0:HALO, :] = jnp.zeros((HALO, c), F32)

    ubuf[HALO:HALO + tt, :] = glu(main_ref[0])

    @pl.when(has_next)
    def _():
        ubuf[HALO + tt:, :] = glu(next_ref[0])

    @pl.when(jnp.logical_not(has_next))
    def _():
        ubuf[HALO + tt:, :] = jnp.zeros((HALO, c), F32)

    span = shifted.shape[1]
    for s in range(1, SUBLANES):
        shifted[s - 1] = ubuf[s:s + span, :]

    rows = 128
    for r in range(0, tt, rows):
        acc = jnp.zeros((rows, c), F32)
        for k in range(CONV_KSIZE):
            off = k + HALO - CONV_PAD
            lo = r + off - off % SUBLANES
            if off % SUBLANES == 0:
                tap = ubuf[lo:lo + rows, :]
            else:
                tap = shifted[off % SUBLANES - 1, lo:lo + rows, :]
            acc = acc + tap * wdw_ref[k:k + 1, :]
        y = acc + bdw_ref[...]
        yc = y - jnp.mean(y, axis=-1, keepdims=True)
        yn = yc * lax.rsqrt(jnp.mean(yc * yc, axis=-1, keepdims=True) + EPS)
        yn = yn * lng_ref[...] + lnb_ref[...]
        sbuf[r:r + rows, :] = _silu(yn).astype(BF16)
    o_ref[0] = jnp.dot(sbuf[...], wpw_ref[...], preferred_element_type=F32) + bpw_ref[...]


def _conv_mixer(layer, conv_in, w_dw, b_dw, ln_g, ln_b, w_pw, b_pw, n_ctx_tiles, with_ctx):
    bsz, t_all, _ = conv_in.shape
    tt = ROW_TILE
    c = GROUP_W
    n_tiles = t_all // tt
    tile0 = 0 if with_ctx else n_ctx_tiles
    per = tt // HALO
    n_halo = t_all // HALO
    return pl.pallas_call(
        functools.partial(_conv_body, n_ctx_tiles, n_tiles, tile0),
        grid=(bsz, n_tiles - tile0),
        in_specs=[pl.BlockSpec((1, tt, 2 * c), lambda b, j: (b, j + tile0, 0)),
                  pl.BlockSpec((1, HALO, 2 * c),
                               lambda b, j: (b, jnp.maximum((j + tile0) * per - 1, 0), 0)),
                  pl.BlockSpec((1, HALO, 2 * c),
                               lambda b, j: (b, jnp.minimum((j + tile0 + 1) * per, n_halo - 1), 0)),
                  _layer_spec(w_dw, layer), _layer_spec(b_dw, layer), _layer_spec(ln_g, layer),
                  _layer_spec(ln_b, layer), _layer_spec(w_pw, layer), _layer_spec(b_pw, layer)],
        out_specs=pl.BlockSpec((1, tt, c), lambda b, j: (b, j, 0)),
        out_shape=jax.ShapeDtypeStruct((bsz, t_all - tile0 * tt, c), F32),
        scratch_shapes=[pltpu.VMEM((tt + 2 * HALO, c), F32),
                        pltpu.VMEM((SUBLANES - 1, tt + 2 * HALO - SUBLANES, c), F32),
                        pltpu.VMEM((tt, c), BF16)],
        compiler_params=_params(("parallel", "parallel")),
        name="conv_mixer",
    )(conv_in, conv_in, conv_in, w_dw, b_dw, ln_g, ln_b, w_pw, b_pw)


def _chunk_cumsum(la, tri_ref):
    w = la.shape[1]
    hi = la.astype(BF16)
    lo = (la - hi.astype(F32)).astype(BF16)
    both = jnp.dot(tri_ref[...], jnp.concatenate([hi, lo], axis=1), preferred_element_type=F32)
    return both[:, :w] + both[:, w:]


def _decay_factors(b, reverse):
    b_tot = b[0:1, :] if reverse else b[CHUNK - 1:CHUNK, :]
    return jnp.exp(b), jnp.exp(-b), jnp.exp(b_tot - b), jnp.exp(b_tot)


def _scan_chunk_head(q, k, v, factors, st, key_mask, value_mask, head_mask):
    nt_dims = (((1,), (1,)), ((), ()))
    e_q, e_inv, e_end, dec = factors
    q_dec = (q * e_q).astype(BF16)
    k_inv = (k * e_inv).astype(BF16)
    k_end = (k * e_end).astype(BF16)
    vb = v.astype(BF16)
    k_blk = jnp.concatenate([k_inv] * N_HEADS, axis=0) * key_mask
    scores = lax.dot_general(q_dec, k_blk, nt_dims, preferred_element_type=F32)
    o_inter = lax.dot_general(q_dec, st.astype(BF16), nt_dims, preferred_element_type=F32)
    kv_t = lax.dot_general(vb, k_end, (((0,), (0,)), ((), ())), preferred_element_type=F32)
    st = st * dec + jnp.where(head_mask, kv_t, 0.0)
    v_blk = jnp.concatenate([vb] * N_HEADS, axis=0) * value_mask
    return (scores, v_blk, o_inter), st


def _scan_chunk_tail(pending, tri):
    scores, v_blk, o_inter = pending
    scores = jnp.where(tri, scores, 0.0).astype(BF16)
    return jnp.dot(scores, v_blk, preferred_element_type=F32) + o_inter


def _scan_body(gf, gb, rf, rb, waf, baf, wab, bab, lg, cosf, sinf, cosb, sinb,
               key_mask_ref, value_mask_ref, cum_f_ref, cum_b_ref,
               gof, gob, rof, rob, st_all, b_all):
    n_batch = gf.shape[0]

    @pl.when(pl.program_id(1) == 0)
    def _():
        st_all[...] = jnp.zeros(st_all.shape, F32)

    tb = gf.shape[1]
    dk_all = N_HEADS * SCAN_DK
    dv_all = GROUP_W
    scale = SCAN_DK ** -0.5
    key_mask = key_mask_ref[...]
    value_mask = value_mask_ref[...]
    r = lax.broadcasted_iota(jnp.int32, (dv_all, dk_all), 0)
    c = lax.broadcasted_iota(jnp.int32, (dv_all, dk_all), 1)
    head_mask = (r // HEAD_DV) == (c // SCAN_DK)
    r = lax.broadcasted_iota(jnp.int32, (CHUNK, dv_all), 0)
    c = lax.broadcasted_iota(jnp.int32, (CHUNK, dv_all), 1) % CHUNK
    tri_f = c <= r
    tri_b = c >= r

    def qkv(ref, bi, lo):
        return (ref[bi, lo:lo + CHUNK, 0:dk_all], ref[bi, lo:lo + CHUNK, dk_all:2 * dk_all],
                ref[bi, lo:lo + CHUNK, 2 * dk_all:2 * dk_all + dv_all])

    def gla_chain(ref, bi, w_ref, bias_ref, cum_ref, reverse):
        slot = int(reverse)
        z = ref[bi, :, 2 * dk_all + 2 * dv_all:]
        pre = jnp.dot(z.astype(BF16), w_ref[...], preferred_element_type=F32) + bias_ref[...]
        log_sig = jnp.minimum(pre, 0.0) - jnp.log1p(jnp.exp(-jnp.abs(pre)))
        b_all[bi, slot] = _chunk_cumsum(log_sig / GLA_TAU, cum_ref)

        def load(lo):
            q, k, v = qkv(ref, bi, lo)
            return q * scale, k, v, _decay_factors(b_all[bi, slot, lo:lo + CHUNK, :], reverse)
        return load

    lane = lax.broadcasted_iota(jnp.int32, (CHUNK, dk_all), 1)
    first_half = (lane % SCAN_DK) < SCAN_DK // 2
    pos = lax.broadcasted_iota(jnp.int32, (CHUNK, dk_all), 0).astype(F32)

    ret_factors = [_decay_factors(((CHUNK - pos) if reverse else (pos + 1.0)) * lg[...], reverse)
                   for reverse in (False, True)]

    def ret_chain(ref, bi, cos_ref, sin_ref, reverse):
        def rope(t, lo):
            rot = jnp.where(first_half, pltpu.roll(t, dk_all - SCAN_DK // 2, 1),
                            pltpu.roll(t, SCAN_DK // 2, 1))
            return t * cos_ref[lo:lo + CHUNK, :] + rot * sin_ref[lo:lo + CHUNK, :]

        def load(lo):
            q, k, v = qkv(ref, bi, lo)
            return rope(q, lo), rope(k * scale, lo), v, ret_factors[int(reverse)]
        return load

    chains = []
    for bi in range(n_batch):
        chains += [
            (gla_chain(gf, bi, waf, baf, cum_f_ref, False), False, gof, bi, tri_f),
            (gla_chain(gb, bi, wab, bab, cum_b_ref, True), True, gob, bi, tri_b),
            (ret_chain(rf, bi, cosf, sinf, False), False, rof, bi, tri_f),
            (ret_chain(rb, bi, cosb, sinb, True), True, rob, bi, tri_b),
        ]
    st_vals = [st_all[n] for n in range(len(chains))]
    n_chunks = tb // CHUNK

    def heads(i):
        pending = []
        for n, (load, reverse, _, _, _) in enumerate(chains):
            lo = (n_chunks - 1 - i if reverse else i) * CHUNK
            q, k, v, factors = load(lo)
            part, st_vals[n] = _scan_chunk_head(q, k, v, factors, st_vals[n],
                                                key_mask, value_mask, head_mask)
            pending.append((lo, part))
        return pending

    pending = heads(0)
    for i in range(n_chunks):
        following = heads(i + 1) if i + 1 < n_chunks else None
        for (lo, part), (_, _, out_ref, bi, tri) in zip(pending, chains):
            out_ref[bi, lo:lo + CHUNK, :] = _scan_chunk_tail(part, tri)
        pending = following
    for n, val in enumerate(st_vals):
        st_all[n] = val


def _scan_mixers(layer, gla_in, ret_in, n_ctx_tiles, w_a_f, b_a_f, w_a_b, b_a_b, log_gamma, cos, sin,
                 key_mask, value_mask, cum_f, cum_b):
    bsz, t_all, _ = gla_in.shape
    tb = ROW_TILE
    n_tiles = t_all // tb
    dk_all = N_HEADS * SCAN_DK
    dv_all = GROUP_W

    def bwd(j):
        return jnp.where(j < n_ctx_tiles, n_ctx_tiles - 1 - j, n_tiles - 1 - (j - n_ctx_tiles))

    nb = SCAN_BATCH if bsz % SCAN_BATCH == 0 else 1
    fwd_blk = lambda b, j: (b, j, 0)
    bwd_blk = lambda b, j: (b, bwd(j), 0)
    in_specs = [pl.BlockSpec((nb, tb, W_GLA), fwd_blk), pl.BlockSpec((nb, tb, W_GLA), bwd_blk),
                pl.BlockSpec((nb, tb, W_RET), fwd_blk), pl.BlockSpec((nb, tb, W_RET), bwd_blk),
                _layer_spec(w_a_f, layer), _layer_spec(b_a_f, layer),
                _layer_spec(w_a_b, layer), _layer_spec(b_a_b, layer),
                _const_spec(log_gamma),
                pl.BlockSpec((tb, dk_all), lambda b, j: (j, 0)),
                pl.BlockSpec((tb, dk_all), lambda b, j: (j, 0)),
                pl.BlockSpec((tb, dk_all), lambda b, j: (bwd(j), 0)),
                pl.BlockSpec((tb, dk_all), lambda b, j: (bwd(j), 0)),
                _const_spec(key_mask), _const_spec(value_mask), _const_spec(cum_f), _const_spec(cum_b)]
    return pl.pallas_call(
        _scan_body,
        grid=(bsz // nb, n_tiles),
        in_specs=in_specs,
        out_specs=[pl.BlockSpec((nb, tb, dv_all), fwd_blk), pl.BlockSpec((nb, tb, dv_all), bwd_blk)] * 2,
        out_shape=[jax.ShapeDtypeStruct((bsz, t_all, dv_all), F32)] * 4,
        scratch_shapes=[pltpu.VMEM((4 * nb, dv_all, dk_all), F32), pltpu.VMEM((nb, 2, tb, dk_all), F32)],
        compiler_params=_params(("parallel", "arbitrary")),
        name="scans",
    )(gla_in, gla_in, ret_in, ret_in, w_a_f, b_a_f, w_a_b, b_a_b, log_gamma, cos, sin, cos, sin,
      key_mask, value_mask, cum_f, cum_b)


def _att_body(n_ctx_tiles, n_tiles, tile0, q_ref, k_ref, vt_ref, o_ref):
    group = q_ref.shape[1]
    tq = q_ref.shape[2]
    tk = vt_ref.shape[4]
    nq = group * tq

    def attend(n_blocks):
        q2 = q_ref[0].reshape(nq, LANES)

        def scores(i):
            kb = k_ref[0, i * tk:(i + 1) * tk, :]
            return lax.dot_general(kb, q2, (((1,), (1,)), ((), ())), preferred_element_type=F32)

        m = acc = None
        ahead = [scores(i) for i in range(min(SCORE_LOOKAHEAD, n_blocks))]
        for i in range(n_blocks):
            s = ahead.pop(0)
            if i + SCORE_LOOKAHEAD < n_blocks:
                ahead.append(scores(i + SCORE_LOOKAHEAD))
            blk_max = jnp.max(s, axis=0, keepdims=True)
            m_new = blk_max if i == 0 else jnp.maximum(m, blk_max)
            p = jnp.exp2(s - m_new).astype(BF16)
            pv = jnp.dot(vt_ref[0, 0, i], p, preferred_element_type=F32)
            acc = pv if i == 0 else jnp.exp2(m - m_new) * acc + pv
            m = m_new
        o_t = acc[:HEAD_DV] / acc[HEAD_DV:HEAD_DV + 1]
        stacked = jnp.concatenate([o_t[:, g * tq:(g + 1) * tq] for g in range(group)], axis=0)
        o_ref[0] = stacked.T

    qi = pl.program_id(2) + tile0
    if tile0 < n_ctx_tiles:
        pl.when(qi < n_ctx_tiles)(lambda: attend(n_ctx_tiles))
    pl.when(qi >= n_ctx_tiles)(lambda: attend(n_tiles))


def _attention(q, k, vt, n_ctx_tiles, with_ctx):
    bsz, _, t_all, _ = q.shape
    tq = ROW_TILE
    n_tiles = t_all // tq
    tile0 = 0 if with_ctx else n_ctx_tiles
    group = N_HEADS // KV_HEADS
    return pl.pallas_call(
        functools.partial(_att_body, n_ctx_tiles, n_tiles, tile0),
        grid=(bsz, KV_HEADS, n_tiles - tile0),
        in_specs=[pl.BlockSpec((1, group, tq, LANES), lambda b, h, i: (b, h, i + tile0, 0)),
                  pl.BlockSpec((1, t_all, LANES), lambda b, h, i: (b, 0, 0)),
                  pl.BlockSpec((1, 1, n_tiles, VT_ROWS, tq), lambda b, h, i: (b, h, 0, 0, 0))],
        out_specs=pl.BlockSpec((1, tq, group * HEAD_DV), lambda b, h, i: (b, i, h)),
        out_shape=jax.ShapeDtypeStruct((bsz, t_all - tile0 * tq, GROUP_W), F32),
        compiler_params=_params(("parallel", "parallel", "arbitrary")),
        name="attention",
    )(q, k, vt)


def _post_body(final, split, n_ctx_tiles, tile0, *refs):
    (yconv, gla_f, gla_b, gla_gate, yatt, ret_f, ret_b, ret_gate, gla_gain, ret_gain, ones_ref) = refs[:11]
    n_src = 2 if split else 1
    lat_ref = refs[11]
    ctx_ref = refs[12] if split else None
    mod_ref, n2g_ref, wout_ref, wup_ref, wdown_ref, fng_ref, o_ref = refs[11 + n_src:18 + n_src]
    xbuf = refs[18 + n_src] if split else None
    tm, d = lat_ref.shape[1], lat_ref.shape[2]
    d_ff = wup_ref.shape[1]

    def finish(of_ref, ob_ref, gate_ref, gain_ref):
        o = of_ref[0] + ob_ref[0]
        ss = _group_sumsq(o, ones_ref[...])
        y = o * lax.rsqrt(ss * (1.0 / HEAD_DV) + EPS) * gain_ref[...]
        return (y * _silu(gate_ref[0])).astype(BF16)

    mixed = jnp.concatenate([yconv[0].astype(BF16), finish(gla_f, gla_b, gla_gate, gla_gain),
                             yatt[0].astype(BF16), finish(ret_f, ret_b, ret_gate, ret_gain)], axis=1)
    o = jnp.dot(mixed, wout_ref[...], preferred_element_type=F32)
    gate1 = mod_ref[:, 2 * d:3 * d]
    shift2 = mod_ref[:, 3 * d:4 * d]
    scale2 = mod_ref[:, 4 * d:5 * d]
    gate2 = mod_ref[:, 5 * d:6 * d]
    x = _load_rows(split, pl.program_id(1) + tile0 < n_ctx_tiles, lat_ref, ctx_ref, xbuf)
    x_mid = x + gate1 * o
    h2 = ((_rms(x_mid) * n2g_ref[...]) * (1.0 + scale2) + shift2).astype(BF16)
    acc = jnp.zeros((tm, d), F32)
    tf = 1024
    for f in range(0, d_ff, tf):
        u = jnp.dot(h2, wup_ref[:, f:f + tf], preferred_element_type=F32)
        a = jnp.square(jnp.maximum(u, 0.0)).astype(BF16)
        acc = acc + jnp.dot(a, wdown_ref[f:f + tf, :], preferred_element_type=F32)
    x_out = x_mid + gate2 * acc
    if final:
        x_out = _rms(x_out) * fng_ref[...]
    o_ref[0] = x_out


def _post(layer, final, yconv, gla_of, gla_ob, gla_in, yatt, ret_of, ret_ob, ret_in, gla_gain, ret_gain,
          ones_blockdiag, x_lat, x_ctx, mods, norm2_g, w_out, w_up, w_down, final_g, n_ctx_tiles,
          with_ctx):
    split = x_ctx is not None
    bsz, _, d = x_lat.shape
    tm = ROW_TILE
    t_all = gla_of.shape[1]
    ctx_row = bsz
    tile0 = 0 if with_ctx else n_ctx_tiles
    n_out = t_all // tm - tile0
    c = GROUP_W
    gate_blk = (2 * N_HEADS * SCAN_DK + c) // c
    full = lambda b, j: (b, j + tile0, 0)
    own = lambda b, j: (b, j, 0)
    assert not split or tile0 == 0
    row_specs, row_args = _row_specs(split, x_lat, x_ctx, tm, n_ctx_tiles, tile0)

    def resident(arr):
        rest = arr.shape[1:]
        return pl.BlockSpec((None,) + rest, lambda *_: (layer,) + (0,) * len(rest),
                            pipeline_mode=pl.Buffered(1))

    return pl.pallas_call(
        functools.partial(_post_body, final, split, n_ctx_tiles, tile0),
        grid=(bsz, n_out),
        in_specs=[pl.BlockSpec((1, tm, c), own),
                  pl.BlockSpec((1, tm, c), full), pl.BlockSpec((1, tm, c), full),
                  pl.BlockSpec((1, tm, c), lambda b, j: (b, j + tile0, gate_blk)),
                  pl.BlockSpec((1, tm, c), own),
                  pl.BlockSpec((1, tm, c), full), pl.BlockSpec((1, tm, c), full),
                  pl.BlockSpec((1, tm, c), lambda b, j: (b, j + tile0, gate_blk)),
                  _layer_spec(gla_gain, layer), _layer_spec(ret_gain, layer),
                  _const_spec(ones_blockdiag)] + row_specs + [
                  pl.BlockSpec((None, None, 1, N_MOD * d),
                               lambda b, j: (layer, jnp.where(j + tile0 < n_ctx_tiles, ctx_row, b), 0, 0)),
                  _layer_spec(norm2_g, layer),
                  resident(w_out), resident(w_up), resident(w_down),
                  _const_spec(final_g)],
        out_specs=pl.BlockSpec((1, tm, d), own),
        out_shape=jax.ShapeDtypeStruct((bsz, n_out * tm, d), F32),
        scratch_shapes=[pltpu.VMEM((tm, d), F32)] if split else [],
        compiler_params=_params(("parallel", "parallel")),
        name="post",
    )(yconv, gla_of, gla_ob, gla_in, yatt, ret_of, ret_ob, ret_in, gla_gain, ret_gain,
      ones_blockdiag, *row_args, mods, norm2_g, w_out, w_up, w_down, final_g)


def _rope_table(t_lat, n_ctx, head_dim, width):
    n_ax = head_dim // 4
    inv = np.float32(ROPE_THETA) ** (-np.arange(n_ax, dtype=np.float32) / np.float32(n_ax))
    pos = np.arange(t_lat)
    ang = np.concatenate([(pos // GRID_W).astype(np.float32)[:, None] * inv,
                          (pos % GRID_W).astype(np.float32)[:, None] * inv], axis=-1).astype(np.float32)
    cos, sin = np.cos(ang), np.sin(ang)
    cos = np.concatenate([np.ones((n_ctx, head_dim // 2), np.float32), cos], axis=0)
    sin = np.concatenate([np.zeros((n_ctx, head_dim // 2), np.float32), sin], axis=0)
    reps = width // head_dim
    return (jnp.asarray(np.tile(np.concatenate([cos, cos], axis=-1), (1, reps)), F32),
            jnp.asarray(np.tile(np.concatenate([-sin, sin], axis=-1), (1, reps)), F32))


def kernel(x, c, ctx, c_ctx, w_mod, b_mod, norm1_g, norm2_g, w_in, conv_w_dw, conv_b_dw, conv_ln_g,
           conv_ln_b, conv_w_pw, conv_b_pw, gla_w_a_f, gla_b_a_f, gla_w_a_b, gla_b_a_b, gla_norm_g,
           att_q_norm_g, att_k_norm_g, ret_norm_g, w_out, w_up, w_down, final_norm_g):
    bsz, t_lat, d = x.shape
    n_ctx = ctx.shape[1]
    depth = w_mod.shape[0]
    assert bsz + 1 <= 8 and n_ctx % ROW_TILE == 0 and t_lat % ROW_TILE == 0
    n_ctx_tiles = n_ctx // ROW_TILE
    dk_all = N_HEADS * SCAN_DK

    cc = jnp.concatenate([c, c_ctx[None, :], jnp.zeros((8 - bsz - 1, d), F32)], axis=0)
    mods = _modulation(cc, w_mod, b_mod)
    mods = mods.reshape(depth, 8, 1, N_MOD * d)

    stack_rows = lambda v: v.reshape(depth, 1, -1)
    w_a = w_in[:, :, :W_IN_A].astype(BF16)
    w_z = jnp.pad(w_in[:, :, W_IN_A:W_IN_A + GLA_RANK], ((0, 0), (0, 0), (0, LANES - GLA_RANK))).astype(BF16)
    w_b = w_in[:, :, W_IN_A + GLA_RANK:].astype(BF16)
    pad_rank = lambda w: jnp.pad(w, ((0, 0), (0, LANES - GLA_RANK), (0, 0))).astype(BF16)
    w_a_f, w_a_b = pad_rank(gla_w_a_f), pad_rank(gla_w_a_b)
    w_pw, w_out_b, w_up_b, w_down_b = (w.astype(BF16) for w in (conv_w_pw, w_out, w_up, w_down))
    q_gain = stack_rows(jnp.tile(att_q_norm_g, (1, N_HEADS)))
    k_gain = stack_rows(jnp.tile(att_k_norm_g, (1, KV_HEADS)))

    cos_att, sin_att = _rope_table(t_lat, n_ctx, HEAD_DV, N_HEADS * HEAD_DV)
    cos_ret, sin_ret = _rope_table(t_lat, n_ctx, SCAN_DK, dk_all)
    gamma = 1.0 - np.exp2(-5.0 - np.arange(N_HEADS, dtype=np.float64))
    log_gamma = jnp.asarray(np.repeat(np.log(gamma), SCAN_DK)[None, :], F32)
    gid = np.arange(GROUP_W) // HEAD_DV
    ones_blockdiag = jnp.asarray(gid[:, None] == gid[None, :], BF16)
    key_mask = jnp.asarray(gid[:, None] == (np.arange(dk_all) // SCAN_DK)[None, :], BF16)
    pos = np.arange(ROW_TILE)
    same_chunk = (pos[:, None] // CHUNK) == (pos[None, :] // CHUNK)
    cum_f = jnp.asarray(same_chunk & (pos[None, :] <= pos[:, None]), BF16)
    cum_b = jnp.asarray(same_chunk & (pos[None, :] >= pos[:, None]), BF16)

    x_lat, x_ctx = x, ctx
    for l in range(depth):
        with_ctx = l < depth - 1
        final = l == depth - 1
        conv_in, gla_in, ret_in, q, k, vt = _in_proj(
            l, x_lat, x_ctx, mods, stack_rows(norm1_g), w_a, w_z, w_b, cos_att, sin_att, q_gain, k_gain,
            ones_blockdiag, n_ctx_tiles)
        y_conv = _conv_mixer(l, conv_in, conv_w_dw, stack_rows(conv_b_dw), stack_rows(conv_ln_g),
                             stack_rows(conv_ln_b), w_pw, stack_rows(conv_b_pw), n_ctx_tiles, with_ctx)
        gla_of, gla_ob, ret_of, ret_ob = _scan_mixers(
            l, gla_in, ret_in, n_ctx_tiles, w_a_f, stack_rows(gla_b_a_f), w_a_b, stack_rows(gla_b_a_b),
            log_gamma, cos_ret, sin_ret, key_mask, ones_blockdiag, cum_f, cum_b)
        y_att = _attention(q, k, vt, n_ctx_tiles, with_ctx)
        x_lat = _post(l, final, y_conv, gla_of, gla_ob, gla_in, y_att, ret_of, ret_ob, ret_in,
                      stack_rows(gla_norm_g), stack_rows(ret_norm_g), ones_blockdiag, x_lat, x_ctx, mods,
                      stack_rows(norm2_g), w_out_b, w_up_b, w_down_b, final_norm_g.reshape(1, -1),
                      n_ctx_tiles, with_ctx)
        x_ctx = None
    return x_lat
```

```python
import functools

import numpy as np
import jax
import jax.numpy as jnp
from jax import lax
from jax.experimental import pallas as pl
from jax.experimental.pallas import tpu as pltpu

F32 = jnp.float32
BF16 = jnp.bfloat16

EPS = 1e-6
GRID_W = 64
ROPE_THETA = 10000.0
N_MOD = 6
CONV_KSIZE = 31
CONV_PAD = (CONV_KSIZE - 1) // 2
HALO = 16
GLA_TAU = 16.0
GLA_RANK = 16
CHUNK = 64
N_HEADS = 4
KV_HEADS = 2
HEAD_DV = 64
SCAN_DK = 32
GROUP_W = 256
ROW_TILE = 256
LANES = 128
SUBLANES = 8
BF16_SUBLANES = 16
VT_ROWS = HEAD_DV + BF16_SUBLANES
LOG2_E = 1.4426950408889634
RIDER_CHUNKS = 64
SCAN_BATCH = 4
SCORE_LOOKAHEAD = 2
V7X_VMEM_BYTES = 64 * 1024 * 1024
VMEM_LIMIT = V7X_VMEM_BYTES * 3 // 4

W_CONV = 2 * GROUP_W
W_GLA = 2 * N_HEADS * SCAN_DK + 2 * GROUP_W + LANES
W_ATT = GROUP_W + 2 * KV_HEADS * HEAD_DV
W_RET = 2 * N_HEADS * SCAN_DK + 2 * GROUP_W
W_IN_A = W_CONV + W_GLA - LANES
W_IN_B = W_ATT + W_RET


def _params(semantics):
    return pltpu.CompilerParams(dimension_semantics=semantics, vmem_limit_bytes=VMEM_LIMIT)


def _layer_spec(arr, layer):
    rest = arr.shape[1:]
    return pl.BlockSpec((None,) + rest, lambda *_: (layer,) + (0,) * len(rest))


def _const_spec(arr):
    return pl.BlockSpec(arr.shape, lambda *_: (0,) * arr.ndim)


def _rider_specs(items, step_of):
    in_specs, args, out_specs, out_shapes = [], [], [], []
    chunk = lambda *g: jnp.minimum(step_of(*g), RIDER_CHUNKS - 1)
    for kind, arr, layer in items:
        _, rows, cols = arr.shape
        rpc = rows // RIDER_CHUNKS
        assert rows % RIDER_CHUNKS == 0 and rpc % BF16_SUBLANES == 0
        in_specs.append(pl.BlockSpec((None, rpc, cols), lambda *g, layer=layer: (layer, chunk(*g), 0)))
        args.append(arr)
        for w in ((W_IN_A, LANES, W_IN_B) if kind == "w_in" else (cols,)):
            out_specs.append(pl.BlockSpec((rpc, w), lambda *g: (chunk(*g), 0)))
            out_shapes.append(jax.ShapeDtypeStruct((rows, w), BF16))
    return in_specs, args, out_specs, out_shapes


def _run_riders(kinds, in_refs, out_refs):
    out_refs = list(out_refs)
    for kind, ref in zip(kinds, in_refs):
        w = ref[...]
        if kind == "w_in":
            z = w[:, W_IN_A:W_IN_A + GLA_RANK]
            pad = jnp.zeros((w.shape[0], LANES - GLA_RANK), F32)
            out_refs.pop(0)[...] = w[:, :W_IN_A].astype(BF16)
            out_refs.pop(0)[...] = jnp.concatenate([z, pad], axis=1).astype(BF16)
            out_refs.pop(0)[...] = w[:, W_IN_A + GLA_RANK:].astype(BF16)
        else:
            out_refs.pop(0)[...] = w.astype(BF16)


def _silu(x):
    return x * jax.nn.sigmoid(x)


def _rms(x):
    return x * lax.rsqrt(jnp.mean(x * x, axis=-1, keepdims=True) + EPS)


def _group_sumsq(t, ones_blockdiag):
    t2 = t * t
    hi = t2.astype(BF16)
    lo = (t2 - hi.astype(F32)).astype(BF16)
    return (jnp.dot(hi, ones_blockdiag, preferred_element_type=F32)
            + jnp.dot(lo, ones_blockdiag, preferred_element_type=F32))


def _load_rows(split, is_ctx, lat_ref, ctx_ref, buf):
    if not split:
        return lat_ref[0]

    @pl.when(is_ctx)
    def _():
        buf[...] = ctx_ref[0]

    @pl.when(jnp.logical_not(is_ctx))
    def _():
        buf[...] = lat_ref[0]

    return buf[...]


def _row_specs(split, x_lat, x_ctx, tm, n_ctx_tiles, tile0):
    d = x_lat.shape[2]
    if not split:
        return [pl.BlockSpec((1, tm, d), lambda b, j: (b, j + tile0, 0))], [x_lat]
    return ([pl.BlockSpec((1, tm, d), lambda b, j: (b, jnp.maximum(j - n_ctx_tiles, 0), 0)),
             pl.BlockSpec((1, tm, d), lambda b, j: (b, jnp.minimum(j, n_ctx_tiles - 1), 0))],
            [x_lat, x_ctx])


def _mod_body(cc_ref, w_ref, b_ref, o_ref):
    s = _silu(cc_ref[...])
    o_ref[0] = jnp.dot(s.astype(BF16), w_ref[0].astype(BF16),
                       preferred_element_type=F32) + b_ref[0]


def _modulation(cc, w_mod, b_mod):
    depth, d, n = w_mod.shape
    tn = 1024
    return pl.pallas_call(
        _mod_body,
        grid=(depth, n // tn),
        in_specs=[pl.BlockSpec((8, d), lambda l, i: (0, 0)),
                  pl.BlockSpec((1, d, tn), lambda l, i: (l, 0, i)),
                  pl.BlockSpec((1, 1, tn), lambda l, i: (l, 0, i))],
        out_specs=pl.BlockSpec((1, 8, tn), lambda l, i: (l, 0, i)),
        out_shape=jax.ShapeDtypeStruct((depth, 8, n), F32),
        compiler_params=_params(("parallel", "parallel")),
        name="modulation",
    )(cc, w_mod, b_mod.reshape(depth, 1, n))


def _inproj_body(split, n_ctx_tiles, *refs):
    n_src = 2 if split else 1
    lat_ref = refs[0]
    ctx_ref = refs[1] if split else None
    (mod_ref, g_ref, wa_ref, wz_ref, wb_ref, cos_ref, sin_ref, qg_ref, kg_ref, ones_ref,
     conv_ref, gla_ref, ret_ref, q_out, k_out, vt_out) = refs[n_src:n_src + 16]
    xbuf = refs[n_src + 16] if split else None
    tm, d = lat_ref.shape[1], lat_ref.shape[2]

    x = _load_rows(split, pl.program_id(1) < n_ctx_tiles, lat_ref, ctx_ref, xbuf)
    shift = mod_ref[:, 0:d]
    scale = mod_ref[:, d:2 * d]
    hb = ((_rms(x) * g_ref[...]) * (1.0 + scale) + shift).astype(BF16)

    def proj(w_ref, lo, width):
        return jnp.dot(hb, w_ref[:, lo:lo + width], preferred_element_type=F32)

    wq = N_HEADS * HEAD_DV
    wk = KV_HEADS * HEAD_DV
    att = proj(wb_ref, 0, W_ATT)
    q = att[:, :wq]
    k = att[:, wq:wq + wk]
    v = att[:, wq + wk:]
    ss_q = _group_sumsq(q, ones_ref[...])
    ss_k = _group_sumsq(k, ones_ref[:wk, :wk])
    gla_ref[0, :, W_GLA - LANES:] = proj(wz_ref, 0, LANES)
    ret_ref[0] = proj(wb_ref, W_ATT, W_RET)
    gla_ref[0, :, 0:W_GLA - LANES] = proj(wa_ref, W_CONV, W_GLA - LANES)
    conv_ref[0] = proj(wa_ref, 0, W_CONV)

    def norm_rope(t, ss, gain, w):
        t = t * lax.rsqrt(ss * (1.0 / HEAD_DV) + EPS) * gain
        lane = lax.broadcasted_iota(jnp.int32, t.shape, 1)
        first_half = (lane % HEAD_DV) < HEAD_DV // 2
        rot = jnp.where(first_half, pltpu.roll(t, w - HEAD_DV // 2, 1), pltpu.roll(t, HEAD_DV // 2, 1))
        return t * cos_ref[:, :w] + rot * sin_ref[:, :w]

    qn = norm_rope(q, ss_q, qg_ref[...], wq) * (HEAD_DV ** -0.5 * LOG2_E)
    kn = norm_rope(k, ss_k, kg_ref[...], wk)
    low = lax.broadcasted_iota(jnp.int32, (tm, LANES), 1) < HEAD_DV
    pair0 = qn[:, :LANES]
    pair1 = qn[:, LANES:]
    q_out[0, 0] = jnp.where(low, pair0, 0.0).astype(BF16)
    q_out[0, 1] = jnp.where(low, pltpu.roll(pair0, HEAD_DV, 1), 0.0).astype(BF16)
    q_out[0, 2] = jnp.where(low, 0.0, pltpu.roll(pair1, HEAD_DV, 1)).astype(BF16)
    q_out[0, 3] = jnp.where(low, 0.0, pair1).astype(BF16)
    k_out[0] = kn.astype(BF16)
    vt = v.T.astype(BF16)
    ones = jnp.ones((VT_ROWS - HEAD_DV, tm), BF16)
    for h in range(KV_HEADS):
        vt_out[0, h, 0, 0:HEAD_DV, :] = vt[h * HEAD_DV:(h + 1) * HEAD_DV, :]
        vt_out[0, h, 0, HEAD_DV:, :] = ones


def _in_proj(layer, x_lat, x_ctx, mods, norm1_g, w_layer, w_a, w_z, w_b, cos, sin, q_gain, k_gain,
             ones_blockdiag, n_ctx_tiles):
    split = x_ctx is not None
    bsz, _, d = x_lat.shape
    tm = ROW_TILE
    t_all = x_lat.shape[1] + (x_ctx.shape[1] if split else 0)
    n_tiles = t_all // tm
    ctx_row = bsz
    wq = N_HEADS * HEAD_DV
    row_specs, row_args = _row_specs(split, x_lat, x_ctx, tm, n_ctx_tiles, 0)
    tile = lambda b, j: (b, j, 0)
    return pl.pallas_call(
        functools.partial(_inproj_body, split, n_ctx_tiles),
        grid=(bsz, n_tiles),
        in_specs=row_specs + [
            pl.BlockSpec((None, None, 1, N_MOD * d),
                         lambda b, j: (layer, jnp.where(j < n_ctx_tiles, ctx_row, b), 0, 0)),
            _layer_spec(norm1_g, layer), _layer_spec(w_a, w_layer), _layer_spec(w_z, w_layer),
            _layer_spec(w_b, w_layer),
            pl.BlockSpec((tm, wq), lambda b, j: (j, 0)), pl.BlockSpec((tm, wq), lambda b, j: (j, 0)),
            _layer_spec(q_gain, layer), _layer_spec(k_gain, layer), _const_spec(ones_blockdiag)],
        out_specs=[pl.BlockSpec((1, tm, W_CONV), tile), pl.BlockSpec((1, tm, W_GLA), tile),
                   pl.BlockSpec((1, tm, W_RET), tile),
                   pl.BlockSpec((1, N_HEADS, tm, LANES), lambda b, j: (b, 0, j, 0)),
                   pl.BlockSpec((1, tm, LANES), tile),
                   pl.BlockSpec((1, KV_HEADS, 1, VT_ROWS, tm), lambda b, j: (b, 0, j, 0, 0))],
        out_shape=[jax.ShapeDtypeStruct((bsz, t_all, W_CONV), F32),
                   jax.ShapeDtypeStruct((bsz, t_all, W_GLA), F32),
                   jax.ShapeDtypeStruct((bsz, t_all, W_RET), F32),
                   jax.ShapeDtypeStruct((bsz, N_HEADS, t_all, LANES), BF16),
                   jax.ShapeDtypeStruct((bsz, t_all, LANES), BF16),
                   jax.ShapeDtypeStruct((bsz, KV_HEADS, n_tiles, VT_ROWS, tm), BF16)],
        scratch_shapes=[pltpu.VMEM((tm, d), F32)] if split else [],
        compiler_params=_params(("parallel", "parallel")),
        name="in_proj",
    )(*row_args, mods, norm1_g, w_a, w_z, w_b, cos, sin, q_gain, k_gain, ones_blockdiag)


def _conv_body(n_ctx_tiles, n_tiles, tile0, rider_kinds, *refs):
    n_rid = len(rider_kinds)
    (main_ref, prev_ref, next_ref, wdw_ref, bdw_ref, lng_ref, lnb_ref, wpw_ref, bpw_ref) = refs[:9]
    o_ref = refs[9 + n_rid]
    ubuf, shifted, sbuf = refs[-3:]
    _run_riders(rider_kinds, refs[9:9 + n_rid], refs[10 + n_rid:-3])
    tt = main_ref.shape[1]
    c = o_ref.shape[2]
    j = pl.program_id(1) + tile0

    def glu(blk):
        return blk[:, :c] * jax.nn.sigmoid(blk[:, c:])

    has_prev = jnp.logical_and(j != 0, j != n_ctx_tiles)
    has_next = jnp.logical_and(j != n_ctx_tiles - 1, j != n_tiles - 1)

    @pl.when(has_prev)
    def _():
        ubuf[0:HALO, :] = glu(prev_ref[0])

    @pl.when(jnp.logical_not(has_prev))
    def _():
        ubuf[0:HALO, :] = jnp.zeros((HALO, c), F32)

    ubuf[HALO:HALO + tt, :] = glu(main_ref[0])

    @pl.when(has_next)
    def _():
        ubuf[HALO + tt:, :] = glu(next_ref[0])

    @pl.when(jnp.logical_not(has_next))
    def _():
        ubuf[HALO + tt:, :] = jnp.zeros((HALO, c), F32)

    span = shifted.shape[1]
    for s in range(1, SUBLANES):
        shifted[s - 1] = ubuf[s:s + span, :]

    rows = 128
    for r in range(0, tt, rows):
        acc = jnp.zeros((rows, c), F32)
        for k in range(CONV_KSIZE):
            off = k + HALO - CONV_PAD
            lo = r + off - off % SUBLANES
            if off % SUBLANES == 0:
                tap = ubuf[lo:lo + rows, :]
            else:
                tap = shifted[off % SUBLANES - 1, lo:lo + rows, :]
            acc = acc + tap * wdw_ref[k:k + 1, :]
        y = acc + bdw_ref[...]
        yc = y - jnp.mean(y, axis=-1, keepdims=True)
        yn = yc * lax.rsqrt(jnp.mean(yc * yc, axis=-1, keepdims=True) + EPS)
        yn = yn * lng_ref[...] + lnb_ref[...]
        sbuf[r:r + rows, :] = _silu(yn).astype(BF16)
    o_ref[0] = jnp.dot(sbuf[...], wpw_ref[...], preferred_element_type=F32) + bpw_ref[...]


def _conv_mixer(layer, conv_in, w_dw, b_dw, ln_g, ln_b, w_pw, b_pw, n_ctx_tiles, with_ctx, riders=()):
    bsz, t_all, _ = conv_in.shape
    tt = ROW_TILE
    c = GROUP_W
    n_tiles = t_all // tt
    tile0 = 0 if with_ctx else n_ctx_tiles
    per = tt // HALO
    n_halo = t_all // HALO
    n_steps = n_tiles - tile0
    assert not riders or bsz * n_steps >= RIDER_CHUNKS
    rid_in, rid_args, rid_out, rid_shapes = _rider_specs(riders, lambda b, j: b * n_steps + j)
    return pl.pallas_call(
        functools.partial(_conv_body, n_ctx_tiles, n_tiles, tile0, tuple(k for k, _, _ in riders)),
        grid=(bsz, n_steps),
        in_specs=[pl.BlockSpec((1, tt, 2 * c), lambda b, j: (b, j + tile0, 0)),
                  pl.BlockSpec((1, HALO, 2 * c),
                               lambda b, j: (b, jnp.maximum((j + tile0) * per - 1, 0), 0)),
                  pl.BlockSpec((1, HALO, 2 * c),
                               lambda b, j: (b, jnp.minimum((j + tile0 + 1) * per, n_halo - 1), 0)),
                  _layer_spec(w_dw, layer), _layer_spec(b_dw, layer), _layer_spec(ln_g, layer),
                  _layer_spec(ln_b, layer), _layer_spec(w_pw, layer), _layer_spec(b_pw, layer)] + rid_in,
        out_specs=[pl.BlockSpec((1, tt, c), lambda b, j: (b, j, 0))] + rid_out,
        out_shape=[jax.ShapeDtypeStruct((bsz, t_all - tile0 * tt, c), F32)] + rid_shapes,
        scratch_shapes=[pltpu.VMEM((tt + 2 * HALO, c), F32),
                        pltpu.VMEM((SUBLANES - 1, tt + 2 * HALO - SUBLANES, c), F32),
                        pltpu.VMEM((tt, c), BF16)],
        compiler_params=_params(("arbitrary", "arbitrary") if riders else ("parallel", "parallel")),
        name="conv_mixer",
    )(conv_in, conv_in, conv_in, w_dw, b_dw, ln_g, ln_b, w_pw, b_pw, *rid_args)


def _chunk_cumsum(la, tri_ref):
    w = la.shape[1]
    hi = la.astype(BF16)
    lo = (la - hi.astype(F32)).astype(BF16)
    both = jnp.dot(tri_ref[...], jnp.concatenate([hi, lo], axis=1), preferred_element_type=F32)
    return both[:, :w] + both[:, w:]


def _decay_factors(b, reverse):
    b_tot = b[0:1, :] if reverse else b[CHUNK - 1:CHUNK, :]
    dk_all = b.shape[1]
    dec_cols = jnp.broadcast_to(jnp.exp(b_tot), (dk_all, dk_all)).T
    dec_cols = jnp.concatenate([dec_cols] * (GROUP_W // dk_all), axis=1)
    return jnp.exp(b), jnp.exp(-b), jnp.exp(b_tot - b), dec_cols


def _scan_chunk_head(q, k, v, factors, st, key_mask, value_mask, head_mask):
    nt_dims = (((1,), (1,)), ((), ()))
    e_q, e_inv, e_end, dec_cols = factors
    q_dec = (q * e_q).astype(BF16)
    k_inv = (k * e_inv).astype(BF16)
    k_end = (k * e_end).astype(BF16)
    vb = v.astype(BF16)
    k_blk = jnp.concatenate([k_inv] * N_HEADS, axis=0) * key_mask
    scores = lax.dot_general(q_dec, k_blk, nt_dims, preferred_element_type=F32)
    o_inter = jnp.dot(q_dec, st.astype(BF16), preferred_element_type=F32)
    kv = lax.dot_general(k_end, vb, (((0,), (0,)), ((), ())), preferred_element_type=F32)
    st = st * dec_cols + jnp.where(head_mask, kv, 0.0)
    v_blk = jnp.concatenate([vb] * N_HEADS, axis=0) * value_mask
    return (scores, v_blk, o_inter), st


def _scan_chunk_tail(pending, tri):
    scores, v_blk, o_inter = pending
    scores = jnp.where(tri, scores, 0.0).astype(BF16)
    return jnp.dot(scores, v_blk, preferred_element_type=F32) + o_inter


def _scan_body(gf, gb, rf, rb, waf, baf, wab, bab, lg, cosf, sinf, cosb, sinb,
               key_mask_ref, value_mask_ref, cum_f_ref, cum_b_ref,
               gof, gob, rof, rob, st_all, b_all):
    n_batch = gf.shape[0]

    @pl.when(pl.program_id(1) == 0)
    def _():
        st_all[...] = jnp.zeros(st_all.shape, F32)

    tb = gf.shape[1]
    dk_all = N_HEADS * SCAN_DK
    dv_all = GROUP_W
    scale = SCAN_DK ** -0.5
    key_mask = key_mask_ref[...]
    value_mask = value_mask_ref[...]
    r = lax.broadcasted_iota(jnp.int32, (dk_all, dv_all), 0)
    c = lax.broadcasted_iota(jnp.int32, (dk_all, dv_all), 1)
    head_mask = (r // SCAN_DK) == (c // HEAD_DV)
    r = lax.broadcasted_iota(jnp.int32, (CHUNK, dv_all), 0)
    c = lax.broadcasted_iota(jnp.int32, (CHUNK, dv_all), 1) % CHUNK
    tri_f = c <= r
    tri_b = c >= r

    def qkv(ref, bi, lo):
        return (ref[bi, lo:lo + CHUNK, 0:dk_all], ref[bi, lo:lo + CHUNK, dk_all:2 * dk_all],
                ref[bi, lo:lo + CHUNK, 2 * dk_all:2 * dk_all + dv_all])

    def gla_chain(ref, bi, w_ref, bias_ref, cum_ref, reverse):
        slot = int(reverse)
        z = ref[bi, :, 2 * dk_all + 2 * dv_all:]
        pre = jnp.dot(z.astype(BF16), w_ref[...], preferred_element_type=F32) + bias_ref[...]
        log_sig = jnp.minimum(pre, 0.0) - jnp.log1p(jnp.exp(-jnp.abs(pre)))
        b_all[bi, slot] = _chunk_cumsum(log_sig / GLA_TAU, cum_ref)

        def load(lo):
            q, k, v = qkv(ref, bi, lo)
            return q * scale, k, v, _decay_factors(b_all[bi, slot, lo:lo + CHUNK, :], reverse)
        return load

    lane = lax.broadcasted_iota(jnp.int32, (CHUNK, dk_all), 1)
    first_half = (lane % SCAN_DK) < SCAN_DK // 2
    pos = lax.broadcasted_iota(jnp.int32, (CHUNK, dk_all), 0).astype(F32)

    ret_factors = [_decay_factors(((CHUNK - pos) if reverse else (pos + 1.0)) * lg[...], reverse)
                   for reverse in (False, True)]

    def ret_chain(ref, bi, cos_ref, sin_ref, reverse):
        def rope(t, lo):
            rot = jnp.where(first_half, pltpu.roll(t, dk_all - SCAN_DK // 2, 1),
                            pltpu.roll(t, SCAN_DK // 2, 1))
            return t * cos_ref[lo:lo + CHUNK, :] + rot * sin_ref[lo:lo + CHUNK, :]

        def load(lo):
            q, k, v = qkv(ref, bi, lo)
            return rope(q, lo), rope(k * scale, lo), v, ret_factors[int(reverse)]
        return load

    chains = []
    for bi in range(n_batch):
        chains += [
            (gla_chain(gf, bi, waf, baf, cum_f_ref, False), False, gof, bi, tri_f),
            (gla_chain(gb, bi, wab, bab, cum_b_ref, True), True, gob, bi, tri_b),
            (ret_chain(rf, bi, cosf, sinf, False), False, rof, bi, tri_f),
            (ret_chain(rb, bi, cosb, sinb, True), True, rob, bi, tri_b),
        ]
    st_vals = [st_all[n] for n in range(len(chains))]
    n_chunks = tb // CHUNK

    def heads(i):
        pending = []
        for n, (load, reverse, _, _, _) in enumerate(chains):
            lo = (n_chunks - 1 - i if reverse else i) * CHUNK
            q, k, v, factors = load(lo)
            part, st_vals[n] = _scan_chunk_head(q, k, v, factors, st_vals[n],
                                                key_mask, value_mask, head_mask)
            pending.append((lo, part))
        return pending

    pending = heads(0)
    for i in range(n_chunks):
        following = heads(i + 1) if i + 1 < n_chunks else None
        for (lo, part), (_, _, out_ref, bi, tri) in zip(pending, chains):
            out_ref[bi, lo:lo + CHUNK, :] = _scan_chunk_tail(part, tri)
        pending = following
    for n, val in enumerate(st_vals):
        st_all[n] = val


def _scan_mixers(layer, gla_in, ret_in, n_ctx_tiles, w_a_f, b_a_f, w_a_b, b_a_b, log_gamma, cos, sin,
                 key_mask, value_mask, cum_f, cum_b):
    bsz, t_all, _ = gla_in.shape
    tb = ROW_TILE
    n_tiles = t_all // tb
    dk_all = N_HEADS * SCAN_DK
    dv_all = GROUP_W

    def bwd(j):
        return jnp.where(j < n_ctx_tiles, n_ctx_tiles - 1 - j, n_tiles - 1 - (j - n_ctx_tiles))

    nb = SCAN_BATCH if bsz % SCAN_BATCH == 0 else 1
    fwd_blk = lambda b, j: (b, j, 0)
    bwd_blk = lambda b, j: (b, bwd(j), 0)
    in_specs = [pl.BlockSpec((nb, tb, W_GLA), fwd_blk), pl.BlockSpec((nb, tb, W_GLA), bwd_blk),
                pl.BlockSpec((nb, tb, W_RET), fwd_blk), pl.BlockSpec((nb, tb, W_RET), bwd_blk),
                _layer_spec(w_a_f, layer), _layer_spec(b_a_f, layer),
                _layer_spec(w_a_b, layer), _layer_spec(b_a_b, layer),
                _const_spec(log_gamma),
                pl.BlockSpec((tb, dk_all), lambda b, j: (j, 0)),
                pl.BlockSpec((tb, dk_all), lambda b, j: (j, 0)),
                pl.BlockSpec((tb, dk_all), lambda b, j: (bwd(j), 0)),
                pl.BlockSpec((tb, dk_all), lambda b, j: (bwd(j), 0)),
                _const_spec(key_mask), _const_spec(value_mask), _const_spec(cum_f), _const_spec(cum_b)]
    return pl.pallas_call(
        _scan_body,
        grid=(bsz // nb, n_tiles),
        in_specs=in_specs,
        out_specs=[pl.BlockSpec((nb, tb, dv_all), fwd_blk), pl.BlockSpec((nb, tb, dv_all), bwd_blk)] * 2,
        out_shape=[jax.ShapeDtypeStruct((bsz, t_all, dv_all), F32)] * 4,
        scratch_shapes=[pltpu.VMEM((4 * nb, dk_all, dv_all), F32), pltpu.VMEM((nb, 2, tb, dk_all), F32)],
        compiler_params=_params(("parallel", "arbitrary")),
        name="scans",
    )(gla_in, gla_in, ret_in, ret_in, w_a_f, b_a_f, w_a_b, b_a_b, log_gamma, cos, sin, cos, sin,
      key_mask, value_mask, cum_f, cum_b)


def _att_body(n_ctx_tiles, n_tiles, tile0, q_ref, k_ref, vt_ref, o_ref):
    group = N_HEADS // KV_HEADS
    tq = q_ref.shape[2]
    tk = vt_ref.shape[4]
    nq = group * tq

    def attend(n_blocks):
        q2 = [q_ref[0, h * group:(h + 1) * group].reshape(nq, LANES) for h in range(KV_HEADS)]

        def scores(h, i):
            kb = k_ref[0, i * tk:(i + 1) * tk, :]
            return lax.dot_general(kb, q2[h], (((1,), (1,)), ((), ())), preferred_element_type=F32)

        m = [None] * KV_HEADS
        acc = [None] * KV_HEADS
        ahead = [[scores(h, i) for h in range(KV_HEADS)] for i in range(min(SCORE_LOOKAHEAD, n_blocks))]
        for i in range(n_blocks):
            s_blk = ahead.pop(0)
            if i + SCORE_LOOKAHEAD < n_blocks:
                ahead.append([scores(h, i + SCORE_LOOKAHEAD) for h in range(KV_HEADS)])
            for h in range(KV_HEADS):
                s = s_blk[h]
                blk_max = jnp.max(s, axis=0, keepdims=True)
                m_new = blk_max if i == 0 else jnp.maximum(m[h], blk_max)
                p = jnp.exp2(s - m_new).astype(BF16)
                pv = jnp.dot(vt_ref[0, h, i], p, preferred_element_type=F32)
                acc[h] = pv if i == 0 else jnp.exp2(m[h] - m_new) * acc[h] + pv
                m[h] = m_new
        halves = []
        for h in range(KV_HEADS):
            o_t = acc[h][:HEAD_DV] / acc[h][HEAD_DV:HEAD_DV + 1]
            stacked = jnp.concatenate([o_t[:, g * tq:(g + 1) * tq] for g in range(group)], axis=0)
            halves.append(stacked.T)
        o_ref[0] = jnp.concatenate(halves, axis=1)

    qi = pl.program_id(1) + tile0
    if tile0 < n_ctx_tiles:
        pl.when(qi < n_ctx_tiles)(lambda: attend(n_ctx_tiles))
    pl.when(qi >= n_ctx_tiles)(lambda: attend(n_tiles))


def _attention(q, k, vt, n_ctx_tiles, with_ctx):
    bsz, _, t_all, _ = q.shape
    tq = ROW_TILE
    n_tiles = t_all // tq
    tile0 = 0 if with_ctx else n_ctx_tiles
    return pl.pallas_call(
        functools.partial(_att_body, n_ctx_tiles, n_tiles, tile0),
        grid=(bsz, n_tiles - tile0),
        in_specs=[pl.BlockSpec((1, N_HEADS, tq, LANES), lambda b, i: (b, 0, i + tile0, 0)),
                  pl.BlockSpec((1, t_all, LANES), lambda b, i: (b, 0, 0)),
                  pl.BlockSpec((1, KV_HEADS, n_tiles, VT_ROWS, tq), lambda b, i: (b, 0, 0, 0, 0))],
        out_specs=pl.BlockSpec((1, tq, GROUP_W), lambda b, i: (b, i, 0)),
        out_shape=jax.ShapeDtypeStruct((bsz, t_all - tile0 * tq, GROUP_W), F32),
        compiler_params=_params(("parallel", "arbitrary")),
        name="attention",
    )(q, k, vt)


def _post_body(final, split, n_ctx_tiles, tile0, rider_kinds, *refs):
    (yconv, gla_f, gla_b, gla_gate, yatt, ret_f, ret_b, ret_gate, gla_gain, ret_gain, ones_ref) = refs[:11]
    n_src = 2 if split else 1
    n_rid = len(rider_kinds)
    n_rid_out = sum(3 if kind == "w_in" else 1 for kind in rider_kinds)
    lat_ref = refs[11]
    ctx_ref = refs[12] if split else None
    n_in = 17 + n_src
    mod_ref, n2g_ref, wout_ref, wup_ref, wdown_ref, fng_ref = refs[11 + n_src:n_in]
    o_ref = refs[n_in + n_rid]
    rider_out = refs[n_in + n_rid + 1:n_in + n_rid + 1 + n_rid_out]
    xbuf = refs[n_in + n_rid + 1 + n_rid_out] if split else None
    _run_riders(rider_kinds, refs[n_in:n_in + n_rid], rider_out)
    tm, d = lat_ref.shape[1], lat_ref.shape[2]
    d_ff = wup_ref.shape[1]

    def finish(of_ref, ob_ref, gate_ref, gain_ref):
        o = of_ref[0] + ob_ref[0]
        ss = _group_sumsq(o, ones_ref[...])
        y = o * lax.rsqrt(ss * (1.0 / HEAD_DV) + EPS) * gain_ref[...]
        return (y * _silu(gate_ref[0])).astype(BF16)

    mixed = jnp.concatenate([yconv[0].astype(BF16), finish(gla_f, gla_b, gla_gate, gla_gain),
                             yatt[0].astype(BF16), finish(ret_f, ret_b, ret_gate, ret_gain)], axis=1)
    o = jnp.dot(mixed, wout_ref[...], preferred_element_type=F32)
    gate1 = mod_ref[:, 2 * d:3 * d]
    shift2 = mod_ref[:, 3 * d:4 * d]
    scale2 = mod_ref[:, 4 * d:5 * d]
    gate2 = mod_ref[:, 5 * d:6 * d]
    x = _load_rows(split, pl.program_id(1) + tile0 < n_ctx_tiles, lat_ref, ctx_ref, xbuf)
    x_mid = x + gate1 * o
    h2 = ((_rms(x_mid) * n2g_ref[...]) * (1.0 + scale2) + shift2).astype(BF16)
    acc = jnp.zeros((tm, d), F32)
    tf = 1024
    for f in range(0, d_ff, tf):
        u = jnp.dot(h2, wup_ref[:, f:f + tf], preferred_element_type=F32)
        a = jnp.square(jnp.maximum(u, 0.0)).astype(BF16)
        acc = acc + jnp.dot(a, wdown_ref[f:f + tf, :], preferred_element_type=F32)
    x_out = x_mid + gate2 * acc
    if final:
        x_out = _rms(x_out) * fng_ref[...]
    o_ref[0] = x_out


def _post(layer, final, yconv, gla_of, gla_ob, gla_in, yatt, ret_of, ret_ob, ret_in, gla_gain, ret_gain,
          ones_blockdiag, x_lat, x_ctx, mods, norm2_g, w_layer, w_out, w_up, w_down, final_g, n_ctx_tiles,
          with_ctx, riders=()):
    split = x_ctx is not None
    bsz, _, d = x_lat.shape
    tm = ROW_TILE
    t_all = gla_of.shape[1]
    ctx_row = bsz
    tile0 = 0 if with_ctx else n_ctx_tiles
    n_out = t_all // tm - tile0
    c = GROUP_W
    gate_blk = (2 * N_HEADS * SCAN_DK + c) // c
    full = lambda b, j: (b, j + tile0, 0)
    own = lambda b, j: (b, j, 0)
    assert not split or tile0 == 0
    row_specs, row_args = _row_specs(split, x_lat, x_ctx, tm, n_ctx_tiles, tile0)

    def resident(arr):
        rest = arr.shape[1:]
        return pl.BlockSpec((None,) + rest, lambda *_: (w_layer,) + (0,) * len(rest),
                            pipeline_mode=pl.Buffered(1))

    assert not riders or bsz * n_out >= RIDER_CHUNKS
    rid_in, rid_args, rid_out, rid_shapes = _rider_specs(riders, lambda b, j: b * n_out + j)
    return pl.pallas_call(
        functools.partial(_post_body, final, split, n_ctx_tiles, tile0, tuple(k for k, _, _ in riders)),
        grid=(bsz, n_out),
        in_specs=[pl.BlockSpec((1, tm, c), own),
                  pl.BlockSpec((1, tm, c), full), pl.BlockSpec((1, tm, c), full),
                  pl.BlockSpec((1, tm, c), lambda b, j: (b, j + tile0, gate_blk)),
                  pl.BlockSpec((1, tm, c), own),
                  pl.BlockSpec((1, tm, c), full), pl.BlockSpec((1, tm, c), full),
                  pl.BlockSpec((1, tm, c), lambda b, j: (b, j + tile0, gate_blk)),
                  _layer_spec(gla_gain, layer), _layer_spec(ret_gain, layer),
                  _const_spec(ones_blockdiag)] + row_specs + [
                  pl.BlockSpec((None, None, 1, N_MOD * d),
                               lambda b, j: (layer, jnp.where(j + tile0 < n_ctx_tiles, ctx_row, b), 0, 0)),
                  _layer_spec(norm2_g, layer),
                  resident(w_out), resident(w_up), resident(w_down),
                  _const_spec(final_g)] + rid_in,
        out_specs=[pl.BlockSpec((1, tm, d), own)] + rid_out,
        out_shape=[jax.ShapeDtypeStruct((bsz, n_out * tm, d), F32)] + rid_shapes,
        scratch_shapes=[pltpu.VMEM((tm, d), F32)] if split else [],
        compiler_params=_params(("arbitrary", "arbitrary") if riders else ("parallel", "parallel")),
        name="post",
    )(yconv, gla_of, gla_ob, gla_in, yatt, ret_of, ret_ob, ret_in, gla_gain, ret_gain,
      ones_blockdiag, *row_args, mods, norm2_g, w_out, w_up, w_down, final_g, *rid_args)


def _rope_table(t_lat, n_ctx, head_dim, width):
    n_ax = head_dim // 4
    inv = np.float32(ROPE_THETA) ** (-np.arange(n_ax, dtype=np.float32) / np.float32(n_ax))
    pos = np.arange(t_lat)
    ang = np.concatenate([(pos // GRID_W).astype(np.float32)[:, None] * inv,
                          (pos % GRID_W).astype(np.float32)[:, None] * inv], axis=-1).astype(np.float32)
    cos, sin = np.cos(ang), np.sin(ang)
    cos = np.concatenate([np.ones((n_ctx, head_dim // 2), np.float32), cos], axis=0)
    sin = np.concatenate([np.zeros((n_ctx, head_dim // 2), np.float32), sin], axis=0)
    reps = width // head_dim
    return (jnp.asarray(np.tile(np.concatenate([cos, cos], axis=-1), (1, reps)), F32),
            jnp.asarray(np.tile(np.concatenate([-sin, sin], axis=-1), (1, reps)), F32))


def kernel(x, c, ctx, c_ctx, w_mod, b_mod, norm1_g, norm2_g, w_in, conv_w_dw, conv_b_dw, conv_ln_g,
           conv_ln_b, conv_w_pw, conv_b_pw, gla_w_a_f, gla_b_a_f, gla_w_a_b, gla_b_a_b, gla_norm_g,
           att_q_norm_g, att_k_norm_g, ret_norm_g, w_out, w_up, w_down, final_norm_g):
    bsz, t_lat, d = x.shape
    n_ctx = ctx.shape[1]
    depth = w_mod.shape[0]
    assert bsz + 1 <= 8 and n_ctx % ROW_TILE == 0 and t_lat % ROW_TILE == 0
    n_ctx_tiles = n_ctx // ROW_TILE
    dk_all = N_HEADS * SCAN_DK

    cc = jnp.concatenate([c, c_ctx[None, :], jnp.zeros((8 - bsz - 1, d), F32)], axis=0)
    mods = _modulation(cc, w_mod, b_mod)
    mods = mods.reshape(depth, 8, 1, N_MOD * d)

    stack_rows = lambda v: v.reshape(depth, 1, -1)
    w0 = w_in[:1]
    in_w = (w0[:, :, :W_IN_A].astype(BF16),
            jnp.pad(w0[:, :, W_IN_A:W_IN_A + GLA_RANK], ((0, 0), (0, 0), (0, LANES - GLA_RANK))).astype(BF16),
            w0[:, :, W_IN_A + GLA_RANK:].astype(BF16))
    pad_rank = lambda w: jnp.pad(w, ((0, 0), (0, LANES - GLA_RANK), (0, 0))).astype(BF16)
    w_a_f, w_a_b = pad_rank(gla_w_a_f), pad_rank(gla_w_a_b)
    w_pw = conv_w_pw.astype(BF16)
    q_gain = stack_rows(jnp.tile(att_q_norm_g, (1, N_HEADS)))
    k_gain = stack_rows(jnp.tile(att_k_norm_g, (1, KV_HEADS)))

    cos_att, sin_att = _rope_table(t_lat, n_ctx, HEAD_DV, N_HEADS * HEAD_DV)
    cos_ret, sin_ret = _rope_table(t_lat, n_ctx, SCAN_DK, dk_all)
    gamma = 1.0 - np.exp2(-5.0 - np.arange(N_HEADS, dtype=np.float64))
    log_gamma = jnp.asarray(np.repeat(np.log(gamma), SCAN_DK)[None, :], F32)
    gid = np.arange(GROUP_W) // HEAD_DV
    ones_blockdiag = jnp.asarray(gid[:, None] == gid[None, :], BF16)
    key_mask = jnp.asarray(gid[:, None] == (np.arange(dk_all) // SCAN_DK)[None, :], BF16)
    pos = np.arange(ROW_TILE)
    same_chunk = (pos[:, None] // CHUNK) == (pos[None, :] // CHUNK)
    cum_f = jnp.asarray(same_chunk & (pos[None, :] <= pos[:, None]), BF16)
    cum_b = jnp.asarray(same_chunk & (pos[None, :] >= pos[:, None]), BF16)

    x_lat, x_ctx = x, ctx
    for l in range(depth):
        with_ctx = l < depth - 1
        final = l == depth - 1
        conv_in, gla_in, ret_in, q, k, vt = _in_proj(
            l, x_lat, x_ctx, mods, stack_rows(norm1_g), 0, *in_w, cos_att, sin_att, q_gain, k_gain,
            ones_blockdiag, n_ctx_tiles)
        conv_riders = [("plain", w, 0) for w in (w_out, w_up, w_down)] if l == 0 else []
        y_conv, *converted = _conv_mixer(
            l, conv_in, conv_w_dw, stack_rows(conv_b_dw), stack_rows(conv_ln_g), stack_rows(conv_ln_b),
            w_pw, stack_rows(conv_b_pw), n_ctx_tiles, with_ctx, conv_riders)
        if l == 0:
            post_w = tuple(w[None] for w in converted)
        gla_of, gla_ob, ret_of, ret_ob = _scan_mixers(
            l, gla_in, ret_in, n_ctx_tiles, w_a_f, stack_rows(gla_b_a_f), w_a_b, stack_rows(gla_b_a_b),
            log_gamma, cos_ret, sin_ret, key_mask, ones_blockdiag, cum_f, cum_b)
        y_att = _attention(q, k, vt, n_ctx_tiles, with_ctx)
        post_riders = [] if final else [("w_in", w_in, l + 1)] + [("plain", w, l + 1)
                                                                  for w in (w_out, w_up, w_down)]
        x_lat, *converted = _post(
            l, final, y_conv, gla_of, gla_ob, gla_in, y_att, ret_of, ret_ob, ret_in,
            stack_rows(gla_norm_g), stack_rows(ret_norm_g), ones_blockdiag, x_lat, x_ctx, mods,
            stack_rows(norm2_g), 0, *post_w, final_norm_g.reshape(1, -1), n_ctx_tiles, with_ctx,
            post_riders)
        if not final:
            in_w = tuple(w[None] for w in converted[:3])
            post_w = tuple(w[None] for w in converted[3:])
        x_ctx = None
    return x_lat
```

```python
import functools

import numpy as np
import jax
import jax.numpy as jnp
from jax import lax
from jax.experimental import pallas as pl
from jax.experimental.pallas import tpu as pltpu

F32 = jnp.float32
BF16 = jnp.bfloat16

EPS = 1e-6
GRID_W = 64
ROPE_THETA = 10000.0
N_MOD = 6
CONV_KSIZE = 31
CONV_PAD = (CONV_KSIZE - 1) // 2
HALO = 16
GLA_TAU = 16.0
GLA_RANK = 16
CHUNK = 64
N_HEADS = 4
KV_HEADS = 2
HEAD_DV = 64
SCAN_DK = 32
GROUP_W = 256
ROW_TILE = 256
LANES = 128
SUBLANES = 8
BF16_SUBLANES = 16
VT_ROWS = HEAD_DV + BF16_SUBLANES
LOG2_E = 1.4426950408889634
RIDER_CHUNKS = 64
SCAN_BATCH = 4
SCORE_LOOKAHEAD = 2
V7X_VMEM_BYTES = 64 * 1024 * 1024
VMEM_LIMIT = V7X_VMEM_BYTES * 3 // 4

W_CONV = 2 * GROUP_W
W_GLA = 2 * N_HEADS * SCAN_DK + 2 * GROUP_W + LANES
W_ATT = GROUP_W + 2 * KV_HEADS * HEAD_DV
W_RET = 2 * N_HEADS * SCAN_DK + 2 * GROUP_W
W_IN_A = W_CONV + W_GLA - LANES
W_IN_B = W_ATT + W_RET


def _params(semantics):
    return pltpu.CompilerParams(dimension_semantics=semantics, vmem_limit_bytes=VMEM_LIMIT)


def _layer_spec(arr, layer):
    rest = arr.shape[1:]
    return pl.BlockSpec((None,) + rest, lambda *_: (layer,) + (0,) * len(rest))


def _const_spec(arr):
    return pl.BlockSpec(arr.shape, lambda *_: (0,) * arr.ndim)


def _rider_specs(items, step_of):
    in_specs, args, out_specs, out_shapes = [], [], [], []
    chunk = lambda *g: jnp.minimum(step_of(*g), RIDER_CHUNKS - 1)
    for kind, arr, layer in items:
        _, rows, cols = arr.shape
        rpc = rows // RIDER_CHUNKS
        assert rows % RIDER_CHUNKS == 0 and rpc % BF16_SUBLANES == 0
        in_specs.append(pl.BlockSpec((None, rpc, cols), lambda *g, layer=layer: (layer, chunk(*g), 0)))
        args.append(arr)
        for w in ((W_IN_A, LANES, W_IN_B) if kind == "w_in" else (cols,)):
            out_specs.append(pl.BlockSpec((rpc, w), lambda *g: (chunk(*g), 0)))
            out_shapes.append(jax.ShapeDtypeStruct((rows, w), BF16))
    return in_specs, args, out_specs, out_shapes


def _run_riders(kinds, in_refs, out_refs):
    out_refs = list(out_refs)
    for kind, ref in zip(kinds, in_refs):
        w = ref[...]
        if kind == "w_in":
            z = w[:, W_IN_A:W_IN_A + GLA_RANK]
            pad = jnp.zeros((w.shape[0], LANES - GLA_RANK), F32)
            out_refs.pop(0)[...] = w[:, :W_IN_A].astype(BF16)
            out_refs.pop(0)[...] = jnp.concatenate([z, pad], axis=1).astype(BF16)
            out_refs.pop(0)[...] = w[:, W_IN_A + GLA_RANK:].astype(BF16)
        else:
            out_refs.pop(0)[...] = w.astype(BF16)


def _silu(x):
    return x * jax.nn.sigmoid(x)


def _rms(x):
    return x * lax.rsqrt(jnp.mean(x * x, axis=-1, keepdims=True) + EPS)


def _group_sumsq(t, ones_blockdiag):
    t2 = t * t
    hi = t2.astype(BF16)
    lo = (t2 - hi.astype(F32)).astype(BF16)
    return (jnp.dot(hi, ones_blockdiag, preferred_element_type=F32)
            + jnp.dot(lo, ones_blockdiag, preferred_element_type=F32))


def _load_rows(split, is_ctx, lat_ref, ctx_ref, buf):
    if not split:
        return lat_ref[0]

    @pl.when(is_ctx)
    def _():
        buf[...] = ctx_ref[0]

    @pl.when(jnp.logical_not(is_ctx))
    def _():
        buf[...] = lat_ref[0]

    return buf[...]


def _row_specs(split, x_lat, x_ctx, tm, n_ctx_tiles, tile0):
    d = x_lat.shape[2]
    if not split:
        return [pl.BlockSpec((1, tm, d), lambda b, j: (b, j + tile0, 0))], [x_lat]
    return ([pl.BlockSpec((1, tm, d), lambda b, j: (b, jnp.maximum(j - n_ctx_tiles, 0), 0)),
             pl.BlockSpec((1, tm, d), lambda b, j: (b, jnp.minimum(j, n_ctx_tiles - 1), 0))],
            [x_lat, x_ctx])


def _mod_body(cc_ref, w_ref, b_ref, o_ref):
    s = _silu(cc_ref[...])
    o_ref[0] = jnp.dot(s.astype(BF16), w_ref[0].astype(BF16),
                       preferred_element_type=F32) + b_ref[0]


def _modulation(cc, w_mod, b_mod):
    depth, d, n = w_mod.shape
    tn = 1024
    return pl.pallas_call(
        _mod_body,
        grid=(depth, n // tn),
        in_specs=[pl.BlockSpec((8, d), lambda l, i: (0, 0)),
                  pl.BlockSpec((1, d, tn), lambda l, i: (l, 0, i)),
                  pl.BlockSpec((1, 1, tn), lambda l, i: (l, 0, i))],
        out_specs=pl.BlockSpec((1, 8, tn), lambda l, i: (l, 0, i)),
        out_shape=jax.ShapeDtypeStruct((depth, 8, n), F32),
        compiler_params=_params(("parallel", "parallel")),
        name="modulation",
    )(cc, w_mod, b_mod.reshape(depth, 1, n))


def _inproj_body(split, n_ctx_tiles, *refs):
    n_src = 2 if split else 1
    lat_ref = refs[0]
    ctx_ref = refs[1] if split else None
    (mod_ref, g_ref, wa_ref, wz_ref, wb_ref, cos_ref, sin_ref, qg_ref, kg_ref, ones_ref,
     conv_ref, gla_ref, ret_ref, q_out, k_out, vt_out) = refs[n_src:n_src + 16]
    xbuf = refs[n_src + 16] if split else None
    tm, d = lat_ref.shape[1], lat_ref.shape[2]

    x = _load_rows(split, pl.program_id(1) < n_ctx_tiles, lat_ref, ctx_ref, xbuf)
    shift = mod_ref[:, 0:d]
    scale = mod_ref[:, d:2 * d]
    hb = ((_rms(x) * g_ref[...]) * (1.0 + scale) + shift).astype(BF16)

    def proj(w_ref, lo, width):
        return jnp.dot(hb, w_ref[:, lo:lo + width], preferred_element_type=F32)

    wq = N_HEADS * HEAD_DV
    wk = KV_HEADS * HEAD_DV
    att = proj(wb_ref, 0, W_ATT)
    q = att[:, :wq]
    k = att[:, wq:wq + wk]
    v = att[:, wq + wk:]
    ss_q = _group_sumsq(q, ones_ref[...])
    ss_k = _group_sumsq(k, ones_ref[:wk, :wk])
    gla_ref[0, :, W_GLA - LANES:] = proj(wz_ref, 0, LANES)
    ret_ref[0] = proj(wb_ref, W_ATT, W_RET)
    gla_ref[0, :, 0:W_GLA - LANES] = proj(wa_ref, W_CONV, W_GLA - LANES)
    conv_ref[0] = proj(wa_ref, 0, W_CONV)

    def norm_rope(t, ss, gain, w):
        t = t * lax.rsqrt(ss * (1.0 / HEAD_DV) + EPS) * gain
        lane = lax.broadcasted_iota(jnp.int32, t.shape, 1)
        first_half = (lane % HEAD_DV) < HEAD_DV // 2
        rot = jnp.where(first_half, pltpu.roll(t, w - HEAD_DV // 2, 1), pltpu.roll(t, HEAD_DV // 2, 1))
        return t * cos_ref[:, :w] + rot * sin_ref[:, :w]

    qn = norm_rope(q, ss_q, qg_ref[...], wq) * (HEAD_DV ** -0.5 * LOG2_E)
    kn = norm_rope(k, ss_k, kg_ref[...], wk)
    low = lax.broadcasted_iota(jnp.int32, (tm, LANES), 1) < HEAD_DV
    pair0 = qn[:, :LANES]
    pair1 = qn[:, LANES:]
    q_out[0, 0] = jnp.where(low, pair0, 0.0).astype(BF16)
    q_out[0, 1] = jnp.where(low, pltpu.roll(pair0, HEAD_DV, 1), 0.0).astype(BF16)
    q_out[0, 2] = jnp.where(low, 0.0, pltpu.roll(pair1, HEAD_DV, 1)).astype(BF16)
    q_out[0, 3] = jnp.where(low, 0.0, pair1).astype(BF16)
    k_out[0] = kn.astype(BF16)
    vt = v.T.astype(BF16)
    ones = jnp.ones((VT_ROWS - HEAD_DV, tm), BF16)
    for h in range(KV_HEADS):
        vt_out[0, h, 0, 0:HEAD_DV, :] = vt[h * HEAD_DV:(h + 1) * HEAD_DV, :]
        vt_out[0, h, 0, HEAD_DV:, :] = ones


def _in_proj(layer, x_lat, x_ctx, mods, norm1_g, w_layer, w_a, w_z, w_b, cos, sin, q_gain, k_gain,
             ones_blockdiag, n_ctx_tiles):
    split = x_ctx is not None
    bsz, _, d = x_lat.shape
    tm = ROW_TILE
    t_all = x_lat.shape[1] + (x_ctx.shape[1] if split else 0)
    n_tiles = t_all // tm
    ctx_row = bsz
    wq = N_HEADS * HEAD_DV
    row_specs, row_args = _row_specs(split, x_lat, x_ctx, tm, n_ctx_tiles, 0)
    tile = lambda b, j: (b, j, 0)
    return pl.pallas_call(
        functools.partial(_inproj_body, split, n_ctx_tiles),
        grid=(bsz, n_tiles),
        in_specs=row_specs + [
            pl.BlockSpec((None, None, 1, N_MOD * d),
                         lambda b, j: (layer, jnp.where(j < n_ctx_tiles, ctx_row, b), 0, 0)),
            _layer_spec(norm1_g, layer), _layer_spec(w_a, w_layer), _layer_spec(w_z, w_layer),
            _layer_spec(w_b, w_layer),
            pl.BlockSpec((tm, wq), lambda b, j: (j, 0)), pl.BlockSpec((tm, wq), lambda b, j: (j, 0)),
            _layer_spec(q_gain, layer), _layer_spec(k_gain, layer), _const_spec(ones_blockdiag)],
        out_specs=[pl.BlockSpec((1, tm, W_CONV), tile), pl.BlockSpec((1, tm, W_GLA), tile),
                   pl.BlockSpec((1, tm, W_RET), tile),
                   pl.BlockSpec((1, N_HEADS, tm, LANES), lambda b, j: (b, 0, j, 0)),
                   pl.BlockSpec((1, tm, LANES), tile),
                   pl.BlockSpec((1, KV_HEADS, 1, VT_ROWS, tm), lambda b, j: (b, 0, j, 0, 0))],
        out_shape=[jax.ShapeDtypeStruct((bsz, t_all, W_CONV), F32),
                   jax.ShapeDtypeStruct((bsz, t_all, W_GLA), F32),
                   jax.ShapeDtypeStruct((bsz, t_all, W_RET), F32),
                   jax.ShapeDtypeStruct((bsz, N_HEADS, t_all, LANES), BF16),
                   jax.ShapeDtypeStruct((bsz, t_all, LANES), BF16),
                   jax.ShapeDtypeStruct((bsz, KV_HEADS, n_tiles, VT_ROWS, tm), BF16)],
        scratch_shapes=[pltpu.VMEM((tm, d), F32)] if split else [],
        compiler_params=_params(("parallel", "parallel")),
        name="in_proj",
    )(*row_args, mods, norm1_g, w_a, w_z, w_b, cos, sin, q_gain, k_gain, ones_blockdiag)


def _conv_body(n_ctx_tiles, n_tiles, tile0, rider_kinds, *refs):
    n_rid = len(rider_kinds)
    (main_ref, prev_ref, next_ref, wdw_ref, bdw_ref, lng_ref, lnb_ref, wpw_ref, bpw_ref) = refs[:9]
    o_ref = refs[9 + n_rid]
    ubuf, shifted, sbuf = refs[-3:]
    _run_riders(rider_kinds, refs[9:9 + n_rid], refs[10 + n_rid:-3])
    tt = main_ref.shape[1]
    c = o_ref.shape[2]
    j = pl.program_id(1) + tile0

    def glu(blk):
        return blk[:, :c] * jax.nn.sigmoid(blk[:, c:])

    has_prev = jnp.logical_and(j != 0, j != n_ctx_tiles)
    has_next = jnp.logical_and(j != n_ctx_tiles - 1, j != n_tiles - 1)

    @pl.when(has_prev)
    def _():
        ubuf[0:HALO, :] = glu(prev_ref[0])

    @pl.when(jnp.logical_not(has_prev))
    def _():
        ubuf[0:HALO, :] = jnp.zeros((HALO, c), F32)

    ubuf[HALO:HALO + tt, :] = glu(main_ref[0])

    @pl.when(has_next)
    def _():
        ubuf[HALO + tt:, :] = glu(next_ref[0])

    @pl.when(jnp.logical_not(has_next))
    def _():
        ubuf[HALO + tt:, :] = jnp.zeros((HALO, c), F32)

    span = shifted.shape[1]
    for s in range(1, SUBLANES):
        shifted[s - 1] = ubuf[s:s + span, :]

    rows = 128
    for r in range(0, tt, rows):
        acc = jnp.zeros((rows, c), F32)
        for k in range(CONV_KSIZE):
            off = k + HALO - CONV_PAD
            lo = r + off - off % SUBLANES
            if off % SUBLANES == 0:
                tap = ubuf[lo:lo + rows, :]
            else:
                tap = shifted[off % SUBLANES - 1, lo:lo + rows, :]
            acc = acc + tap * wdw_ref[k:k + 1, :]
        y = acc + bdw_ref[...]
        yc = y - jnp.mean(y, axis=-1, keepdims=True)
        yn = yc * lax.rsqrt(jnp.mean(yc * yc, axis=-1, keepdims=True) + EPS)
        yn = yn * lng_ref[...] + lnb_ref[...]
        sbuf[r:r + rows, :] = _silu(yn).astype(BF16)
    o_ref[0] = jnp.dot(sbuf[...], wpw_ref[...], preferred_element_type=F32) + bpw_ref[...]


def _conv_mixer(layer, conv_in, w_dw, b_dw, ln_g, ln_b, w_pw, b_pw, n_ctx_tiles, with_ctx, riders=()):
    bsz, t_all, _ = conv_in.shape
    tt = ROW_TILE
    c = GROUP_W
    n_tiles = t_all // tt
    tile0 = 0 if with_ctx else n_ctx_tiles
    per = tt // HALO
    n_halo = t_all // HALO
    n_steps = n_tiles - tile0
    assert not riders or bsz * n_steps >= RIDER_CHUNKS
    rid_in, rid_args, rid_out, rid_shapes = _rider_specs(riders, lambda b, j: b * n_steps + j)
    return pl.pallas_call(
        functools.partial(_conv_body, n_ctx_tiles, n_tiles, tile0, tuple(k for k, _, _ in riders)),
        grid=(bsz, n_steps),
        in_specs=[pl.BlockSpec((1, tt, 2 * c), lambda b, j: (b, j + tile0, 0)),
                  pl.BlockSpec((1, HALO, 2 * c),
                               lambda b, j: (b, jnp.maximum((j + tile0) * per - 1, 0), 0)),
                  pl.BlockSpec((1, HALO, 2 * c),
                               lambda b, j: (b, jnp.minimum((j + tile0 + 1) * per, n_halo - 1), 0)),
                  _layer_spec(w_dw, layer), _layer_spec(b_dw, layer), _layer_spec(ln_g, layer),
                  _layer_spec(ln_b, layer), _layer_spec(w_pw, layer), _layer_spec(b_pw, layer)] + rid_in,
        out_specs=[pl.BlockSpec((1, tt, c), lambda b, j: (b, j, 0))] + rid_out,
        out_shape=[jax.ShapeDtypeStruct((bsz, t_all - tile0 * tt, c), F32)] + rid_shapes,
        scratch_shapes=[pltpu.VMEM((tt + 2 * HALO, c), F32),
                        pltpu.VMEM((SUBLANES - 1, tt + 2 * HALO - SUBLANES, c), F32),
                        pltpu.VMEM((tt, c), BF16)],
        compiler_params=_params(("arbitrary", "arbitrary") if riders else ("parallel", "parallel")),
        name="conv_mixer",
    )(conv_in, conv_in, conv_in, w_dw, b_dw, ln_g, ln_b, w_pw, b_pw, *rid_args)


def _chunk_cumsum(la, tri_ref):
    w = la.shape[1]
    hi = la.astype(BF16)
    lo = (la - hi.astype(F32)).astype(BF16)
    both = jnp.dot(tri_ref[...], jnp.concatenate([hi, lo], axis=1), preferred_element_type=F32)
    return both[:, :w] + both[:, w:]


def _decay_factors(b, reverse):
    b_tot = b[0:1, :] if reverse else b[CHUNK - 1:CHUNK, :]
    dk_all = b.shape[1]
    dec_cols = jnp.broadcast_to(jnp.exp(b_tot), (dk_all, dk_all)).T
    dec_cols = jnp.concatenate([dec_cols] * (GROUP_W // dk_all), axis=1)
    return jnp.exp(b), jnp.exp(-b), jnp.exp(b_tot - b), dec_cols


def _scan_chunk_head(q, k, v, factors, st, key_mask, value_mask, head_mask):
    nt_dims = (((1,), (1,)), ((), ()))
    e_q, e_inv, e_end, dec_cols = factors
    q_dec = (q * e_q).astype(BF16)
    k_inv = (k * e_inv).astype(BF16)
    k_end = (k * e_end).astype(BF16)
    vb = v.astype(BF16)
    k_blk = jnp.concatenate([k_inv] * N_HEADS, axis=0) * key_mask
    scores = lax.dot_general(q_dec, k_blk, nt_dims, preferred_element_type=F32)
    o_inter = jnp.dot(q_dec, st.astype(BF16), preferred_element_type=F32)
    kv = lax.dot_general(k_end, vb, (((0,), (0,)), ((), ())), preferred_element_type=F32)
    st = st * dec_cols + jnp.where(head_mask, kv, 0.0)
    v_blk = jnp.concatenate([vb] * N_HEADS, axis=0) * value_mask
    return (scores, v_blk, o_inter), st


def _scan_chunk_tail(pending, tri):
    scores, v_blk, o_inter = pending
    scores = jnp.where(tri, scores, 0.0).astype(BF16)
    return jnp.dot(scores, v_blk, preferred_element_type=F32) + o_inter


def _scan_body(gf, gb, rf, rb, waf, baf, wab, bab, lg, cosf, sinf, cosb, sinb,
               key_mask_ref, value_mask_ref, cum_f_ref, cum_b_ref,
               gof, gob, rof, rob, st_all, b_all):
    n_batch = gf.shape[0]

    @pl.when(pl.program_id(1) == 0)
    def _():
        st_all[...] = jnp.zeros(st_all.shape, F32)

    tb = gf.shape[1]
    dk_all = N_HEADS * SCAN_DK
    dv_all = GROUP_W
    scale = SCAN_DK ** -0.5
    key_mask = key_mask_ref[...]
    value_mask = value_mask_ref[...]
    r = lax.broadcasted_iota(jnp.int32, (dk_all, dv_all), 0)
    c = lax.broadcasted_iota(jnp.int32, (dk_all, dv_all), 1)
    head_mask = (r // SCAN_DK) == (c // HEAD_DV)
    r = lax.broadcasted_iota(jnp.int32, (CHUNK, dv_all), 0)
    c = lax.broadcasted_iota(jnp.int32, (CHUNK, dv_all), 1) % CHUNK
    tri_f = c <= r
    tri_b = c >= r

    def qkv(ref, bi, lo):
        return (ref[bi, lo:lo + CHUNK, 0:dk_all], ref[bi, lo:lo + CHUNK, dk_all:2 * dk_all],
                ref[bi, lo:lo + CHUNK, 2 * dk_all:2 * dk_all + dv_all])

    def gla_chain(ref, bi, w_ref, bias_ref, cum_ref, reverse):
        slot = int(reverse)
        z = ref[bi, :, 2 * dk_all + 2 * dv_all:]
        pre = jnp.dot(z.astype(BF16), w_ref[...], preferred_element_type=F32) + bias_ref[...]
        log_sig = jnp.minimum(pre, 0.0) - jnp.log1p(jnp.exp(-jnp.abs(pre)))
        b_all[bi, slot] = _chunk_cumsum(log_sig / GLA_TAU, cum_ref)

        def load(lo):
            q, k, v = qkv(ref, bi, lo)
            return q * scale, k, v, _decay_factors(b_all[bi, slot, lo:lo + CHUNK, :], reverse)
        return load

    lane = lax.broadcasted_iota(jnp.int32, (CHUNK, dk_all), 1)
    first_half = (lane % SCAN_DK) < SCAN_DK // 2
    pos = lax.broadcasted_iota(jnp.int32, (CHUNK, dk_all), 0).astype(F32)

    ret_factors = [_decay_factors(((CHUNK - pos) if reverse else (pos + 1.0)) * lg[...], reverse)
                   for reverse in (False, True)]

    def ret_chain(ref, bi, cos_ref, sin_ref, reverse):
        def rope(t, lo):
            rot = jnp.where(first_half, pltpu.roll(t, dk_all - SCAN_DK // 2, 1),
                            pltpu.roll(t, SCAN_DK // 2, 1))
            return t * cos_ref[lo:lo + CHUNK, :] + rot * sin_ref[lo:lo + CHUNK, :]

        def load(lo):
            q, k, v = qkv(ref, bi, lo)
            return rope(q, lo), rope(k * scale, lo), v, ret_factors[int(reverse)]
        return load

    chains = []
    for bi in range(n_batch):
        chains += [
            (gla_chain(gf, bi, waf, baf, cum_f_ref, False), False, gof, bi, tri_f),
            (gla_chain(gb, bi, wab, bab, cum_b_ref, True), True, gob, bi, tri_b),
            (ret_chain(rf, bi, cosf, sinf, False), False, rof, bi, tri_f),
            (ret_chain(rb, bi, cosb, sinb, True), True, rob, bi, tri_b),
        ]
    st_vals = [st_all[n] for n in range(len(chains))]
    n_chunks = tb // CHUNK

    def heads(i):
        pending = []
        for n, (load, reverse, _, _, _) in enumerate(chains):
            lo = (n_chunks - 1 - i if reverse else i) * CHUNK
            q, k, v, factors = load(lo)
            part, st_vals[n] = _scan_chunk_head(q, k, v, factors, st_vals[n],
                                                key_mask, value_mask, head_mask)
            pending.append((lo, part))
        return pending

    pending = heads(0)
    for i in range(n_chunks):
        following = heads(i + 1) if i + 1 < n_chunks else None
        for (lo, part), (_, _, out_ref, bi, tri) in zip(pending, chains):
            out_ref[bi, lo:lo + CHUNK, :] = _scan_chunk_tail(part, tri)
        pending = following
    for n, val in enumerate(st_vals):
        st_all[n] = val


def _scan_mixers(layer, gla_in, ret_in, n_ctx_tiles, w_a_f, b_a_f, w_a_b, b_a_b, log_gamma, cos, sin,
                 key_mask, value_mask, cum_f, cum_b):
    bsz, t_all, _ = gla_in.shape
    tb = ROW_TILE
    n_tiles = t_all // tb
    dk_all = N_HEADS * SCAN_DK
    dv_all = GROUP_W

    def bwd(j):
        return jnp.where(j < n_ctx_tiles, n_ctx_tiles - 1 - j, n_tiles - 1 - (j - n_ctx_tiles))

    nb = SCAN_BATCH if bsz % SCAN_BATCH == 0 else 1
    fwd_blk = lambda b, j: (b, j, 0)
    bwd_blk = lambda b, j: (b, bwd(j), 0)
    in_specs = [pl.BlockSpec((nb, tb, W_GLA), fwd_blk), pl.BlockSpec((nb, tb, W_GLA), bwd_blk),
                pl.BlockSpec((nb, tb, W_RET), fwd_blk), pl.BlockSpec((nb, tb, W_RET), bwd_blk),
                _layer_spec(w_a_f, layer), _layer_spec(b_a_f, layer),
                _layer_spec(w_a_b, layer), _layer_spec(b_a_b, layer),
                _const_spec(log_gamma),
                pl.BlockSpec((tb, dk_all), lambda b, j: (j, 0)),
                pl.BlockSpec((tb, dk_all), lambda b, j: (j, 0)),
                pl.BlockSpec((tb, dk_all), lambda b, j: (bwd(j), 0)),
                pl.BlockSpec((tb, dk_all), lambda b, j: (bwd(j), 0)),
                _const_spec(key_mask), _const_spec(value_mask), _const_spec(cum_f), _const_spec(cum_b)]
    return pl.pallas_call(
        _scan_body,
        grid=(bsz // nb, n_tiles),
        in_specs=in_specs,
        out_specs=[pl.BlockSpec((nb, tb, dv_all), fwd_blk), pl.BlockSpec((nb, tb, dv_all), bwd_blk)] * 2,
        out_shape=[jax.ShapeDtypeStruct((bsz, t_all, dv_all), F32)] * 4,
        scratch_shapes=[pltpu.VMEM((4 * nb, dk_all, dv_all), F32), pltpu.VMEM((nb, 2, tb, dk_all), F32)],
        compiler_params=_params(("parallel", "arbitrary")),
        name="scans",
    )(gla_in, gla_in, ret_in, ret_in, w_a_f, b_a_f, w_a_b, b_a_b, log_gamma, cos, sin, cos, sin,
      key_mask, value_mask, cum_f, cum_b)


def _att_body(n_ctx_tiles, n_tiles, tile0, q_ref, k_ref, vt_ref, o_ref):
    group = q_ref.shape[1]
    tq = q_ref.shape[2]
    tk = vt_ref.shape[4]
    nq = group * tq

    def attend(n_blocks):
        q2 = q_ref[0].reshape(nq, LANES)

        def scores(i):
            kb = k_ref[0, i * tk:(i + 1) * tk, :]
            return lax.dot_general(kb, q2, (((1,), (1,)), ((), ())), preferred_element_type=F32)

        m = acc = None
        ahead = [scores(i) for i in range(min(SCORE_LOOKAHEAD, n_blocks))]
        for i in range(n_blocks):
            s = ahead.pop(0)
            if i + SCORE_LOOKAHEAD < n_blocks:
                ahead.append(scores(i + SCORE_LOOKAHEAD))
            blk_max = jnp.max(s, axis=0, keepdims=True)
            m_new = blk_max if i == 0 else jnp.maximum(m, blk_max)
            p = jnp.exp2(s - m_new).astype(BF16)
            pv = jnp.dot(vt_ref[0, 0, i], p, preferred_element_type=F32)
            acc = pv if i == 0 else jnp.exp2(m - m_new) * acc + pv
            m = m_new
        o_t = acc[:HEAD_DV] / acc[HEAD_DV:HEAD_DV + 1]
        stacked = jnp.concatenate([o_t[:, g * tq:(g + 1) * tq] for g in range(group)], axis=0)
        o_ref[0] = stacked.T

    qi = pl.program_id(2) + tile0
    if tile0 < n_ctx_tiles:
        pl.when(qi < n_ctx_tiles)(lambda: attend(n_ctx_tiles))
    pl.when(qi >= n_ctx_tiles)(lambda: attend(n_tiles))


def _attention(q, k, vt, n_ctx_tiles, with_ctx):
    bsz, _, t_all, _ = q.shape
    tq = ROW_TILE
    n_tiles = t_all // tq
    tile0 = 0 if with_ctx else n_ctx_tiles
    group = N_HEADS // KV_HEADS
    return pl.pallas_call(
        functools.partial(_att_body, n_ctx_tiles, n_tiles, tile0),
        grid=(bsz, KV_HEADS, n_tiles - tile0),
        in_specs=[pl.BlockSpec((1, group, tq, LANES), lambda b, h, i: (b, h, i + tile0, 0)),
                  pl.BlockSpec((1, t_all, LANES), lambda b, h, i: (b, 0, 0)),
                  pl.BlockSpec((1, 1, n_tiles, VT_ROWS, tq), lambda b, h, i: (b, h, 0, 0, 0))],
        out_specs=pl.BlockSpec((1, tq, group * HEAD_DV), lambda b, h, i: (b, i, h)),
        out_shape=jax.ShapeDtypeStruct((bsz, t_all - tile0 * tq, GROUP_W), F32),
        compiler_params=_params(("parallel", "parallel", "arbitrary")),
        name="attention",
    )(q, k, vt)


def _post_body(final, split, n_ctx_tiles, tile0, rider_kinds, *refs):
    (yconv, gla_f, gla_b, gla_gate, yatt, ret_f, ret_b, ret_gate, gla_gain, ret_gain, ones_ref) = refs[:11]
    n_src = 2 if split else 1
    n_rid = len(rider_kinds)
    n_rid_out = sum(3 if kind == "w_in" else 1 for kind in rider_kinds)
    lat_ref = refs[11]
    ctx_ref = refs[12] if split else None
    n_in = 17 + n_src
    mod_ref, n2g_ref, wout_ref, wup_ref, wdown_ref, fng_ref = refs[11 + n_src:n_in]
    o_ref = refs[n_in + n_rid]
    rider_out = refs[n_in + n_rid + 1:n_in + n_rid + 1 + n_rid_out]
    xbuf = refs[n_in + n_rid + 1 + n_rid_out] if split else None
    _run_riders(rider_kinds, refs[n_in:n_in + n_rid], rider_out)
    tm, d = lat_ref.shape[1], lat_ref.shape[2]
    d_ff = wup_ref.shape[1]

    def finish(of_ref, ob_ref, gate_ref, gain_ref):
        o = of_ref[0] + ob_ref[0]
        ss = _group_sumsq(o, ones_ref[...])
        y = o * lax.rsqrt(ss * (1.0 / HEAD_DV) + EPS) * gain_ref[...]
        return (y * _silu(gate_ref[0])).astype(BF16)

    mixed = jnp.concatenate([yconv[0].astype(BF16), finish(gla_f, gla_b, gla_gate, gla_gain),
                             yatt[0].astype(BF16), finish(ret_f, ret_b, ret_gate, ret_gain)], axis=1)
    o = jnp.dot(mixed, wout_ref[...], preferred_element_type=F32)
    gate1 = mod_ref[:, 2 * d:3 * d]
    shift2 = mod_ref[:, 3 * d:4 * d]
    scale2 = mod_ref[:, 4 * d:5 * d]
    gate2 = mod_ref[:, 5 * d:6 * d]
    x = _load_rows(split, pl.program_id(1) + tile0 < n_ctx_tiles, lat_ref, ctx_ref, xbuf)
    x_mid = x + gate1 * o
    h2 = ((_rms(x_mid) * n2g_ref[...]) * (1.0 + scale2) + shift2).astype(BF16)
    acc = jnp.zeros((tm, d), F32)
    tf = 1024
    for f in range(0, d_ff, tf):
        u = jnp.dot(h2, wup_ref[:, f:f + tf], preferred_element_type=F32)
        a = jnp.square(jnp.maximum(u, 0.0)).astype(BF16)
        acc = acc + jnp.dot(a, wdown_ref[f:f + tf, :], preferred_element_type=F32)
    x_out = x_mid + gate2 * acc
    if final:
        x_out = _rms(x_out) * fng_ref[...]
    o_ref[0] = x_out


def _post(layer, final, yconv, gla_of, gla_ob, gla_in, yatt, ret_of, ret_ob, ret_in, gla_gain, ret_gain,
          ones_blockdiag, x_lat, x_ctx, mods, norm2_g, w_layer, w_out, w_up, w_down, final_g, n_ctx_tiles,
          with_ctx, riders=()):
    split = x_ctx is not None
    bsz, _, d = x_lat.shape
    tm = ROW_TILE
    t_all = gla_of.shape[1]
    ctx_row = bsz
    tile0 = 0 if with_ctx else n_ctx_tiles
    n_out = t_all // tm - tile0
    c = GROUP_W
    gate_blk = (2 * N_HEADS * SCAN_DK + c) // c
    full = lambda b, j: (b, j + tile0, 0)
    own = lambda b, j: (b, j, 0)
    assert not split or tile0 == 0
    row_specs, row_args = _row_specs(split, x_lat, x_ctx, tm, n_ctx_tiles, tile0)

    def resident(arr):
        rest = arr.shape[1:]
        return pl.BlockSpec((None,) + rest, lambda *_: (w_layer,) + (0,) * len(rest),
                            pipeline_mode=pl.Buffered(1))

    assert not riders or bsz * n_out >= RIDER_CHUNKS
    rid_in, rid_args, rid_out, rid_shapes = _rider_specs(riders, lambda b, j: b * n_out + j)
    return pl.pallas_call(
        functools.partial(_post_body, final, split, n_ctx_tiles, tile0, tuple(k for k, _, _ in riders)),
        grid=(bsz, n_out),
        in_specs=[pl.BlockSpec((1, tm, c), own),
                  pl.BlockSpec((1, tm, c), full), pl.BlockSpec((1, tm, c), full),
                  pl.BlockSpec((1, tm, c), lambda b, j: (b, j + tile0, gate_blk)),
                  pl.BlockSpec((1, tm, c), own),
                  pl.BlockSpec((1, tm, c), full), pl.BlockSpec((1, tm, c), full),
                  pl.BlockSpec((1, tm, c), lambda b, j: (b, j + tile0, gate_blk)),
                  _layer_spec(gla_gain, layer), _layer_spec(ret_gain, layer),
                  _const_spec(ones_blockdiag)] + row_specs + [
                  pl.BlockSpec((None, None, 1, N_MOD * d),
                               lambda b, j: (layer, jnp.where(j + tile0 < n_ctx_tiles, ctx_row, b), 0, 0)),
                  _layer_spec(norm2_g, layer),
                  resident(w_out), resident(w_up), resident(w_down),
                  _const_spec(final_g)] + rid_in,
        out_specs=[pl.BlockSpec((1, tm, d), own)] + rid_out,
        out_shape=[jax.ShapeDtypeStruct((bsz, n_out * tm, d), F32)] + rid_shapes,
        scratch_shapes=[pltpu.VMEM((tm, d), F32)] if split else [],
        compiler_params=_params(("arbitrary", "arbitrary") if riders else ("parallel", "parallel")),
        name="post",
    )(yconv, gla_of, gla_ob, gla_in, yatt, ret_of, ret_ob, ret_in, gla_gain, ret_gain,
      ones_blockdiag, *row_args, mods, norm2_g, w_out, w_up, w_down, final_g, *rid_args)


def _rope_table(t_lat, n_ctx, head_dim, width):
    n_ax = head_dim // 4
    inv = np.float32(ROPE_THETA) ** (-np.arange(n_ax, dtype=np.float32) / np.float32(n_ax))
    pos = np.arange(t_lat)
    ang = np.concatenate([(pos // GRID_W).astype(np.float32)[:, None] * inv,
                          (pos % GRID_W).astype(np.float32)[:, None] * inv], axis=-1).astype(np.float32)
    cos, sin = np.cos(ang), np.sin(ang)
    cos = np.concatenate([np.ones((n_ctx, head_dim // 2), np.float32), cos], axis=0)
    sin = np.concatenate([np.zeros((n_ctx, head_dim // 2), np.float32), sin], axis=0)
    reps = width // head_dim
    return (jnp.asarray(np.tile(np.concatenate([cos, cos], axis=-1), (1, reps)), F32),
            jnp.asarray(np.tile(np.concatenate([-sin, sin], axis=-1), (1, reps)), F32))


def kernel(x, c, ctx, c_ctx, w_mod, b_mod, norm1_g, norm2_g, w_in, conv_w_dw, conv_b_dw, conv_ln_g,
           conv_ln_b, conv_w_pw, conv_b_pw, gla_w_a_f, gla_b_a_f, gla_w_a_b, gla_b_a_b, gla_norm_g,
           att_q_norm_g, att_k_norm_g, ret_norm_g, w_out, w_up, w_down, final_norm_g):
    bsz, t_lat, d = x.shape
    n_ctx = ctx.shape[1]
    depth = w_mod.shape[0]
    assert bsz + 1 <= 8 and n_ctx % ROW_TILE == 0 and t_lat % ROW_TILE == 0
    n_ctx_tiles = n_ctx // ROW_TILE
    dk_all = N_HEADS * SCAN_DK

    cc = jnp.concatenate([c, c_ctx[None, :], jnp.zeros((8 - bsz - 1, d), F32)], axis=0)
    mods = _modulation(cc, w_mod, b_mod)
    mods = mods.reshape(depth, 8, 1, N_MOD * d)

    stack_rows = lambda v: v.reshape(depth, 1, -1)
    in_w = (w_in[:1, :, :W_IN_A].astype(BF16),
            jnp.pad(w_in[:1, :, W_IN_A:W_IN_A + GLA_RANK].astype(BF16),
                    ((0, 0), (0, 0), (0, LANES - GLA_RANK))),
            w_in[:1, :, W_IN_A + GLA_RANK:].astype(BF16))
    pad_rank = lambda w: jnp.pad(w, ((0, 0), (0, LANES - GLA_RANK), (0, 0))).astype(BF16)
    w_a_f, w_a_b = pad_rank(gla_w_a_f), pad_rank(gla_w_a_b)
    w_pw = conv_w_pw.astype(BF16)
    q_gain = stack_rows(jnp.tile(att_q_norm_g, (1, N_HEADS)))
    k_gain = stack_rows(jnp.tile(att_k_norm_g, (1, KV_HEADS)))

    cos_att, sin_att = _rope_table(t_lat, n_ctx, HEAD_DV, N_HEADS * HEAD_DV)
    cos_ret, sin_ret = _rope_table(t_lat, n_ctx, SCAN_DK, dk_all)
    gamma = 1.0 - np.exp2(-5.0 - np.arange(N_HEADS, dtype=np.float64))
    log_gamma = jnp.asarray(np.repeat(np.log(gamma), SCAN_DK)[None, :], F32)
    gid = np.arange(GROUP_W) // HEAD_DV
    ones_blockdiag = jnp.asarray(gid[:, None] == gid[None, :], BF16)
    key_mask = jnp.asarray(gid[:, None] == (np.arange(dk_all) // SCAN_DK)[None, :], BF16)
    pos = np.arange(ROW_TILE)
    same_chunk = (pos[:, None] // CHUNK) == (pos[None, :] // CHUNK)
    cum_f = jnp.asarray(same_chunk & (pos[None, :] <= pos[:, None]), BF16)
    cum_b = jnp.asarray(same_chunk & (pos[None, :] >= pos[:, None]), BF16)

    x_lat, x_ctx = x, ctx
    for l in range(depth):
        with_ctx = l < depth - 1
        final = l == depth - 1
        conv_in, gla_in, ret_in, q, k, vt = _in_proj(
            l, x_lat, x_ctx, mods, stack_rows(norm1_g), 0, *in_w, cos_att, sin_att, q_gain, k_gain,
            ones_blockdiag, n_ctx_tiles)
        conv_riders = [("plain", w, 0) for w in (w_out, w_up, w_down)] if l == 0 else []
        y_conv, *converted = _conv_mixer(
            l, conv_in, conv_w_dw, stack_rows(conv_b_dw), stack_rows(conv_ln_g), stack_rows(conv_ln_b),
            w_pw, stack_rows(conv_b_pw), n_ctx_tiles, with_ctx, conv_riders)
        if l == 0:
            post_w = tuple(w[None] for w in converted)
        gla_of, gla_ob, ret_of, ret_ob = _scan_mixers(
            l, gla_in, ret_in, n_ctx_tiles, w_a_f, stack_rows(gla_b_a_f), w_a_b, stack_rows(gla_b_a_b),
            log_gamma, cos_ret, sin_ret, key_mask, ones_blockdiag, cum_f, cum_b)
        y_att = _attention(q, k, vt, n_ctx_tiles, with_ctx)
        post_riders = [] if final else [("w_in", w_in, l + 1)] + [("plain", w, l + 1)
                                                                  for w in (w_out, w_up, w_down)]
        x_lat, *converted = _post(
            l, final, y_conv, gla_of, gla_ob, gla_in, y_att, ret_of, ret_ob, ret_in,
            stack_rows(gla_norm_g), stack_rows(ret_norm_g), ones_blockdiag, x_lat, x_ctx, mods,
            stack_rows(norm2_g), 0, *post_w, final_norm_g.reshape(1, -1), n_ctx_tiles, with_ctx,
            post_riders)
        if not final:
            in_w = tuple(w[None] for w in converted[:3])
            post_w = tuple(w[None] for w in converted[3:])
        x_ctx = None
    return x_lat
```

```python
import functools

import numpy as np
import jax
import jax.numpy as jnp
from jax import lax
from jax.experimental import pallas as pl
from jax.experimental.pallas import tpu as pltpu

F32 = jnp.float32
BF16 = jnp.bfloat16

EPS = 1e-6
GRID_W = 64
ROPE_THETA = 10000.0
N_MOD = 6
CONV_KSIZE = 31
CONV_PAD = (CONV_KSIZE - 1) // 2
HALO = 16
GLA_TAU = 16.0
GLA_RANK = 16
CHUNK = 64
N_HEADS = 4
KV_HEADS = 2
HEAD_DV = 64
SCAN_DK = 32
GROUP_W = 256
ROW_TILE = 256
LANES = 128
SUBLANES = 8
BF16_SUBLANES = 16
VT_ROWS = HEAD_DV + BF16_SUBLANES
LOG2_E = 1.4426950408889634
RIDER_CHUNKS = 64
MIXER_BATCH = 2
SCAN_BATCH = 4
SCORE_LOOKAHEAD = 2
V7X_VMEM_BYTES = 64 * 1024 * 1024
VMEM_LIMIT = V7X_VMEM_BYTES * 3 // 4

W_CONV = 2 * GROUP_W
W_GLA = 2 * N_HEADS * SCAN_DK + 2 * GROUP_W + LANES
W_ATT = GROUP_W + 2 * KV_HEADS * HEAD_DV
W_RET = 2 * N_HEADS * SCAN_DK + 2 * GROUP_W
W_IN_A = W_CONV + W_GLA - LANES
W_IN_B = W_ATT + W_RET


def _params(semantics):
    return pltpu.CompilerParams(dimension_semantics=semantics, vmem_limit_bytes=VMEM_LIMIT)


def _layer_spec(arr, layer):
    rest = arr.shape[1:]
    return pl.BlockSpec((None,) + rest, lambda *_: (layer,) + (0,) * len(rest))


def _const_spec(arr):
    return pl.BlockSpec(arr.shape, lambda *_: (0,) * arr.ndim)


def _rider_specs(items, step_of, n_steps):
    in_specs, args, out_specs, out_shapes = [], [], [], []
    n_chunks = 1
    while 2 * n_chunks <= min(n_steps, RIDER_CHUNKS):
        n_chunks *= 2
    chunk = lambda *g: jnp.minimum(step_of(*g), n_chunks - 1)
    for kind, arr, layer in items:
        _, rows, cols = arr.shape
        rpc = rows // n_chunks
        assert rows % n_chunks == 0 and rpc % BF16_SUBLANES == 0
        in_specs.append(pl.BlockSpec((None, rpc, cols), lambda *g, layer=layer: (layer, chunk(*g), 0)))
        args.append(arr)
        for w in ((W_IN_A, LANES, W_IN_B) if kind == "w_in" else (cols,)):
            out_specs.append(pl.BlockSpec((rpc, w), lambda *g: (chunk(*g), 0)))
            out_shapes.append(jax.ShapeDtypeStruct((rows, w), BF16))
    return in_specs, args, out_specs, out_shapes


def _run_riders(kinds, in_refs, out_refs):
    out_refs = list(out_refs)
    for kind, ref in zip(kinds, in_refs):
        w = ref[...]
        if kind == "w_in":
            z = w[:, W_IN_A:W_IN_A + GLA_RANK]
            pad = jnp.zeros((w.shape[0], LANES - GLA_RANK), F32)
            out_refs.pop(0)[...] = w[:, :W_IN_A].astype(BF16)
            out_refs.pop(0)[...] = jnp.concatenate([z, pad], axis=1).astype(BF16)
            out_refs.pop(0)[...] = w[:, W_IN_A + GLA_RANK:].astype(BF16)
        else:
            out_refs.pop(0)[...] = w.astype(BF16)


def _silu(x):
    return x * jax.nn.sigmoid(x)


def _rms(x):
    return x * lax.rsqrt(jnp.mean(x * x, axis=-1, keepdims=True) + EPS)


def _group_sumsq(t, ones_blockdiag):
    t2 = t * t
    hi = t2.astype(BF16)
    lo = (t2 - hi.astype(F32)).astype(BF16)
    return (jnp.dot(hi, ones_blockdiag, preferred_element_type=F32)
            + jnp.dot(lo, ones_blockdiag, preferred_element_type=F32))


def _load_rows(split, is_ctx, lat_ref, ctx_ref, buf):
    if not split:
        return lat_ref[0]

    @pl.when(is_ctx)
    def _():
        buf[...] = ctx_ref[0]

    @pl.when(jnp.logical_not(is_ctx))
    def _():
        buf[...] = lat_ref[0]

    return buf[...]


def _row_specs(split, x_lat, x_ctx, tm, n_ctx_tiles, tile0):
    d = x_lat.shape[2]
    if not split:
        return [pl.BlockSpec((1, tm, d), lambda b, j: (b, j + tile0, 0))], [x_lat]
    return ([pl.BlockSpec((1, tm, d), lambda b, j: (b, jnp.maximum(j - n_ctx_tiles, 0), 0)),
             pl.BlockSpec((1, tm, d), lambda b, j: (b, jnp.minimum(j, n_ctx_tiles - 1), 0))],
            [x_lat, x_ctx])


def _mod_body(cc_ref, w_ref, b_ref, o_ref):
    s = _silu(cc_ref[...])
    o_ref[0] = jnp.dot(s.astype(BF16), w_ref[0].astype(BF16),
                       preferred_element_type=F32) + b_ref[0]


def _modulation(cc, w_mod, b_mod):
    depth, d, n = w_mod.shape
    tn = 1024
    return pl.pallas_call(
        _mod_body,
        grid=(depth, n // tn),
        in_specs=[pl.BlockSpec((8, d), lambda l, i: (0, 0)),
                  pl.BlockSpec((1, d, tn), lambda l, i: (l, 0, i)),
                  pl.BlockSpec((1, 1, tn), lambda l, i: (l, 0, i))],
        out_specs=pl.BlockSpec((1, 8, tn), lambda l, i: (l, 0, i)),
        out_shape=jax.ShapeDtypeStruct((depth, 8, n), F32),
        compiler_params=_params(("parallel", "parallel")),
        name="modulation",
    )(cc, w_mod, b_mod.reshape(depth, 1, n))


def _inproj_body(split, n_ctx_tiles, *refs):
    n_src = 2 if split else 1
    lat_ref = refs[0]
    ctx_ref = refs[1] if split else None
    (mod_ref, g_ref, wa_ref, wz_ref, wb_ref, cos_ref, sin_ref, qg_ref, kg_ref, ones_ref,
     conv_ref, gla_ref, ret_ref, q_out, k_out, vt_out) = refs[n_src:n_src + 16]
    xbuf = refs[n_src + 16] if split else None
    tm, d = lat_ref.shape[1], lat_ref.shape[2]

    x = _load_rows(split, pl.program_id(1) < n_ctx_tiles, lat_ref, ctx_ref, xbuf)
    shift = mod_ref[:, 0:d]
    scale = mod_ref[:, d:2 * d]
    hb = ((_rms(x) * g_ref[...]) * (1.0 + scale) + shift).astype(BF16)

    def proj(w_ref, lo, width):
        return jnp.dot(hb, w_ref[:, lo:lo + width], preferred_element_type=F32)

    wq = N_HEADS * HEAD_DV
    wk = KV_HEADS * HEAD_DV
    att = proj(wb_ref, 0, W_ATT)
    q = att[:, :wq]
    k = att[:, wq:wq + wk]
    v = att[:, wq + wk:]
    ss_q = _group_sumsq(q, ones_ref[...])
    ss_k = _group_sumsq(k, ones_ref[:wk, :wk])
    gla_ref[0, :, W_GLA - LANES:] = proj(wz_ref, 0, LANES)
    ret_ref[0] = proj(wb_ref, W_ATT, W_RET)
    gla_ref[0, :, 0:W_GLA - LANES] = proj(wa_ref, W_CONV, W_GLA - LANES)
    conv_ref[0] = proj(wa_ref, 0, W_CONV)

    def norm_rope(t, ss, gain, w):
        t = t * lax.rsqrt(ss * (1.0 / HEAD_DV) + EPS) * gain
        lane = lax.broadcasted_iota(jnp.int32, t.shape, 1)
        first_half = (lane % HEAD_DV) < HEAD_DV // 2
        rot = jnp.where(first_half, pltpu.roll(t, w - HEAD_DV // 2, 1), pltpu.roll(t, HEAD_DV // 2, 1))
        return t * cos_ref[:, :w] + rot * sin_ref[:, :w]

    qn = norm_rope(q, ss_q, qg_ref[...], wq) * (HEAD_DV ** -0.5 * LOG2_E)
    kn = norm_rope(k, ss_k, kg_ref[...], wk)
    low = lax.broadcasted_iota(jnp.int32, (tm, LANES), 1) < HEAD_DV
    pair0 = qn[:, :LANES]
    pair1 = qn[:, LANES:]
    q_out[0, 0] = jnp.where(low, pair0, 0.0).astype(BF16)
    q_out[0, 1] = jnp.where(low, pltpu.roll(pair0, HEAD_DV, 1), 0.0).astype(BF16)
    q_out[0, 2] = jnp.where(low, 0.0, pltpu.roll(pair1, HEAD_DV, 1)).astype(BF16)
    q_out[0, 3] = jnp.where(low, 0.0, pair1).astype(BF16)
    k_out[0] = kn.astype(BF16)
    vt = v.T.astype(BF16)
    ones = jnp.ones((VT_ROWS - HEAD_DV, tm), BF16)
    for h in range(KV_HEADS):
        vt_out[0, h, 0, 0:HEAD_DV, :] = vt[h * HEAD_DV:(h + 1) * HEAD_DV, :]
        vt_out[0, h, 0, HEAD_DV:, :] = ones


def _in_proj(layer, x_lat, x_ctx, mods, norm1_g, w_layer, w_a, w_z, w_b, cos, sin, q_gain, k_gain,
             ones_blockdiag, n_ctx_tiles):
    split = x_ctx is not None
    bsz, _, d = x_lat.shape
    tm = ROW_TILE
    t_all = x_lat.shape[1] + (x_ctx.shape[1] if split else 0)
    n_tiles = t_all // tm
    ctx_row = bsz
    wq = N_HEADS * HEAD_DV
    row_specs, row_args = _row_specs(split, x_lat, x_ctx, tm, n_ctx_tiles, 0)
    tile = lambda b, j: (b, j, 0)
    return pl.pallas_call(
        functools.partial(_inproj_body, split, n_ctx_tiles),
        grid=(bsz, n_tiles),
        in_specs=row_specs + [
            pl.BlockSpec((None, None, 1, N_MOD * d),
                         lambda b, j: (layer, jnp.where(j < n_ctx_tiles, ctx_row, b), 0, 0)),
            _layer_spec(norm1_g, layer), _layer_spec(w_a, w_layer), _layer_spec(w_z, w_layer),
            _layer_spec(w_b, w_layer),
            pl.BlockSpec((tm, wq), lambda b, j: (j, 0)), pl.BlockSpec((tm, wq), lambda b, j: (j, 0)),
            _layer_spec(q_gain, layer), _layer_spec(k_gain, layer), _const_spec(ones_blockdiag)],
        out_specs=[pl.BlockSpec((1, tm, W_CONV), tile), pl.BlockSpec((1, tm, W_GLA), tile),
                   pl.BlockSpec((1, tm, W_RET), tile),
                   pl.BlockSpec((1, N_HEADS, tm, LANES), lambda b, j: (b, 0, j, 0)),
                   pl.BlockSpec((1, tm, LANES), tile),
                   pl.BlockSpec((1, KV_HEADS, 1, VT_ROWS, tm), lambda b, j: (b, 0, j, 0, 0))],
        out_shape=[jax.ShapeDtypeStruct((bsz, t_all, W_CONV), F32),
                   jax.ShapeDtypeStruct((bsz, t_all, W_GLA), F32),
                   jax.ShapeDtypeStruct((bsz, t_all, W_RET), F32),
                   jax.ShapeDtypeStruct((bsz, N_HEADS, t_all, LANES), BF16),
                   jax.ShapeDtypeStruct((bsz, t_all, LANES), BF16),
                   jax.ShapeDtypeStruct((bsz, KV_HEADS, n_tiles, VT_ROWS, tm), BF16)],
        scratch_shapes=[pltpu.VMEM((tm, d), F32)] if split else [],
        compiler_params=_params(("parallel", "parallel")),
        name="in_proj",
    )(*row_args, mods, norm1_g, w_a, w_z, w_b, cos, sin, q_gain, k_gain, ones_blockdiag)


def _conv_body(n_ctx_tiles, n_tiles, tile0, rider_kinds, *refs):
    n_rid = len(rider_kinds)
    (main_ref, prev_ref, next_ref, wdw_ref, bdw_ref, lng_ref, lnb_ref, wpw_ref, bpw_ref) = refs[:9]
    o_ref = refs[9 + n_rid]
    ubuf, shifted, sbuf = refs[-3:]
    _run_riders(rider_kinds, refs[9:9 + n_rid], refs[10 + n_rid:-3])
    tt = main_ref.shape[1]
    c = o_ref.shape[2]
    j = pl.program_id(1) + tile0

    def glu(blk):
        return blk[:, :c] * jax.nn.sigmoid(blk[:, c:])

    has_prev = jnp.logical_and(j != 0, j != n_ctx_tiles)
    has_next = jnp.logical_and(j != n_ctx_tiles - 1, j != n_tiles - 1)

    @pl.when(has_prev)
    def _():
        ubuf[0:HALO, :] = glu(prev_ref[0])

    @pl.when(jnp.logical_not(has_prev))
    def _():
        ubuf[0:HALO, :] = jnp.zeros((HALO, c), F32)

    ubuf[HALO:HALO + tt, :] = glu(main_ref[0])

    @pl.when(has_next)
    def _():
        ubuf[HALO + tt:, :] = glu(next_ref[0])

    @pl.when(jnp.logical_not(has_next))
    def _():
        ubuf[HALO + tt:, :] = jnp.zeros((HALO, c), F32)

    span = shifted.shape[1]
    for s in range(1, SUBLANES):
        shifted[s - 1] = ubuf[s:s + span, :]

    rows = 128
    for r in range(0, tt, rows):
        acc = jnp.zeros((rows, c), F32)
        for k in range(CONV_KSIZE):
            off = k + HALO - CONV_PAD
            lo = r + off - off % SUBLANES
            if off % SUBLANES == 0:
                tap = ubuf[lo:lo + rows, :]
            else:
                tap = shifted[off % SUBLANES - 1, lo:lo + rows, :]
            acc = acc + tap * wdw_ref[k:k + 1, :]
        y = acc + bdw_ref[...]
        yc = y - jnp.mean(y, axis=-1, keepdims=True)
        yn = yc * lax.rsqrt(jnp.mean(yc * yc, axis=-1, keepdims=True) + EPS)
        yn = yn * lng_ref[...] + lnb_ref[...]
        sbuf[r:r + rows, :] = _silu(yn).astype(BF16)
    o_ref[0] = jnp.dot(sbuf[...], wpw_ref[...], preferred_element_type=F32) + bpw_ref[...]


def _conv_mixer(layer, conv_in, w_dw, b_dw, ln_g, ln_b, w_pw, b_pw, n_ctx_tiles, with_ctx, riders=()):
    bsz, t_all, _ = conv_in.shape
    tt = ROW_TILE
    c = GROUP_W
    n_tiles = t_all // tt
    tile0 = 0 if with_ctx else n_ctx_tiles
    per = tt // HALO
    n_halo = t_all // HALO
    n_steps = n_tiles - tile0
    rid_in, rid_args, rid_out, rid_shapes = _rider_specs(riders, lambda b, j: b * n_steps + j,
                                                         bsz * n_steps)
    return pl.pallas_call(
        functools.partial(_conv_body, n_ctx_tiles, n_tiles, tile0, tuple(k for k, _, _ in riders)),
        grid=(bsz, n_steps),
        in_specs=[pl.BlockSpec((1, tt, 2 * c), lambda b, j: (b, j + tile0, 0)),
                  pl.BlockSpec((1, HALO, 2 * c),
                               lambda b, j: (b, jnp.maximum((j + tile0) * per - 1, 0), 0)),
                  pl.BlockSpec((1, HALO, 2 * c),
                               lambda b, j: (b, jnp.minimum((j + tile0 + 1) * per, n_halo - 1), 0)),
                  _layer_spec(w_dw, layer), _layer_spec(b_dw, layer), _layer_spec(ln_g, layer),
                  _layer_spec(ln_b, layer), _layer_spec(w_pw, layer), _layer_spec(b_pw, layer)] + rid_in,
        out_specs=[pl.BlockSpec((1, tt, c), lambda b, j: (b, j, 0))] + rid_out,
        out_shape=[jax.ShapeDtypeStruct((bsz, t_all - tile0 * tt, c), F32)] + rid_shapes,
        scratch_shapes=[pltpu.VMEM((tt + 2 * HALO, c), F32),
                        pltpu.VMEM((SUBLANES - 1, tt + 2 * HALO - SUBLANES, c), F32),
                        pltpu.VMEM((tt, c), BF16)],
        compiler_params=_params(("arbitrary", "arbitrary") if riders else ("parallel", "parallel")),
        name="conv_mixer",
    )(conv_in, conv_in, conv_in, w_dw, b_dw, ln_g, ln_b, w_pw, b_pw, *rid_args)


def _chunk_cumsum(la, tri_ref):
    w = la.shape[1]
    hi = la.astype(BF16)
    lo = (la - hi.astype(F32)).astype(BF16)
    both = jnp.dot(tri_ref[...], jnp.concatenate([hi, lo], axis=1), preferred_element_type=F32)
    return both[:, :w] + both[:, w:]


def _decay_factors(b, reverse):
    b_tot = b[0:1, :] if reverse else b[CHUNK - 1:CHUNK, :]
    dk_all = b.shape[1]
    dec_cols = jnp.broadcast_to(jnp.exp(b_tot), (dk_all, dk_all)).T
    dec_cols = jnp.concatenate([dec_cols] * (GROUP_W // dk_all), axis=1)
    return jnp.exp(b), jnp.exp(-b), jnp.exp(b_tot - b), dec_cols


def _scan_chunk_head(q, k, v, factors, st, key_mask, value_mask, head_mask):
    nt_dims = (((1,), (1,)), ((), ()))
    e_q, e_inv, e_end, dec_cols = factors
    q_dec = (q * e_q).astype(BF16)
    k_inv = (k * e_inv).astype(BF16)
    k_end = (k * e_end).astype(BF16)
    vb = v.astype(BF16)
    k_blk = jnp.concatenate([k_inv] * N_HEADS, axis=0) * key_mask
    scores = lax.dot_general(q_dec, k_blk, nt_dims, preferred_element_type=F32)
    o_inter = jnp.dot(q_dec, st.astype(BF16), preferred_element_type=F32)
    kv = lax.dot_general(k_end, vb, (((0,), (0,)), ((), ())), preferred_element_type=F32)
    st = st * dec_cols + jnp.where(head_mask, kv, 0.0)
    v_blk = jnp.concatenate([vb] * N_HEADS, axis=0) * value_mask
    return (scores, v_blk, o_inter), st


def _scan_chunk_tail(pending, tri):
    scores, v_blk, o_inter = pending
    scores = jnp.where(tri, scores, 0.0).astype(BF16)
    return jnp.dot(scores, v_blk, preferred_element_type=F32) + o_inter


def _scan_body(gf, gb, rf, rb, waf, baf, wab, bab, lg, cosf, sinf, cosb, sinb,
               key_mask_ref, value_mask_ref, cum_f_ref, cum_b_ref,
               gof, gob, rof, rob, st_all, b_all):
    n_batch = gf.shape[0]

    @pl.when(pl.program_id(1) == 0)
    def _():
        st_all[...] = jnp.zeros(st_all.shape, F32)

    tb = gf.shape[1]
    dk_all = N_HEADS * SCAN_DK
    dv_all = GROUP_W
    scale = SCAN_DK ** -0.5
    key_mask = key_mask_ref[...]
    value_mask = value_mask_ref[...]
    r = lax.broadcasted_iota(jnp.int32, (dk_all, dv_all), 0)
    c = lax.broadcasted_iota(jnp.int32, (dk_all, dv_all), 1)
    head_mask = (r // SCAN_DK) == (c // HEAD_DV)
    r = lax.broadcasted_iota(jnp.int32, (CHUNK, dv_all), 0)
    c = lax.broadcasted_iota(jnp.int32, (CHUNK, dv_all), 1) % CHUNK
    tri_f = c <= r
    tri_b = c >= r

    def qkv(ref, bi, lo):
        return (ref[bi, lo:lo + CHUNK, 0:dk_all], ref[bi, lo:lo + CHUNK, dk_all:2 * dk_all],
                ref[bi, lo:lo + CHUNK, 2 * dk_all:2 * dk_all + dv_all])

    def gla_chain(ref, bi, w_ref, bias_ref, cum_ref, reverse):
        slot = int(reverse)
        z = ref[bi, :, 2 * dk_all + 2 * dv_all:]
        pre = jnp.dot(z.astype(BF16), w_ref[...], preferred_element_type=F32) + bias_ref[...]
        log_sig = jnp.minimum(pre, 0.0) - jnp.log1p(jnp.exp(-jnp.abs(pre)))
        b_all[bi, slot] = _chunk_cumsum(log_sig / GLA_TAU, cum_ref)

        def load(lo):
            q, k, v = qkv(ref, bi, lo)
            return q * scale, k, v, _decay_factors(b_all[bi, slot, lo:lo + CHUNK, :], reverse)
        return load

    lane = lax.broadcasted_iota(jnp.int32, (CHUNK, dk_all), 1)
    first_half = (lane % SCAN_DK) < SCAN_DK // 2
    pos = lax.broadcasted_iota(jnp.int32, (CHUNK, dk_all), 0).astype(F32)

    ret_factors = [_decay_factors(((CHUNK - pos) if reverse else (pos + 1.0)) * lg[...], reverse)
                   for reverse in (False, True)]

    def ret_chain(ref, bi, cos_ref, sin_ref, reverse):
        def rope(t, lo):
            rot = jnp.where(first_half, pltpu.roll(t, dk_all - SCAN_DK // 2, 1),
                            pltpu.roll(t, SCAN_DK // 2, 1))
            return t * cos_ref[lo:lo + CHUNK, :] + rot * sin_ref[lo:lo + CHUNK, :]

        def load(lo):
            q, k, v = qkv(ref, bi, lo)
            return rope(q, lo), rope(k * scale, lo), v, ret_factors[int(reverse)]
        return load

    chains = []
    for bi in range(n_batch):
        chains += [
            (gla_chain(gf, bi, waf, baf, cum_f_ref, False), False, gof, bi, tri_f),
            (gla_chain(gb, bi, wab, bab, cum_b_ref, True), True, gob, bi, tri_b),
            (ret_chain(rf, bi, cosf, sinf, False), False, rof, bi, tri_f),
            (ret_chain(rb, bi, cosb, sinb, True), True, rob, bi, tri_b),
        ]
    st_vals = [st_all[n] for n in range(len(chains))]
    n_chunks = tb // CHUNK

    def heads(i):
        pending = []
        for n, (load, reverse, _, _, _) in enumerate(chains):
            lo = (n_chunks - 1 - i if reverse else i) * CHUNK
            q, k, v, factors = load(lo)
            part, st_vals[n] = _scan_chunk_head(q, k, v, factors, st_vals[n],
                                                key_mask, value_mask, head_mask)
            pending.append((lo, part))
        return pending

    pending = heads(0)
    for i in range(n_chunks):
        following = heads(i + 1) if i + 1 < n_chunks else None
        for (lo, part), (_, _, out_ref, bi, tri) in zip(pending, chains):
            out_ref[bi, lo:lo + CHUNK, :] = _scan_chunk_tail(part, tri)
        pending = following
    for n, val in enumerate(st_vals):
        st_all[n] = val


def _scan_mixers(layer, gla_in, ret_in, n_ctx_tiles, w_a_f, b_a_f, w_a_b, b_a_b, log_gamma, cos, sin,
                 key_mask, value_mask, cum_f, cum_b):
    bsz, t_all, _ = gla_in.shape
    tb = ROW_TILE
    n_tiles = t_all // tb
    dk_all = N_HEADS * SCAN_DK
    dv_all = GROUP_W

    def bwd(j):
        return jnp.where(j < n_ctx_tiles, n_ctx_tiles - 1 - j, n_tiles - 1 - (j - n_ctx_tiles))

    nb = SCAN_BATCH if bsz % SCAN_BATCH == 0 else 1
    fwd_blk = lambda b, j: (b, j, 0)
    bwd_blk = lambda b, j: (b, bwd(j), 0)
    in_specs = [pl.BlockSpec((nb, tb, W_GLA), fwd_blk), pl.BlockSpec((nb, tb, W_GLA), bwd_blk),
                pl.BlockSpec((nb, tb, W_RET), fwd_blk), pl.BlockSpec((nb, tb, W_RET), bwd_blk),
                _layer_spec(w_a_f, layer), _layer_spec(b_a_f, layer),
                _layer_spec(w_a_b, layer), _layer_spec(b_a_b, layer),
                _const_spec(log_gamma),
                pl.BlockSpec((tb, dk_all), lambda b, j: (j, 0)),
                pl.BlockSpec((tb, dk_all), lambda b, j: (j, 0)),
                pl.BlockSpec((tb, dk_all), lambda b, j: (bwd(j), 0)),
                pl.BlockSpec((tb, dk_all), lambda b, j: (bwd(j), 0)),
                _const_spec(key_mask), _const_spec(value_mask), _const_spec(cum_f), _const_spec(cum_b)]
    return pl.pallas_call(
        _scan_body,
        grid=(bsz // nb, n_tiles),
        in_specs=in_specs,
        out_specs=[pl.BlockSpec((nb, tb, dv_all), fwd_blk), pl.BlockSpec((nb, tb, dv_all), bwd_blk)] * 2,
        out_shape=[jax.ShapeDtypeStruct((bsz, t_all, dv_all), F32)] * 4,
        scratch_shapes=[pltpu.VMEM((4 * nb, dk_all, dv_all), F32), pltpu.VMEM((nb, 2, tb, dk_all), F32)],
        compiler_params=_params(("parallel", "arbitrary")),
        name="scans",
    )(gla_in, gla_in, ret_in, ret_in, w_a_f, b_a_f, w_a_b, b_a_b, log_gamma, cos, sin, cos, sin,
      key_mask, value_mask, cum_f, cum_b)


def _conv_tiles(j, n_ctx_tiles, n_tiles, main_ref, prev_ref, next_ref, wdw_ref, bdw_ref, lng_ref,
                lnb_ref, wpw_ref, bpw_ref, o_ref, ubuf, shifted, sbuf):
    n_batch, tt = main_ref.shape[0], main_ref.shape[1]
    c = o_ref.shape[2]

    def glu(blk):
        return blk[:, :c] * jax.nn.sigmoid(blk[:, c:])

    has_prev = jnp.logical_and(j != 0, j != n_ctx_tiles)
    has_next = jnp.logical_and(j != n_ctx_tiles - 1, j != n_tiles - 1)

    @pl.when(has_prev)
    def _():
        for bi in range(n_batch):
            ubuf[bi, 0:HALO, :] = glu(prev_ref[bi])

    @pl.when(jnp.logical_not(has_prev))
    def _():
        ubuf[:, 0:HALO, :] = jnp.zeros((n_batch, HALO, c), F32)

    @pl.when(has_next)
    def _():
        for bi in range(n_batch):
            ubuf[bi, HALO + tt:, :] = glu(next_ref[bi])

    @pl.when(jnp.logical_not(has_next))
    def _():
        ubuf[:, HALO + tt:, :] = jnp.zeros((n_batch, HALO, c), F32)

    span = shifted.shape[2]
    rows = 128
    for bi in range(n_batch):
        ubuf[bi, HALO:HALO + tt, :] = glu(main_ref[bi])
        for s in range(1, SUBLANES):
            shifted[bi, s - 1] = ubuf[bi, s:s + span, :]
        for r in range(0, tt, rows):
            acc = jnp.zeros((rows, c), F32)
            for k in range(CONV_KSIZE):
                off = k + HALO - CONV_PAD
                lo = r + off - off % SUBLANES
                if off % SUBLANES == 0:
                    tap = ubuf[bi, lo:lo + rows, :]
                else:
                    tap = shifted[bi, off % SUBLANES - 1, lo:lo + rows, :]
                acc = acc + tap * wdw_ref[k:k + 1, :]
            y = acc + bdw_ref[...]
            yc = y - jnp.mean(y, axis=-1, keepdims=True)
            yn = yc * lax.rsqrt(jnp.mean(yc * yc, axis=-1, keepdims=True) + EPS)
            yn = yn * lng_ref[...] + lnb_ref[...]
            sbuf[bi, r:r + rows, :] = _silu(yn).astype(BF16)
        o_ref[bi] = jnp.dot(sbuf[bi], wpw_ref[...], preferred_element_type=F32) + bpw_ref[...]


def _mixers_body(n_ctx_tiles, n_tiles, rider_kinds, *refs):
    n_rid = len(rider_kinds)
    scan_in, conv_in = refs[:17], refs[17:26]
    rider_in = refs[26:26 + n_rid]
    outs = refs[26 + n_rid:]
    scan_out, conv_out, rider_out = outs[:4], outs[4], outs[5:-5]
    scan_scratch, conv_scratch = outs[-5:-3], outs[-3:]
    _run_riders(rider_kinds, rider_in, rider_out)
    _scan_body(*scan_in, *scan_out, *scan_scratch)
    _conv_tiles(pl.program_id(1), n_ctx_tiles, n_tiles, *conv_in, conv_out, *conv_scratch)


def _mixers(layer, gla_in, ret_in, conv_in, n_ctx_tiles, w_a_f, b_a_f, w_a_b, b_a_b, log_gamma, cos, sin,
            key_mask, value_mask, cum_f, cum_b, w_dw, b_dw, ln_g, ln_b, w_pw, b_pw, riders=()):
    bsz, t_all, _ = gla_in.shape
    tb = ROW_TILE
    n_tiles = t_all // tb
    dk_all = N_HEADS * SCAN_DK
    dv_all = GROUP_W
    c = GROUP_W
    nb = MIXER_BATCH if bsz % MIXER_BATCH == 0 else 1
    per = tb // HALO
    n_halo = t_all // HALO

    def bwd(j):
        return jnp.where(j < n_ctx_tiles, n_ctx_tiles - 1 - j, n_tiles - 1 - (j - n_ctx_tiles))

    fwd_blk = lambda b, j: (b, j, 0)
    bwd_blk = lambda b, j: (b, bwd(j), 0)
    n_steps = (bsz // nb) * n_tiles
    rid_in, rid_args, rid_out, rid_shapes = _rider_specs(riders, lambda b, j: b * n_tiles + j, n_steps)
    in_specs = [pl.BlockSpec((nb, tb, W_GLA), fwd_blk), pl.BlockSpec((nb, tb, W_GLA), bwd_blk),
                pl.BlockSpec((nb, tb, W_RET), fwd_blk), pl.BlockSpec((nb, tb, W_RET), bwd_blk),
                _layer_spec(w_a_f, layer), _layer_spec(b_a_f, layer),
                _layer_spec(w_a_b, layer), _layer_spec(b_a_b, layer),
                _const_spec(log_gamma),
                pl.BlockSpec((tb, dk_all), lambda b, j: (j, 0)),
                pl.BlockSpec((tb, dk_all), lambda b, j: (j, 0)),
                pl.BlockSpec((tb, dk_all), lambda b, j: (bwd(j), 0)),
                pl.BlockSpec((tb, dk_all), lambda b, j: (bwd(j), 0)),
                _const_spec(key_mask), _const_spec(value_mask), _const_spec(cum_f), _const_spec(cum_b),
                pl.BlockSpec((nb, tb, 2 * c), fwd_blk),
                pl.BlockSpec((nb, HALO, 2 * c), lambda b, j: (b, jnp.maximum(j * per - 1, 0), 0)),
                pl.BlockSpec((nb, HALO, 2 * c), lambda b, j: (b, jnp.minimum((j + 1) * per, n_halo - 1), 0)),
                _layer_spec(w_dw, layer), _layer_spec(b_dw, layer), _layer_spec(ln_g, layer),
                _layer_spec(ln_b, layer), _layer_spec(w_pw, layer), _layer_spec(b_pw, layer)] + rid_in
    return pl.pallas_call(
        functools.partial(_mixers_body, n_ctx_tiles, n_tiles, tuple(k for k, _, _ in riders)),
        grid=(bsz // nb, n_tiles),
        in_specs=in_specs,
        out_specs=[pl.BlockSpec((nb, tb, dv_all), fwd_blk), pl.BlockSpec((nb, tb, dv_all), bwd_blk)] * 2
        + [pl.BlockSpec((nb, tb, c), fwd_blk)] + rid_out,
        out_shape=[jax.ShapeDtypeStruct((bsz, t_all, dv_all), F32)] * 5 + rid_shapes,
        scratch_shapes=[pltpu.VMEM((4 * nb, dk_all, dv_all), F32), pltpu.VMEM((nb, 2, tb, dk_all), F32),
                        pltpu.VMEM((nb, tb + 2 * HALO, c), F32),
                        pltpu.VMEM((nb, SUBLANES - 1, tb + 2 * HALO - SUBLANES, c), F32),
                        pltpu.VMEM((nb, tb, c), BF16)],
        compiler_params=_params(("arbitrary", "arbitrary")),
        name="mixers",
    )(gla_in, gla_in, ret_in, ret_in, w_a_f, b_a_f, w_a_b, b_a_b, log_gamma, cos, sin, cos, sin,
      key_mask, value_mask, cum_f, cum_b, conv_in, conv_in, conv_in, w_dw, b_dw, ln_g, ln_b, w_pw, b_pw,
      *rid_args)


def _att_body(n_ctx_tiles, n_tiles, tile0, q_ref, k_ref, vt_ref, o_ref):
    group = q_ref.shape[1]
    tq = q_ref.shape[2]
    tk = vt_ref.shape[4]
    nq = group * tq

    def attend(n_blocks):
        q2 = q_ref[0].reshape(nq, LANES)

        def scores(i):
            kb = k_ref[0, i * tk:(i + 1) * tk, :]
            return lax.dot_general(kb, q2, (((1,), (1,)), ((), ())), preferred_element_type=F32)

        m = acc = None
        ahead = [scores(i) for i in range(min(SCORE_LOOKAHEAD, n_blocks))]
        for i in range(n_blocks):
            s = ahead.pop(0)
            if i + SCORE_LOOKAHEAD < n_blocks:
                ahead.append(scores(i + SCORE_LOOKAHEAD))
            blk_max = jnp.max(s, axis=0, keepdims=True)
            m_new = blk_max if i == 0 else jnp.maximum(m, blk_max)
            p = jnp.exp2(s.astype(BF16) - m_new.astype(BF16))
            pv = jnp.dot(vt_ref[0, 0, i], p, preferred_element_type=F32)
            acc = pv if i == 0 else jnp.exp2(m - m_new) * acc + pv
            m = m_new
        o_t = acc[:HEAD_DV] / acc[HEAD_DV:HEAD_DV + 1]
        stacked = jnp.concatenate([o_t[:, g * tq:(g + 1) * tq] for g in range(group)], axis=0)
        o_ref[0] = stacked.T

    qi = pl.program_id(2) + tile0
    if tile0 < n_ctx_tiles:
        pl.when(qi < n_ctx_tiles)(lambda: attend(n_ctx_tiles))
    pl.when(qi >= n_ctx_tiles)(lambda: attend(n_tiles))


def _attention(q, k, vt, n_ctx_tiles, with_ctx):
    bsz, _, t_all, _ = q.shape
    tq = ROW_TILE
    n_tiles = t_all // tq
    tile0 = 0 if with_ctx else n_ctx_tiles
    group = N_HEADS // KV_HEADS
    return pl.pallas_call(
        functools.partial(_att_body, n_ctx_tiles, n_tiles, tile0),
        grid=(bsz, KV_HEADS, n_tiles - tile0),
        in_specs=[pl.BlockSpec((1, group, tq, LANES), lambda b, h, i: (b, h, i + tile0, 0)),
                  pl.BlockSpec((1, t_all, LANES), lambda b, h, i: (b, 0, 0)),
                  pl.BlockSpec((1, 1, n_tiles, VT_ROWS, tq), lambda b, h, i: (b, h, 0, 0, 0))],
        out_specs=pl.BlockSpec((1, tq, group * HEAD_DV), lambda b, h, i: (b, i, h)),
        out_shape=jax.ShapeDtypeStruct((bsz, t_all - tile0 * tq, GROUP_W), F32),
        compiler_params=_params(("parallel", "parallel", "arbitrary")),
        name="attention",
    )(q, k, vt)


def _post_body(final, split, n_ctx_tiles, tile0, rider_kinds, *refs):
    (yconv, gla_f, gla_b, gla_gate, yatt, ret_f, ret_b, ret_gate, gla_gain, ret_gain, ones_ref) = refs[:11]
    n_src = 2 if split else 1
    n_rid = len(rider_kinds)
    n_rid_out = sum(3 if kind == "w_in" else 1 for kind in rider_kinds)
    lat_ref = refs[11]
    ctx_ref = refs[12] if split else None
    n_in = 17 + n_src
    mod_ref, n2g_ref, wout_ref, wup_ref, wdown_ref, fng_ref = refs[11 + n_src:n_in]
    o_ref = refs[n_in + n_rid]
    rider_out = refs[n_in + n_rid + 1:n_in + n_rid + 1 + n_rid_out]
    xbuf = refs[n_in + n_rid + 1 + n_rid_out] if split else None
    _run_riders(rider_kinds, refs[n_in:n_in + n_rid], rider_out)
    tm, d = lat_ref.shape[1], lat_ref.shape[2]
    d_ff = wup_ref.shape[1]

    def finish(of_ref, ob_ref, gate_ref, gain_ref):
        o = of_ref[0] + ob_ref[0]
        ss = _group_sumsq(o, ones_ref[...])
        y = o * lax.rsqrt(ss * (1.0 / HEAD_DV) + EPS) * gain_ref[...]
        return (y * _silu(gate_ref[0])).astype(BF16)

    mixed = jnp.concatenate([yconv[0].astype(BF16), finish(gla_f, gla_b, gla_gate, gla_gain),
                             yatt[0].astype(BF16), finish(ret_f, ret_b, ret_gate, ret_gain)], axis=1)
    o = jnp.dot(mixed, wout_ref[...], preferred_element_type=F32)
    gate1 = mod_ref[:, 2 * d:3 * d]
    shift2 = mod_ref[:, 3 * d:4 * d]
    scale2 = mod_ref[:, 4 * d:5 * d]
    gate2 = mod_ref[:, 5 * d:6 * d]
    x = _load_rows(split, pl.program_id(1) + tile0 < n_ctx_tiles, lat_ref, ctx_ref, xbuf)
    x_mid = x + gate1 * o
    h2 = ((_rms(x_mid) * n2g_ref[...]) * (1.0 + scale2) + shift2).astype(BF16)
    acc = jnp.zeros((tm, d), F32)
    tf = 1024
    for f in range(0, d_ff, tf):
        u = jnp.dot(h2, wup_ref[:, f:f + tf], preferred_element_type=F32)
        a = jnp.square(jnp.maximum(u, 0.0)).astype(BF16)
        acc = acc + jnp.dot(a, wdown_ref[f:f + tf, :], preferred_element_type=F32)
    x_out = x_mid + gate2 * acc
    if final:
        x_out = _rms(x_out) * fng_ref[...]
    o_ref[0] = x_out


def _post(layer, final, yconv, gla_of, gla_ob, gla_in, yatt, ret_of, ret_ob, ret_in, gla_gain, ret_gain,
          ones_blockdiag, x_lat, x_ctx, mods, norm2_g, w_layer, w_out, w_up, w_down, final_g, n_ctx_tiles,
          with_ctx, riders=()):
    split = x_ctx is not None
    bsz, _, d = x_lat.shape
    tm = ROW_TILE
    t_all = gla_of.shape[1]
    ctx_row = bsz
    tile0 = 0 if with_ctx else n_ctx_tiles
    n_out = t_all // tm - tile0
    c = GROUP_W
    gate_blk = (2 * N_HEADS * SCAN_DK + c) // c
    full = lambda b, j: (b, j + tile0, 0)
    own = lambda b, j: (b, j, 0)
    assert not split or tile0 == 0
    row_specs, row_args = _row_specs(split, x_lat, x_ctx, tm, n_ctx_tiles, tile0)

    def resident(arr):
        rest = arr.shape[1:]
        return pl.BlockSpec((None,) + rest, lambda *_: (w_layer,) + (0,) * len(rest),
                            pipeline_mode=pl.Buffered(1))

    rid_in, rid_args, rid_out, rid_shapes = _rider_specs(riders, lambda b, j: b * n_out + j, bsz * n_out)
    return pl.pallas_call(
        functools.partial(_post_body, final, split, n_ctx_tiles, tile0, tuple(k for k, _, _ in riders)),
        grid=(bsz, n_out),
        in_specs=[pl.BlockSpec((1, tm, c), full),
                  pl.BlockSpec((1, tm, c), full), pl.BlockSpec((1, tm, c), full),
                  pl.BlockSpec((1, tm, c), lambda b, j: (b, j + tile0, gate_blk)),
                  pl.BlockSpec((1, tm, c), own),
                  pl.BlockSpec((1, tm, c), full), pl.BlockSpec((1, tm, c), full),
                  pl.BlockSpec((1, tm, c), lambda b, j: (b, j + tile0, gate_blk)),
                  _layer_spec(gla_gain, layer), _layer_spec(ret_gain, layer),
                  _const_spec(ones_blockdiag)] + row_specs + [
                  pl.BlockSpec((None, None, 1, N_MOD * d),
                               lambda b, j: (layer, jnp.where(j + tile0 < n_ctx_tiles, ctx_row, b), 0, 0)),
                  _layer_spec(norm2_g, layer),
                  resident(w_out), resident(w_up), resident(w_down),
                  _const_spec(final_g)] + rid_in,
        out_specs=[pl.BlockSpec((1, tm, d), own)] + rid_out,
        out_shape=[jax.ShapeDtypeStruct((bsz, n_out * tm, d), F32)] + rid_shapes,
        scratch_shapes=[pltpu.VMEM((tm, d), F32)] if split else [],
        compiler_params=_params(("arbitrary", "arbitrary") if riders else ("parallel", "parallel")),
        name="post",
    )(yconv, gla_of, gla_ob, gla_in, yatt, ret_of, ret_ob, ret_in, gla_gain, ret_gain,
      ones_blockdiag, *row_args, mods, norm2_g, w_out, w_up, w_down, final_g, *rid_args)


def _rope_table(t_lat, n_ctx, head_dim, width):
    n_ax = head_dim // 4
    inv = np.float32(ROPE_THETA) ** (-np.arange(n_ax, dtype=np.float32) / np.float32(n_ax))
    pos = np.arange(t_lat)
    ang = np.concatenate([(pos // GRID_W).astype(np.float32)[:, None] * inv,
                          (pos % GRID_W).astype(np.float32)[:, None] * inv], axis=-1).astype(np.float32)
    cos, sin = np.cos(ang), np.sin(ang)
    cos = np.concatenate([np.ones((n_ctx, head_dim // 2), np.float32), cos], axis=0)
    sin = np.concatenate([np.zeros((n_ctx, head_dim // 2), np.float32), sin], axis=0)
    reps = width // head_dim
    return (jnp.asarray(np.tile(np.concatenate([cos, cos], axis=-1), (1, reps)), F32),
            jnp.asarray(np.tile(np.concatenate([-sin, sin], axis=-1), (1, reps)), F32))


def kernel(x, c, ctx, c_ctx, w_mod, b_mod, norm1_g, norm2_g, w_in, conv_w_dw, conv_b_dw, conv_ln_g,
           conv_ln_b, conv_w_pw, conv_b_pw, gla_w_a_f, gla_b_a_f, gla_w_a_b, gla_b_a_b, gla_norm_g,
           att_q_norm_g, att_k_norm_g, ret_norm_g, w_out, w_up, w_down, final_norm_g):
    bsz, t_lat, d = x.shape
    n_ctx = ctx.shape[1]
    depth = w_mod.shape[0]
    assert bsz + 1 <= 8 and n_ctx % ROW_TILE == 0 and t_lat % ROW_TILE == 0
    n_ctx_tiles = n_ctx // ROW_TILE
    dk_all = N_HEADS * SCAN_DK

    cc = jnp.concatenate([c, c_ctx[None, :], jnp.zeros((8 - bsz - 1, d), F32)], axis=0)
    mods = _modulation(cc, w_mod, b_mod)
    mods = mods.reshape(depth, 8, 1, N_MOD * d)

    stack_rows = lambda v: v.reshape(depth, 1, -1)
    in_w = (w_in[:1, :, :W_IN_A].astype(BF16),
            jnp.pad(w_in[:1, :, W_IN_A:W_IN_A + GLA_RANK].astype(BF16),
                    ((0, 0), (0, 0), (0, LANES - GLA_RANK))),
            w_in[:1, :, W_IN_A + GLA_RANK:].astype(BF16))
    pad_rank = lambda w: jnp.pad(w, ((0, 0), (0, LANES - GLA_RANK), (0, 0))).astype(BF16)
    w_a_f, w_a_b = pad_rank(gla_w_a_f), pad_rank(gla_w_a_b)
    w_pw = conv_w_pw.astype(BF16)
    q_gain = stack_rows(jnp.tile(att_q_norm_g, (1, N_HEADS)))
    k_gain = stack_rows(jnp.tile(att_k_norm_g, (1, KV_HEADS)))

    cos_att, sin_att = _rope_table(t_lat, n_ctx, HEAD_DV, N_HEADS * HEAD_DV)
    cos_ret, sin_ret = _rope_table(t_lat, n_ctx, SCAN_DK, dk_all)
    gamma = 1.0 - np.exp2(-5.0 - np.arange(N_HEADS, dtype=np.float64))
    log_gamma = jnp.asarray(np.repeat(np.log(gamma), SCAN_DK)[None, :], F32)
    gid = np.arange(GROUP_W) // HEAD_DV
    ones_blockdiag = jnp.asarray(gid[:, None] == gid[None, :], BF16)
    key_mask = jnp.asarray(gid[:, None] == (np.arange(dk_all) // SCAN_DK)[None, :], BF16)
    pos = np.arange(ROW_TILE)
    same_chunk = (pos[:, None] // CHUNK) == (pos[None, :] // CHUNK)
    cum_f = jnp.asarray(same_chunk & (pos[None, :] <= pos[:, None]), BF16)
    cum_b = jnp.asarray(same_chunk & (pos[None, :] >= pos[:, None]), BF16)

    x_lat, x_ctx = x, ctx
    for l in range(depth):
        with_ctx = l < depth - 1
        final = l == depth - 1
        conv_in, gla_in, ret_in, q, k, vt = _in_proj(
            l, x_lat, x_ctx, mods, stack_rows(norm1_g), 0, *in_w, cos_att, sin_att, q_gain, k_gain,
            ones_blockdiag, n_ctx_tiles)
        mixer_riders = [("plain", w, 0) for w in (w_out, w_up, w_down)] if l == 0 else []
        gla_of, gla_ob, ret_of, ret_ob, y_conv, *converted = _mixers(
            l, gla_in, ret_in, conv_in, n_ctx_tiles, w_a_f, stack_rows(gla_b_a_f), w_a_b,
            stack_rows(gla_b_a_b), log_gamma, cos_ret, sin_ret, key_mask, ones_blockdiag, cum_f, cum_b,
            conv_w_dw, stack_rows(conv_b_dw), stack_rows(conv_ln_g), stack_rows(conv_ln_b), w_pw,
            stack_rows(conv_b_pw), mixer_riders)
        if l == 0:
            post_w = tuple(w[None] for w in converted)
        y_att = _attention(q, k, vt, n_ctx_tiles, with_ctx)
        post_riders = [] if final else [("w_in", w_in, l + 1)] + [("plain", w, l + 1)
                                                                  for w in (w_out, w_up, w_down)]
        x_lat, *converted = _post(
            l, final, y_conv, gla_of, gla_ob, gla_in, y_att, ret_of, ret_ob, ret_in,
            stack_rows(gla_norm_g), stack_rows(ret_norm_g), ones_blockdiag, x_lat, x_ctx, mods,
            stack_rows(norm2_g), 0, *post_w, final_norm_g.reshape(1, -1), n_ctx_tiles, with_ctx,
            post_riders)
        if not final:
            in_w = tuple(w[None] for w in converted[:3])
            post_w = tuple(w[None] for w in converted[3:])
        x_ctx = None
    return x_lat
```

```python
import functools

import numpy as np
import jax
import jax.numpy as jnp
from jax import lax
from jax.experimental import pallas as pl
from jax.experimental.pallas import tpu as pltpu

F32 = jnp.float32
BF16 = jnp.bfloat16

EPS = 1e-6
GRID_W = 64
ROPE_THETA = 10000.0
N_MOD = 6
CONV_KSIZE = 31
CONV_PAD = (CONV_KSIZE - 1) // 2
HALO = 16
GLA_TAU = 16.0
GLA_RANK = 16
CHUNK = 64
N_HEADS = 4
KV_HEADS = 2
HEAD_DV = 64
SCAN_DK = 32
GROUP_W = 256
ROW_TILE = 256
LANES = 128
SUBLANES = 8
BF16_SUBLANES = 16
VT_ROWS = HEAD_DV + BF16_SUBLANES
LOG2_E = 1.4426950408889634
RIDER_CHUNKS = 64
MIXER_BATCH = 2
SCORE_LOOKAHEAD = 2
V7X_VMEM_BYTES = 64 * 1024 * 1024
VMEM_LIMIT = V7X_VMEM_BYTES * 3 // 4

W_CONV = 2 * GROUP_W
W_GLA = 2 * N_HEADS * SCAN_DK + 2 * GROUP_W + LANES
W_ATT = GROUP_W + 2 * KV_HEADS * HEAD_DV
W_RET = 2 * N_HEADS * SCAN_DK + 2 * GROUP_W
W_IN_A = W_CONV + W_GLA - LANES
W_IN_B = W_ATT + W_RET


def _params(semantics):
    return pltpu.CompilerParams(dimension_semantics=semantics, vmem_limit_bytes=VMEM_LIMIT)


def _layer_spec(arr, layer):
    rest = arr.shape[1:]
    return pl.BlockSpec((None,) + rest, lambda *_: (layer,) + (0,) * len(rest))


def _const_spec(arr):
    return pl.BlockSpec(arr.shape, lambda *_: (0,) * arr.ndim)


def _rider_specs(items, step_of, n_steps):
    in_specs, args, out_specs, out_shapes = [], [], [], []
    n_chunks = 1
    while 2 * n_chunks <= min(n_steps, RIDER_CHUNKS):
        n_chunks *= 2
    chunk = lambda *g: jnp.minimum(step_of(*g), n_chunks - 1)
    for arr, layer in items:
        _, rows, cols = arr.shape
        rpc = rows // n_chunks
        assert rows % n_chunks == 0 and rpc % BF16_SUBLANES == 0
        in_specs.append(pl.BlockSpec((None, rpc, cols), lambda *g, layer=layer: (layer, chunk(*g), 0)))
        args.append(arr)
        out_specs.append(pl.BlockSpec((rpc, cols), lambda *g: (chunk(*g), 0)))
        out_shapes.append(jax.ShapeDtypeStruct((rows, cols), BF16))
    return in_specs, args, out_specs, out_shapes


def _run_riders(in_refs, out_refs):
    for src, dst in zip(in_refs, out_refs):
        dst[...] = src[...].astype(BF16)


def _silu(x):
    return x * jax.nn.sigmoid(x)


def _rms(x):
    return x * lax.rsqrt(jnp.mean(x * x, axis=-1, keepdims=True) + EPS)


def _group_sumsq(t, ones_blockdiag):
    t2 = t * t
    hi = t2.astype(BF16)
    lo = (t2 - hi.astype(F32)).astype(BF16)
    return (jnp.dot(hi, ones_blockdiag, preferred_element_type=F32)
            + jnp.dot(lo, ones_blockdiag, preferred_element_type=F32))


def _load_rows(split, is_ctx, lat_ref, ctx_ref, buf):
    if not split:
        return lat_ref[0]

    @pl.when(is_ctx)
    def _():
        buf[...] = ctx_ref[0]

    @pl.when(jnp.logical_not(is_ctx))
    def _():
        buf[...] = lat_ref[0]

    return buf[...]


def _row_specs(split, x_lat, x_ctx, tm, n_ctx_tiles, tile0):
    d = x_lat.shape[2]
    if not split:
        return [pl.BlockSpec((1, tm, d), lambda b, j: (b, j + tile0, 0))], [x_lat]
    return ([pl.BlockSpec((1, tm, d), lambda b, j: (b, jnp.maximum(j - n_ctx_tiles, 0), 0)),
             pl.BlockSpec((1, tm, d), lambda b, j: (b, jnp.minimum(j, n_ctx_tiles - 1), 0))],
            [x_lat, x_ctx])


def _mod_body(cc_ref, w_ref, b_ref, o_ref):
    s = _silu(cc_ref[...])
    o_ref[0] = jnp.dot(s.astype(BF16), w_ref[0].astype(BF16),
                       preferred_element_type=F32) + b_ref[0]


def _modulation(cc, w_mod, b_mod):
    depth, d, n = w_mod.shape
    tn = 1024
    return pl.pallas_call(
        _mod_body,
        grid=(depth, n // tn),
        in_specs=[pl.BlockSpec((8, d), lambda l, i: (0, 0)),
                  pl.BlockSpec((1, d, tn), lambda l, i: (l, 0, i)),
                  pl.BlockSpec((1, 1, tn), lambda l, i: (l, 0, i))],
        out_specs=pl.BlockSpec((1, 8, tn), lambda l, i: (l, 0, i)),
        out_shape=jax.ShapeDtypeStruct((depth, 8, n), F32),
        compiler_params=_params(("parallel", "parallel")),
        name="modulation",
    )(cc, w_mod, b_mod.reshape(depth, 1, n))


def _inproj_body(split, n_ctx_tiles, *refs):
    n_src = 2 if split else 1
    lat_ref = refs[0]
    ctx_ref = refs[1] if split else None
    (mod_ref, g_ref, wa_ref, wz_ref, wb_ref, cos_ref, sin_ref, qg_ref, kg_ref, ones_ref,
     conv_ref, gla_ref, ret_ref, q_out, k_out, vt_out) = refs[n_src:n_src + 16]
    xbuf = refs[n_src + 16] if split else None
    tm, d = lat_ref.shape[1], lat_ref.shape[2]

    x = _load_rows(split, pl.program_id(1) < n_ctx_tiles, lat_ref, ctx_ref, xbuf)
    shift = mod_ref[:, 0:d]
    scale = mod_ref[:, d:2 * d]
    hb = ((_rms(x) * g_ref[...]) * (1.0 + scale) + shift).astype(BF16)

    def proj(w_ref, lo, width):
        return jnp.dot(hb, w_ref[:, lo:lo + width], preferred_element_type=F32)

    wq = N_HEADS * HEAD_DV
    wk = KV_HEADS * HEAD_DV
    att = proj(wb_ref, 0, W_ATT)
    q = att[:, :wq]
    k = att[:, wq:wq + wk]
    v = att[:, wq + wk:]
    ss_q = _group_sumsq(q, ones_ref[...])
    ss_k = _group_sumsq(k, ones_ref[:wk, :wk])
    gla_ref[0, :, W_GLA - LANES:] = proj(wz_ref, 0, LANES)
    ret_ref[0] = proj(wb_ref, W_ATT, W_RET)
    gla_ref[0, :, 0:W_GLA - LANES] = proj(wa_ref, W_CONV, W_GLA - LANES)
    conv_ref[0] = proj(wa_ref, 0, W_CONV)

    def norm_rope(t, ss, gain, w):
        t = t * lax.rsqrt(ss * (1.0 / HEAD_DV) + EPS) * gain
        lane = lax.broadcasted_iota(jnp.int32, t.shape, 1)
        first_half = (lane % HEAD_DV) < HEAD_DV // 2
        rot = jnp.where(first_half, pltpu.roll(t, w - HEAD_DV // 2, 1), pltpu.roll(t, HEAD_DV // 2, 1))
        return t * cos_ref[:, :w] + rot * sin_ref[:, :w]

    qn = norm_rope(q, ss_q, qg_ref[...], wq) * (HEAD_DV ** -0.5 * LOG2_E)
    kn = norm_rope(k, ss_k, kg_ref[...], wk)
    low = lax.broadcasted_iota(jnp.int32, (tm, LANES), 1) < HEAD_DV
    pair0 = qn[:, :LANES]
    pair1 = qn[:, LANES:]
    q_out[0, 0] = jnp.where(low, pair0, 0.0).astype(BF16)
    q_out[0, 1] = jnp.where(low, pltpu.roll(pair0, HEAD_DV, 1), 0.0).astype(BF16)
    q_out[0, 2] = jnp.where(low, 0.0, pltpu.roll(pair1, HEAD_DV, 1)).astype(BF16)
    q_out[0, 3] = jnp.where(low, 0.0, pair1).astype(BF16)
    k_out[0] = kn.astype(BF16)
    vt = v.T.astype(BF16)
    ones = jnp.ones((VT_ROWS - HEAD_DV, tm), BF16)
    for h in range(KV_HEADS):
        vt_out[0, h, 0, 0:HEAD_DV, :] = vt[h * HEAD_DV:(h + 1) * HEAD_DV, :]
        vt_out[0, h, 0, HEAD_DV:, :] = ones


def _in_proj(layer, x_lat, x_ctx, mods, norm1_g, w_layer, w_a, w_z, w_b, cos, sin, q_gain, k_gain,
             ones_blockdiag, n_ctx_tiles):
    split = x_ctx is not None
    bsz, _, d = x_lat.shape
    tm = ROW_TILE
    t_all = x_lat.shape[1] + (x_ctx.shape[1] if split else 0)
    n_tiles = t_all // tm
    ctx_row = bsz
    wq = N_HEADS * HEAD_DV
    row_specs, row_args = _row_specs(split, x_lat, x_ctx, tm, n_ctx_tiles, 0)
    tile = lambda b, j: (b, j, 0)
    return pl.pallas_call(
        functools.partial(_inproj_body, split, n_ctx_tiles),
        grid=(bsz, n_tiles),
        in_specs=row_specs + [
            pl.BlockSpec((None, None, 1, N_MOD * d),
                         lambda b, j: (layer, jnp.where(j < n_ctx_tiles, ctx_row, b), 0, 0)),
            _layer_spec(norm1_g, layer), _layer_spec(w_a, w_layer), _layer_spec(w_z, w_layer),
            _layer_spec(w_b, w_layer),
            pl.BlockSpec((tm, wq), lambda b, j: (j, 0)), pl.BlockSpec((tm, wq), lambda b, j: (j, 0)),
            _layer_spec(q_gain, layer), _layer_spec(k_gain, layer), _const_spec(ones_blockdiag)],
        out_specs=[pl.BlockSpec((1, tm, W_CONV), tile), pl.BlockSpec((1, tm, W_GLA), tile),
                   pl.BlockSpec((1, tm, W_RET), tile),
                   pl.BlockSpec((1, N_HEADS, tm, LANES), lambda b, j: (b, 0, j, 0)),
                   pl.BlockSpec((1, tm, LANES), tile),
                   pl.BlockSpec((1, KV_HEADS, 1, VT_ROWS, tm), lambda b, j: (b, 0, j, 0, 0))],
        out_shape=[jax.ShapeDtypeStruct((bsz, t_all, W_CONV), F32),
                   jax.ShapeDtypeStruct((bsz, t_all, W_GLA), F32),
                   jax.ShapeDtypeStruct((bsz, t_all, W_RET), F32),
                   jax.ShapeDtypeStruct((bsz, N_HEADS, t_all, LANES), BF16),
                   jax.ShapeDtypeStruct((bsz, t_all, LANES), BF16),
                   jax.ShapeDtypeStruct((bsz, KV_HEADS, n_tiles, VT_ROWS, tm), BF16)],
        scratch_shapes=[pltpu.VMEM((tm, d), F32)] if split else [],
        compiler_params=_params(("parallel", "parallel")),
        name="in_proj",
    )(*row_args, mods, norm1_g, w_a, w_z, w_b, cos, sin, q_gain, k_gain, ones_blockdiag)


def _chunk_cumsum(la, tri_ref):
    w = la.shape[1]
    hi = la.astype(BF16)
    lo = (la - hi.astype(F32)).astype(BF16)
    both = jnp.dot(tri_ref[...], jnp.concatenate([hi, lo], axis=1), preferred_element_type=F32)
    return both[:, :w] + both[:, w:]


def _decay_factors(b, reverse):
    b_tot = b[0:1, :] if reverse else b[CHUNK - 1:CHUNK, :]
    dk_all = b.shape[1]
    dec_cols = jnp.broadcast_to(jnp.exp(b_tot), (dk_all, dk_all)).T
    dec_cols = jnp.concatenate([dec_cols] * (GROUP_W // dk_all), axis=1)
    return jnp.exp(b), jnp.exp(-b), jnp.exp(b_tot - b), dec_cols


def _scan_chunk_head(q, k, v, factors, st, key_mask, value_mask, head_mask):
    nt_dims = (((1,), (1,)), ((), ()))
    e_q, e_inv, e_end, dec_cols = factors
    q_dec = (q * e_q).astype(BF16)
    k_inv = (k * e_inv).astype(BF16)
    k_end = (k * e_end).astype(BF16)
    vb = v.astype(BF16)
    k_blk = jnp.concatenate([k_inv] * N_HEADS, axis=0) * key_mask
    scores = lax.dot_general(q_dec, k_blk, nt_dims, preferred_element_type=F32)
    o_inter = jnp.dot(q_dec, st.astype(BF16), preferred_element_type=F32)
    kv = lax.dot_general(k_end, vb, (((0,), (0,)), ((), ())), preferred_element_type=F32)
    st = st * dec_cols + jnp.where(head_mask, kv, 0.0)
    v_blk = jnp.concatenate([vb] * N_HEADS, axis=0) * value_mask
    return (scores, v_blk, o_inter), st


def _scan_chunk_tail(pending, tri):
    scores, v_blk, o_inter = pending
    scores = jnp.where(tri, scores, 0.0).astype(BF16)
    return jnp.dot(scores, v_blk, preferred_element_type=F32) + o_inter


def _scan_body(gf, gb, rf, rb, waf, baf, wab, bab, lg, cosf, sinf, cosb, sinb,
               key_mask_ref, value_mask_ref, cum_f_ref, cum_b_ref,
               gof, gob, rof, rob, st_all, b_all):
    n_batch = gf.shape[0]

    @pl.when(pl.program_id(1) == 0)
    def _():
        st_all[...] = jnp.zeros(st_all.shape, F32)

    tb = gf.shape[1]
    dk_all = N_HEADS * SCAN_DK
    dv_all = GROUP_W
    scale = SCAN_DK ** -0.5
    key_mask = key_mask_ref[...]
    value_mask = value_mask_ref[...]
    r = lax.broadcasted_iota(jnp.int32, (dk_all, dv_all), 0)
    c = lax.broadcasted_iota(jnp.int32, (dk_all, dv_all), 1)
    head_mask = (r // SCAN_DK) == (c // HEAD_DV)
    r = lax.broadcasted_iota(jnp.int32, (CHUNK, dv_all), 0)
    c = lax.broadcasted_iota(jnp.int32, (CHUNK, dv_all), 1) % CHUNK
    tri_f = c <= r
    tri_b = c >= r

    def qkv(ref, bi, lo):
        return (ref[bi, lo:lo + CHUNK, 0:dk_all], ref[bi, lo:lo + CHUNK, dk_all:2 * dk_all],
                ref[bi, lo:lo + CHUNK, 2 * dk_all:2 * dk_all + dv_all])

    def gla_chain(ref, bi, w_ref, bias_ref, cum_ref, reverse):
        slot = int(reverse)
        z = ref[bi, :, 2 * dk_all + 2 * dv_all:]
        pre = jnp.dot(z.astype(BF16), w_ref[...], preferred_element_type=F32) + bias_ref[...]
        log_sig = jnp.minimum(pre, 0.0) - jnp.log1p(jnp.exp(-jnp.abs(pre)))
        b_all[bi, slot] = _chunk_cumsum(log_sig / GLA_TAU, cum_ref)

        def load(lo):
            q, k, v = qkv(ref, bi, lo)
            return q * scale, k, v, _decay_factors(b_all[bi, slot, lo:lo + CHUNK, :], reverse)
        return load

    lane = lax.broadcasted_iota(jnp.int32, (CHUNK, dk_all), 1)
    first_half = (lane % SCAN_DK) < SCAN_DK // 2
    pos = lax.broadcasted_iota(jnp.int32, (CHUNK, dk_all), 0).astype(F32)

    ret_factors = [_decay_factors(((CHUNK - pos) if reverse else (pos + 1.0)) * lg[...], reverse)
                   for reverse in (False, True)]

    def ret_chain(ref, bi, cos_ref, sin_ref, reverse):
        def rope(t, lo):
            rot = jnp.where(first_half, pltpu.roll(t, dk_all - SCAN_DK // 2, 1),
                            pltpu.roll(t, SCAN_DK // 2, 1))
            return t * cos_ref[lo:lo + CHUNK, :] + rot * sin_ref[lo:lo + CHUNK, :]

        def load(lo):
            q, k, v = qkv(ref, bi, lo)
            return rope(q, lo), rope(k * scale, lo), v, ret_factors[int(reverse)]
        return load

    chains = []
    for bi in range(n_batch):
        chains += [
            (gla_chain(gf, bi, waf, baf, cum_f_ref, False), False, gof, bi, tri_f),
            (gla_chain(gb, bi, wab, bab, cum_b_ref, True), True, gob, bi, tri_b),
            (ret_chain(rf, bi, cosf, sinf, False), False, rof, bi, tri_f),
            (ret_chain(rb, bi, cosb, sinb, True), True, rob, bi, tri_b),
        ]
    st_vals = [st_all[n] for n in range(len(chains))]
    n_chunks = tb // CHUNK

    def heads(i):
        pending = []
        for n, (load, reverse, _, _, _) in enumerate(chains):
            lo = (n_chunks - 1 - i if reverse else i) * CHUNK
            q, k, v, factors = load(lo)
            part, st_vals[n] = _scan_chunk_head(q, k, v, factors, st_vals[n],
                                                key_mask, value_mask, head_mask)
            pending.append((lo, part))
        return pending

    pending = heads(0)
    for i in range(n_chunks):
        following = heads(i + 1) if i + 1 < n_chunks else None
        for (lo, part), (_, _, out_ref, bi, tri) in zip(pending, chains):
            out_ref[bi, lo:lo + CHUNK, :] = _scan_chunk_tail(part, tri)
        pending = following
    for n, val in enumerate(st_vals):
        st_all[n] = val


def _conv_tiles(j, n_ctx_tiles, n_tiles, main_ref, prev_ref, next_ref, wdw_ref, bdw_ref, lng_ref,
                lnb_ref, wpw_ref, bpw_ref, o_ref, ubuf, shifted, sbuf):
    n_batch, tt = main_ref.shape[0], main_ref.shape[1]
    c = o_ref.shape[2]

    def glu(blk):
        return blk[:, :c] * jax.nn.sigmoid(blk[:, c:])

    has_prev = jnp.logical_and(j != 0, j != n_ctx_tiles)
    has_next = jnp.logical_and(j != n_ctx_tiles - 1, j != n_tiles - 1)

    @pl.when(has_prev)
    def _():
        for bi in range(n_batch):
            ubuf[bi, 0:HALO, :] = glu(prev_ref[bi])

    @pl.when(jnp.logical_not(has_prev))
    def _():
        ubuf[:, 0:HALO, :] = jnp.zeros((n_batch, HALO, c), F32)

    @pl.when(has_next)
    def _():
        for bi in range(n_batch):
            ubuf[bi, HALO + tt:, :] = glu(next_ref[bi])

    @pl.when(jnp.logical_not(has_next))
    def _():
        ubuf[:, HALO + tt:, :] = jnp.zeros((n_batch, HALO, c), F32)

    span = shifted.shape[2]
    rows = 128
    for bi in range(n_batch):
        ubuf[bi, HALO:HALO + tt, :] = glu(main_ref[bi])
        for s in range(1, SUBLANES):
            shifted[bi, s - 1] = ubuf[bi, s:s + span, :]
        for r in range(0, tt, rows):
            acc = jnp.zeros((rows, c), F32)
            for k in range(CONV_KSIZE):
                off = k + HALO - CONV_PAD
                lo = r + off - off % SUBLANES
                if off % SUBLANES == 0:
                    tap = ubuf[bi, lo:lo + rows, :]
                else:
                    tap = shifted[bi, off % SUBLANES - 1, lo:lo + rows, :]
                acc = acc + tap * wdw_ref[k:k + 1, :]
            y = acc + bdw_ref[...]
            yc = y - jnp.mean(y, axis=-1, keepdims=True)
            yn = yc * lax.rsqrt(jnp.mean(yc * yc, axis=-1, keepdims=True) + EPS)
            yn = yn * lng_ref[...] + lnb_ref[...]
            sbuf[bi, r:r + rows, :] = _silu(yn).astype(BF16)
        o_ref[bi] = jnp.dot(sbuf[bi], wpw_ref[...], preferred_element_type=F32) + bpw_ref[...]


def _mixers_body(n_ctx_tiles, n_tiles, n_rid, *refs):
    scan_in, conv_in = refs[:17], refs[17:26]
    rider_in = refs[26:26 + n_rid]
    outs = refs[26 + n_rid:]
    scan_out, conv_out, rider_out = outs[:4], outs[4], outs[5:-5]
    scan_scratch, conv_scratch = outs[-5:-3], outs[-3:]
    _run_riders(rider_in, rider_out)
    _scan_body(*scan_in, *scan_out, *scan_scratch)
    _conv_tiles(pl.program_id(1), n_ctx_tiles, n_tiles, *conv_in, conv_out, *conv_scratch)


def _mixers(layer, gla_in, ret_in, conv_in, n_ctx_tiles, w_a_f, b_a_f, w_a_b, b_a_b, log_gamma, cos, sin,
            key_mask, value_mask, cum_f, cum_b, w_dw, b_dw, ln_g, ln_b, w_pw, b_pw, riders=()):
    bsz, t_all, _ = gla_in.shape
    tb = ROW_TILE
    n_tiles = t_all // tb
    dk_all = N_HEADS * SCAN_DK
    dv_all = GROUP_W
    c = GROUP_W
    nb = MIXER_BATCH if bsz % MIXER_BATCH == 0 else 1
    per = tb // HALO
    n_halo = t_all // HALO

    def bwd(j):
        return jnp.where(j < n_ctx_tiles, n_ctx_tiles - 1 - j, n_tiles - 1 - (j - n_ctx_tiles))

    fwd_blk = lambda b, j: (b, j, 0)
    bwd_blk = lambda b, j: (b, bwd(j), 0)
    n_steps = (bsz // nb) * n_tiles
    rid_in, rid_args, rid_out, rid_shapes = _rider_specs(riders, lambda b, j: b * n_tiles + j, n_steps)
    in_specs = [pl.BlockSpec((nb, tb, W_GLA), fwd_blk), pl.BlockSpec((nb, tb, W_GLA), bwd_blk),
                pl.BlockSpec((nb, tb, W_RET), fwd_blk), pl.BlockSpec((nb, tb, W_RET), bwd_blk),
                _layer_spec(w_a_f, layer), _layer_spec(b_a_f, layer),
                _layer_spec(w_a_b, layer), _layer_spec(b_a_b, layer),
                _const_spec(log_gamma),
                pl.BlockSpec((tb, dk_all), lambda b, j: (j, 0)),
                pl.BlockSpec((tb, dk_all), lambda b, j: (j, 0)),
                pl.BlockSpec((tb, dk_all), lambda b, j: (bwd(j), 0)),
                pl.BlockSpec((tb, dk_all), lambda b, j: (bwd(j), 0)),
                _const_spec(key_mask), _const_spec(value_mask), _const_spec(cum_f), _const_spec(cum_b),
                pl.BlockSpec((nb, tb, 2 * c), fwd_blk),
                pl.BlockSpec((nb, HALO, 2 * c), lambda b, j: (b, jnp.maximum(j * per - 1, 0), 0)),
                pl.BlockSpec((nb, HALO, 2 * c), lambda b, j: (b, jnp.minimum((j + 1) * per, n_halo - 1), 0)),
                _layer_spec(w_dw, layer), _layer_spec(b_dw, layer), _layer_spec(ln_g, layer),
                _layer_spec(ln_b, layer), _layer_spec(w_pw, layer), _layer_spec(b_pw, layer)] + rid_in
    return pl.pallas_call(
        functools.partial(_mixers_body, n_ctx_tiles, n_tiles, len(riders)),
        grid=(bsz // nb, n_tiles),
        in_specs=in_specs,
        out_specs=[pl.BlockSpec((nb, tb, dv_all), fwd_blk), pl.BlockSpec((nb, tb, dv_all), bwd_blk)] * 2
        + [pl.BlockSpec((nb, tb, c), fwd_blk)] + rid_out,
        out_shape=[jax.ShapeDtypeStruct((bsz, t_all, dv_all), F32)] * 5 + rid_shapes,
        scratch_shapes=[pltpu.VMEM((4 * nb, dk_all, dv_all), F32), pltpu.VMEM((nb, 2, tb, dk_all), F32),
                        pltpu.VMEM((nb, tb + 2 * HALO, c), F32),
                        pltpu.VMEM((nb, SUBLANES - 1, tb + 2 * HALO - SUBLANES, c), F32),
                        pltpu.VMEM((nb, tb, c), BF16)],
        compiler_params=_params(("arbitrary", "arbitrary")),
        name="mixers",
    )(gla_in, gla_in, ret_in, ret_in, w_a_f, b_a_f, w_a_b, b_a_b, log_gamma, cos, sin, cos, sin,
      key_mask, value_mask, cum_f, cum_b, conv_in, conv_in, conv_in, w_dw, b_dw, ln_g, ln_b, w_pw, b_pw,
      *rid_args)


def _att_body(n_ctx_tiles, n_tiles, tile0, q_ref, k_ref, vt_ref, o_ref):
    group = q_ref.shape[1]
    tq = q_ref.shape[2]
    tk = vt_ref.shape[4]
    nq = group * tq

    def attend(n_blocks):
        q2 = q_ref[0].reshape(nq, LANES)

        def scores(i):
            kb = k_ref[0, i * tk:(i + 1) * tk, :]
            return lax.dot_general(kb, q2, (((1,), (1,)), ((), ())), preferred_element_type=F32)

        m = acc = None
        ahead = [scores(i) for i in range(min(SCORE_LOOKAHEAD, n_blocks))]
        for i in range(n_blocks):
            s = ahead.pop(0)
            if i + SCORE_LOOKAHEAD < n_blocks:
                ahead.append(scores(i + SCORE_LOOKAHEAD))
            blk_max = jnp.max(s, axis=0, keepdims=True)
            m_new = blk_max if i == 0 else jnp.maximum(m, blk_max)
            p = jnp.exp2(s - m_new).astype(BF16)
            pv = jnp.dot(vt_ref[0, 0, i], p, preferred_element_type=F32)
            acc = pv if i == 0 else jnp.exp2(m - m_new) * acc + pv
            m = m_new
        o_t = acc[:HEAD_DV] / acc[HEAD_DV:HEAD_DV + 1]
        stacked = jnp.concatenate([o_t[:, g * tq:(g + 1) * tq] for g in range(group)], axis=0)
        o_ref[0] = stacked.T

    qi = pl.program_id(2) + tile0
    if tile0 < n_ctx_tiles:
        pl.when(qi < n_ctx_tiles)(lambda: attend(n_ctx_tiles))
    pl.when(qi >= n_ctx_tiles)(lambda: attend(n_tiles))


def _attention(q, k, vt, n_ctx_tiles, with_ctx):
    bsz, _, t_all, _ = q.shape
    tq = ROW_TILE
    n_tiles = t_all // tq
    tile0 = 0 if with_ctx else n_ctx_tiles
    group = N_HEADS // KV_HEADS
    return pl.pallas_call(
        functools.partial(_att_body, n_ctx_tiles, n_tiles, tile0),
        grid=(bsz, KV_HEADS, n_tiles - tile0),
        in_specs=[pl.BlockSpec((1, group, tq, LANES), lambda b, h, i: (b, h, i + tile0, 0)),
                  pl.BlockSpec((1, t_all, LANES), lambda b, h, i: (b, 0, 0)),
                  pl.BlockSpec((1, 1, n_tiles, VT_ROWS, tq), lambda b, h, i: (b, h, 0, 0, 0))],
        out_specs=pl.BlockSpec((1, tq, group * HEAD_DV), lambda b, h, i: (b, i, h)),
        out_shape=jax.ShapeDtypeStruct((bsz, t_all - tile0 * tq, GROUP_W), F32),
        compiler_params=_params(("parallel", "parallel", "arbitrary")),
        name="attention",
    )(q, k, vt)


def _post_body(final, split, n_ctx_tiles, tile0, n_rid, *refs):
    (yconv, gla_f, gla_b, gla_gate, yatt, ret_f, ret_b, ret_gate, gla_gain, ret_gain, ones_ref) = refs[:11]
    n_src = 2 if split else 1
    lat_ref = refs[11]
    ctx_ref = refs[12] if split else None
    n_in = 17 + n_src
    mod_ref, n2g_ref, wout_ref, wup_ref, wdown_ref, fng_ref = refs[11 + n_src:n_in]
    o_ref = refs[n_in + n_rid]
    xbuf = refs[n_in + 2 * n_rid + 1] if split else None
    _run_riders(refs[n_in:n_in + n_rid], refs[n_in + n_rid + 1:n_in + 2 * n_rid + 1])
    tm, d = lat_ref.shape[1], lat_ref.shape[2]
    d_ff = wup_ref.shape[1]

    def finish(of_ref, ob_ref, gate_ref, gain_ref):
        o = of_ref[0] + ob_ref[0]
        ss = _group_sumsq(o, ones_ref[...])
        y = o * lax.rsqrt(ss * (1.0 / HEAD_DV) + EPS) * gain_ref[...]
        return (y * _silu(gate_ref[0])).astype(BF16)

    mixed = jnp.concatenate([yconv[0].astype(BF16), finish(gla_f, gla_b, gla_gate, gla_gain),
                             yatt[0].astype(BF16), finish(ret_f, ret_b, ret_gate, ret_gain)], axis=1)
    o = jnp.dot(mixed, wout_ref[...], preferred_element_type=F32)
    gate1 = mod_ref[:, 2 * d:3 * d]
    shift2 = mod_ref[:, 3 * d:4 * d]
    scale2 = mod_ref[:, 4 * d:5 * d]
    gate2 = mod_ref[:, 5 * d:6 * d]
    x = _load_rows(split, pl.program_id(1) + tile0 < n_ctx_tiles, lat_ref, ctx_ref, xbuf)
    x_mid = x + gate1 * o
    h2 = ((_rms(x_mid) * n2g_ref[...]) * (1.0 + scale2) + shift2).astype(BF16)
    acc = jnp.zeros((tm, d), F32)
    tf = 1024
    for f in range(0, d_ff, tf):
        u = jnp.dot(h2, wup_ref[:, f:f + tf], preferred_element_type=F32)
        a = jnp.square(jnp.maximum(u, 0.0)).astype(BF16)
        acc = acc + jnp.dot(a, wdown_ref[f:f + tf, :], preferred_element_type=F32)
    x_out = x_mid + gate2 * acc
    if final:
        x_out = _rms(x_out) * fng_ref[...]
    o_ref[0] = x_out


def _post(layer, final, yconv, gla_of, gla_ob, gla_in, yatt, ret_of, ret_ob, ret_in, gla_gain, ret_gain,
          ones_blockdiag, x_lat, x_ctx, mods, norm2_g, w_layer, w_out, w_up, w_down, final_g, n_ctx_tiles,
          with_ctx, riders=()):
    split = x_ctx is not None
    bsz, _, d = x_lat.shape
    tm = ROW_TILE
    t_all = gla_of.shape[1]
    ctx_row = bsz
    tile0 = 0 if with_ctx else n_ctx_tiles
    n_out = t_all // tm - tile0
    c = GROUP_W
    gate_blk = (2 * N_HEADS * SCAN_DK + c) // c
    full = lambda b, j: (b, j + tile0, 0)
    own = lambda b, j: (b, j, 0)
    assert not split or tile0 == 0
    row_specs, row_args = _row_specs(split, x_lat, x_ctx, tm, n_ctx_tiles, tile0)

    def resident(arr):
        rest = arr.shape[1:]
        return pl.BlockSpec((None,) + rest, lambda *_: (w_layer,) + (0,) * len(rest),
                            pipeline_mode=pl.Buffered(1))

    rid_in, rid_args, rid_out, rid_shapes = _rider_specs(riders, lambda b, j: b * n_out + j, bsz * n_out)
    return pl.pallas_call(
        functools.partial(_post_body, final, split, n_ctx_tiles, tile0, len(riders)),
        grid=(bsz, n_out),
        in_specs=[pl.BlockSpec((1, tm, c), full),
                  pl.BlockSpec((1, tm, c), full), pl.BlockSpec((1, tm, c), full),
                  pl.BlockSpec((1, tm, c), lambda b, j: (b, j + tile0, gate_blk)),
                  pl.BlockSpec((1, tm, c), own),
                  pl.BlockSpec((1, tm, c), full), pl.BlockSpec((1, tm, c), full),
                  pl.BlockSpec((1, tm, c), lambda b, j: (b, j + tile0, gate_blk)),
                  _layer_spec(gla_gain, layer), _layer_spec(ret_gain, layer),
                  _const_spec(ones_blockdiag)] + row_specs + [
                  pl.BlockSpec((None, None, 1, N_MOD * d),
                               lambda b, j: (layer, jnp.where(j + tile0 < n_ctx_tiles, ctx_row, b), 0, 0)),
                  _layer_spec(norm2_g, layer),
                  resident(w_out), resident(w_up), resident(w_down),
                  _const_spec(final_g)] + rid_in,
        out_specs=[pl.BlockSpec((1, tm, d), own)] + rid_out,
        out_shape=[jax.ShapeDtypeStruct((bsz, n_out * tm, d), F32)] + rid_shapes,
        scratch_shapes=[pltpu.VMEM((tm, d), F32)] if split else [],
        compiler_params=_params(("arbitrary", "arbitrary") if riders else ("parallel", "parallel")),
        name="post",
    )(yconv, gla_of, gla_ob, gla_in, yatt, ret_of, ret_ob, ret_in, gla_gain, ret_gain,
      ones_blockdiag, *row_args, mods, norm2_g, w_out, w_up, w_down, final_g, *rid_args)


def _rope_table(t_lat, n_ctx, head_dim, width):
    n_ax = head_dim // 4
    inv = np.float32(ROPE_THETA) ** (-np.arange(n_ax, dtype=np.float32) / np.float32(n_ax))
    pos = np.arange(t_lat)
    ang = np.concatenate([(pos // GRID_W).astype(np.float32)[:, None] * inv,
                          (pos % GRID_W).astype(np.float32)[:, None] * inv], axis=-1).astype(np.float32)
    cos, sin = np.cos(ang), np.sin(ang)
    cos = np.concatenate([np.ones((n_ctx, head_dim // 2), np.float32), cos], axis=0)
    sin = np.concatenate([np.zeros((n_ctx, head_dim // 2), np.float32), sin], axis=0)
    reps = width // head_dim
    return (jnp.asarray(np.tile(np.concatenate([cos, cos], axis=-1), (1, reps)), F32),
            jnp.asarray(np.tile(np.concatenate([-sin, sin], axis=-1), (1, reps)), F32))


def kernel(x, c, ctx, c_ctx, w_mod, b_mod, norm1_g, norm2_g, w_in, conv_w_dw, conv_b_dw, conv_ln_g,
           conv_ln_b, conv_w_pw, conv_b_pw, gla_w_a_f, gla_b_a_f, gla_w_a_b, gla_b_a_b, gla_norm_g,
           att_q_norm_g, att_k_norm_g, ret_norm_g, w_out, w_up, w_down, final_norm_g):
    bsz, t_lat, d = x.shape
    n_ctx = ctx.shape[1]
    depth = w_mod.shape[0]
    assert bsz + 1 <= 8 and n_ctx % ROW_TILE == 0 and t_lat % ROW_TILE == 0
    n_ctx_tiles = n_ctx // ROW_TILE
    dk_all = N_HEADS * SCAN_DK

    cc = jnp.concatenate([c, c_ctx[None, :], jnp.zeros((8 - bsz - 1, d), F32)], axis=0)
    mods = _modulation(cc, w_mod, b_mod)
    mods = mods.reshape(depth, 8, 1, N_MOD * d)

    stack_rows = lambda v: v.reshape(depth, 1, -1)
    w_in_b = w_in.astype(BF16)
    in_w = (w_in_b[:, :, :W_IN_A],
            jnp.pad(w_in_b[:, :, W_IN_A:W_IN_A + GLA_RANK], ((0, 0), (0, 0), (0, LANES - GLA_RANK))),
            w_in_b[:, :, W_IN_A + GLA_RANK:])
    pad_rank = lambda w: jnp.pad(w, ((0, 0), (0, LANES - GLA_RANK), (0, 0))).astype(BF16)
    w_a_f, w_a_b = pad_rank(gla_w_a_f), pad_rank(gla_w_a_b)
    w_pw = conv_w_pw.astype(BF16)
    q_gain = stack_rows(jnp.tile(att_q_norm_g, (1, N_HEADS)))
    k_gain = stack_rows(jnp.tile(att_k_norm_g, (1, KV_HEADS)))

    cos_att, sin_att = _rope_table(t_lat, n_ctx, HEAD_DV, N_HEADS * HEAD_DV)
    cos_ret, sin_ret = _rope_table(t_lat, n_ctx, SCAN_DK, dk_all)
    gamma = 1.0 - np.exp2(-5.0 - np.arange(N_HEADS, dtype=np.float64))
    log_gamma = jnp.asarray(np.repeat(np.log(gamma), SCAN_DK)[None, :], F32)
    gid = np.arange(GROUP_W) // HEAD_DV
    ones_blockdiag = jnp.asarray(gid[:, None] == gid[None, :], BF16)
    key_mask = jnp.asarray(gid[:, None] == (np.arange(dk_all) // SCAN_DK)[None, :], BF16)
    pos = np.arange(ROW_TILE)
    same_chunk = (pos[:, None] // CHUNK) == (pos[None, :] // CHUNK)
    cum_f = jnp.asarray(same_chunk & (pos[None, :] <= pos[:, None]), BF16)
    cum_b = jnp.asarray(same_chunk & (pos[None, :] >= pos[:, None]), BF16)

    x_lat, x_ctx = x, ctx
    for l in range(depth):
        with_ctx = l < depth - 1
        final = l == depth - 1
        conv_in, gla_in, ret_in, q, k, vt = _in_proj(
            l, x_lat, x_ctx, mods, stack_rows(norm1_g), l, *in_w, cos_att, sin_att, q_gain, k_gain,
            ones_blockdiag, n_ctx_tiles)
        mixer_riders = [(w, 0) for w in (w_out, w_up, w_down)] if l == 0 else []
        gla_of, gla_ob, ret_of, ret_ob, y_conv, *converted = _mixers(
            l, gla_in, ret_in, conv_in, n_ctx_tiles, w_a_f, stack_rows(gla_b_a_f), w_a_b,
            stack_rows(gla_b_a_b), log_gamma, cos_ret, sin_ret, key_mask, ones_blockdiag, cum_f, cum_b,
            conv_w_dw, stack_rows(conv_b_dw), stack_rows(conv_ln_g), stack_rows(conv_ln_b), w_pw,
            stack_rows(conv_b_pw), mixer_riders)
        if l == 0:
            post_w = tuple(w[None] for w in converted)
        y_att = _attention(q, k, vt, n_ctx_tiles, with_ctx)
        post_riders = [] if final else [(w, l + 1) for w in (w_out, w_up, w_down)]
        x_lat, *converted = _post(
            l, final, y_conv, gla_of, gla_ob, gla_in, y_att, ret_of, ret_ob, ret_in,
            stack_rows(gla_norm_g), stack_rows(ret_norm_g), ones_blockdiag, x_lat, x_ctx, mods,
            stack_rows(norm2_g), 0, *post_w, final_norm_g.reshape(1, -1), n_ctx_tiles, with_ctx,
            post_riders)
        if not final:
            post_w = tuple(w[None] for w in converted)
        x_ctx = None
    return x_lat
```

```python
import functools

import numpy as np
import jax
import jax.numpy as jnp
from jax import lax
from jax.experimental import pallas as pl
from jax.experimental.pallas import tpu as pltpu

F32 = jnp.float32
BF16 = jnp.bfloat16

EPS = 1e-6
GRID_W = 64
ROPE_THETA = 10000.0
N_MOD = 6
CONV_KSIZE = 31
CONV_PAD = (CONV_KSIZE - 1) // 2
HALO = 16
GLA_TAU = 16.0
GLA_RANK = 16
CHUNK = 64
N_HEADS = 4
KV_HEADS = 2
HEAD_DV = 64
SCAN_DK = 32
GROUP_W = 256
ROW_TILE = 256
LANES = 128
SUBLANES = 8
BF16_SUBLANES = 16
VT_ROWS = HEAD_DV + BF16_SUBLANES
LOG2_E = 1.4426950408889634
RIDER_CHUNKS = 64
ROW_BATCH = 2
MIXER_BATCH = 2
SCORE_LOOKAHEAD = 2
V7X_VMEM_BYTES = 64 * 1024 * 1024
VMEM_LIMIT = V7X_VMEM_BYTES * 3 // 4

W_CONV = 2 * GROUP_W
W_GLA = 2 * N_HEADS * SCAN_DK + 2 * GROUP_W + LANES
W_ATT = GROUP_W + 2 * KV_HEADS * HEAD_DV
W_RET = 2 * N_HEADS * SCAN_DK + 2 * GROUP_W
W_IN_A = W_CONV + W_GLA - LANES
W_IN_B = W_ATT + W_RET


def _params(semantics):
    return pltpu.CompilerParams(dimension_semantics=semantics, vmem_limit_bytes=VMEM_LIMIT)


def _layer_spec(arr, layer):
    rest = arr.shape[1:]
    return pl.BlockSpec((None,) + rest, lambda *_: (layer,) + (0,) * len(rest))


def _const_spec(arr):
    return pl.BlockSpec(arr.shape, lambda *_: (0,) * arr.ndim)


def _rider_specs(items, step_of, n_steps):
    in_specs, args, out_specs, out_shapes = [], [], [], []
    n_chunks = 1
    while 2 * n_chunks <= min(n_steps, RIDER_CHUNKS):
        n_chunks *= 2
    chunk = lambda *g: jnp.minimum(step_of(*g), n_chunks - 1)
    for arr, layer in items:
        _, rows, cols = arr.shape
        rpc = rows // n_chunks
        assert rows % n_chunks == 0 and rpc % BF16_SUBLANES == 0
        in_specs.append(pl.BlockSpec((None, rpc, cols), lambda *g, layer=layer: (layer, chunk(*g), 0)))
        args.append(arr)
        out_specs.append(pl.BlockSpec((rpc, cols), lambda *g: (chunk(*g), 0)))
        out_shapes.append(jax.ShapeDtypeStruct((rows, cols), BF16))
    return in_specs, args, out_specs, out_shapes


def _run_riders(in_refs, out_refs):
    for src, dst in zip(in_refs, out_refs):
        dst[...] = src[...].astype(BF16)


def _silu(x):
    return x * jax.nn.sigmoid(x)


def _rms(x):
    return x * lax.rsqrt(jnp.mean(x * x, axis=-1, keepdims=True) + EPS)


def _group_sumsq(t, ones_blockdiag):
    t2 = t * t
    hi = t2.astype(BF16)
    lo = (t2 - hi.astype(F32)).astype(BF16)
    return (jnp.dot(hi, ones_blockdiag, preferred_element_type=F32)
            + jnp.dot(lo, ones_blockdiag, preferred_element_type=F32))


def _load_rows(split, is_ctx, lat_ref, ctx_ref, buf):
    if not split:
        return lat_ref[...]

    @pl.when(is_ctx)
    def _():
        buf[...] = ctx_ref[...]

    @pl.when(jnp.logical_not(is_ctx))
    def _():
        buf[...] = lat_ref[...]

    return buf[...]


def _row_specs(split, x_lat, x_ctx, nb, tm, n_ctx_tiles, tile0):
    d = x_lat.shape[2]
    if not split:
        return [pl.BlockSpec((nb, tm, d), lambda b, j: (b, j + tile0, 0))], [x_lat]
    return ([pl.BlockSpec((nb, tm, d), lambda b, j: (b, jnp.maximum(j - n_ctx_tiles, 0), 0)),
             pl.BlockSpec((nb, tm, d), lambda b, j: (b, jnp.minimum(j, n_ctx_tiles - 1), 0))],
            [x_lat, x_ctx])


def _mod_specs(layer, nb, width, ctx_row):
    return [pl.BlockSpec((None, nb, 1, width), lambda b, j: (layer, b, 0, 0)),
            pl.BlockSpec((None, 1, 1, width), lambda b, j: (layer, ctx_row, 0, 0))]


def _mod_row(is_ctx, lat_mod_ref, ctx_mod_ref, bi):
    f = is_ctx.astype(F32)
    return f * ctx_mod_ref[0] + (1.0 - f) * lat_mod_ref[bi]


def _mod_body(cc_ref, w_ref, b_ref, o_ref):
    s = _silu(cc_ref[...])
    o_ref[0] = jnp.dot(s.astype(BF16), w_ref[0].astype(BF16),
                       preferred_element_type=F32) + b_ref[0]


def _modulation(cc, w_mod, b_mod):
    depth, d, n = w_mod.shape
    tn = 1024
    return pl.pallas_call(
        _mod_body,
        grid=(depth, n // tn),
        in_specs=[pl.BlockSpec((8, d), lambda l, i: (0, 0)),
                  pl.BlockSpec((1, d, tn), lambda l, i: (l, 0, i)),
                  pl.BlockSpec((1, 1, tn), lambda l, i: (l, 0, i))],
        out_specs=pl.BlockSpec((1, 8, tn), lambda l, i: (l, 0, i)),
        out_shape=jax.ShapeDtypeStruct((depth, 8, n), F32),
        compiler_params=_params(("parallel", "parallel")),
        name="modulation",
    )(cc, w_mod, b_mod.reshape(depth, 1, n))


def _inproj_body(split, n_ctx_tiles, *refs):
    n_src = 2 if split else 1
    lat_ref = refs[0]
    ctx_ref = refs[1] if split else None
    (mod_ref, modc_ref, g_ref, wa_ref, wz_ref, wb_ref, cos_ref, sin_ref, qg_ref, kg_ref, ones_ref,
     conv_ref, gla_ref, ret_ref, q_out, k_out, vt_out) = refs[n_src:n_src + 17]
    xbuf = refs[n_src + 17] if split else None
    nb, tm, d = lat_ref.shape
    is_ctx = pl.program_id(1) < n_ctx_tiles

    x = _load_rows(split, is_ctx, lat_ref, ctx_ref, xbuf)
    hb = []
    for bi in range(nb):
        mod = _mod_row(is_ctx, mod_ref, modc_ref, bi)
        shift, scale = mod[:, 0:d], mod[:, d:2 * d]
        hb.append(((_rms(x[bi]) * g_ref[...]) * (1.0 + scale) + shift).astype(BF16))
    hb = jnp.concatenate(hb, axis=0)

    def proj(w_ref, lo, width):
        return jnp.dot(hb, w_ref[:, lo:lo + width], preferred_element_type=F32)

    def grouped(t):
        return t.reshape(nb, tm, t.shape[1])

    wq = N_HEADS * HEAD_DV
    wk = KV_HEADS * HEAD_DV
    att = proj(wb_ref, 0, W_ATT)
    ss_q_all = _group_sumsq(att[:, :wq], ones_ref[...])
    ss_k_all = _group_sumsq(att[:, wq:wq + wk], ones_ref[:wk, :wk])
    gla_ref[:, :, W_GLA - LANES:] = grouped(proj(wz_ref, 0, LANES))
    ret_ref[...] = grouped(proj(wb_ref, W_ATT, W_RET))
    gla_ref[:, :, 0:W_GLA - LANES] = grouped(proj(wa_ref, W_CONV, W_GLA - LANES))
    conv_ref[...] = grouped(proj(wa_ref, 0, W_CONV))

    def norm_rope(t, ss, gain, w):
        t = t * lax.rsqrt(ss * (1.0 / HEAD_DV) + EPS) * gain
        lane = lax.broadcasted_iota(jnp.int32, t.shape, 1)
        first_half = (lane % HEAD_DV) < HEAD_DV // 2
        rot = jnp.where(first_half, pltpu.roll(t, w - HEAD_DV // 2, 1), pltpu.roll(t, HEAD_DV // 2, 1))
        return t * cos_ref[:, :w] + rot * sin_ref[:, :w]

    low = lax.broadcasted_iota(jnp.int32, (tm, LANES), 1) < HEAD_DV
    ones = jnp.ones((VT_ROWS - HEAD_DV, tm), BF16)
    for bi in range(nb):
        rows = slice(bi * tm, (bi + 1) * tm)
        q = att[rows, :wq]
        k = att[rows, wq:wq + wk]
        v = att[rows, wq + wk:]
        qn = norm_rope(q, ss_q_all[rows], qg_ref[...], wq) * (HEAD_DV ** -0.5 * LOG2_E)
        kn = norm_rope(k, ss_k_all[rows], kg_ref[...], wk)
        pair0 = qn[:, :LANES]
        pair1 = qn[:, LANES:]
        q_out[bi, 0] = jnp.where(low, pair0, 0.0).astype(BF16)
        q_out[bi, 1] = jnp.where(low, pltpu.roll(pair0, HEAD_DV, 1), 0.0).astype(BF16)
        q_out[bi, 2] = jnp.where(low, 0.0, pltpu.roll(pair1, HEAD_DV, 1)).astype(BF16)
        q_out[bi, 3] = jnp.where(low, 0.0, pair1).astype(BF16)
        k_out[bi] = kn.astype(BF16)
        vt = v.T.astype(BF16)
        for h in range(KV_HEADS):
            vt_out[bi, h, 0, 0:HEAD_DV, :] = vt[h * HEAD_DV:(h + 1) * HEAD_DV, :]
            vt_out[bi, h, 0, HEAD_DV:, :] = ones


def _in_proj(layer, x_lat, x_ctx, mods, norm1_g, w_layer, w_a, w_z, w_b, cos, sin, q_gain, k_gain,
             ones_blockdiag, n_ctx_tiles):
    split = x_ctx is not None
    bsz, _, d = x_lat.shape
    tm = ROW_TILE
    t_all = x_lat.shape[1] + (x_ctx.shape[1] if split else 0)
    n_tiles = t_all // tm
    ctx_row = bsz
    wq = N_HEADS * HEAD_DV
    nb = 1
    row_specs, row_args = _row_specs(split, x_lat, x_ctx, nb, tm, n_ctx_tiles, 0)
    tile = lambda b, j: (b, j, 0)
    return pl.pallas_call(
        functools.partial(_inproj_body, split, n_ctx_tiles),
        grid=(bsz // nb, n_tiles),
        in_specs=row_specs + _mod_specs(layer, nb, N_MOD * d, ctx_row) + [
            _layer_spec(norm1_g, layer), _layer_spec(w_a, w_layer), _layer_spec(w_z, w_layer),
            _layer_spec(w_b, w_layer),
            pl.BlockSpec((tm, wq), lambda b, j: (j, 0)), pl.BlockSpec((tm, wq), lambda b, j: (j, 0)),
            _layer_spec(q_gain, layer), _layer_spec(k_gain, layer), _const_spec(ones_blockdiag)],
        out_specs=[pl.BlockSpec((nb, tm, W_CONV), tile), pl.BlockSpec((nb, tm, W_GLA), tile),
                   pl.BlockSpec((nb, tm, W_RET), tile),
                   pl.BlockSpec((nb, N_HEADS, tm, LANES), lambda b, j: (b, 0, j, 0)),
                   pl.BlockSpec((nb, tm, LANES), tile),
                   pl.BlockSpec((nb, KV_HEADS, 1, VT_ROWS, tm), lambda b, j: (b, 0, j, 0, 0))],
        out_shape=[jax.ShapeDtypeStruct((bsz, t_all, W_CONV), F32),
                   jax.ShapeDtypeStruct((bsz, t_all, W_GLA), F32),
                   jax.ShapeDtypeStruct((bsz, t_all, W_RET), F32),
                   jax.ShapeDtypeStruct((bsz, N_HEADS, t_all, LANES), BF16),
                   jax.ShapeDtypeStruct((bsz, t_all, LANES), BF16),
                   jax.ShapeDtypeStruct((bsz, KV_HEADS, n_tiles, VT_ROWS, tm), BF16)],
        scratch_shapes=[pltpu.VMEM((nb, tm, d), F32)] if split else [],
        compiler_params=_params(("parallel", "parallel")),
        name="in_proj",
    )(*row_args, mods, mods, norm1_g, w_a, w_z, w_b, cos, sin, q_gain, k_gain, ones_blockdiag)


def _chunk_cumsum(la, tri_ref):
    w = la.shape[1]
    hi = la.astype(BF16)
    lo = (la - hi.astype(F32)).astype(BF16)
    both = jnp.dot(tri_ref[...], jnp.concatenate([hi, lo], axis=1), preferred_element_type=F32)
    return both[:, :w] + both[:, w:]


def _decay_factors(b, reverse):
    b_tot = b[0:1, :] if reverse else b[CHUNK - 1:CHUNK, :]
    dk_all = b.shape[1]
    dec_cols = jnp.broadcast_to(jnp.exp(b_tot), (dk_all, dk_all)).T
    dec_cols = jnp.concatenate([dec_cols] * (GROUP_W // dk_all), axis=1)
    return jnp.exp(b), jnp.exp(-b), jnp.exp(b_tot - b), dec_cols


def _scan_chunk_head(q, k, v, factors, st, key_mask, value_mask, head_mask):
    nt_dims = (((1,), (1,)), ((), ()))
    e_q, e_inv, e_end, dec_cols = factors
    q_dec = (q * e_q).astype(BF16)
    k_inv = (k * e_inv).astype(BF16)
    k_end = (k * e_end).astype(BF16)
    vb = v.astype(BF16)
    k_blk = jnp.concatenate([k_inv] * N_HEADS, axis=0) * key_mask
    scores = lax.dot_general(q_dec, k_blk, nt_dims, preferred_element_type=F32)
    o_inter = jnp.dot(q_dec, st.astype(BF16), preferred_element_type=F32)
    kv = lax.dot_general(k_end, vb, (((0,), (0,)), ((), ())), preferred_element_type=F32)
    st = st * dec_cols + jnp.where(head_mask, kv, 0.0)
    v_blk = jnp.concatenate([vb] * N_HEADS, axis=0) * value_mask
    return (scores, v_blk, o_inter), st


def _scan_chunk_tail(pending, tri):
    scores, v_blk, o_inter = pending
    scores = jnp.where(tri, scores, 0.0).astype(BF16)
    return jnp.dot(scores, v_blk, preferred_element_type=F32) + o_inter


def _scan_body(gf, gb, rf, rb, waf, baf, wab, bab, lg, cosf, sinf, cosb, sinb,
               key_mask_ref, value_mask_ref, cum_f_ref, cum_b_ref,
               gof, gob, rof, rob, st_all, b_all):
    n_batch = gf.shape[0]

    @pl.when(pl.program_id(1) == 0)
    def _():
        st_all[...] = jnp.zeros(st_all.shape, F32)

    tb = gf.shape[1]
    dk_all = N_HEADS * SCAN_DK
    dv_all = GROUP_W
    scale = SCAN_DK ** -0.5
    key_mask = key_mask_ref[...]
    value_mask = value_mask_ref[...]
    r = lax.broadcasted_iota(jnp.int32, (dk_all, dv_all), 0)
    c = lax.broadcasted_iota(jnp.int32, (dk_all, dv_all), 1)
    head_mask = (r // SCAN_DK) == (c // HEAD_DV)
    r = lax.broadcasted_iota(jnp.int32, (CHUNK, dv_all), 0)
    c = lax.broadcasted_iota(jnp.int32, (CHUNK, dv_all), 1) % CHUNK
    tri_f = c <= r
    tri_b = c >= r

    def qkv(ref, bi, lo):
        return (ref[bi, lo:lo + CHUNK, 0:dk_all], ref[bi, lo:lo + CHUNK, dk_all:2 * dk_all],
                ref[bi, lo:lo + CHUNK, 2 * dk_all:2 * dk_all + dv_all])

    def gla_chain(ref, bi, w_ref, bias_ref, cum_ref, reverse):
        slot = int(reverse)
        z = ref[bi, :, 2 * dk_all + 2 * dv_all:]
        pre = jnp.dot(z.astype(BF16), w_ref[...], preferred_element_type=F32) + bias_ref[...]
        log_sig = jnp.minimum(pre, 0.0) - jnp.log1p(jnp.exp(-jnp.abs(pre)))
        b_all[bi, slot] = _chunk_cumsum(log_sig / GLA_TAU, cum_ref)

        def load(lo):
            q, k, v = qkv(ref, bi, lo)
            return q * scale, k, v, _decay_factors(b_all[bi, slot, lo:lo + CHUNK, :], reverse)
        return load

    lane = lax.broadcasted_iota(jnp.int32, (CHUNK, dk_all), 1)
    first_half = (lane % SCAN_DK) < SCAN_DK // 2
    pos = lax.broadcasted_iota(jnp.int32, (CHUNK, dk_all), 0).astype(F32)

    ret_factors = [_decay_factors(((CHUNK - pos) if reverse else (pos + 1.0)) * lg[...], reverse)
                   for reverse in (False, True)]

    def ret_chain(ref, bi, cos_ref, sin_ref, reverse):
        def rope(t, lo):
            rot = jnp.where(first_half, pltpu.roll(t, dk_all - SCAN_DK // 2, 1),
                            pltpu.roll(t, SCAN_DK // 2, 1))
            return t * cos_ref[lo:lo + CHUNK, :] + rot * sin_ref[lo:lo + CHUNK, :]

        def load(lo):
            q, k, v = qkv(ref, bi, lo)
            return rope(q, lo), rope(k * scale, lo), v, ret_factors[int(reverse)]
        return load

    chains = []
    for bi in range(n_batch):
        chains += [
            (gla_chain(gf, bi, waf, baf, cum_f_ref, False), False, gof, bi, tri_f),
            (gla_chain(gb, bi, wab, bab, cum_b_ref, True), True, gob, bi, tri_b),
            (ret_chain(rf, bi, cosf, sinf, False), False, rof, bi, tri_f),
            (ret_chain(rb, bi, cosb, sinb, True), True, rob, bi, tri_b),
        ]
    st_vals = [st_all[n] for n in range(len(chains))]
    n_chunks = tb // CHUNK

    def heads(i):
        pending = []
        for n, (load, reverse, _, _, _) in enumerate(chains):
            lo = (n_chunks - 1 - i if reverse else i) * CHUNK
            q, k, v, factors = load(lo)
            part, st_vals[n] = _scan_chunk_head(q, k, v, factors, st_vals[n],
                                                key_mask, value_mask, head_mask)
            pending.append((lo, part))
        return pending

    pending = heads(0)
    for i in range(n_chunks):
        following = heads(i + 1) if i + 1 < n_chunks else None
        for (lo, part), (_, _, out_ref, bi, tri) in zip(pending, chains):
            out_ref[bi, lo:lo + CHUNK, :] = _scan_chunk_tail(part, tri)
        pending = following
    for n, val in enumerate(st_vals):
        st_all[n] = val


def _conv_tiles(j, n_ctx_tiles, n_tiles, main_ref, prev_ref, next_ref, wdw_ref, bdw_ref, lng_ref,
                lnb_ref, wpw_ref, bpw_ref, o_ref, ubuf, shifted, sbuf):
    n_batch, tt = main_ref.shape[0], main_ref.shape[1]
    c = o_ref.shape[2]

    def glu(blk):
        return blk[:, :c] * jax.nn.sigmoid(blk[:, c:])

    has_prev = jnp.logical_and(j != 0, j != n_ctx_tiles)
    has_next = jnp.logical_and(j != n_ctx_tiles - 1, j != n_tiles - 1)

    @pl.when(has_prev)
    def _():
        for bi in range(n_batch):
            ubuf[bi, 0:HALO, :] = glu(prev_ref[bi])

    @pl.when(jnp.logical_not(has_prev))
    def _():
        ubuf[:, 0:HALO, :] = jnp.zeros((n_batch, HALO, c), F32)

    @pl.when(has_next)
    def _():
        for bi in range(n_batch):
            ubuf[bi, HALO + tt:, :] = glu(next_ref[bi])

    @pl.when(jnp.logical_not(has_next))
    def _():
        ubuf[:, HALO + tt:, :] = jnp.zeros((n_batch, HALO, c), F32)

    span = shifted.shape[2]
    rows = 128
    for bi in range(n_batch):
        ubuf[bi, HALO:HALO + tt, :] = glu(main_ref[bi])
        for s in range(1, SUBLANES):
            shifted[bi, s - 1] = ubuf[bi, s:s + span, :]
        for r in range(0, tt, rows):
            acc = jnp.zeros((rows, c), F32)
            for k in range(CONV_KSIZE):
                off = k + HALO - CONV_PAD
                lo = r + off - off % SUBLANES
                if off % SUBLANES == 0:
                    tap = ubuf[bi, lo:lo + rows, :]
                else:
                    tap = shifted[bi, off % SUBLANES - 1, lo:lo + rows, :]
                acc = acc + tap * wdw_ref[k:k + 1, :]
            y = acc + bdw_ref[...]
            yc = y - jnp.mean(y, axis=-1, keepdims=True)
            yn = yc * lax.rsqrt(jnp.mean(yc * yc, axis=-1, keepdims=True) + EPS)
            yn = yn * lng_ref[...] + lnb_ref[...]
            sbuf[bi, r:r + rows, :] = _silu(yn).astype(BF16)
        o_ref[bi] = jnp.dot(sbuf[bi], wpw_ref[...], preferred_element_type=F32) + bpw_ref[...]


def _mixers_body(n_ctx_tiles, n_tiles, n_rid, *refs):
    scan_in, conv_in = refs[:17], refs[17:26]
    rider_in = refs[26:26 + n_rid]
    outs = refs[26 + n_rid:]
    scan_out, conv_out, rider_out = outs[:4], outs[4], outs[5:-5]
    scan_scratch, conv_scratch = outs[-5:-3], outs[-3:]
    _run_riders(rider_in, rider_out)
    _scan_body(*scan_in, *scan_out, *scan_scratch)
    _conv_tiles(pl.program_id(1), n_ctx_tiles, n_tiles, *conv_in, conv_out, *conv_scratch)


def _mixers(layer, gla_in, ret_in, conv_in, n_ctx_tiles, w_a_f, b_a_f, w_a_b, b_a_b, log_gamma, cos, sin,
            key_mask, value_mask, cum_f, cum_b, w_dw, b_dw, ln_g, ln_b, w_pw, b_pw, riders=()):
    bsz, t_all, _ = gla_in.shape
    tb = ROW_TILE
    n_tiles = t_all // tb
    dk_all = N_HEADS * SCAN_DK
    dv_all = GROUP_W
    c = GROUP_W
    nb = MIXER_BATCH if bsz % MIXER_BATCH == 0 else 1
    per = tb // HALO
    n_halo = t_all // HALO

    def bwd(j):
        return jnp.where(j < n_ctx_tiles, n_ctx_tiles - 1 - j, n_tiles - 1 - (j - n_ctx_tiles))

    fwd_blk = lambda b, j: (b, j, 0)
    bwd_blk = lambda b, j: (b, bwd(j), 0)
    n_steps = (bsz // nb) * n_tiles
    rid_in, rid_args, rid_out, rid_shapes = _rider_specs(riders, lambda b, j: b * n_tiles + j, n_steps)
    in_specs = [pl.BlockSpec((nb, tb, W_GLA), fwd_blk), pl.BlockSpec((nb, tb, W_GLA), bwd_blk),
                pl.BlockSpec((nb, tb, W_RET), fwd_blk), pl.BlockSpec((nb, tb, W_RET), bwd_blk),
                _layer_spec(w_a_f, layer), _layer_spec(b_a_f, layer),
                _layer_spec(w_a_b, layer), _layer_spec(b_a_b, layer),
                _const_spec(log_gamma),
                pl.BlockSpec((tb, dk_all), lambda b, j: (j, 0)),
                pl.BlockSpec((tb, dk_all), lambda b, j: (j, 0)),
                pl.BlockSpec((tb, dk_all), lambda b, j: (bwd(j), 0)),
                pl.BlockSpec((tb, dk_all), lambda b, j: (bwd(j), 0)),
                _const_spec(key_mask), _const_spec(value_mask), _const_spec(cum_f), _const_spec(cum_b),
                pl.BlockSpec((nb, tb, 2 * c), fwd_blk),
                pl.BlockSpec((nb, HALO, 2 * c), lambda b, j: (b, jnp.maximum(j * per - 1, 0), 0)),
                pl.BlockSpec((nb, HALO, 2 * c), lambda b, j: (b, jnp.minimum((j + 1) * per, n_halo - 1), 0)),
                _layer_spec(w_dw, layer), _layer_spec(b_dw, layer), _layer_spec(ln_g, layer),
                _layer_spec(ln_b, layer), _layer_spec(w_pw, layer), _layer_spec(b_pw, layer)] + rid_in
    return pl.pallas_call(
        functools.partial(_mixers_body, n_ctx_tiles, n_tiles, len(riders)),
        grid=(bsz // nb, n_tiles),
        in_specs=in_specs,
        out_specs=[pl.BlockSpec((nb, tb, dv_all), fwd_blk), pl.BlockSpec((nb, tb, dv_all), bwd_blk)] * 2
        + [pl.BlockSpec((nb, tb, c), fwd_blk)] + rid_out,
        out_shape=[jax.ShapeDtypeStruct((bsz, t_all, dv_all), F32)] * 5 + rid_shapes,
        scratch_shapes=[pltpu.VMEM((4 * nb, dk_all, dv_all), F32), pltpu.VMEM((nb, 2, tb, dk_all), F32),
                        pltpu.VMEM((nb, tb + 2 * HALO, c), F32),
                        pltpu.VMEM((nb, SUBLANES - 1, tb + 2 * HALO - SUBLANES, c), F32),
                        pltpu.VMEM((nb, tb, c), BF16)],
        compiler_params=_params(("arbitrary", "arbitrary")),
        name="mixers",
    )(gla_in, gla_in, ret_in, ret_in, w_a_f, b_a_f, w_a_b, b_a_b, log_gamma, cos, sin, cos, sin,
      key_mask, value_mask, cum_f, cum_b, conv_in, conv_in, conv_in, w_dw, b_dw, ln_g, ln_b, w_pw, b_pw,
      *rid_args)


def _att_body(n_ctx_tiles, n_tiles, tile0, q_ref, k_ref, vt_ref, o_ref):
    group = q_ref.shape[1]
    tq = q_ref.shape[2]
    tk = vt_ref.shape[4]
    nq = group * tq

    def attend(n_blocks):
        q2 = q_ref[0].reshape(nq, LANES)

        def scores(i):
            kb = k_ref[0, i * tk:(i + 1) * tk, :]
            return lax.dot_general(kb, q2, (((1,), (1,)), ((), ())), preferred_element_type=F32)

        m = acc = None
        ahead = [scores(i) for i in range(min(SCORE_LOOKAHEAD, n_blocks))]
        for i in range(n_blocks):
            s = ahead.pop(0)
            if i + SCORE_LOOKAHEAD < n_blocks:
                ahead.append(scores(i + SCORE_LOOKAHEAD))
            blk_max = jnp.max(s, axis=0, keepdims=True)
            m_new = blk_max if i == 0 else jnp.maximum(m, blk_max)
            p = jnp.exp2(s - m_new).astype(BF16)
            pv = jnp.dot(vt_ref[0, 0, i], p, preferred_element_type=F32)
            acc = pv if i == 0 else jnp.exp2(m - m_new) * acc + pv
            m = m_new
        o_t = acc[:HEAD_DV] / acc[HEAD_DV:HEAD_DV + 1]
        stacked = jnp.concatenate([o_t[:, g * tq:(g + 1) * tq] for g in range(group)], axis=0)
        o_ref[0] = stacked.T

    qi = pl.program_id(2) + tile0
    if tile0 < n_ctx_tiles:
        pl.when(qi < n_ctx_tiles)(lambda: attend(n_ctx_tiles))
    pl.when(qi >= n_ctx_tiles)(lambda: attend(n_tiles))


def _attention(q, k, vt, n_ctx_tiles, with_ctx):
    bsz, _, t_all, _ = q.shape
    tq = ROW_TILE
    n_tiles = t_all // tq
    tile0 = 0 if with_ctx else n_ctx_tiles
    group = N_HEADS // KV_HEADS
    return pl.pallas_call(
        functools.partial(_att_body, n_ctx_tiles, n_tiles, tile0),
        grid=(bsz, KV_HEADS, n_tiles - tile0),
        in_specs=[pl.BlockSpec((1, group, tq, LANES), lambda b, h, i: (b, h, i + tile0, 0)),
                  pl.BlockSpec((1, t_all, LANES), lambda b, h, i: (b, 0, 0)),
                  pl.BlockSpec((1, 1, n_tiles, VT_ROWS, tq), lambda b, h, i: (b, h, 0, 0, 0))],
        out_specs=pl.BlockSpec((1, tq, group * HEAD_DV), lambda b, h, i: (b, i, h)),
        out_shape=jax.ShapeDtypeStruct((bsz, t_all - tile0 * tq, GROUP_W), F32),
        compiler_params=_params(("parallel", "parallel", "arbitrary")),
        name="attention",
    )(q, k, vt)


def _post_body(final, split, n_ctx_tiles, tile0, n_rid, *refs):
    (yconv, gla_f, gla_b, gla_gate, yatt, ret_f, ret_b, ret_gate, gla_gain, ret_gain, ones_ref) = refs[:11]
    n_src = 2 if split else 1
    lat_ref = refs[11]
    ctx_ref = refs[12] if split else None
    n_in = 18 + n_src
    mod_ref, modc_ref, n2g_ref, wout_ref, wup_ref, wdown_ref, fng_ref = refs[11 + n_src:n_in]
    o_ref = refs[n_in + n_rid]
    xbuf = refs[n_in + 2 * n_rid + 1] if split else None
    _run_riders(refs[n_in:n_in + n_rid], refs[n_in + n_rid + 1:n_in + 2 * n_rid + 1])
    nb, tm, d = lat_ref.shape
    rows = nb * tm
    d_ff = wup_ref.shape[1]
    is_ctx = pl.program_id(1) + tile0 < n_ctx_tiles

    def stacked(ref):
        return ref[...].reshape(rows, ref.shape[2])

    def finish(of_ref, ob_ref, gate_ref, gain_ref):
        o = stacked(of_ref) + stacked(ob_ref)
        ss = _group_sumsq(o, ones_ref[...])
        y = o * lax.rsqrt(ss * (1.0 / HEAD_DV) + EPS) * gain_ref[...]
        return (y * _silu(stacked(gate_ref))).astype(BF16)

    mixed = jnp.concatenate([stacked(yconv).astype(BF16), finish(gla_f, gla_b, gla_gate, gla_gain),
                             stacked(yatt).astype(BF16), finish(ret_f, ret_b, ret_gate, ret_gain)], axis=1)
    o = jnp.dot(mixed, wout_ref[...], preferred_element_type=F32)
    x = _load_rows(split, is_ctx, lat_ref, ctx_ref, xbuf)
    mods = [_mod_row(is_ctx, mod_ref, modc_ref, bi) for bi in range(nb)]
    x_mid, h2 = [], []
    for bi in range(nb):
        gate1 = mods[bi][:, 2 * d:3 * d]
        shift2 = mods[bi][:, 3 * d:4 * d]
        scale2 = mods[bi][:, 4 * d:5 * d]
        x_mid.append(x[bi] + gate1 * o[bi * tm:(bi + 1) * tm])
        h2.append(((_rms(x_mid[bi]) * n2g_ref[...]) * (1.0 + scale2) + shift2).astype(BF16))
    h2 = jnp.concatenate(h2, axis=0)
    acc = jnp.zeros((rows, d), F32)
    tf = 1024
    for f in range(0, d_ff, tf):
        u = jnp.dot(h2, wup_ref[:, f:f + tf], preferred_element_type=F32)
        a = jnp.square(jnp.maximum(u, 0.0)).astype(BF16)
        acc = acc + jnp.dot(a, wdown_ref[f:f + tf, :], preferred_element_type=F32)
    for bi in range(nb):
        gate2 = mods[bi][:, 5 * d:6 * d]
        x_out = x_mid[bi] + gate2 * acc[bi * tm:(bi + 1) * tm]
        if final:
            x_out = _rms(x_out) * fng_ref[...]
        o_ref[bi] = x_out


def _post(layer, final, yconv, gla_of, gla_ob, gla_in, yatt, ret_of, ret_ob, ret_in, gla_gain, ret_gain,
          ones_blockdiag, x_lat, x_ctx, mods, norm2_g, w_layer, w_out, w_up, w_down, final_g, n_ctx_tiles,
          with_ctx, riders=()):
    split = x_ctx is not None
    bsz, _, d = x_lat.shape
    tm = ROW_TILE
    t_all = gla_of.shape[1]
    ctx_row = bsz
    tile0 = 0 if with_ctx else n_ctx_tiles
    n_out = t_all // tm - tile0
    c = GROUP_W
    gate_blk = (2 * N_HEADS * SCAN_DK + c) // c
    full = lambda b, j: (b, j + tile0, 0)
    own = lambda b, j: (b, j, 0)
    assert not split or tile0 == 0
    nb = ROW_BATCH if bsz % ROW_BATCH == 0 else 1
    row_specs, row_args = _row_specs(split, x_lat, x_ctx, nb, tm, n_ctx_tiles, tile0)

    def resident(arr):
        rest = arr.shape[1:]
        return pl.BlockSpec((None,) + rest, lambda *_: (w_layer,) + (0,) * len(rest),
                            pipeline_mode=pl.Buffered(1))

    n_steps = (bsz // nb) * n_out
    rid_in, rid_args, rid_out, rid_shapes = _rider_specs(riders, lambda b, j: b * n_out + j, n_steps)
    return pl.pallas_call(
        functools.partial(_post_body, final, split, n_ctx_tiles, tile0, len(riders)),
        grid=(bsz // nb, n_out),
        in_specs=[pl.BlockSpec((nb, tm, c), full),
                  pl.BlockSpec((nb, tm, c), full), pl.BlockSpec((nb, tm, c), full),
                  pl.BlockSpec((nb, tm, c), lambda b, j: (b, j + tile0, gate_blk)),
                  pl.BlockSpec((nb, tm, c), own),
                  pl.BlockSpec((nb, tm, c), full), pl.BlockSpec((nb, tm, c), full),
                  pl.BlockSpec((nb, tm, c), lambda b, j: (b, j + tile0, gate_blk)),
                  _layer_spec(gla_gain, layer), _layer_spec(ret_gain, layer),
                  _const_spec(ones_blockdiag)] + row_specs + _mod_specs(layer, nb, N_MOD * d, ctx_row) + [
                  _layer_spec(norm2_g, layer),
                  resident(w_out), resident(w_up), resident(w_down),
                  _const_spec(final_g)] + rid_in,
        out_specs=[pl.BlockSpec((nb, tm, d), own)] + rid_out,
        out_shape=[jax.ShapeDtypeStruct((bsz, n_out * tm, d), F32)] + rid_shapes,
        scratch_shapes=[pltpu.VMEM((nb, tm, d), F32)] if split else [],
        compiler_params=_params(("arbitrary", "arbitrary") if riders else ("parallel", "parallel")),
        name="post",
    )(yconv, gla_of, gla_ob, gla_in, yatt, ret_of, ret_ob, ret_in, gla_gain, ret_gain,
      ones_blockdiag, *row_args, mods, mods, norm2_g, w_out, w_up, w_down, final_g, *rid_args)


def _rope_table(t_lat, n_ctx, head_dim, width):
    n_ax = head_dim // 4
    inv = np.float32(ROPE_THETA) ** (-np.arange(n_ax, dtype=np.float32) / np.float32(n_ax))
    pos = np.arange(t_lat)
    ang = np.concatenate([(pos // GRID_W).astype(np.float32)[:, None] * inv,
                          (pos % GRID_W).astype(np.float32)[:, None] * inv], axis=-1).astype(np.float32)
    cos, sin = np.cos(ang), np.sin(ang)
    cos = np.concatenate([np.ones((n_ctx, head_dim // 2), np.float32), cos], axis=0)
    sin = np.concatenate([np.zeros((n_ctx, head_dim // 2), np.float32), sin], axis=0)
    reps = width // head_dim
    return (jnp.asarray(np.tile(np.concatenate([cos, cos], axis=-1), (1, reps)), F32),
            jnp.asarray(np.tile(np.concatenate([-sin, sin], axis=-1), (1, reps)), F32))


def kernel(x, c, ctx, c_ctx, w_mod, b_mod, norm1_g, norm2_g, w_in, conv_w_dw, conv_b_dw, conv_ln_g,
           conv_ln_b, conv_w_pw, conv_b_pw, gla_w_a_f, gla_b_a_f, gla_w_a_b, gla_b_a_b, gla_norm_g,
           att_q_norm_g, att_k_norm_g, ret_norm_g, w_out, w_up, w_down, final_norm_g):
    bsz, t_lat, d = x.shape
    n_ctx = ctx.shape[1]
    depth = w_mod.shape[0]
    assert bsz + 1 <= 8 and n_ctx % ROW_TILE == 0 and t_lat % ROW_TILE == 0
    n_ctx_tiles = n_ctx // ROW_TILE
    dk_all = N_HEADS * SCAN_DK

    cc = jnp.concatenate([c, c_ctx[None, :], jnp.zeros((8 - bsz - 1, d), F32)], axis=0)
    mods = _modulation(cc, w_mod, b_mod)
    mods = mods.reshape(depth, 8, 1, N_MOD * d)

    stack_rows = lambda v: v.reshape(depth, 1, -1)
    w_in_b = w_in.astype(BF16)
    in_w = (w_in_b[:, :, :W_IN_A],
            jnp.pad(w_in_b[:, :, W_IN_A:W_IN_A + GLA_RANK], ((0, 0), (0, 0), (0, LANES - GLA_RANK))),
            w_in_b[:, :, W_IN_A + GLA_RANK:])
    pad_rank = lambda w: jnp.pad(w, ((0, 0), (0, LANES - GLA_RANK), (0, 0))).astype(BF16)
    w_a_f, w_a_b = pad_rank(gla_w_a_f), pad_rank(gla_w_a_b)
    w_pw = conv_w_pw.astype(BF16)
    q_gain = stack_rows(jnp.tile(att_q_norm_g, (1, N_HEADS)))
    k_gain = stack_rows(jnp.tile(att_k_norm_g, (1, KV_HEADS)))

    cos_att, sin_att = _rope_table(t_lat, n_ctx, HEAD_DV, N_HEADS * HEAD_DV)
    cos_ret, sin_ret = _rope_table(t_lat, n_ctx, SCAN_DK, dk_all)
    gamma = 1.0 - np.exp2(-5.0 - np.arange(N_HEADS, dtype=np.float64))
    log_gamma = jnp.asarray(np.repeat(np.log(gamma), SCAN_DK)[None, :], F32)
    gid = np.arange(GROUP_W) // HEAD_DV
    ones_blockdiag = jnp.asarray(gid[:, None] == gid[None, :], BF16)
    key_mask = jnp.asarray(gid[:, None] == (np.arange(dk_all) // SCAN_DK)[None, :], BF16)
    pos = np.arange(ROW_TILE)
    same_chunk = (pos[:, None] // CHUNK) == (pos[None, :] // CHUNK)
    cum_f = jnp.asarray(same_chunk & (pos[None, :] <= pos[:, None]), BF16)
    cum_b = jnp.asarray(same_chunk & (pos[None, :] >= pos[:, None]), BF16)

    x_lat, x_ctx = x, ctx
    for l in range(depth):
        with_ctx = l < depth - 1
        final = l == depth - 1
        conv_in, gla_in, ret_in, q, k, vt = _in_proj(
            l, x_lat, x_ctx, mods, stack_rows(norm1_g), l, *in_w, cos_att, sin_att, q_gain, k_gain,
            ones_blockdiag, n_ctx_tiles)
        mixer_riders = [(w, 0) for w in (w_out, w_up, w_down)] if l == 0 else []
        gla_of, gla_ob, ret_of, ret_ob, y_conv, *converted = _mixers(
            l, gla_in, ret_in, conv_in, n_ctx_tiles, w_a_f, stack_rows(gla_b_a_f), w_a_b,
            stack_rows(gla_b_a_b), log_gamma, cos_ret, sin_ret, key_mask, ones_blockdiag, cum_f, cum_b,
            conv_w_dw, stack_rows(conv_b_dw), stack_rows(conv_ln_g), stack_rows(conv_ln_b), w_pw,
            stack_rows(conv_b_pw), mixer_riders)
        if l == 0:
            post_w = tuple(w[None] for w in converted)
        y_att = _attention(q, k, vt, n_ctx_tiles, with_ctx)
        post_riders = [] if final else [(w, l + 1) for w in (w_out, w_up, w_down)]
        x_lat, *converted = _post(
            l, final, y_conv, gla_of, gla_ob, gla_in, y_att, ret_of, ret_ob, ret_in,
            stack_rows(gla_norm_g), stack_rows(ret_norm_g), ones_blockdiag, x_lat, x_ctx, mods,
            stack_rows(norm2_g), 0, *post_w, final_norm_g.reshape(1, -1), n_ctx_tiles, with_ctx,
            post_riders)
        if not final:
            post_w = tuple(w[None] for w in converted)
        x_ctx = None
    return x_lat
```

```python
import functools

import numpy as np
import jax
import jax.numpy as jnp
from jax import lax
from jax.experimental import pallas as pl
from jax.experimental.pallas import tpu as pltpu

F32 = jnp.float32
BF16 = jnp.bfloat16

EPS = 1e-6
GRID_W = 64
ROPE_THETA = 10000.0
N_MOD = 6
CONV_KSIZE = 31
CONV_PAD = (CONV_KSIZE - 1) // 2
HALO = 16
GLA_TAU = 16.0
GLA_RANK = 16
CHUNK = 64
N_HEADS = 4
KV_HEADS = 2
HEAD_DV = 64
SCAN_DK = 32
GROUP_W = 256
ROW_TILE = 256
LANES = 128
SUBLANES = 8
BF16_SUBLANES = 16
VT_ROWS = HEAD_DV + BF16_SUBLANES
LOG2_E = 1.4426950408889634
RIDER_CHUNKS = 64
ROW_BATCH = 2
ATT_BATCH = 2
MIXER_BATCH = 2
SCORE_LOOKAHEAD = 2
V7X_VMEM_BYTES = 64 * 1024 * 1024
VMEM_LIMIT = V7X_VMEM_BYTES * 3 // 4

W_CONV = 2 * GROUP_W
W_GLA = 2 * N_HEADS * SCAN_DK + 2 * GROUP_W + LANES
W_ATT = GROUP_W + 2 * KV_HEADS * HEAD_DV
W_RET = 2 * N_HEADS * SCAN_DK + 2 * GROUP_W
W_IN_A = W_CONV + W_GLA - LANES
W_IN_B = W_ATT + W_RET


def _params(semantics):
    return pltpu.CompilerParams(dimension_semantics=semantics, vmem_limit_bytes=VMEM_LIMIT)


def _layer_spec(arr, layer):
    rest = arr.shape[1:]
    return pl.BlockSpec((None,) + rest, lambda *_: (layer,) + (0,) * len(rest))


def _const_spec(arr):
    return pl.BlockSpec(arr.shape, lambda *_: (0,) * arr.ndim)


def _rider_specs(items, step_of, n_steps):
    in_specs, args, out_specs, out_shapes = [], [], [], []
    n_chunks = 1
    while 2 * n_chunks <= min(n_steps, RIDER_CHUNKS):
        n_chunks *= 2
    chunk = lambda *g: jnp.minimum(step_of(*g), n_chunks - 1)
    for arr, layer in items:
        _, rows, cols = arr.shape
        rpc = rows // n_chunks
        assert rows % n_chunks == 0 and rpc % BF16_SUBLANES == 0
        in_specs.append(pl.BlockSpec((None, rpc, cols), lambda *g, layer=layer: (layer, chunk(*g), 0)))
        args.append(arr)
        out_specs.append(pl.BlockSpec((rpc, cols), lambda *g: (chunk(*g), 0)))
        out_shapes.append(jax.ShapeDtypeStruct((rows, cols), BF16))
    return in_specs, args, out_specs, out_shapes


def _run_riders(in_refs, out_refs):
    for src, dst in zip(in_refs, out_refs):
        dst[...] = src[...].astype(BF16)


def _silu(x):
    return x * jax.nn.sigmoid(x)


def _rms(x):
    return x * lax.rsqrt(jnp.mean(x * x, axis=-1, keepdims=True) + EPS)


def _group_sumsq(t, ones_blockdiag):
    t2 = t * t
    hi = t2.astype(BF16)
    lo = (t2 - hi.astype(F32)).astype(BF16)
    return (jnp.dot(hi, ones_blockdiag, preferred_element_type=F32)
            + jnp.dot(lo, ones_blockdiag, preferred_element_type=F32))


def _load_rows(split, is_ctx, lat_ref, ctx_ref, buf):
    if not split:
        return lat_ref[...]

    @pl.when(is_ctx)
    def _():
        buf[...] = ctx_ref[...]

    @pl.when(jnp.logical_not(is_ctx))
    def _():
        buf[...] = lat_ref[...]

    return buf[...]


def _row_specs(split, x_lat, x_ctx, nb, tm, n_ctx_tiles, tile0):
    d = x_lat.shape[2]
    if not split:
        return [pl.BlockSpec((nb, tm, d), lambda b, j: (b, j + tile0, 0))], [x_lat]
    return ([pl.BlockSpec((nb, tm, d), lambda b, j: (b, jnp.maximum(j - n_ctx_tiles, 0), 0)),
             pl.BlockSpec((nb, tm, d), lambda b, j: (b, jnp.minimum(j, n_ctx_tiles - 1), 0))],
            [x_lat, x_ctx])


def _mod_specs(layer, nb, width, ctx_row):
    return [pl.BlockSpec((None, nb, 1, width), lambda b, j: (layer, b, 0, 0)),
            pl.BlockSpec((None, 1, 1, width), lambda b, j: (layer, ctx_row, 0, 0))]


def _mod_row(is_ctx, lat_mod_ref, ctx_mod_ref, bi):
    f = is_ctx.astype(F32)
    return f * ctx_mod_ref[0] + (1.0 - f) * lat_mod_ref[bi]


def _mod_body(cc_ref, w_ref, b_ref, o_ref):
    s = _silu(cc_ref[...])
    o_ref[0] = jnp.dot(s.astype(BF16), w_ref[0].astype(BF16),
                       preferred_element_type=F32) + b_ref[0]


def _modulation(cc, w_mod, b_mod):
    depth, d, n = w_mod.shape
    tn = 1024
    return pl.pallas_call(
        _mod_body,
        grid=(depth, n // tn),
        in_specs=[pl.BlockSpec((8, d), lambda l, i: (0, 0)),
                  pl.BlockSpec((1, d, tn), lambda l, i: (l, 0, i)),
                  pl.BlockSpec((1, 1, tn), lambda l, i: (l, 0, i))],
        out_specs=pl.BlockSpec((1, 8, tn), lambda l, i: (l, 0, i)),
        out_shape=jax.ShapeDtypeStruct((depth, 8, n), F32),
        compiler_params=_params(("parallel", "parallel")),
        name="modulation",
    )(cc, w_mod, b_mod.reshape(depth, 1, n))


def _inproj_body(split, n_ctx_tiles, *refs):
    n_src = 2 if split else 1
    lat_ref = refs[0]
    ctx_ref = refs[1] if split else None
    (mod_ref, modc_ref, g_ref, wa_ref, wz_ref, wb_ref, cos_ref, sin_ref, qg_ref, kg_ref, ones_ref,
     conv_ref, gla_ref, ret_ref, q_out, k_out, vt_out) = refs[n_src:n_src + 17]
    xbuf = refs[n_src + 17] if split else None
    nb, tm, d = lat_ref.shape
    is_ctx = pl.program_id(1) < n_ctx_tiles

    x = _load_rows(split, is_ctx, lat_ref, ctx_ref, xbuf)
    hb = []
    for bi in range(nb):
        mod = _mod_row(is_ctx, mod_ref, modc_ref, bi)
        shift, scale = mod[:, 0:d], mod[:, d:2 * d]
        hb.append(((_rms(x[bi]) * g_ref[...]) * (1.0 + scale) + shift).astype(BF16))
    hb = jnp.concatenate(hb, axis=0)

    def proj(w_ref, lo, width):
        return jnp.dot(hb, w_ref[:, lo:lo + width], preferred_element_type=F32)

    def grouped(t):
        return t.reshape(nb, tm, t.shape[1])

    wq = N_HEADS * HEAD_DV
    wk = KV_HEADS * HEAD_DV
    att = proj(wb_ref, 0, W_ATT)
    ss_q_all = _group_sumsq(att[:, :wq], ones_ref[...])
    ss_k_all = _group_sumsq(att[:, wq:wq + wk], ones_ref[:wk, :wk])
    gla_ref[:, :, W_GLA - LANES:] = grouped(proj(wz_ref, 0, LANES))
    ret_ref[...] = grouped(proj(wb_ref, W_ATT, W_RET))
    gla_ref[:, :, 0:W_GLA - LANES] = grouped(proj(wa_ref, W_CONV, W_GLA - LANES))
    conv_ref[...] = grouped(proj(wa_ref, 0, W_CONV))

    def norm_rope(t, ss, gain, w):
        t = t * lax.rsqrt(ss * (1.0 / HEAD_DV) + EPS) * gain
        lane = lax.broadcasted_iota(jnp.int32, t.shape, 1)
        first_half = (lane % HEAD_DV) < HEAD_DV // 2
        rot = jnp.where(first_half, pltpu.roll(t, w - HEAD_DV // 2, 1), pltpu.roll(t, HEAD_DV // 2, 1))
        return t * cos_ref[:, :w] + rot * sin_ref[:, :w]

    low = lax.broadcasted_iota(jnp.int32, (tm, LANES), 1) < HEAD_DV
    ones = jnp.ones((VT_ROWS - HEAD_DV, tm), BF16)
    for bi in range(nb):
        rows = slice(bi * tm, (bi + 1) * tm)
        q = att[rows, :wq]
        k = att[rows, wq:wq + wk]
        v = att[rows, wq + wk:]
        qn = norm_rope(q, ss_q_all[rows], qg_ref[...], wq) * (HEAD_DV ** -0.5 * LOG2_E)
        kn = norm_rope(k, ss_k_all[rows], kg_ref[...], wk)
        pair0 = qn[:, :LANES]
        pair1 = qn[:, LANES:]
        q_out[bi, 0] = jnp.where(low, pair0, 0.0).astype(BF16)
        q_out[bi, 1] = jnp.where(low, pltpu.roll(pair0, HEAD_DV, 1), 0.0).astype(BF16)
        q_out[bi, 2] = jnp.where(low, 0.0, pltpu.roll(pair1, HEAD_DV, 1)).astype(BF16)
        q_out[bi, 3] = jnp.where(low, 0.0, pair1).astype(BF16)
        k_out[bi] = kn.astype(BF16)
        vt = v.T.astype(BF16)
        for h in range(KV_HEADS):
            vt_out[bi, h, 0, 0:HEAD_DV, :] = vt[h * HEAD_DV:(h + 1) * HEAD_DV, :]
            vt_out[bi, h, 0, HEAD_DV:, :] = ones


def _in_proj(layer, x_lat, x_ctx, mods, norm1_g, w_layer, w_a, w_z, w_b, cos, sin, q_gain, k_gain,
             ones_blockdiag, n_ctx_tiles):
    split = x_ctx is not None
    bsz, _, d = x_lat.shape
    tm = ROW_TILE
    t_all = x_lat.shape[1] + (x_ctx.shape[1] if split else 0)
    n_tiles = t_all // tm
    ctx_row = bsz
    wq = N_HEADS * HEAD_DV
    nb = 1
    row_specs, row_args = _row_specs(split, x_lat, x_ctx, nb, tm, n_ctx_tiles, 0)
    tile = lambda b, j: (b, j, 0)
    return pl.pallas_call(
        functools.partial(_inproj_body, split, n_ctx_tiles),
        grid=(bsz // nb, n_tiles),
        in_specs=row_specs + _mod_specs(layer, nb, N_MOD * d, ctx_row) + [
            _layer_spec(norm1_g, layer), _layer_spec(w_a, w_layer), _layer_spec(w_z, w_layer),
            _layer_spec(w_b, w_layer),
            pl.BlockSpec((tm, wq), lambda b, j: (j, 0)), pl.BlockSpec((tm, wq), lambda b, j: (j, 0)),
            _layer_spec(q_gain, layer), _layer_spec(k_gain, layer), _const_spec(ones_blockdiag)],
        out_specs=[pl.BlockSpec((nb, tm, W_CONV), tile), pl.BlockSpec((nb, tm, W_GLA), tile),
                   pl.BlockSpec((nb, tm, W_RET), tile),
                   pl.BlockSpec((nb, N_HEADS, tm, LANES), lambda b, j: (b, 0, j, 0)),
                   pl.BlockSpec((nb, tm, LANES), tile),
                   pl.BlockSpec((nb, KV_HEADS, 1, VT_ROWS, tm), lambda b, j: (b, 0, j, 0, 0))],
        out_shape=[jax.ShapeDtypeStruct((bsz, t_all, W_CONV), F32),
                   jax.ShapeDtypeStruct((bsz, t_all, W_GLA), F32),
                   jax.ShapeDtypeStruct((bsz, t_all, W_RET), F32),
                   jax.ShapeDtypeStruct((bsz, N_HEADS, t_all, LANES), BF16),
                   jax.ShapeDtypeStruct((bsz, t_all, LANES), BF16),
                   jax.ShapeDtypeStruct((bsz, KV_HEADS, n_tiles, VT_ROWS, tm), BF16)],
        scratch_shapes=[pltpu.VMEM((nb, tm, d), F32)] if split else [],
        compiler_params=_params(("parallel", "parallel")),
        name="in_proj",
    )(*row_args, mods, mods, norm1_g, w_a, w_z, w_b, cos, sin, q_gain, k_gain, ones_blockdiag)


def _chunk_cumsum(la, tri_ref):
    w = la.shape[1]
    hi = la.astype(BF16)
    lo = (la - hi.astype(F32)).astype(BF16)
    both = jnp.dot(tri_ref[...], jnp.concatenate([hi, lo], axis=1), preferred_element_type=F32)
    return both[:, :w] + both[:, w:]


def _decay_factors(b, reverse):
    b_tot = b[0:1, :] if reverse else b[CHUNK - 1:CHUNK, :]
    dk_all = b.shape[1]
    dec_cols = jnp.broadcast_to(jnp.exp(b_tot), (dk_all, dk_all)).T
    dec_cols = jnp.concatenate([dec_cols] * (GROUP_W // dk_all), axis=1)
    return jnp.exp(b), jnp.exp(-b), jnp.exp(b_tot - b), dec_cols


def _scan_chunk_head(q, k, v, factors, st, key_mask, value_mask, head_mask):
    nt_dims = (((1,), (1,)), ((), ()))
    e_q, e_inv, e_end, dec_cols = factors
    q_dec = (q * e_q).astype(BF16)
    k_inv = (k * e_inv).astype(BF16)
    k_end = (k * e_end).astype(BF16)
    vb = v.astype(BF16)
    k_blk = jnp.concatenate([k_inv] * N_HEADS, axis=0) * key_mask
    scores = lax.dot_general(q_dec, k_blk, nt_dims, preferred_element_type=F32)
    o_inter = jnp.dot(q_dec, st.astype(BF16), preferred_element_type=F32)
    kv = lax.dot_general(k_end, vb, (((0,), (0,)), ((), ())), preferred_element_type=F32)
    st = st * dec_cols + jnp.where(head_mask, kv, 0.0)
    v_blk = jnp.concatenate([vb] * N_HEADS, axis=0) * value_mask
    return (scores, v_blk, o_inter), st


def _scan_chunk_tail(pending, tri):
    scores, v_blk, o_inter = pending
    scores = jnp.where(tri, scores, 0.0).astype(BF16)
    return jnp.dot(scores, v_blk, preferred_element_type=F32) + o_inter


def _scan_body(gf, gb, rf, rb, waf, baf, wab, bab, lg, cosf, sinf, cosb, sinb,
               key_mask_ref, value_mask_ref, cum_f_ref, cum_b_ref,
               gof, gob, rof, rob, st_all, b_all):
    n_batch = gf.shape[0]

    @pl.when(pl.program_id(1) == 0)
    def _():
        st_all[...] = jnp.zeros(st_all.shape, F32)

    tb = gf.shape[1]
    dk_all = N_HEADS * SCAN_DK
    dv_all = GROUP_W
    scale = SCAN_DK ** -0.5
    key_mask = key_mask_ref[...]
    value_mask = value_mask_ref[...]
    r = lax.broadcasted_iota(jnp.int32, (dk_all, dv_all), 0)
    c = lax.broadcasted_iota(jnp.int32, (dk_all, dv_all), 1)
    head_mask = (r // SCAN_DK) == (c // HEAD_DV)
    r = lax.broadcasted_iota(jnp.int32, (CHUNK, dv_all), 0)
    c = lax.broadcasted_iota(jnp.int32, (CHUNK, dv_all), 1) % CHUNK
    tri_f = c <= r
    tri_b = c >= r

    def qkv(ref, bi, lo):
        return (ref[bi, lo:lo + CHUNK, 0:dk_all], ref[bi, lo:lo + CHUNK, dk_all:2 * dk_all],
                ref[bi, lo:lo + CHUNK, 2 * dk_all:2 * dk_all + dv_all])

    def gla_chain(ref, bi, w_ref, bias_ref, cum_ref, reverse):
        slot = int(reverse)
        z = ref[bi, :, 2 * dk_all + 2 * dv_all:]
        pre = jnp.dot(z.astype(BF16), w_ref[...], preferred_element_type=F32) + bias_ref[...]
        log_sig = jnp.minimum(pre, 0.0) - jnp.log1p(jnp.exp(-jnp.abs(pre)))
        b_all[bi, slot] = _chunk_cumsum(log_sig / GLA_TAU, cum_ref)

        def load(lo):
            q, k, v = qkv(ref, bi, lo)
            return q * scale, k, v, _decay_factors(b_all[bi, slot, lo:lo + CHUNK, :], reverse)
        return load

    lane = lax.broadcasted_iota(jnp.int32, (CHUNK, dk_all), 1)
    first_half = (lane % SCAN_DK) < SCAN_DK // 2
    pos = lax.broadcasted_iota(jnp.int32, (CHUNK, dk_all), 0).astype(F32)

    ret_factors = [_decay_factors(((CHUNK - pos) if reverse else (pos + 1.0)) * lg[...], reverse)
                   for reverse in (False, True)]

    def ret_chain(ref, bi, cos_ref, sin_ref, reverse):
        def rope(t, lo):
            rot = jnp.where(first_half, pltpu.roll(t, dk_all - SCAN_DK // 2, 1),
                            pltpu.roll(t, SCAN_DK // 2, 1))
            return t * cos_ref[lo:lo + CHUNK, :] + rot * sin_ref[lo:lo + CHUNK, :]

        def load(lo):
            q, k, v = qkv(ref, bi, lo)
            return rope(q, lo), rope(k * scale, lo), v, ret_factors[int(reverse)]
        return load

    chains = []
    for bi in range(n_batch):
        chains += [
            (gla_chain(gf, bi, waf, baf, cum_f_ref, False), False, gof, bi, tri_f),
            (gla_chain(gb, bi, wab, bab, cum_b_ref, True), True, gob, bi, tri_b),
            (ret_chain(rf, bi, cosf, sinf, False), False, rof, bi, tri_f),
            (ret_chain(rb, bi, cosb, sinb, True), True, rob, bi, tri_b),
        ]
    st_vals = [st_all[n] for n in range(len(chains))]
    n_chunks = tb // CHUNK

    def heads(i):
        pending = []
        for n, (load, reverse, _, _, _) in enumerate(chains):
            lo = (n_chunks - 1 - i if reverse else i) * CHUNK
            q, k, v, factors = load(lo)
            part, st_vals[n] = _scan_chunk_head(q, k, v, factors, st_vals[n],
                                                key_mask, value_mask, head_mask)
            pending.append((lo, part))
        return pending

    pending = heads(0)
    for i in range(n_chunks):
        following = heads(i + 1) if i + 1 < n_chunks else None
        for (lo, part), (_, _, out_ref, bi, tri) in zip(pending, chains):
            out_ref[bi, lo:lo + CHUNK, :] = _scan_chunk_tail(part, tri)
        pending = following
    for n, val in enumerate(st_vals):
        st_all[n] = val


def _conv_tiles(j, n_ctx_tiles, n_tiles, main_ref, prev_ref, next_ref, wdw_ref, bdw_ref, lng_ref,
                lnb_ref, wpw_ref, bpw_ref, o_ref, ubuf, shifted, sbuf):
    n_batch, tt = main_ref.shape[0], main_ref.shape[1]
    c = o_ref.shape[2]

    def glu(blk):
        return blk[:, :c] * jax.nn.sigmoid(blk[:, c:])

    has_prev = jnp.logical_and(j != 0, j != n_ctx_tiles)
    has_next = jnp.logical_and(j != n_ctx_tiles - 1, j != n_tiles - 1)

    @pl.when(has_prev)
    def _():
        for bi in range(n_batch):
            ubuf[bi, 0:HALO, :] = glu(prev_ref[bi])

    @pl.when(jnp.logical_not(has_prev))
    def _():
        ubuf[:, 0:HALO, :] = jnp.zeros((n_batch, HALO, c), F32)

    @pl.when(has_next)
    def _():
        for bi in range(n_batch):
            ubuf[bi, HALO + tt:, :] = glu(next_ref[bi])

    @pl.when(jnp.logical_not(has_next))
    def _():
        ubuf[:, HALO + tt:, :] = jnp.zeros((n_batch, HALO, c), F32)

    span = shifted.shape[2]
    rows = 128
    for bi in range(n_batch):
        ubuf[bi, HALO:HALO + tt, :] = glu(main_ref[bi])
        for s in range(1, SUBLANES):
            shifted[bi, s - 1] = ubuf[bi, s:s + span, :]
        for r in range(0, tt, rows):
            acc = jnp.zeros((rows, c), F32)
            for k in range(CONV_KSIZE):
                off = k + HALO - CONV_PAD
                lo = r + off - off % SUBLANES
                if off % SUBLANES == 0:
                    tap = ubuf[bi, lo:lo + rows, :]
                else:
                    tap = shifted[bi, off % SUBLANES - 1, lo:lo + rows, :]
                acc = acc + tap * wdw_ref[k:k + 1, :]
            y = acc + bdw_ref[...]
            yc = y - jnp.mean(y, axis=-1, keepdims=True)
            yn = yc * lax.rsqrt(jnp.mean(yc * yc, axis=-1, keepdims=True) + EPS)
            yn = yn * lng_ref[...] + lnb_ref[...]
            sbuf[bi, r:r + rows, :] = _silu(yn).astype(BF16)
        o_ref[bi] = jnp.dot(sbuf[bi], wpw_ref[...], preferred_element_type=F32) + bpw_ref[...]


def _mixers_body(n_ctx_tiles, n_tiles, n_rid, *refs):
    scan_in, conv_in = refs[:17], refs[17:26]
    rider_in = refs[26:26 + n_rid]
    outs = refs[26 + n_rid:]
    scan_out, conv_out, rider_out = outs[:4], outs[4], outs[5:-5]
    scan_scratch, conv_scratch = outs[-5:-3], outs[-3:]
    _run_riders(rider_in, rider_out)
    _scan_body(*scan_in, *scan_out, *scan_scratch)
    _conv_tiles(pl.program_id(1), n_ctx_tiles, n_tiles, *conv_in, conv_out, *conv_scratch)


def _mixers(layer, gla_in, ret_in, conv_in, n_ctx_tiles, w_a_f, b_a_f, w_a_b, b_a_b, log_gamma, cos, sin,
            key_mask, value_mask, cum_f, cum_b, w_dw, b_dw, ln_g, ln_b, w_pw, b_pw, riders=()):
    bsz, t_all, _ = gla_in.shape
    tb = ROW_TILE
    n_tiles = t_all // tb
    dk_all = N_HEADS * SCAN_DK
    dv_all = GROUP_W
    c = GROUP_W
    nb = MIXER_BATCH if bsz % MIXER_BATCH == 0 else 1
    per = tb // HALO
    n_halo = t_all // HALO

    def bwd(j):
        return jnp.where(j < n_ctx_tiles, n_ctx_tiles - 1 - j, n_tiles - 1 - (j - n_ctx_tiles))

    fwd_blk = lambda b, j: (b, j, 0)
    bwd_blk = lambda b, j: (b, bwd(j), 0)
    n_steps = (bsz // nb) * n_tiles
    rid_in, rid_args, rid_out, rid_shapes = _rider_specs(riders, lambda b, j: b * n_tiles + j, n_steps)
    in_specs = [pl.BlockSpec((nb, tb, W_GLA), fwd_blk), pl.BlockSpec((nb, tb, W_GLA), bwd_blk),
                pl.BlockSpec((nb, tb, W_RET), fwd_blk), pl.BlockSpec((nb, tb, W_RET), bwd_blk),
                _layer_spec(w_a_f, layer), _layer_spec(b_a_f, layer),
                _layer_spec(w_a_b, layer), _layer_spec(b_a_b, layer),
                _const_spec(log_gamma),
                pl.BlockSpec((tb, dk_all), lambda b, j: (j, 0)),
                pl.BlockSpec((tb, dk_all), lambda b, j: (j, 0)),
                pl.BlockSpec((tb, dk_all), lambda b, j: (bwd(j), 0)),
                pl.BlockSpec((tb, dk_all), lambda b, j: (bwd(j), 0)),
                _const_spec(key_mask), _const_spec(value_mask), _const_spec(cum_f), _const_spec(cum_b),
                pl.BlockSpec((nb, tb, 2 * c), fwd_blk),
                pl.BlockSpec((nb, HALO, 2 * c), lambda b, j: (b, jnp.maximum(j * per - 1, 0), 0)),
                pl.BlockSpec((nb, HALO, 2 * c), lambda b, j: (b, jnp.minimum((j + 1) * per, n_halo - 1), 0)),
                _layer_spec(w_dw, layer), _layer_spec(b_dw, layer), _layer_spec(ln_g, layer),
                _layer_spec(ln_b, layer), _layer_spec(w_pw, layer), _layer_spec(b_pw, layer)] + rid_in
    return pl.pallas_call(
        functools.partial(_mixers_body, n_ctx_tiles, n_tiles, len(riders)),
        grid=(bsz // nb, n_tiles),
        in_specs=in_specs,
        out_specs=[pl.BlockSpec((nb, tb, dv_all), fwd_blk), pl.BlockSpec((nb, tb, dv_all), bwd_blk)] * 2
        + [pl.BlockSpec((nb, tb, c), fwd_blk)] + rid_out,
        out_shape=[jax.ShapeDtypeStruct((bsz, t_all, dv_all), F32)] * 5 + rid_shapes,
        scratch_shapes=[pltpu.VMEM((4 * nb, dk_all, dv_all), F32), pltpu.VMEM((nb, 2, tb, dk_all), F32),
                        pltpu.VMEM((nb, tb + 2 * HALO, c), F32),
                        pltpu.VMEM((nb, SUBLANES - 1, tb + 2 * HALO - SUBLANES, c), F32),
                        pltpu.VMEM((nb, tb, c), BF16)],
        compiler_params=_params(("arbitrary", "arbitrary")),
        name="mixers",
    )(gla_in, gla_in, ret_in, ret_in, w_a_f, b_a_f, w_a_b, b_a_b, log_gamma, cos, sin, cos, sin,
      key_mask, value_mask, cum_f, cum_b, conv_in, conv_in, conv_in, w_dw, b_dw, ln_g, ln_b, w_pw, b_pw,
      *rid_args)


def _att_body(n_ctx_tiles, n_tiles, tile0, q_ref, k_ref, vt_ref, o_ref):
    group = N_HEADS // KV_HEADS
    tq = q_ref.shape[2]
    tk = vt_ref.shape[4]
    nq = group * tq

    def attend_head(bi, h, n_blocks):
        q2 = q_ref[bi, h * group:(h + 1) * group].reshape(nq, LANES)

        def scores(i):
            kb = k_ref[bi, i * tk:(i + 1) * tk, :]
            return lax.dot_general(kb, q2, (((1,), (1,)), ((), ())), preferred_element_type=F32)

        m = acc = None
        ahead = [scores(i) for i in range(min(SCORE_LOOKAHEAD, n_blocks))]
        for i in range(n_blocks):
            s = ahead.pop(0)
            if i + SCORE_LOOKAHEAD < n_blocks:
                ahead.append(scores(i + SCORE_LOOKAHEAD))
            blk_max = jnp.max(s, axis=0, keepdims=True)
            m_new = blk_max if i == 0 else jnp.maximum(m, blk_max)
            p = jnp.exp2(s - m_new).astype(BF16)
            pv = jnp.dot(vt_ref[bi, h, i], p, preferred_element_type=F32)
            acc = pv if i == 0 else jnp.exp2(m - m_new) * acc + pv
            m = m_new
        o_t = acc[:HEAD_DV] / acc[HEAD_DV:HEAD_DV + 1]
        stacked = jnp.concatenate([o_t[:, g * tq:(g + 1) * tq] for g in range(group)], axis=0)
        o_ref[bi, :, h * group * HEAD_DV:(h + 1) * group * HEAD_DV] = stacked.T

    def attend(n_blocks):
        for bi in range(q_ref.shape[0]):
            for h in range(KV_HEADS):
                attend_head(bi, h, n_blocks)

    qi = pl.program_id(1) + tile0
    if tile0 < n_ctx_tiles:
        pl.when(qi < n_ctx_tiles)(lambda: attend(n_ctx_tiles))
    pl.when(qi >= n_ctx_tiles)(lambda: attend(n_tiles))


def _attention(q, k, vt, n_ctx_tiles, with_ctx):
    bsz, _, t_all, _ = q.shape
    tq = ROW_TILE
    n_tiles = t_all // tq
    tile0 = 0 if with_ctx else n_ctx_tiles
    nb = ATT_BATCH if bsz % ATT_BATCH == 0 else 1
    return pl.pallas_call(
        functools.partial(_att_body, n_ctx_tiles, n_tiles, tile0),
        grid=(bsz // nb, n_tiles - tile0),
        in_specs=[pl.BlockSpec((nb, N_HEADS, tq, LANES), lambda b, i: (b, 0, i + tile0, 0)),
                  pl.BlockSpec((nb, t_all, LANES), lambda b, i: (b, 0, 0)),
                  pl.BlockSpec((nb, KV_HEADS, n_tiles, VT_ROWS, tq), lambda b, i: (b, 0, 0, 0, 0))],
        out_specs=pl.BlockSpec((nb, tq, GROUP_W), lambda b, i: (b, i, 0)),
        out_shape=jax.ShapeDtypeStruct((bsz, t_all - tile0 * tq, GROUP_W), F32),
        compiler_params=_params(("parallel", "arbitrary")),
        name="attention",
    )(q, k, vt)


def _post_body(final, split, n_ctx_tiles, tile0, n_rid, *refs):
    (yconv, gla_f, gla_b, gla_gate, yatt, ret_f, ret_b, ret_gate, gla_gain, ret_gain, ones_ref) = refs[:11]
    n_src = 2 if split else 1
    lat_ref = refs[11]
    ctx_ref = refs[12] if split else None
    n_in = 18 + n_src
    mod_ref, modc_ref, n2g_ref, wout_ref, wup_ref, wdown_ref, fng_ref = refs[11 + n_src:n_in]
    o_ref = refs[n_in + n_rid]
    xbuf = refs[n_in + 2 * n_rid + 1] if split else None
    _run_riders(refs[n_in:n_in + n_rid], refs[n_in + n_rid + 1:n_in + 2 * n_rid + 1])
    nb, tm, d = lat_ref.shape
    rows = nb * tm
    d_ff = wup_ref.shape[1]
    is_ctx = pl.program_id(1) + tile0 < n_ctx_tiles

    def stacked(ref):
        return ref[...].reshape(rows, ref.shape[2])

    def finish(of_ref, ob_ref, gate_ref, gain_ref):
        o = stacked(of_ref) + stacked(ob_ref)
        ss = _group_sumsq(o, ones_ref[...])
        y = o * lax.rsqrt(ss * (1.0 / HEAD_DV) + EPS) * gain_ref[...]
        return (y * _silu(stacked(gate_ref))).astype(BF16)

    mixed = jnp.concatenate([stacked(yconv).astype(BF16), finish(gla_f, gla_b, gla_gate, gla_gain),
                             stacked(yatt).astype(BF16), finish(ret_f, ret_b, ret_gate, ret_gain)], axis=1)
    o = jnp.dot(mixed, wout_ref[...], preferred_element_type=F32)
    x = _load_rows(split, is_ctx, lat_ref, ctx_ref, xbuf)
    mods = [_mod_row(is_ctx, mod_ref, modc_ref, bi) for bi in range(nb)]
    x_mid, h2 = [], []
    for bi in range(nb):
        gate1 = mods[bi][:, 2 * d:3 * d]
        shift2 = mods[bi][:, 3 * d:4 * d]
        scale2 = mods[bi][:, 4 * d:5 * d]
        x_mid.append(x[bi] + gate1 * o[bi * tm:(bi + 1) * tm])
        h2.append(((_rms(x_mid[bi]) * n2g_ref[...]) * (1.0 + scale2) + shift2).astype(BF16))
    h2 = jnp.concatenate(h2, axis=0)
    acc = jnp.zeros((rows, d), F32)
    tf = 1024
    for f in range(0, d_ff, tf):
        u = jnp.dot(h2, wup_ref[:, f:f + tf], preferred_element_type=F32)
        a = jnp.square(jnp.maximum(u, 0.0)).astype(BF16)
        acc = acc + jnp.dot(a, wdown_ref[f:f + tf, :], preferred_element_type=F32)
    for bi in range(nb):
        gate2 = mods[bi][:, 5 * d:6 * d]
        x_out = x_mid[bi] + gate2 * acc[bi * tm:(bi + 1) * tm]
        if final:
            x_out = _rms(x_out) * fng_ref[...]
        o_ref[bi] = x_out


def _post(layer, final, yconv, gla_of, gla_ob, gla_in, yatt, ret_of, ret_ob, ret_in, gla_gain, ret_gain,
          ones_blockdiag, x_lat, x_ctx, mods, norm2_g, w_layer, w_out, w_up, w_down, final_g, n_ctx_tiles,
          with_ctx, riders=()):
    split = x_ctx is not None
    bsz, _, d = x_lat.shape
    tm = ROW_TILE
    t_all = gla_of.shape[1]
    ctx_row = bsz
    tile0 = 0 if with_ctx else n_ctx_tiles
    n_out = t_all // tm - tile0
    c = GROUP_W
    gate_blk = (2 * N_HEADS * SCAN_DK + c) // c
    full = lambda b, j: (b, j + tile0, 0)
    own = lambda b, j: (b, j, 0)
    assert not split or tile0 == 0
    nb = ROW_BATCH if bsz % ROW_BATCH == 0 else 1
    row_specs, row_args = _row_specs(split, x_lat, x_ctx, nb, tm, n_ctx_tiles, tile0)

    def resident(arr):
        rest = arr.shape[1:]
        return pl.BlockSpec((None,) + rest, lambda *_: (w_layer,) + (0,) * len(rest),
                            pipeline_mode=pl.Buffered(1))

    n_steps = (bsz // nb) * n_out
    rid_in, rid_args, rid_out, rid_shapes = _rider_specs(riders, lambda b, j: b * n_out + j, n_steps)
    return pl.pallas_call(
        functools.partial(_post_body, final, split, n_ctx_tiles, tile0, len(riders)),
        grid=(bsz // nb, n_out),
        in_specs=[pl.BlockSpec((nb, tm, c), full),
                  pl.BlockSpec((nb, tm, c), full), pl.BlockSpec((nb, tm, c), full),
                  pl.BlockSpec((nb, tm, c), lambda b, j: (b, j + tile0, gate_blk)),
                  pl.BlockSpec((nb, tm, c), own),
                  pl.BlockSpec((nb, tm, c), full), pl.BlockSpec((nb, tm, c), full),
                  pl.BlockSpec((nb, tm, c), lambda b, j: (b, j + tile0, gate_blk)),
                  _layer_spec(gla_gain, layer), _layer_spec(ret_gain, layer),
                  _const_spec(ones_blockdiag)] + row_specs + _mod_specs(layer, nb, N_MOD * d, ctx_row) + [
                  _layer_spec(norm2_g, layer),
                  resident(w_out), resident(w_up), resident(w_down),
                  _const_spec(final_g)] + rid_in,
        out_specs=[pl.BlockSpec((nb, tm, d), own)] + rid_out,
        out_shape=[jax.ShapeDtypeStruct((bsz, n_out * tm, d), F32)] + rid_shapes,
        scratch_shapes=[pltpu.VMEM((nb, tm, d), F32)] if split else [],
        compiler_params=_params(("arbitrary", "arbitrary") if riders else ("parallel", "parallel")),
        name="post",
    )(yconv, gla_of, gla_ob, gla_in, yatt, ret_of, ret_ob, ret_in, gla_gain, ret_gain,
      ones_blockdiag, *row_args, mods, mods, norm2_g, w_out, w_up, w_down, final_g, *rid_args)


def _rope_table(t_lat, n_ctx, head_dim, width):
    n_ax = head_dim // 4
    inv = np.float32(ROPE_THETA) ** (-np.arange(n_ax, dtype=np.float32) / np.float32(n_ax))
    pos = np.arange(t_lat)
    ang = np.concatenate([(pos // GRID_W).astype(np.float32)[:, None] * inv,
                          (pos % GRID_W).astype(np.float32)[:, None] * inv], axis=-1).astype(np.float32)
    cos, sin = np.cos(ang), np.sin(ang)
    cos = np.concatenate([np.ones((n_ctx, head_dim // 2), np.float32), cos], axis=0)
    sin = np.concatenate([np.zeros((n_ctx, head_dim // 2), np.float32), sin], axis=0)
    reps = width // head_dim
    return (jnp.asarray(np.tile(np.concatenate([cos, cos], axis=-1), (1, reps)), F32),
            jnp.asarray(np.tile(np.concatenate([-sin, sin], axis=-1), (1, reps)), F32))


def kernel(x, c, ctx, c_ctx, w_mod, b_mod, norm1_g, norm2_g, w_in, conv_w_dw, conv_b_dw, conv_ln_g,
           conv_ln_b, conv_w_pw, conv_b_pw, gla_w_a_f, gla_b_a_f, gla_w_a_b, gla_b_a_b, gla_norm_g,
           att_q_norm_g, att_k_norm_g, ret_norm_g, w_out, w_up, w_down, final_norm_g):
    bsz, t_lat, d = x.shape
    n_ctx = ctx.shape[1]
    depth = w_mod.shape[0]
    assert bsz + 1 <= 8 and n_ctx % ROW_TILE == 0 and t_lat % ROW_TILE == 0
    n_ctx_tiles = n_ctx // ROW_TILE
    dk_all = N_HEADS * SCAN_DK

    cc = jnp.concatenate([c, c_ctx[None, :], jnp.zeros((8 - bsz - 1, d), F32)], axis=0)
    mods = _modulation(cc, w_mod, b_mod)
    mods = mods.reshape(depth, 8, 1, N_MOD * d)

    stack_rows = lambda v: v.reshape(depth, 1, -1)
    w_in_b = w_in.astype(BF16)
    in_w = (w_in_b[:, :, :W_IN_A],
            jnp.pad(w_in_b[:, :, W_IN_A:W_IN_A + GLA_RANK], ((0, 0), (0, 0), (0, LANES - GLA_RANK))),
            w_in_b[:, :, W_IN_A + GLA_RANK:])
    pad_rank = lambda w: jnp.pad(w, ((0, 0), (0, LANES - GLA_RANK), (0, 0))).astype(BF16)
    w_a_f, w_a_b = pad_rank(gla_w_a_f), pad_rank(gla_w_a_b)
    w_pw = conv_w_pw.astype(BF16)
    q_gain = stack_rows(jnp.tile(att_q_norm_g, (1, N_HEADS)))
    k_gain = stack_rows(jnp.tile(att_k_norm_g, (1, KV_HEADS)))

    cos_att, sin_att = _rope_table(t_lat, n_ctx, HEAD_DV, N_HEADS * HEAD_DV)
    cos_ret, sin_ret = _rope_table(t_lat, n_ctx, SCAN_DK, dk_all)
    gamma = 1.0 - np.exp2(-5.0 - np.arange(N_HEADS, dtype=np.float64))
    log_gamma = jnp.asarray(np.repeat(np.log(gamma), SCAN_DK)[None, :], F32)
    gid = np.arange(GROUP_W) // HEAD_DV
    ones_blockdiag = jnp.asarray(gid[:, None] == gid[None, :], BF16)
    key_mask = jnp.asarray(gid[:, None] == (np.arange(dk_all) // SCAN_DK)[None, :], BF16)
    pos = np.arange(ROW_TILE)
    same_chunk = (pos[:, None] // CHUNK) == (pos[None, :] // CHUNK)
    cum_f = jnp.asarray(same_chunk & (pos[None, :] <= pos[:, None]), BF16)
    cum_b = jnp.asarray(same_chunk & (pos[None, :] >= pos[:, None]), BF16)

    x_lat, x_ctx = x, ctx
    for l in range(depth):
        with_ctx = l < depth - 1
        final = l == depth - 1
        conv_in, gla_in, ret_in, q, k, vt = _in_proj(
            l, x_lat, x_ctx, mods, stack_rows(norm1_g), l, *in_w, cos_att, sin_att, q_gain, k_gain,
            ones_blockdiag, n_ctx_tiles)
        mixer_riders = [(w, 0) for w in (w_out, w_up, w_down)] if l == 0 else []
        gla_of, gla_ob, ret_of, ret_ob, y_conv, *converted = _mixers(
            l, gla_in, ret_in, conv_in, n_ctx_tiles, w_a_f, stack_rows(gla_b_a_f), w_a_b,
            stack_rows(gla_b_a_b), log_gamma, cos_ret, sin_ret, key_mask, ones_blockdiag, cum_f, cum_b,
            conv_w_dw, stack_rows(conv_b_dw), stack_rows(conv_ln_g), stack_rows(conv_ln_b), w_pw,
            stack_rows(conv_b_pw), mixer_riders)
        if l == 0:
            post_w = tuple(w[None] for w in converted)
        y_att = _attention(q, k, vt, n_ctx_tiles, with_ctx)
        post_riders = [] if final else [(w, l + 1) for w in (w_out, w_up, w_down)]
        x_lat, *converted = _post(
            l, final, y_conv, gla_of, gla_ob, gla_in, y_att, ret_of, ret_ob, ret_in,
            stack_rows(gla_norm_g), stack_rows(ret_norm_g), ones_blockdiag, x_lat, x_ctx, mods,
            stack_rows(norm2_g), 0, *post_w, final_norm_g.reshape(1, -1), n_ctx_tiles, with_ctx,
            post_riders)
        if not final:
            post_w = tuple(w[None] for w in converted)
        x_ctx = None
    return x_lat
```

```python
import functools

import numpy as np
import jax
import jax.numpy as jnp
from jax import lax
from jax.experimental import pallas as pl
from jax.experimental.pallas import tpu as pltpu

F32 = jnp.float32
BF16 = jnp.bfloat16

EPS = 1e-6
GRID_W = 64
ROPE_THETA = 10000.0
N_MOD = 6
CONV_KSIZE = 31
CONV_PAD = (CONV_KSIZE - 1) // 2
HALO = 16
GLA_TAU = 16.0
GLA_RANK = 16
CHUNK = 64
N_HEADS = 4
KV_HEADS = 2
HEAD_DV = 64
SCAN_DK = 32
GROUP_W = 256
ROW_TILE = 256
LANES = 128
SUBLANES = 8
BF16_SUBLANES = 16
VT_ROWS = HEAD_DV + BF16_SUBLANES
LOG2_E = 1.4426950408889634
RIDER_CHUNKS = 64
IN_PROJ_BATCH = 4
ROW_BATCH = 2
ATT_BATCH = 2
MIXER_BATCH = 2
SCORE_LOOKAHEAD = 2
V7X_VMEM_BYTES = 64 * 1024 * 1024
VMEM_LIMIT = V7X_VMEM_BYTES * 3 // 4

W_CONV = 2 * GROUP_W
W_GLA = 2 * N_HEADS * SCAN_DK + 2 * GROUP_W + LANES
W_ATT = GROUP_W + 2 * KV_HEADS * HEAD_DV
W_RET = 2 * N_HEADS * SCAN_DK + 2 * GROUP_W
W_IN_A = W_CONV + W_GLA - LANES
W_IN_B = W_ATT + W_RET


def _params(semantics):
    return pltpu.CompilerParams(dimension_semantics=semantics, vmem_limit_bytes=VMEM_LIMIT)


def _layer_spec(arr, layer):
    rest = arr.shape[1:]
    return pl.BlockSpec((None,) + rest, lambda *_: (layer,) + (0,) * len(rest))


def _const_spec(arr):
    return pl.BlockSpec(arr.shape, lambda *_: (0,) * arr.ndim)


def _rider_specs(items, step_of, n_steps):
    in_specs, args, out_specs, out_shapes = [], [], [], []
    n_chunks = 1
    while 2 * n_chunks <= min(n_steps, RIDER_CHUNKS):
        n_chunks *= 2
    chunk = lambda *g: jnp.minimum(step_of(*g), n_chunks - 1)
    for arr, layer in items:
        _, rows, cols = arr.shape
        rpc = rows // n_chunks
        assert rows % n_chunks == 0 and rpc % BF16_SUBLANES == 0
        in_specs.append(pl.BlockSpec((None, rpc, cols), lambda *g, layer=layer: (layer, chunk(*g), 0)))
        args.append(arr)
        out_specs.append(pl.BlockSpec((rpc, cols), lambda *g: (chunk(*g), 0)))
        out_shapes.append(jax.ShapeDtypeStruct((rows, cols), BF16))
    return in_specs, args, out_specs, out_shapes


def _run_riders(in_refs, out_refs):
    for src, dst in zip(in_refs, out_refs):
        dst[...] = src[...].astype(BF16)


def _silu(x):
    return x * jax.nn.sigmoid(x)


def _rms(x):
    return x * lax.rsqrt(jnp.mean(x * x, axis=-1, keepdims=True) + EPS)


def _group_sumsq(t, ones_blockdiag):
    t2 = t * t
    hi = t2.astype(BF16)
    lo = (t2 - hi.astype(F32)).astype(BF16)
    return (jnp.dot(hi, ones_blockdiag, preferred_element_type=F32)
            + jnp.dot(lo, ones_blockdiag, preferred_element_type=F32))


def _load_rows(split, is_ctx, lat_ref, ctx_ref, buf):
    if not split:
        return lat_ref[...]

    @pl.when(is_ctx)
    def _():
        buf[...] = ctx_ref[...]

    @pl.when(jnp.logical_not(is_ctx))
    def _():
        buf[...] = lat_ref[...]

    return buf[...]


def _row_specs(split, x_lat, x_ctx, nb, tm, n_ctx_tiles, tile0):
    d = x_lat.shape[2]
    if not split:
        return [pl.BlockSpec((nb, tm, d), lambda b, j: (b, j + tile0, 0))], [x_lat]
    return ([pl.BlockSpec((nb, tm, d), lambda b, j: (b, jnp.maximum(j - n_ctx_tiles, 0), 0)),
             pl.BlockSpec((nb, tm, d), lambda b, j: (b, jnp.minimum(j, n_ctx_tiles - 1), 0))],
            [x_lat, x_ctx])


def _mod_specs(layer, nb, width, ctx_row):
    return [pl.BlockSpec((None, nb, 1, width), lambda b, j: (layer, b, 0, 0)),
            pl.BlockSpec((None, 1, 1, width), lambda b, j: (layer, ctx_row, 0, 0))]


def _mod_row(is_ctx, lat_mod_ref, ctx_mod_ref, bi):
    f = is_ctx.astype(F32)
    return f * ctx_mod_ref[0] + (1.0 - f) * lat_mod_ref[bi]


def _mod_body(cc_ref, w_ref, b_ref, o_ref):
    s = _silu(cc_ref[...])
    o_ref[0] = jnp.dot(s.astype(BF16), w_ref[0].astype(BF16),
                       preferred_element_type=F32) + b_ref[0]


def _modulation(cc, w_mod, b_mod):
    depth, d, n = w_mod.shape
    tn = 1024
    return pl.pallas_call(
        _mod_body,
        grid=(depth, n // tn),
        in_specs=[pl.BlockSpec((8, d), lambda l, i: (0, 0)),
                  pl.BlockSpec((1, d, tn), lambda l, i: (l, 0, i)),
                  pl.BlockSpec((1, 1, tn), lambda l, i: (l, 0, i))],
        out_specs=pl.BlockSpec((1, 8, tn), lambda l, i: (l, 0, i)),
        out_shape=jax.ShapeDtypeStruct((depth, 8, n), F32),
        compiler_params=_params(("parallel", "parallel")),
        name="modulation",
    )(cc, w_mod, b_mod.reshape(depth, 1, n))


def _inproj_body(split, n_ctx_tiles, *refs):
    n_src = 2 if split else 1
    lat_ref = refs[0]
    ctx_ref = refs[1] if split else None
    (mod_ref, modc_ref, g_ref, wa_ref, wz_ref, wb_ref, cos_ref, sin_ref, qg_ref, kg_ref, ones_ref,
     conv_ref, gla_ref, ret_ref, q_out, k_out, vt_out) = refs[n_src:n_src + 17]
    xbuf = refs[n_src + 17] if split else None
    nb, tm, d = lat_ref.shape
    is_ctx = pl.program_id(1) < n_ctx_tiles

    x = _load_rows(split, is_ctx, lat_ref, ctx_ref, xbuf)
    wq = N_HEADS * HEAD_DV
    wk = KV_HEADS * HEAD_DV

    def norm_rope(t, ss, gain, w):
        t = t * lax.rsqrt(ss * (1.0 / HEAD_DV) + EPS) * gain
        lane = lax.broadcasted_iota(jnp.int32, t.shape, 1)
        first_half = (lane % HEAD_DV) < HEAD_DV // 2
        rot = jnp.where(first_half, pltpu.roll(t, w - HEAD_DV // 2, 1), pltpu.roll(t, HEAD_DV // 2, 1))
        return t * cos_ref[:, :w] + rot * sin_ref[:, :w]

    low = lax.broadcasted_iota(jnp.int32, (tm, LANES), 1) < HEAD_DV
    ones = jnp.ones((VT_ROWS - HEAD_DV, tm), BF16)
    for bi in range(nb):
        mod = _mod_row(is_ctx, mod_ref, modc_ref, bi)
        shift, scale = mod[:, 0:d], mod[:, d:2 * d]
        hb = ((_rms(x[bi]) * g_ref[...]) * (1.0 + scale) + shift).astype(BF16)

        def proj(w_ref, lo, width):
            return jnp.dot(hb, w_ref[:, lo:lo + width], preferred_element_type=F32)

        att = proj(wb_ref, 0, W_ATT)
        q = att[:, :wq]
        k = att[:, wq:wq + wk]
        v = att[:, wq + wk:]
        ss_q = _group_sumsq(q, ones_ref[...])
        ss_k = _group_sumsq(k, ones_ref[:wk, :wk])
        gla_ref[bi, :, W_GLA - LANES:] = proj(wz_ref, 0, LANES)
        ret_ref[bi] = proj(wb_ref, W_ATT, W_RET)
        gla_ref[bi, :, 0:W_GLA - LANES] = proj(wa_ref, W_CONV, W_GLA - LANES)
        conv_ref[bi] = proj(wa_ref, 0, W_CONV)

        qn = norm_rope(q, ss_q, qg_ref[...], wq) * (HEAD_DV ** -0.5 * LOG2_E)
        kn = norm_rope(k, ss_k, kg_ref[...], wk)
        pair0 = qn[:, :LANES]
        pair1 = qn[:, LANES:]
        q_out[bi, 0] = jnp.where(low, pair0, 0.0).astype(BF16)
        q_out[bi, 1] = jnp.where(low, pltpu.roll(pair0, HEAD_DV, 1), 0.0).astype(BF16)
        q_out[bi, 2] = jnp.where(low, 0.0, pltpu.roll(pair1, HEAD_DV, 1)).astype(BF16)
        q_out[bi, 3] = jnp.where(low, 0.0, pair1).astype(BF16)
        k_out[bi] = kn.astype(BF16)
        vt = v.T.astype(BF16)
        for h in range(KV_HEADS):
            vt_out[bi, h, 0, 0:HEAD_DV, :] = vt[h * HEAD_DV:(h + 1) * HEAD_DV, :]
            vt_out[bi, h, 0, HEAD_DV:, :] = ones


def _in_proj(layer, x_lat, x_ctx, mods, norm1_g, w_layer, w_a, w_z, w_b, cos, sin, q_gain, k_gain,
             ones_blockdiag, n_ctx_tiles):
    split = x_ctx is not None
    bsz, _, d = x_lat.shape
    tm = ROW_TILE
    t_all = x_lat.shape[1] + (x_ctx.shape[1] if split else 0)
    n_tiles = t_all // tm
    ctx_row = bsz
    wq = N_HEADS * HEAD_DV
    nb = IN_PROJ_BATCH if bsz % IN_PROJ_BATCH == 0 else 1
    row_specs, row_args = _row_specs(split, x_lat, x_ctx, nb, tm, n_ctx_tiles, 0)
    tile = lambda b, j: (b, j, 0)
    return pl.pallas_call(
        functools.partial(_inproj_body, split, n_ctx_tiles),
        grid=(bsz // nb, n_tiles),
        in_specs=row_specs + _mod_specs(layer, nb, N_MOD * d, ctx_row) + [
            _layer_spec(norm1_g, layer), _layer_spec(w_a, w_layer), _layer_spec(w_z, w_layer),
            _layer_spec(w_b, w_layer),
            pl.BlockSpec((tm, wq), lambda b, j: (j, 0)), pl.BlockSpec((tm, wq), lambda b, j: (j, 0)),
            _layer_spec(q_gain, layer), _layer_spec(k_gain, layer), _const_spec(ones_blockdiag)],
        out_specs=[pl.BlockSpec((nb, tm, W_CONV), tile), pl.BlockSpec((nb, tm, W_GLA), tile),
                   pl.BlockSpec((nb, tm, W_RET), tile),
                   pl.BlockSpec((nb, N_HEADS, tm, LANES), lambda b, j: (b, 0, j, 0)),
                   pl.BlockSpec((nb, tm, LANES), tile),
                   pl.BlockSpec((nb, KV_HEADS, 1, VT_ROWS, tm), lambda b, j: (b, 0, j, 0, 0))],
        out_shape=[jax.ShapeDtypeStruct((bsz, t_all, W_CONV), F32),
                   jax.ShapeDtypeStruct((bsz, t_all, W_GLA), F32),
                   jax.ShapeDtypeStruct((bsz, t_all, W_RET), F32),
                   jax.ShapeDtypeStruct((bsz, N_HEADS, t_all, LANES), BF16),
                   jax.ShapeDtypeStruct((bsz, t_all, LANES), BF16),
                   jax.ShapeDtypeStruct((bsz, KV_HEADS, n_tiles, VT_ROWS, tm), BF16)],
        scratch_shapes=[pltpu.VMEM((nb, tm, d), F32)] if split else [],
        compiler_params=_params(("parallel", "parallel")),
        name="in_proj",
    )(*row_args, mods, mods, norm1_g, w_a, w_z, w_b, cos, sin, q_gain, k_gain, ones_blockdiag)


def _chunk_cumsum(la, tri_ref):
    w = la.shape[1]
    hi = la.astype(BF16)
    lo = (la - hi.astype(F32)).astype(BF16)
    both = jnp.dot(tri_ref[...], jnp.concatenate([hi, lo], axis=1), preferred_element_type=F32)
    return both[:, :w] + both[:, w:]


def _decay_factors(b, reverse):
    b_tot = b[0:1, :] if reverse else b[CHUNK - 1:CHUNK, :]
    dk_all = b.shape[1]
    dec_cols = jnp.broadcast_to(jnp.exp(b_tot), (dk_all, dk_all)).T
    dec_cols = jnp.concatenate([dec_cols] * (GROUP_W // dk_all), axis=1)
    return jnp.exp(b), jnp.exp(-b), jnp.exp(b_tot - b), dec_cols


def _scan_chunk_head(q, k, v, factors, st, key_mask, value_mask, head_mask):
    nt_dims = (((1,), (1,)), ((), ()))
    e_q, e_inv, e_end, dec_cols = factors
    q_dec = (q * e_q).astype(BF16)
    k_inv = (k * e_inv).astype(BF16)
    k_end = (k * e_end).astype(BF16)
    vb = v.astype(BF16)
    k_blk = jnp.concatenate([k_inv] * N_HEADS, axis=0) * key_mask
    scores = lax.dot_general(q_dec, k_blk, nt_dims, preferred_element_type=F32)
    o_inter = jnp.dot(q_dec, st.astype(BF16), preferred_element_type=F32)
    kv = lax.dot_general(k_end, vb, (((0,), (0,)), ((), ())), preferred_element_type=F32)
    st = st * dec_cols + jnp.where(head_mask, kv, 0.0)
    v_blk = jnp.concatenate([vb] * N_HEADS, axis=0) * value_mask
    return (scores, v_blk, o_inter), st


def _scan_chunk_tail(pending, tri):
    scores, v_blk, o_inter = pending
    scores = jnp.where(tri, scores, 0.0).astype(BF16)
    return jnp.dot(scores, v_blk, preferred_element_type=F32) + o_inter


def _scan_body(gf, gb, rf, rb, waf, baf, wab, bab, lg, cosf, sinf, cosb, sinb,
               key_mask_ref, value_mask_ref, cum_f_ref, cum_b_ref,
               gof, gob, rof, rob, st_all, b_all):
    n_batch = gf.shape[0]

    @pl.when(pl.program_id(1) == 0)
    def _():
        st_all[...] = jnp.zeros(st_all.shape, F32)

    tb = gf.shape[1]
    dk_all = N_HEADS * SCAN_DK
    dv_all = GROUP_W
    scale = SCAN_DK ** -0.5
    key_mask = key_mask_ref[...]
    value_mask = value_mask_ref[...]
    r = lax.broadcasted_iota(jnp.int32, (dk_all, dv_all), 0)
    c = lax.broadcasted_iota(jnp.int32, (dk_all, dv_all), 1)
    head_mask = (r // SCAN_DK) == (c // HEAD_DV)
    r = lax.broadcasted_iota(jnp.int32, (CHUNK, dv_all), 0)
    c = lax.broadcasted_iota(jnp.int32, (CHUNK, dv_all), 1) % CHUNK
    tri_f = c <= r
    tri_b = c >= r

    def qkv(ref, bi, lo):
        return (ref[bi, lo:lo + CHUNK, 0:dk_all], ref[bi, lo:lo + CHUNK, dk_all:2 * dk_all],
                ref[bi, lo:lo + CHUNK, 2 * dk_all:2 * dk_all + dv_all])

    def gla_chain(ref, bi, w_ref, bias_ref, cum_ref, reverse):
        slot = int(reverse)
        z = ref[bi, :, 2 * dk_all + 2 * dv_all:]
        pre = jnp.dot(z.astype(BF16), w_ref[...], preferred_element_type=F32) + bias_ref[...]
        log_sig = jnp.minimum(pre, 0.0) - jnp.log1p(jnp.exp(-jnp.abs(pre)))
        b_all[bi, slot] = _chunk_cumsum(log_sig / GLA_TAU, cum_ref)

        def load(lo):
            q, k, v = qkv(ref, bi, lo)
            return q * scale, k, v, _decay_factors(b_all[bi, slot, lo:lo + CHUNK, :], reverse)
        return load

    lane = lax.broadcasted_iota(jnp.int32, (CHUNK, dk_all), 1)
    first_half = (lane % SCAN_DK) < SCAN_DK // 2
    pos = lax.broadcasted_iota(jnp.int32, (CHUNK, dk_all), 0).astype(F32)

    ret_factors = [_decay_factors(((CHUNK - pos) if reverse else (pos + 1.0)) * lg[...], reverse)
                   for reverse in (False, True)]

    def ret_chain(ref, bi, cos_ref, sin_ref, reverse):
        def rope(t, lo):
            rot = jnp.where(first_half, pltpu.roll(t, dk_all - SCAN_DK // 2, 1),
                            pltpu.roll(t, SCAN_DK // 2, 1))
            return t * cos_ref[lo:lo + CHUNK, :] + rot * sin_ref[lo:lo + CHUNK, :]

        def load(lo):
            q, k, v = qkv(ref, bi, lo)
            return rope(q, lo), rope(k * scale, lo), v, ret_factors[int(reverse)]
        return load

    chains = []
    for bi in range(n_batch):
        chains += [
            (gla_chain(gf, bi, waf, baf, cum_f_ref, False), False, gof, bi, tri_f),
            (gla_chain(gb, bi, wab, bab, cum_b_ref, True), True, gob, bi, tri_b),
            (ret_chain(rf, bi, cosf, sinf, False), False, rof, bi, tri_f),
            (ret_chain(rb, bi, cosb, sinb, True), True, rob, bi, tri_b),
        ]
    st_vals = [st_all[n] for n in range(len(chains))]
    n_chunks = tb // CHUNK

    def heads(i):
        pending = []
        for n, (load, reverse, _, _, _) in enumerate(chains):
            lo = (n_chunks - 1 - i if reverse else i) * CHUNK
            q, k, v, factors = load(lo)
            part, st_vals[n] = _scan_chunk_head(q, k, v, factors, st_vals[n],
                                                key_mask, value_mask, head_mask)
            pending.append((lo, part))
        return pending

    pending = heads(0)
    for i in range(n_chunks):
        following = heads(i + 1) if i + 1 < n_chunks else None
        for (lo, part), (_, _, out_ref, bi, tri) in zip(pending, chains):
            out_ref[bi, lo:lo + CHUNK, :] = _scan_chunk_tail(part, tri)
        pending = following
    for n, val in enumerate(st_vals):
        st_all[n] = val


def _conv_tiles(j, n_ctx_tiles, n_tiles, main_ref, prev_ref, next_ref, wdw_ref, bdw_ref, lng_ref,
                lnb_ref, wpw_ref, bpw_ref, o_ref, ubuf, shifted, sbuf):
    n_batch, tt = main_ref.shape[0], main_ref.shape[1]
    c = o_ref.shape[2]

    def glu(blk):
        return blk[:, :c] * jax.nn.sigmoid(blk[:, c:])

    has_prev = jnp.logical_and(j != 0, j != n_ctx_tiles)
    has_next = jnp.logical_and(j != n_ctx_tiles - 1, j != n_tiles - 1)

    @pl.when(has_prev)
    def _():
        for bi in range(n_batch):
            ubuf[bi, 0:HALO, :] = glu(prev_ref[bi])

    @pl.when(jnp.logical_not(has_prev))
    def _():
        ubuf[:, 0:HALO, :] = jnp.zeros((n_batch, HALO, c), F32)

    @pl.when(has_next)
    def _():
        for bi in range(n_batch):
            ubuf[bi, HALO + tt:, :] = glu(next_ref[bi])

    @pl.when(jnp.logical_not(has_next))
    def _():
        ubuf[:, HALO + tt:, :] = jnp.zeros((n_batch, HALO, c), F32)

    span = shifted.shape[2]
    rows = 128
    for bi in range(n_batch):
        ubuf[bi, HALO:HALO + tt, :] = glu(main_ref[bi])
        for s in range(1, SUBLANES):
            shifted[bi, s - 1] = ubuf[bi, s:s + span, :]
        for r in range(0, tt, rows):
            acc = jnp.zeros((rows, c), F32)
            for k in range(CONV_KSIZE):
                off = k + HALO - CONV_PAD
                lo = r + off - off % SUBLANES
                if off % SUBLANES == 0:
                    tap = ubuf[bi, lo:lo + rows, :]
                else:
                    tap = shifted[bi, off % SUBLANES - 1, lo:lo + rows, :]
                acc = acc + tap * wdw_ref[k:k + 1, :]
            y = acc + bdw_ref[...]
            yc = y - jnp.mean(y, axis=-1, keepdims=True)
            yn = yc * lax.rsqrt(jnp.mean(yc * yc, axis=-1, keepdims=True) + EPS)
            yn = yn * lng_ref[...] + lnb_ref[...]
            sbuf[bi, r:r + rows, :] = _silu(yn).astype(BF16)
        o_ref[bi] = jnp.dot(sbuf[bi], wpw_ref[...], preferred_element_type=F32) + bpw_ref[...]


def _mixers_body(n_ctx_tiles, n_tiles, n_rid, *refs):
    scan_in, conv_in = refs[:17], refs[17:26]
    rider_in = refs[26:26 + n_rid]
    outs = refs[26 + n_rid:]
    scan_out, conv_out, rider_out = outs[:4], outs[4], outs[5:-5]
    scan_scratch, conv_scratch = outs[-5:-3], outs[-3:]
    _run_riders(rider_in, rider_out)
    _scan_body(*scan_in, *scan_out, *scan_scratch)
    _conv_tiles(pl.program_id(1), n_ctx_tiles, n_tiles, *conv_in, conv_out, *conv_scratch)


def _mixers(layer, gla_in, ret_in, conv_in, n_ctx_tiles, w_a_f, b_a_f, w_a_b, b_a_b, log_gamma, cos, sin,
            key_mask, value_mask, cum_f, cum_b, w_dw, b_dw, ln_g, ln_b, w_pw, b_pw, riders=()):
    bsz, t_all, _ = gla_in.shape
    tb = ROW_TILE
    n_tiles = t_all // tb
    dk_all = N_HEADS * SCAN_DK
    dv_all = GROUP_W
    c = GROUP_W
    nb = MIXER_BATCH if bsz % MIXER_BATCH == 0 else 1
    per = tb // HALO
    n_halo = t_all // HALO

    def bwd(j):
        return jnp.where(j < n_ctx_tiles, n_ctx_tiles - 1 - j, n_tiles - 1 - (j - n_ctx_tiles))

    fwd_blk = lambda b, j: (b, j, 0)
    bwd_blk = lambda b, j: (b, bwd(j), 0)
    n_steps = (bsz // nb) * n_tiles
    rid_in, rid_args, rid_out, rid_shapes = _rider_specs(riders, lambda b, j: b * n_tiles + j, n_steps)
    in_specs = [pl.BlockSpec((nb, tb, W_GLA), fwd_blk), pl.BlockSpec((nb, tb, W_GLA), bwd_blk),
                pl.BlockSpec((nb, tb, W_RET), fwd_blk), pl.BlockSpec((nb, tb, W_RET), bwd_blk),
                _layer_spec(w_a_f, layer), _layer_spec(b_a_f, layer),
                _layer_spec(w_a_b, layer), _layer_spec(b_a_b, layer),
                _const_spec(log_gamma),
                pl.BlockSpec((tb, dk_all), lambda b, j: (j, 0)),
                pl.BlockSpec((tb, dk_all), lambda b, j: (j, 0)),
                pl.BlockSpec((tb, dk_all), lambda b, j: (bwd(j), 0)),
                pl.BlockSpec((tb, dk_all), lambda b, j: (bwd(j), 0)),
                _const_spec(key_mask), _const_spec(value_mask), _const_spec(cum_f), _const_spec(cum_b),
                pl.BlockSpec((nb, tb, 2 * c), fwd_blk),
                pl.BlockSpec((nb, HALO, 2 * c), lambda b, j: (b, jnp.maximum(j * per - 1, 0), 0)),
                pl.BlockSpec((nb, HALO, 2 * c), lambda b, j: (b, jnp.minimum((j + 1) * per, n_halo - 1), 0)),
                _layer_spec(w_dw, layer), _layer_spec(b_dw, layer), _layer_spec(ln_g, layer),
                _layer_spec(ln_b, layer), _layer_spec(w_pw, layer), _layer_spec(b_pw, layer)] + rid_in
    return pl.pallas_call(
        functools.partial(_mixers_body, n_ctx_tiles, n_tiles, len(riders)),
        grid=(bsz // nb, n_tiles),
        in_specs=in_specs,
        out_specs=[pl.BlockSpec((nb, tb, dv_all), fwd_blk), pl.BlockSpec((nb, tb, dv_all), bwd_blk)] * 2
        + [pl.BlockSpec((nb, tb, c), fwd_blk)] + rid_out,
        out_shape=[jax.ShapeDtypeStruct((bsz, t_all, dv_all), F32)] * 5 + rid_shapes,
        scratch_shapes=[pltpu.VMEM((4 * nb, dk_all, dv_all), F32), pltpu.VMEM((nb, 2, tb, dk_all), F32),
                        pltpu.VMEM((nb, tb + 2 * HALO, c), F32),
                        pltpu.VMEM((nb, SUBLANES - 1, tb + 2 * HALO - SUBLANES, c), F32),
                        pltpu.VMEM((nb, tb, c), BF16)],
        compiler_params=_params(("arbitrary", "arbitrary")),
        name="mixers",
    )(gla_in, gla_in, ret_in, ret_in, w_a_f, b_a_f, w_a_b, b_a_b, log_gamma, cos, sin, cos, sin,
      key_mask, value_mask, cum_f, cum_b, conv_in, conv_in, conv_in, w_dw, b_dw, ln_g, ln_b, w_pw, b_pw,
      *rid_args)


def _att_body(n_ctx_tiles, n_tiles, tile0, q_ref, k_ref, vt_ref, o_ref):
    group = N_HEADS // KV_HEADS
    tq = q_ref.shape[2]
    tk = vt_ref.shape[4]
    nq = group * tq

    def attend_head(bi, h, n_blocks):
        q2 = q_ref[bi, h * group:(h + 1) * group].reshape(nq, LANES)

        def scores(i):
            kb = k_ref[bi, i * tk:(i + 1) * tk, :]
            return lax.dot_general(kb, q2, (((1,), (1,)), ((), ())), preferred_element_type=F32)

        m = acc = None
        ahead = [scores(i) for i in range(min(SCORE_LOOKAHEAD, n_blocks))]
        for i in range(n_blocks):
            s = ahead.pop(0)
            if i + SCORE_LOOKAHEAD < n_blocks:
                ahead.append(scores(i + SCORE_LOOKAHEAD))
            blk_max = jnp.max(s, axis=0, keepdims=True)
            m_new = blk_max if i == 0 else jnp.maximum(m, blk_max)
            p = jnp.exp2(s - m_new).astype(BF16)
            pv = jnp.dot(vt_ref[bi, h, i], p, preferred_element_type=F32)
            acc = pv if i == 0 else jnp.exp2(m - m_new) * acc + pv
            m = m_new
        o_t = acc[:HEAD_DV] / acc[HEAD_DV:HEAD_DV + 1]
        stacked = jnp.concatenate([o_t[:, g * tq:(g + 1) * tq] for g in range(group)], axis=0)
        o_ref[bi, :, h * group * HEAD_DV:(h + 1) * group * HEAD_DV] = stacked.T

    def attend(n_blocks):
        for bi in range(q_ref.shape[0]):
            for h in range(KV_HEADS):
                attend_head(bi, h, n_blocks)

    qi = pl.program_id(1) + tile0
    if tile0 < n_ctx_tiles:
        pl.when(qi < n_ctx_tiles)(lambda: attend(n_ctx_tiles))
    pl.when(qi >= n_ctx_tiles)(lambda: attend(n_tiles))


def _attention(q, k, vt, n_ctx_tiles, with_ctx):
    bsz, _, t_all, _ = q.shape
    tq = ROW_TILE
    n_tiles = t_all // tq
    tile0 = 0 if with_ctx else n_ctx_tiles
    nb = ATT_BATCH if bsz % ATT_BATCH == 0 else 1
    return pl.pallas_call(
        functools.partial(_att_body, n_ctx_tiles, n_tiles, tile0),
        grid=(bsz // nb, n_tiles - tile0),
        in_specs=[pl.BlockSpec((nb, N_HEADS, tq, LANES), lambda b, i: (b, 0, i + tile0, 0)),
                  pl.BlockSpec((nb, t_all, LANES), lambda b, i: (b, 0, 0)),
                  pl.BlockSpec((nb, KV_HEADS, n_tiles, VT_ROWS, tq), lambda b, i: (b, 0, 0, 0, 0))],
        out_specs=pl.BlockSpec((nb, tq, GROUP_W), lambda b, i: (b, i, 0)),
        out_shape=jax.ShapeDtypeStruct((bsz, t_all - tile0 * tq, GROUP_W), F32),
        compiler_params=_params(("parallel", "arbitrary")),
        name="attention",
    )(q, k, vt)


def _post_body(final, split, n_ctx_tiles, tile0, n_rid, *refs):
    (yconv, gla_f, gla_b, gla_gate, yatt, ret_f, ret_b, ret_gate, gla_gain, ret_gain, ones_ref) = refs[:11]
    n_src = 2 if split else 1
    lat_ref = refs[11]
    ctx_ref = refs[12] if split else None
    n_in = 18 + n_src
    mod_ref, modc_ref, n2g_ref, wout_ref, wup_ref, wdown_ref, fng_ref = refs[11 + n_src:n_in]
    o_ref = refs[n_in + n_rid]
    xbuf = refs[n_in + 2 * n_rid + 1] if split else None
    _run_riders(refs[n_in:n_in + n_rid], refs[n_in + n_rid + 1:n_in + 2 * n_rid + 1])
    nb, tm, d = lat_ref.shape
    rows = nb * tm
    d_ff = wup_ref.shape[1]
    is_ctx = pl.program_id(1) + tile0 < n_ctx_tiles

    def stacked(ref):
        return ref[...].reshape(rows, ref.shape[2])

    def finish(of_ref, ob_ref, gate_ref, gain_ref):
        o = stacked(of_ref) + stacked(ob_ref)
        ss = _group_sumsq(o, ones_ref[...])
        y = o * lax.rsqrt(ss * (1.0 / HEAD_DV) + EPS) * gain_ref[...]
        return (y * _silu(stacked(gate_ref))).astype(BF16)

    mixed = jnp.concatenate([stacked(yconv).astype(BF16), finish(gla_f, gla_b, gla_gate, gla_gain),
                             stacked(yatt).astype(BF16), finish(ret_f, ret_b, ret_gate, ret_gain)], axis=1)
    o = jnp.dot(mixed, wout_ref[...], preferred_element_type=F32)
    x = _load_rows(split, is_ctx, lat_ref, ctx_ref, xbuf)
    mods = [_mod_row(is_ctx, mod_ref, modc_ref, bi) for bi in range(nb)]
    x_mid, h2 = [], []
    for bi in range(nb):
        gate1 = mods[bi][:, 2 * d:3 * d]
        shift2 = mods[bi][:, 3 * d:4 * d]
        scale2 = mods[bi][:, 4 * d:5 * d]
        x_mid.append(x[bi] + gate1 * o[bi * tm:(bi + 1) * tm])
        h2.append(((_rms(x_mid[bi]) * n2g_ref[...]) * (1.0 + scale2) + shift2).astype(BF16))
    h2 = jnp.concatenate(h2, axis=0)
    acc = jnp.zeros((rows, d), F32)
    tf = 1024
    for f in range(0, d_ff, tf):
        u = jnp.dot(h2, wup_ref[:, f:f + tf], preferred_element_type=F32)
        a = jnp.square(jnp.maximum(u, 0.0)).astype(BF16)
        acc = acc + jnp.dot(a, wdown_ref[f:f + tf, :], preferred_element_type=F32)
    for bi in range(nb):
        gate2 = mods[bi][:, 5 * d:6 * d]
        x_out = x_mid[bi] + gate2 * acc[bi * tm:(bi + 1) * tm]
        if final:
            x_out = _rms(x_out) * fng_ref[...]
        o_ref[bi] = x_out


def _post(layer, final, yconv, gla_of, gla_ob, gla_in, yatt, ret_of, ret_ob, ret_in, gla_gain, ret_gain,
          ones_blockdiag, x_lat, x_ctx, mods, norm2_g, w_layer, w_out, w_up, w_down, final_g, n_ctx_tiles,
          with_ctx, riders=()):
    split = x_ctx is not None
    bsz, _, d = x_lat.shape
    tm = ROW_TILE
    t_all = gla_of.shape[1]
    ctx_row = bsz
    tile0 = 0 if with_ctx else n_ctx_tiles
    n_out = t_all // tm - tile0
    c = GROUP_W
    gate_blk = (2 * N_HEADS * SCAN_DK + c) // c
    full = lambda b, j: (b, j + tile0, 0)
    own = lambda b, j: (b, j, 0)
    assert not split or tile0 == 0
    nb = ROW_BATCH if bsz % ROW_BATCH == 0 else 1
    row_specs, row_args = _row_specs(split, x_lat, x_ctx, nb, tm, n_ctx_tiles, tile0)

    def resident(arr):
        rest = arr.shape[1:]
        return pl.BlockSpec((None,) + rest, lambda *_: (w_layer,) + (0,) * len(rest),
                            pipeline_mode=pl.Buffered(1))

    n_steps = (bsz // nb) * n_out
    rid_in, rid_args, rid_out, rid_shapes = _rider_specs(riders, lambda b, j: b * n_out + j, n_steps)
    return pl.pallas_call(
        functools.partial(_post_body, final, split, n_ctx_tiles, tile0, len(riders)),
        grid=(bsz // nb, n_out),
        in_specs=[pl.BlockSpec((nb, tm, c), full),
                  pl.BlockSpec((nb, tm, c), full), pl.BlockSpec((nb, tm, c), full),
                  pl.BlockSpec((nb, tm, c), lambda b, j: (b, j + tile0, gate_blk)),
                  pl.BlockSpec((nb, tm, c), own),
                  pl.BlockSpec((nb, tm, c), full), pl.BlockSpec((nb, tm, c), full),
                  pl.BlockSpec((nb, tm, c), lambda b, j: (b, j + tile0, gate_blk)),
                  _layer_spec(gla_gain, layer), _layer_spec(ret_gain, layer),
                  _const_spec(ones_blockdiag)] + row_specs + _mod_specs(layer, nb, N_MOD * d, ctx_row) + [
                  _layer_spec(norm2_g, layer),
                  resident(w_out), resident(w_up), resident(w_down),
                  _const_spec(final_g)] + rid_in,
        out_specs=[pl.BlockSpec((nb, tm, d), own)] + rid_out,
        out_shape=[jax.ShapeDtypeStruct((bsz, n_out * tm, d), F32)] + rid_shapes,
        scratch_shapes=[pltpu.VMEM((nb, tm, d), F32)] if split else [],
        compiler_params=_params(("arbitrary", "arbitrary") if riders else ("parallel", "parallel")),
        name="post",
    )(yconv, gla_of, gla_ob, gla_in, yatt, ret_of, ret_ob, ret_in, gla_gain, ret_gain,
      ones_blockdiag, *row_args, mods, mods, norm2_g, w_out, w_up, w_down, final_g, *rid_args)


def _rope_table(t_lat, n_ctx, head_dim, width):
    n_ax = head_dim // 4
    inv = np.float32(ROPE_THETA) ** (-np.arange(n_ax, dtype=np.float32) / np.float32(n_ax))
    pos = np.arange(t_lat)
    ang = np.concatenate([(pos // GRID_W).astype(np.float32)[:, None] * inv,
                          (pos % GRID_W).astype(np.float32)[:, None] * inv], axis=-1).astype(np.float32)
    cos, sin = np.cos(ang), np.sin(ang)
    cos = np.concatenate([np.ones((n_ctx, head_dim // 2), np.float32), cos], axis=0)
    sin = np.concatenate([np.zeros((n_ctx, head_dim // 2), np.float32), sin], axis=0)
    reps = width // head_dim
    return (jnp.asarray(np.tile(np.concatenate([cos, cos], axis=-1), (1, reps)), F32),
            jnp.asarray(np.tile(np.concatenate([-sin, sin], axis=-1), (1, reps)), F32))


def kernel(x, c, ctx, c_ctx, w_mod, b_mod, norm1_g, norm2_g, w_in, conv_w_dw, conv_b_dw, conv_ln_g,
           conv_ln_b, conv_w_pw, conv_b_pw, gla_w_a_f, gla_b_a_f, gla_w_a_b, gla_b_a_b, gla_norm_g,
           att_q_norm_g, att_k_norm_g, ret_norm_g, w_out, w_up, w_down, final_norm_g):
    bsz, t_lat, d = x.shape
    n_ctx = ctx.shape[1]
    depth = w_mod.shape[0]
    assert bsz + 1 <= 8 and n_ctx % ROW_TILE == 0 and t_lat % ROW_TILE == 0
    n_ctx_tiles = n_ctx // ROW_TILE
    dk_all = N_HEADS * SCAN_DK

    cc = jnp.concatenate([c, c_ctx[None, :], jnp.zeros((8 - bsz - 1, d), F32)], axis=0)
    mods = _modulation(cc, w_mod, b_mod)
    mods = mods.reshape(depth, 8, 1, N_MOD * d)

    stack_rows = lambda v: v.reshape(depth, 1, -1)
    w_in_b = w_in.astype(BF16)
    in_w = (w_in_b[:, :, :W_IN_A],
            jnp.pad(w_in_b[:, :, W_IN_A:W_IN_A + GLA_RANK], ((0, 0), (0, 0), (0, LANES - GLA_RANK))),
            w_in_b[:, :, W_IN_A + GLA_RANK:])
    pad_rank = lambda w: jnp.pad(w, ((0, 0), (0, LANES - GLA_RANK), (0, 0))).astype(BF16)
    w_a_f, w_a_b = pad_rank(gla_w_a_f), pad_rank(gla_w_a_b)
    w_pw = conv_w_pw.astype(BF16)
    q_gain = stack_rows(jnp.tile(att_q_norm_g, (1, N_HEADS)))
    k_gain = stack_rows(jnp.tile(att_k_norm_g, (1, KV_HEADS)))

    cos_att, sin_att = _rope_table(t_lat, n_ctx, HEAD_DV, N_HEADS * HEAD_DV)
    cos_ret, sin_ret = _rope_table(t_lat, n_ctx, SCAN_DK, dk_all)
    gamma = 1.0 - np.exp2(-5.0 - np.arange(N_HEADS, dtype=np.float64))
    log_gamma = jnp.asarray(np.repeat(np.log(gamma), SCAN_DK)[None, :], F32)
    gid = np.arange(GROUP_W) // HEAD_DV
    ones_blockdiag = jnp.asarray(gid[:, None] == gid[None, :], BF16)
    key_mask = jnp.asarray(gid[:, None] == (np.arange(dk_all) // SCAN_DK)[None, :], BF16)
    pos = np.arange(ROW_TILE)
    same_chunk = (pos[:, None] // CHUNK) == (pos[None, :] // CHUNK)
    cum_f = jnp.asarray(same_chunk & (pos[None, :] <= pos[:, None]), BF16)
    cum_b = jnp.asarray(same_chunk & (pos[None, :] >= pos[:, None]), BF16)

    x_lat, x_ctx = x, ctx
    for l in range(depth):
        with_ctx = l < depth - 1
        final = l == depth - 1
        conv_in, gla_in, ret_in, q, k, vt = _in_proj(
            l, x_lat, x_ctx, mods, stack_rows(norm1_g), l, *in_w, cos_att, sin_att, q_gain, k_gain,
            ones_blockdiag, n_ctx_tiles)
        mixer_riders = [(w, 0) for w in (w_out, w_up, w_down)] if l == 0 else []
        gla_of, gla_ob, ret_of, ret_ob, y_conv, *converted = _mixers(
            l, gla_in, ret_in, conv_in, n_ctx_tiles, w_a_f, stack_rows(gla_b_a_f), w_a_b,
            stack_rows(gla_b_a_b), log_gamma, cos_ret, sin_ret, key_mask, ones_blockdiag, cum_f, cum_b,
            conv_w_dw, stack_rows(conv_b_dw), stack_rows(conv_ln_g), stack_rows(conv_ln_b), w_pw,
            stack_rows(conv_b_pw), mixer_riders)
        if l == 0:
            post_w = tuple(w[None] for w in converted)
        y_att = _attention(q, k, vt, n_ctx_tiles, with_ctx)
        post_riders = [] if final else [(w, l + 1) for w in (w_out, w_up, w_down)]
        x_lat, *converted = _post(
            l, final, y_conv, gla_of, gla_ob, gla_in, y_att, ret_of, ret_ob, ret_in,
            stack_rows(gla_norm_g), stack_rows(ret_norm_g), ones_blockdiag, x_lat, x_ctx, mods,
            stack_rows(norm2_g), 0, *post_w, final_norm_g.reshape(1, -1), n_ctx_tiles, with_ctx,
            post_riders)
        if not final:
            post_w = tuple(w[None] for w in converted)
        x_ctx = None
    return x_lat
```

```python
import functools

import numpy as np
import jax
import jax.numpy as jnp
from jax import lax
from jax.experimental import pallas as pl
from jax.experimental.pallas import tpu as pltpu

F32 = jnp.float32
BF16 = jnp.bfloat16

EPS = 1e-6
GRID_W = 64
ROPE_THETA = 10000.0
N_MOD = 6
CONV_KSIZE = 31
CONV_PAD = (CONV_KSIZE - 1) // 2
HALO = 16
GLA_TAU = 16.0
GLA_RANK = 16
CHUNK = 64
N_HEADS = 4
KV_HEADS = 2
HEAD_DV = 64
SCAN_DK = 32
GROUP_W = 256
ROW_TILE = 256
LANES = 128
SUBLANES = 8
BF16_SUBLANES = 16
VT_ROWS = HEAD_DV + BF16_SUBLANES
LOG2_E = 1.4426950408889634
RIDER_CHUNKS = 64
IN_PROJ_BATCH = 4
ROW_BATCH = 2
ATT_BATCH = 2
MIXER_BATCH = 2
SCORE_LOOKAHEAD = 2
V7X_VMEM_BYTES = 64 * 1024 * 1024
VMEM_LIMIT = V7X_VMEM_BYTES * 3 // 4

W_CONV = 2 * GROUP_W
W_GLA_QKVR = 2 * N_HEADS * SCAN_DK + 2 * GROUP_W
W_GLA = W_GLA_QKVR + 2 * N_HEADS * SCAN_DK
W_ATT = GROUP_W + 2 * KV_HEADS * HEAD_DV
W_RET = 2 * N_HEADS * SCAN_DK + 2 * GROUP_W
W_IN_A = W_CONV + W_GLA_QKVR
W_IN_B = W_ATT + W_RET


def _params(semantics):
    return pltpu.CompilerParams(dimension_semantics=semantics, vmem_limit_bytes=VMEM_LIMIT)


def _layer_spec(arr, layer):
    rest = arr.shape[1:]
    return pl.BlockSpec((None,) + rest, lambda *_: (layer,) + (0,) * len(rest))


def _const_spec(arr):
    return pl.BlockSpec(arr.shape, lambda *_: (0,) * arr.ndim)


def _rider_specs(items, step_of, n_steps):
    in_specs, args, out_specs, out_shapes = [], [], [], []
    n_chunks = 1
    while 2 * n_chunks <= min(n_steps, RIDER_CHUNKS):
        n_chunks *= 2
    chunk = lambda *g: jnp.minimum(step_of(*g), n_chunks - 1)
    for arr, layer in items:
        _, rows, cols = arr.shape
        rpc = rows // n_chunks
        assert rows % n_chunks == 0 and rpc % BF16_SUBLANES == 0
        in_specs.append(pl.BlockSpec((None, rpc, cols), lambda *g, layer=layer: (layer, chunk(*g), 0)))
        args.append(arr)
        out_specs.append(pl.BlockSpec((rpc, cols), lambda *g: (chunk(*g), 0)))
        out_shapes.append(jax.ShapeDtypeStruct((rows, cols), BF16))
    return in_specs, args, out_specs, out_shapes


def _run_riders(in_refs, out_refs):
    for src, dst in zip(in_refs, out_refs):
        dst[...] = src[...].astype(BF16)


def _silu(x):
    return x * jax.nn.sigmoid(x)


def _rms(x):
    return x * lax.rsqrt(jnp.mean(x * x, axis=-1, keepdims=True) + EPS)


def _group_sumsq(t, ones_blockdiag):
    t2 = t * t
    hi = t2.astype(BF16)
    lo = (t2 - hi.astype(F32)).astype(BF16)
    return (jnp.dot(hi, ones_blockdiag, preferred_element_type=F32)
            + jnp.dot(lo, ones_blockdiag, preferred_element_type=F32))


def _load_rows(split, is_ctx, lat_ref, ctx_ref, buf):
    if not split:
        return lat_ref[...]

    @pl.when(is_ctx)
    def _():
        buf[...] = ctx_ref[...]

    @pl.when(jnp.logical_not(is_ctx))
    def _():
        buf[...] = lat_ref[...]

    return buf[...]


def _row_specs(split, x_lat, x_ctx, nb, tm, n_ctx_tiles, tile0):
    d = x_lat.shape[2]
    if not split:
        return [pl.BlockSpec((nb, tm, d), lambda b, j: (b, j + tile0, 0))], [x_lat]
    return ([pl.BlockSpec((nb, tm, d), lambda b, j: (b, jnp.maximum(j - n_ctx_tiles, 0), 0)),
             pl.BlockSpec((nb, tm, d), lambda b, j: (b, jnp.minimum(j, n_ctx_tiles - 1), 0))],
            [x_lat, x_ctx])


def _mod_specs(layer, nb, width, ctx_row):
    return [pl.BlockSpec((None, nb, 1, width), lambda b, j: (layer, b, 0, 0)),
            pl.BlockSpec((None, 1, 1, width), lambda b, j: (layer, ctx_row, 0, 0))]


def _mod_row(is_ctx, lat_mod_ref, ctx_mod_ref, bi):
    f = is_ctx.astype(F32)
    return f * ctx_mod_ref[0] + (1.0 - f) * lat_mod_ref[bi]


def _mod_body(cc_ref, w_ref, b_ref, o_ref):
    s = _silu(cc_ref[...])
    o_ref[0] = jnp.dot(s.astype(BF16), w_ref[0].astype(BF16),
                       preferred_element_type=F32) + b_ref[0]


def _modulation(cc, w_mod, b_mod):
    depth, d, n = w_mod.shape
    tn = 1024
    return pl.pallas_call(
        _mod_body,
        grid=(depth, n // tn),
        in_specs=[pl.BlockSpec((8, d), lambda l, i: (0, 0)),
                  pl.BlockSpec((1, d, tn), lambda l, i: (l, 0, i)),
                  pl.BlockSpec((1, 1, tn), lambda l, i: (l, 0, i))],
        out_specs=pl.BlockSpec((1, 8, tn), lambda l, i: (l, 0, i)),
        out_shape=jax.ShapeDtypeStruct((depth, 8, n), F32),
        compiler_params=_params(("parallel", "parallel")),
        name="modulation",
    )(cc, w_mod, b_mod.reshape(depth, 1, n))


def _inproj_body(split, n_ctx_tiles, *refs):
    n_src = 2 if split else 1
    lat_ref = refs[0]
    ctx_ref = refs[1] if split else None
    (mod_ref, modc_ref, g_ref, wa_ref, wz_ref, wb_ref, cos_ref, sin_ref, qg_ref, kg_ref, ones_ref,
     waf_ref, baf_ref, wab_ref, bab_ref,
     conv_ref, gla_ref, ret_ref, q_out, k_out, vt_out) = refs[n_src:n_src + 21]
    xbuf = refs[n_src + 21] if split else None
    nb, tm, d = lat_ref.shape
    is_ctx = pl.program_id(1) < n_ctx_tiles

    x = _load_rows(split, is_ctx, lat_ref, ctx_ref, xbuf)
    wq = N_HEADS * HEAD_DV
    wk = KV_HEADS * HEAD_DV
    dk_all = N_HEADS * SCAN_DK

    def norm_rope(t, ss, gain, w):
        t = t * lax.rsqrt(ss * (1.0 / HEAD_DV) + EPS) * gain
        lane = lax.broadcasted_iota(jnp.int32, t.shape, 1)
        first_half = (lane % HEAD_DV) < HEAD_DV // 2
        rot = jnp.where(first_half, pltpu.roll(t, w - HEAD_DV // 2, 1), pltpu.roll(t, HEAD_DV // 2, 1))
        return t * cos_ref[:, :w] + rot * sin_ref[:, :w]

    low = lax.broadcasted_iota(jnp.int32, (tm, LANES), 1) < HEAD_DV
    ones = jnp.ones((VT_ROWS - HEAD_DV, tm), BF16)
    for bi in range(nb):
        mod = _mod_row(is_ctx, mod_ref, modc_ref, bi)
        shift, scale = mod[:, 0:d], mod[:, d:2 * d]
        hb = ((_rms(x[bi]) * g_ref[...]) * (1.0 + scale) + shift).astype(BF16)

        def proj(w_ref, lo, width):
            return jnp.dot(hb, w_ref[:, lo:lo + width], preferred_element_type=F32)

        att = proj(wb_ref, 0, W_ATT)
        q = att[:, :wq]
        k = att[:, wq:wq + wk]
        v = att[:, wq + wk:]
        ss_q = _group_sumsq(q, ones_ref[...])
        ss_k = _group_sumsq(k, ones_ref[:wk, :wk])
        z = proj(wz_ref, 0, LANES).astype(BF16)
        ret_ref[bi] = proj(wb_ref, W_ATT, W_RET)
        gla_ref[bi, :, 0:W_GLA_QKVR] = proj(wa_ref, W_CONV, W_GLA_QKVR)
        ag = proj(wa_ref, 0, W_CONV)
        conv_ref[bi] = ag[:, :GROUP_W] * jax.nn.sigmoid(ag[:, GROUP_W:])
        for n, (w_ref, bias_ref) in enumerate(((waf_ref, baf_ref), (wab_ref, bab_ref))):
            pre = jnp.dot(z, w_ref[...], preferred_element_type=F32) + bias_ref[...]
            log_sig = jnp.minimum(pre, 0.0) - jnp.log(1.0 + jnp.exp(-jnp.abs(pre)))
            lo = W_GLA_QKVR + n * dk_all
            gla_ref[bi, :, lo:lo + dk_all] = log_sig / GLA_TAU

        qn = norm_rope(q, ss_q, qg_ref[...], wq) * (HEAD_DV ** -0.5 * LOG2_E)
        kn = norm_rope(k, ss_k, kg_ref[...], wk)
        pair0 = qn[:, :LANES]
        pair1 = qn[:, LANES:]
        q_out[bi, 0] = jnp.where(low, pair0, 0.0).astype(BF16)
        q_out[bi, 1] = jnp.where(low, pltpu.roll(pair0, HEAD_DV, 1), 0.0).astype(BF16)
        q_out[bi, 2] = jnp.where(low, 0.0, pltpu.roll(pair1, HEAD_DV, 1)).astype(BF16)
        q_out[bi, 3] = jnp.where(low, 0.0, pair1).astype(BF16)
        k_out[bi] = kn.astype(BF16)
        vt = v.T.astype(BF16)
        for h in range(KV_HEADS):
            vt_out[bi, h, 0, 0:HEAD_DV, :] = vt[h * HEAD_DV:(h + 1) * HEAD_DV, :]
            vt_out[bi, h, 0, HEAD_DV:, :] = ones


def _in_proj(layer, x_lat, x_ctx, mods, norm1_g, w_layer, w_a, w_z, w_b, cos, sin, q_gain, k_gain,
             ones_blockdiag, w_a_f, b_a_f, w_a_b, b_a_b, n_ctx_tiles):
    split = x_ctx is not None
    bsz, _, d = x_lat.shape
    tm = ROW_TILE
    t_all = x_lat.shape[1] + (x_ctx.shape[1] if split else 0)
    n_tiles = t_all // tm
    ctx_row = bsz
    wq = N_HEADS * HEAD_DV
    nb = IN_PROJ_BATCH if bsz % IN_PROJ_BATCH == 0 else 1
    row_specs, row_args = _row_specs(split, x_lat, x_ctx, nb, tm, n_ctx_tiles, 0)
    tile = lambda b, j: (b, j, 0)
    return pl.pallas_call(
        functools.partial(_inproj_body, split, n_ctx_tiles),
        grid=(bsz // nb, n_tiles),
        in_specs=row_specs + _mod_specs(layer, nb, N_MOD * d, ctx_row) + [
            _layer_spec(norm1_g, layer), _layer_spec(w_a, w_layer), _layer_spec(w_z, w_layer),
            _layer_spec(w_b, w_layer),
            pl.BlockSpec((tm, wq), lambda b, j: (j, 0)), pl.BlockSpec((tm, wq), lambda b, j: (j, 0)),
            _layer_spec(q_gain, layer), _layer_spec(k_gain, layer), _const_spec(ones_blockdiag),
            _layer_spec(w_a_f, layer), _layer_spec(b_a_f, layer),
            _layer_spec(w_a_b, layer), _layer_spec(b_a_b, layer)],
        out_specs=[pl.BlockSpec((nb, tm, GROUP_W), tile), pl.BlockSpec((nb, tm, W_GLA), tile),
                   pl.BlockSpec((nb, tm, W_RET), tile),
                   pl.BlockSpec((nb, N_HEADS, tm, LANES), lambda b, j: (b, 0, j, 0)),
                   pl.BlockSpec((nb, tm, LANES), tile),
                   pl.BlockSpec((nb, KV_HEADS, 1, VT_ROWS, tm), lambda b, j: (b, 0, j, 0, 0))],
        out_shape=[jax.ShapeDtypeStruct((bsz, t_all, GROUP_W), F32),
                   jax.ShapeDtypeStruct((bsz, t_all, W_GLA), F32),
                   jax.ShapeDtypeStruct((bsz, t_all, W_RET), F32),
                   jax.ShapeDtypeStruct((bsz, N_HEADS, t_all, LANES), BF16),
                   jax.ShapeDtypeStruct((bsz, t_all, LANES), BF16),
                   jax.ShapeDtypeStruct((bsz, KV_HEADS, n_tiles, VT_ROWS, tm), BF16)],
        scratch_shapes=[pltpu.VMEM((nb, tm, d), F32)] if split else [],
        compiler_params=_params(("parallel", "parallel")),
        name="in_proj",
    )(*row_args, mods, mods, norm1_g, w_a, w_z, w_b, cos, sin, q_gain, k_gain, ones_blockdiag,
      w_a_f, b_a_f, w_a_b, b_a_b)


def _chunk_cumsum(la, tri_ref):
    w = la.shape[1]
    hi = la.astype(BF16)
    lo = (la - hi.astype(F32)).astype(BF16)
    both = jnp.dot(tri_ref[...], jnp.concatenate([hi, lo], axis=1), preferred_element_type=F32)
    return both[:, :w] + both[:, w:]


def _decay_factors(b, reverse):
    b_tot = b[0:1, :] if reverse else b[CHUNK - 1:CHUNK, :]
    dk_all = b.shape[1]
    dec_cols = jnp.broadcast_to(jnp.exp(b_tot), (dk_all, dk_all)).T
    dec_cols = jnp.concatenate([dec_cols] * (GROUP_W // dk_all), axis=1)
    return jnp.exp(b), jnp.exp(-b), jnp.exp(b_tot - b), dec_cols


def _scan_chunk_head(q, k, v, factors, st, key_mask, value_mask, head_mask):
    nt_dims = (((1,), (1,)), ((), ()))
    e_q, e_inv, e_end, dec_cols = factors
    q_dec = (q * e_q).astype(BF16)
    k_inv = (k * e_inv).astype(BF16)
    k_end = (k * e_end).astype(BF16)
    vb = v.astype(BF16)
    k_blk = jnp.concatenate([k_inv] * N_HEADS, axis=0) * key_mask
    scores = lax.dot_general(q_dec, k_blk, nt_dims, preferred_element_type=F32)
    o_inter = jnp.dot(q_dec, st.astype(BF16), preferred_element_type=F32)
    kv = lax.dot_general(k_end, vb, (((0,), (0,)), ((), ())), preferred_element_type=F32)
    st = st * dec_cols + jnp.where(head_mask, kv, 0.0)
    v_blk = jnp.concatenate([vb] * N_HEADS, axis=0) * value_mask
    return (scores, v_blk, o_inter), st


def _scan_chunk_tail(pending, tri):
    scores, v_blk, o_inter = pending
    scores = jnp.where(tri, scores, 0.0).astype(BF16)
    return jnp.dot(scores, v_blk, preferred_element_type=F32) + o_inter


def _scan_body(gf, gb, rf, rb, lg, cosf, sinf, cosb, sinb,
               key_mask_ref, value_mask_ref, cum_f_ref, cum_b_ref,
               gof, gob, rof, rob, st_all, b_all):
    n_batch = gf.shape[0]

    @pl.when(pl.program_id(1) == 0)
    def _():
        st_all[...] = jnp.zeros(st_all.shape, F32)

    tb = gf.shape[1]
    dk_all = N_HEADS * SCAN_DK
    dv_all = GROUP_W
    scale = SCAN_DK ** -0.5
    key_mask = key_mask_ref[...]
    value_mask = value_mask_ref[...]
    r = lax.broadcasted_iota(jnp.int32, (dk_all, dv_all), 0)
    c = lax.broadcasted_iota(jnp.int32, (dk_all, dv_all), 1)
    head_mask = (r // SCAN_DK) == (c // HEAD_DV)
    r = lax.broadcasted_iota(jnp.int32, (CHUNK, dv_all), 0)
    c = lax.broadcasted_iota(jnp.int32, (CHUNK, dv_all), 1) % CHUNK
    tri_f = c <= r
    tri_b = c >= r

    def qkv(ref, bi, lo):
        return (ref[bi, lo:lo + CHUNK, 0:dk_all], ref[bi, lo:lo + CHUNK, dk_all:2 * dk_all],
                ref[bi, lo:lo + CHUNK, 2 * dk_all:2 * dk_all + dv_all])

    def gla_chain(ref, bi, cum_ref, reverse):
        slot = int(reverse)
        lo = W_GLA_QKVR + slot * dk_all
        b_all[bi, slot] = _chunk_cumsum(ref[bi, :, lo:lo + dk_all], cum_ref)

        def load(lo):
            q, k, v = qkv(ref, bi, lo)
            return q * scale, k, v, _decay_factors(b_all[bi, slot, lo:lo + CHUNK, :], reverse)
        return load

    lane = lax.broadcasted_iota(jnp.int32, (CHUNK, dk_all), 1)
    first_half = (lane % SCAN_DK) < SCAN_DK // 2
    pos = lax.broadcasted_iota(jnp.int32, (CHUNK, dk_all), 0).astype(F32)

    ret_factors = [_decay_factors(((CHUNK - pos) if reverse else (pos + 1.0)) * lg[...], reverse)
                   for reverse in (False, True)]

    def ret_chain(ref, bi, cos_ref, sin_ref, reverse):
        def rope(t, lo):
            rot = jnp.where(first_half, pltpu.roll(t, dk_all - SCAN_DK // 2, 1),
                            pltpu.roll(t, SCAN_DK // 2, 1))
            return t * cos_ref[lo:lo + CHUNK, :] + rot * sin_ref[lo:lo + CHUNK, :]

        def load(lo):
            q, k, v = qkv(ref, bi, lo)
            return rope(q, lo), rope(k * scale, lo), v, ret_factors[int(reverse)]
        return load

    chains = []
    for bi in range(n_batch):
        chains += [
            (gla_chain(gf, bi, cum_f_ref, False), False, gof, bi, tri_f),
            (gla_chain(gb, bi, cum_b_ref, True), True, gob, bi, tri_b),
            (ret_chain(rf, bi, cosf, sinf, False), False, rof, bi, tri_f),
            (ret_chain(rb, bi, cosb, sinb, True), True, rob, bi, tri_b),
        ]
    st_vals = [st_all[n] for n in range(len(chains))]
    n_chunks = tb // CHUNK

    def heads(i):
        pending = []
        for n, (load, reverse, _, _, _) in enumerate(chains):
            lo = (n_chunks - 1 - i if reverse else i) * CHUNK
            q, k, v, factors = load(lo)
            part, st_vals[n] = _scan_chunk_head(q, k, v, factors, st_vals[n],
                                                key_mask, value_mask, head_mask)
            pending.append((lo, part))
        return pending

    pending = heads(0)
    for i in range(n_chunks):
        following = heads(i + 1) if i + 1 < n_chunks else None
        for (lo, part), (_, _, out_ref, bi, tri) in zip(pending, chains):
            out_ref[bi, lo:lo + CHUNK, :] = _scan_chunk_tail(part, tri)
        pending = following
    for n, val in enumerate(st_vals):
        st_all[n] = val


def _conv_tiles(j, n_ctx_tiles, n_tiles, main_ref, prev_ref, next_ref, wdw_ref, bdw_ref, lng_ref,
                lnb_ref, wpw_ref, bpw_ref, o_ref, ubuf, shifted, sbuf):
    n_batch, tt = main_ref.shape[0], main_ref.shape[1]
    c = o_ref.shape[2]

    has_prev = jnp.logical_and(j != 0, j != n_ctx_tiles)
    has_next = jnp.logical_and(j != n_ctx_tiles - 1, j != n_tiles - 1)

    @pl.when(has_prev)
    def _():
        ubuf[:, 0:HALO, :] = prev_ref[...]

    @pl.when(jnp.logical_not(has_prev))
    def _():
        ubuf[:, 0:HALO, :] = jnp.zeros((n_batch, HALO, c), F32)

    @pl.when(has_next)
    def _():
        ubuf[:, HALO + tt:, :] = next_ref[...]

    @pl.when(jnp.logical_not(has_next))
    def _():
        ubuf[:, HALO + tt:, :] = jnp.zeros((n_batch, HALO, c), F32)

    span = shifted.shape[2]
    rows = 128
    for bi in range(n_batch):
        ubuf[bi, HALO:HALO + tt, :] = main_ref[bi]
        for s in range(1, SUBLANES):
            shifted[bi, s - 1] = ubuf[bi, s:s + span, :]
        for r in range(0, tt, rows):
            acc = jnp.zeros((rows, c), F32)
            for k in range(CONV_KSIZE):
                off = k + HALO - CONV_PAD
                lo = r + off - off % SUBLANES
                if off % SUBLANES == 0:
                    tap = ubuf[bi, lo:lo + rows, :]
                else:
                    tap = shifted[bi, off % SUBLANES - 1, lo:lo + rows, :]
                acc = acc + tap * wdw_ref[k:k + 1, :]
            y = acc + bdw_ref[...]
            yc = y - jnp.mean(y, axis=-1, keepdims=True)
            yn = yc * lax.rsqrt(jnp.mean(yc * yc, axis=-1, keepdims=True) + EPS)
            yn = yn * lng_ref[...] + lnb_ref[...]
            sbuf[bi, r:r + rows, :] = _silu(yn).astype(BF16)
        o_ref[bi] = jnp.dot(sbuf[bi], wpw_ref[...], preferred_element_type=F32) + bpw_ref[...]


def _mixers_body(n_ctx_tiles, n_tiles, n_rid, *refs):
    scan_in, conv_in = refs[:13], refs[13:22]
    rider_in = refs[22:22 + n_rid]
    outs = refs[22 + n_rid:]
    scan_out, conv_out, rider_out = outs[:4], outs[4], outs[5:-5]
    scan_scratch, conv_scratch = outs[-5:-3], outs[-3:]
    _run_riders(rider_in, rider_out)
    _scan_body(*scan_in, *scan_out, *scan_scratch)
    _conv_tiles(pl.program_id(1), n_ctx_tiles, n_tiles, *conv_in, conv_out, *conv_scratch)


def _mixers(layer, gla_in, ret_in, conv_in, n_ctx_tiles, log_gamma, cos, sin,
            key_mask, value_mask, cum_f, cum_b, w_dw, b_dw, ln_g, ln_b, w_pw, b_pw, riders=()):
    bsz, t_all, _ = gla_in.shape
    tb = ROW_TILE
    n_tiles = t_all // tb
    dk_all = N_HEADS * SCAN_DK
    dv_all = GROUP_W
    c = GROUP_W
    nb = MIXER_BATCH if bsz % MIXER_BATCH == 0 else 1
    per = tb // HALO
    n_halo = t_all // HALO

    def bwd(j):
        return jnp.where(j < n_ctx_tiles, n_ctx_tiles - 1 - j, n_tiles - 1 - (j - n_ctx_tiles))

    fwd_blk = lambda b, j: (b, j, 0)
    bwd_blk = lambda b, j: (b, bwd(j), 0)
    n_steps = (bsz // nb) * n_tiles
    rid_in, rid_args, rid_out, rid_shapes = _rider_specs(riders, lambda b, j: b * n_tiles + j, n_steps)
    in_specs = [pl.BlockSpec((nb, tb, W_GLA), fwd_blk), pl.BlockSpec((nb, tb, W_GLA), bwd_blk),
                pl.BlockSpec((nb, tb, W_RET), fwd_blk), pl.BlockSpec((nb, tb, W_RET), bwd_blk),
                _const_spec(log_gamma),
                pl.BlockSpec((tb, dk_all), lambda b, j: (j, 0)),
                pl.BlockSpec((tb, dk_all), lambda b, j: (j, 0)),
                pl.BlockSpec((tb, dk_all), lambda b, j: (bwd(j), 0)),
                pl.BlockSpec((tb, dk_all), lambda b, j: (bwd(j), 0)),
                _const_spec(key_mask), _const_spec(value_mask), _const_spec(cum_f), _const_spec(cum_b),
                pl.BlockSpec((nb, tb, c), fwd_blk),
                pl.BlockSpec((nb, HALO, c), lambda b, j: (b, jnp.maximum(j * per - 1, 0), 0)),
                pl.BlockSpec((nb, HALO, c), lambda b, j: (b, jnp.minimum((j + 1) * per, n_halo - 1), 0)),
                _layer_spec(w_dw, layer), _layer_spec(b_dw, layer), _layer_spec(ln_g, layer),
                _layer_spec(ln_b, layer), _layer_spec(w_pw, layer), _layer_spec(b_pw, layer)] + rid_in
    return pl.pallas_call(
        functools.partial(_mixers_body, n_ctx_tiles, n_tiles, len(riders)),
        grid=(bsz // nb, n_tiles),
        in_specs=in_specs,
        out_specs=[pl.BlockSpec((nb, tb, dv_all), fwd_blk), pl.BlockSpec((nb, tb, dv_all), bwd_blk)] * 2
        + [pl.BlockSpec((nb, tb, c), fwd_blk)] + rid_out,
        out_shape=[jax.ShapeDtypeStruct((bsz, t_all, dv_all), F32)] * 5 + rid_shapes,
        scratch_shapes=[pltpu.VMEM((4 * nb, dk_all, dv_all), F32), pltpu.VMEM((nb, 2, tb, dk_all), F32),
                        pltpu.VMEM((nb, tb + 2 * HALO, c), F32),
                        pltpu.VMEM((nb, SUBLANES - 1, tb + 2 * HALO - SUBLANES, c), F32),
                        pltpu.VMEM((nb, tb, c), BF16)],
        compiler_params=_params(("arbitrary", "arbitrary")),
        name="mixers",
    )(gla_in, gla_in, ret_in, ret_in, log_gamma, cos, sin, cos, sin,
      key_mask, value_mask, cum_f, cum_b, conv_in, conv_in, conv_in, w_dw, b_dw, ln_g, ln_b, w_pw, b_pw,
      *rid_args)


def _att_body(n_ctx_tiles, n_tiles, tile0, q_ref, k_ref, vt_ref, o_ref):
    group = N_HEADS // KV_HEADS
    tq = q_ref.shape[2]
    tk = vt_ref.shape[4]
    nq = group * tq

    def attend_head(bi, h, n_blocks):
        q2 = q_ref[bi, h * group:(h + 1) * group].reshape(nq, LANES)

        def scores(i):
            kb = k_ref[bi, i * tk:(i + 1) * tk, :]
            return lax.dot_general(kb, q2, (((1,), (1,)), ((), ())), preferred_element_type=F32)

        m = acc = None
        ahead = [scores(i) for i in range(min(SCORE_LOOKAHEAD, n_blocks))]
        for i in range(n_blocks):
            s = ahead.pop(0)
            if i + SCORE_LOOKAHEAD < n_blocks:
                ahead.append(scores(i + SCORE_LOOKAHEAD))
            blk_max = jnp.max(s, axis=0, keepdims=True)
            m_new = blk_max if i == 0 else jnp.maximum(m, blk_max)
            p = jnp.exp2(s - m_new).astype(BF16)
            pv = jnp.dot(vt_ref[bi, h, i], p, preferred_element_type=F32)
            acc = pv if i == 0 else jnp.exp2(m - m_new) * acc + pv
            m = m_new
        o_t = acc[:HEAD_DV] / acc[HEAD_DV:HEAD_DV + 1]
        stacked = jnp.concatenate([o_t[:, g * tq:(g + 1) * tq] for g in range(group)], axis=0)
        o_ref[bi, :, h * group * HEAD_DV:(h + 1) * group * HEAD_DV] = stacked.T

    def attend(n_blocks):
        for bi in range(q_ref.shape[0]):
            for h in range(KV_HEADS):
                attend_head(bi, h, n_blocks)

    qi = pl.program_id(1) + tile0
    if tile0 < n_ctx_tiles:
        pl.when(qi < n_ctx_tiles)(lambda: attend(n_ctx_tiles))
    pl.when(qi >= n_ctx_tiles)(lambda: attend(n_tiles))


def _attention(q, k, vt, n_ctx_tiles, with_ctx):
    bsz, _, t_all, _ = q.shape
    tq = ROW_TILE
    n_tiles = t_all // tq
    tile0 = 0 if with_ctx else n_ctx_tiles
    nb = ATT_BATCH if bsz % ATT_BATCH == 0 else 1
    return pl.pallas_call(
        functools.partial(_att_body, n_ctx_tiles, n_tiles, tile0),
        grid=(bsz // nb, n_tiles - tile0),
        in_specs=[pl.BlockSpec((nb, N_HEADS, tq, LANES), lambda b, i: (b, 0, i + tile0, 0)),
                  pl.BlockSpec((nb, t_all, LANES), lambda b, i: (b, 0, 0)),
                  pl.BlockSpec((nb, KV_HEADS, n_tiles, VT_ROWS, tq), lambda b, i: (b, 0, 0, 0, 0))],
        out_specs=pl.BlockSpec((nb, tq, GROUP_W), lambda b, i: (b, i, 0)),
        out_shape=jax.ShapeDtypeStruct((bsz, t_all - tile0 * tq, GROUP_W), F32),
        compiler_params=_params(("parallel", "arbitrary")),
        name="attention",
    )(q, k, vt)


def _post_body(final, split, n_ctx_tiles, tile0, n_rid, *refs):
    (yconv, gla_f, gla_b, gla_gate, yatt, ret_f, ret_b, ret_gate, gla_gain, ret_gain, ones_ref) = refs[:11]
    n_src = 2 if split else 1
    lat_ref = refs[11]
    ctx_ref = refs[12] if split else None
    n_in = 18 + n_src
    mod_ref, modc_ref, n2g_ref, wout_ref, wup_ref, wdown_ref, fng_ref = refs[11 + n_src:n_in]
    o_ref = refs[n_in + n_rid]
    xbuf = refs[n_in + 2 * n_rid + 1] if split else None
    _run_riders(refs[n_in:n_in + n_rid], refs[n_in + n_rid + 1:n_in + 2 * n_rid + 1])
    nb, tm, d = lat_ref.shape
    rows = nb * tm
    d_ff = wup_ref.shape[1]
    is_ctx = pl.program_id(1) + tile0 < n_ctx_tiles

    def stacked(ref):
        return ref[...].reshape(rows, ref.shape[2])

    def finish(of_ref, ob_ref, gate_ref, gain_ref):
        o = stacked(of_ref) + stacked(ob_ref)
        ss = _group_sumsq(o, ones_ref[...])
        y = o * lax.rsqrt(ss * (1.0 / HEAD_DV) + EPS) * gain_ref[...]
        return (y * _silu(stacked(gate_ref))).astype(BF16)

    mixed = jnp.concatenate([stacked(yconv).astype(BF16), finish(gla_f, gla_b, gla_gate, gla_gain),
                             stacked(yatt).astype(BF16), finish(ret_f, ret_b, ret_gate, ret_gain)], axis=1)
    o = jnp.dot(mixed, wout_ref[...], preferred_element_type=F32)
    x = _load_rows(split, is_ctx, lat_ref, ctx_ref, xbuf)
    mods = [_mod_row(is_ctx, mod_ref, modc_ref, bi) for bi in range(nb)]
    x_mid, h2 = [], []
    for bi in range(nb):
        gate1 = mods[bi][:, 2 * d:3 * d]
        shift2 = mods[bi][:, 3 * d:4 * d]
        scale2 = mods[bi][:, 4 * d:5 * d]
        x_mid.append(x[bi] + gate1 * o[bi * tm:(bi + 1) * tm])
        h2.append(((_rms(x_mid[bi]) * n2g_ref[...]) * (1.0 + scale2) + shift2).astype(BF16))
    h2 = jnp.concatenate(h2, axis=0)
    acc = jnp.zeros((rows, d), F32)
    tf = 1024
    for f in range(0, d_ff, tf):
        u = jnp.dot(h2, wup_ref[:, f:f + tf], preferred_element_type=F32)
        a = jnp.square(jnp.maximum(u, 0.0)).astype(BF16)
        acc = acc + jnp.dot(a, wdown_ref[f:f + tf, :], preferred_element_type=F32)
    for bi in range(nb):
        gate2 = mods[bi][:, 5 * d:6 * d]
        x_out = x_mid[bi] + gate2 * acc[bi * tm:(bi + 1) * tm]
        if final:
            x_out = _rms(x_out) * fng_ref[...]
        o_ref[bi] = x_out


def _post(layer, final, yconv, gla_of, gla_ob, gla_in, yatt, ret_of, ret_ob, ret_in, gla_gain, ret_gain,
          ones_blockdiag, x_lat, x_ctx, mods, norm2_g, w_layer, w_out, w_up, w_down, final_g, n_ctx_tiles,
          with_ctx, riders=()):
    split = x_ctx is not None
    bsz, _, d = x_lat.shape
    tm = ROW_TILE
    t_all = gla_of.shape[1]
    ctx_row = bsz
    tile0 = 0 if with_ctx else n_ctx_tiles
    n_out = t_all // tm - tile0
    c = GROUP_W
    gate_blk = (2 * N_HEADS * SCAN_DK + c) // c
    full = lambda b, j: (b, j + tile0, 0)
    own = lambda b, j: (b, j, 0)
    assert not split or tile0 == 0
    nb = ROW_BATCH if bsz % ROW_BATCH == 0 else 1
    row_specs, row_args = _row_specs(split, x_lat, x_ctx, nb, tm, n_ctx_tiles, tile0)

    def resident(arr):
        rest = arr.shape[1:]
        return pl.BlockSpec((None,) + rest, lambda *_: (w_layer,) + (0,) * len(rest),
                            pipeline_mode=pl.Buffered(1))

    n_steps = (bsz // nb) * n_out
    rid_in, rid_args, rid_out, rid_shapes = _rider_specs(riders, lambda b, j: b * n_out + j, n_steps)
    return pl.pallas_call(
        functools.partial(_post_body, final, split, n_ctx_tiles, tile0, len(riders)),
        grid=(bsz // nb, n_out),
        in_specs=[pl.BlockSpec((nb, tm, c), full),
                  pl.BlockSpec((nb, tm, c), full), pl.BlockSpec((nb, tm, c), full),
                  pl.BlockSpec((nb, tm, c), lambda b, j: (b, j + tile0, gate_blk)),
                  pl.BlockSpec((nb, tm, c), own),
                  pl.BlockSpec((nb, tm, c), full), pl.BlockSpec((nb, tm, c), full),
                  pl.BlockSpec((nb, tm, c), lambda b, j: (b, j + tile0, gate_blk)),
                  _layer_spec(gla_gain, layer), _layer_spec(ret_gain, layer),
                  _const_spec(ones_blockdiag)] + row_specs + _mod_specs(layer, nb, N_MOD * d, ctx_row) + [
                  _layer_spec(norm2_g, layer),
                  resident(w_out), resident(w_up), resident(w_down),
                  _const_spec(final_g)] + rid_in,
        out_specs=[pl.BlockSpec((nb, tm, d), own)] + rid_out,
        out_shape=[jax.ShapeDtypeStruct((bsz, n_out * tm, d), F32)] + rid_shapes,
        scratch_shapes=[pltpu.VMEM((nb, tm, d), F32)] if split else [],
        compiler_params=_params(("arbitrary", "arbitrary") if riders else ("parallel", "parallel")),
        name="post",
    )(yconv, gla_of, gla_ob, gla_in, yatt, ret_of, ret_ob, ret_in, gla_gain, ret_gain,
      ones_blockdiag, *row_args, mods, mods, norm2_g, w_out, w_up, w_down, final_g, *rid_args)


def _rope_table(t_lat, n_ctx, head_dim, width):
    n_ax = head_dim // 4
    inv = np.float32(ROPE_THETA) ** (-np.arange(n_ax, dtype=np.float32) / np.float32(n_ax))
    pos = np.arange(t_lat)
    ang = np.concatenate([(pos // GRID_W).astype(np.float32)[:, None] * inv,
                          (pos % GRID_W).astype(np.float32)[:, None] * inv], axis=-1).astype(np.float32)
    cos, sin = np.cos(ang), np.sin(ang)
    cos = np.concatenate([np.ones((n_ctx, head_dim // 2), np.float32), cos], axis=0)
    sin = np.concatenate([np.zeros((n_ctx, head_dim // 2), np.float32), sin], axis=0)
    reps = width // head_dim
    return (jnp.asarray(np.tile(np.concatenate([cos, cos], axis=-1), (1, reps)), F32),
            jnp.asarray(np.tile(np.concatenate([-sin, sin], axis=-1), (1, reps)), F32))


def kernel(x, c, ctx, c_ctx, w_mod, b_mod, norm1_g, norm2_g, w_in, conv_w_dw, conv_b_dw, conv_ln_g,
           conv_ln_b, conv_w_pw, conv_b_pw, gla_w_a_f, gla_b_a_f, gla_w_a_b, gla_b_a_b, gla_norm_g,
           att_q_norm_g, att_k_norm_g, ret_norm_g, w_out, w_up, w_down, final_norm_g):
    bsz, t_lat, d = x.shape
    n_ctx = ctx.shape[1]
    depth = w_mod.shape[0]
    assert bsz + 1 <= 8 and n_ctx % ROW_TILE == 0 and t_lat % ROW_TILE == 0
    n_ctx_tiles = n_ctx // ROW_TILE
    dk_all = N_HEADS * SCAN_DK

    cc = jnp.concatenate([c, c_ctx[None, :], jnp.zeros((8 - bsz - 1, d), F32)], axis=0)
    mods = _modulation(cc, w_mod, b_mod)
    mods = mods.reshape(depth, 8, 1, N_MOD * d)

    stack_rows = lambda v: v.reshape(depth, 1, -1)
    w_in_b = w_in.astype(BF16)
    in_w = (w_in_b[:, :, :W_IN_A],
            jnp.pad(w_in_b[:, :, W_IN_A:W_IN_A + GLA_RANK], ((0, 0), (0, 0), (0, LANES - GLA_RANK))),
            w_in_b[:, :, W_IN_A + GLA_RANK:])
    pad_rank = lambda w: jnp.pad(w, ((0, 0), (0, LANES - GLA_RANK), (0, 0))).astype(BF16)
    w_a_f, w_a_b = pad_rank(gla_w_a_f), pad_rank(gla_w_a_b)
    w_pw = conv_w_pw.astype(BF16)
    q_gain = stack_rows(jnp.tile(att_q_norm_g, (1, N_HEADS)))
    k_gain = stack_rows(jnp.tile(att_k_norm_g, (1, KV_HEADS)))

    cos_att, sin_att = _rope_table(t_lat, n_ctx, HEAD_DV, N_HEADS * HEAD_DV)
    cos_ret, sin_ret = _rope_table(t_lat, n_ctx, SCAN_DK, dk_all)
    gamma = 1.0 - np.exp2(-5.0 - np.arange(N_HEADS, dtype=np.float64))
    log_gamma = jnp.asarray(np.repeat(np.log(gamma), SCAN_DK)[None, :], F32)
    gid = np.arange(GROUP_W) // HEAD_DV
    ones_blockdiag = jnp.asarray(gid[:, None] == gid[None, :], BF16)
    key_mask = jnp.asarray(gid[:, None] == (np.arange(dk_all) // SCAN_DK)[None, :], BF16)
    pos = np.arange(ROW_TILE)
    same_chunk = (pos[:, None] // CHUNK) == (pos[None, :] // CHUNK)
    cum_f = jnp.asarray(same_chunk & (pos[None, :] <= pos[:, None]), BF16)
    cum_b = jnp.asarray(same_chunk & (pos[None, :] >= pos[:, None]), BF16)

    x_lat, x_ctx = x, ctx
    for l in range(depth):
        with_ctx = l < depth - 1
        final = l == depth - 1
        conv_in, gla_in, ret_in, q, k, vt = _in_proj(
            l, x_lat, x_ctx, mods, stack_rows(norm1_g), l, *in_w, cos_att, sin_att, q_gain, k_gain,
            ones_blockdiag, w_a_f, stack_rows(gla_b_a_f), w_a_b, stack_rows(gla_b_a_b), n_ctx_tiles)
        mixer_riders = [(w, 0) for w in (w_out, w_up, w_down)] if l == 0 else []
        gla_of, gla_ob, ret_of, ret_ob, y_conv, *converted = _mixers(
            l, gla_in, ret_in, conv_in, n_ctx_tiles, log_gamma, cos_ret, sin_ret, key_mask,
            ones_blockdiag, cum_f, cum_b, conv_w_dw, stack_rows(conv_b_dw), stack_rows(conv_ln_g),
            stack_rows(conv_ln_b), w_pw, stack_rows(conv_b_pw), mixer_riders)
        if l == 0:
            post_w = tuple(w[None] for w in converted)
        y_att = _attention(q, k, vt, n_ctx_tiles, with_ctx)
        post_riders = [] if final else [(w, l + 1) for w in (w_out, w_up, w_down)]
        x_lat, *converted = _post(
            l, final, y_conv, gla_of, gla_ob, gla_in, y_att, ret_of, ret_ob, ret_in,
            stack_rows(gla_norm_g), stack_rows(ret_norm_g), ones_blockdiag, x_lat, x_ctx, mods,
            stack_rows(norm2_g), 0, *post_w, final_norm_g.reshape(1, -1), n_ctx_tiles, with_ctx,
            post_riders)
        if not final:
            post_w = tuple(w[None] for w in converted)
        x_ctx = None
    return x_lat
```

```python
import functools

import numpy as np
import jax
import jax.numpy as jnp
from jax import lax
from jax.experimental import pallas as pl
from jax.experimental.pallas import tpu as pltpu

F32 = jnp.float32
BF16 = jnp.bfloat16

EPS = 1e-6
GRID_W = 64
ROPE_THETA = 10000.0
N_MOD = 6
CONV_KSIZE = 31
CONV_PAD = (CONV_KSIZE - 1) // 2
HALO = 16
GLA_TAU = 16.0
GLA_RANK = 16
CHUNK = 64
N_HEADS = 4
KV_HEADS = 2
HEAD_DV = 64
SCAN_DK = 32
GROUP_W = 256
ROW_TILE = 256
LANES = 128
SUBLANES = 8
BF16_SUBLANES = 16
VT_ROWS = HEAD_DV + BF16_SUBLANES
LOG2_E = 1.4426950408889634
RIDER_CHUNKS = 64
IN_PROJ_BATCH = 4
ROW_BATCH = 2
ATT_BATCH = 2
MIXER_BATCH = 2
SCORE_LOOKAHEAD = 2
V7X_VMEM_BYTES = 64 * 1024 * 1024
VMEM_LIMIT = V7X_VMEM_BYTES * 3 // 4

W_CONV = 2 * GROUP_W
W_GLA_QKVR = 2 * N_HEADS * SCAN_DK + 2 * GROUP_W
W_GLA = W_GLA_QKVR + 2 * N_HEADS * SCAN_DK
W_ATT = GROUP_W + 2 * KV_HEADS * HEAD_DV
W_RET = 2 * N_HEADS * SCAN_DK + 2 * GROUP_W
W_IN_A = W_CONV + W_GLA_QKVR


def _params(semantics):
    return pltpu.CompilerParams(dimension_semantics=semantics, vmem_limit_bytes=VMEM_LIMIT)


def _layer_spec(arr, layer):
    rest = arr.shape[1:]
    return pl.BlockSpec((None,) + rest, lambda *_: (layer,) + (0,) * len(rest))


def _const_spec(arr):
    return pl.BlockSpec(arr.shape, lambda *_: (0,) * arr.ndim)


def _rider_specs(items, step_of, n_steps):
    in_specs, args, out_specs, out_shapes = [], [], [], []
    n_chunks = 1
    while 2 * n_chunks <= min(n_steps, RIDER_CHUNKS):
        n_chunks *= 2
    chunk = lambda *g: jnp.minimum(step_of(*g), n_chunks - 1)
    for arr, layer in items:
        _, rows, cols = arr.shape
        rpc = rows // n_chunks
        assert rows % n_chunks == 0 and rpc % BF16_SUBLANES == 0
        in_specs.append(pl.BlockSpec((None, rpc, cols), lambda *g, layer=layer: (layer, chunk(*g), 0)))
        args.append(arr)
        out_specs.append(pl.BlockSpec((rpc, cols), lambda *g: (chunk(*g), 0)))
        out_shapes.append(jax.ShapeDtypeStruct((rows, cols), BF16))
    return in_specs, args, out_specs, out_shapes


def _run_riders(in_refs, out_refs):
    for src, dst in zip(in_refs, out_refs):
        dst[...] = src[...].astype(BF16)


def _silu(x):
    return x * jax.nn.sigmoid(x)


def _rms(x):
    return x * lax.rsqrt(jnp.mean(x * x, axis=-1, keepdims=True) + EPS)


def _group_sumsq(t, ones_blockdiag):
    t2 = t * t
    hi = t2.astype(BF16)
    lo = (t2 - hi.astype(F32)).astype(BF16)
    return (jnp.dot(hi, ones_blockdiag, preferred_element_type=F32)
            + jnp.dot(lo, ones_blockdiag, preferred_element_type=F32))


def _load_rows(split, is_ctx, lat_ref, ctx_ref, buf):
    if not split:
        return lat_ref[...]

    @pl.when(is_ctx)
    def _():
        buf[...] = ctx_ref[...]

    @pl.when(jnp.logical_not(is_ctx))
    def _():
        buf[...] = lat_ref[...]

    return buf[...]


def _row_specs(split, x_lat, x_ctx, nb, tm, n_ctx_tiles, tile0):
    d = x_lat.shape[2]
    if not split:
        return [pl.BlockSpec((nb, tm, d), lambda b, j: (b, j + tile0, 0))], [x_lat]
    return ([pl.BlockSpec((nb, tm, d), lambda b, j: (b, jnp.maximum(j - n_ctx_tiles, 0), 0)),
             pl.BlockSpec((nb, tm, d), lambda b, j: (b, jnp.minimum(j, n_ctx_tiles - 1), 0))],
            [x_lat, x_ctx])


def _mod_specs(layer, nb, width, ctx_row):
    return [pl.BlockSpec((None, nb, 1, width), lambda b, j: (layer, b, 0, 0)),
            pl.BlockSpec((None, 1, 1, width), lambda b, j: (layer, ctx_row, 0, 0))]


def _mod_row(is_ctx, lat_mod_ref, ctx_mod_ref, bi):
    f = is_ctx.astype(F32)
    return f * ctx_mod_ref[0] + (1.0 - f) * lat_mod_ref[bi]


def _mod_body(cc_ref, w_ref, b_ref, o_ref):
    s = _silu(cc_ref[...])
    o_ref[0] = jnp.dot(s.astype(BF16), w_ref[0].astype(BF16),
                       preferred_element_type=F32) + b_ref[0]


def _modulation(cc, w_mod, b_mod):
    depth, d, n = w_mod.shape
    tn = 1024
    return pl.pallas_call(
        _mod_body,
        grid=(depth, n // tn),
        in_specs=[pl.BlockSpec((8, d), lambda l, i: (0, 0)),
                  pl.BlockSpec((1, d, tn), lambda l, i: (l, 0, i)),
                  pl.BlockSpec((1, 1, tn), lambda l, i: (l, 0, i))],
        out_specs=pl.BlockSpec((1, 8, tn), lambda l, i: (l, 0, i)),
        out_shape=jax.ShapeDtypeStruct((depth, 8, n), F32),
        compiler_params=_params(("parallel", "parallel")),
        name="modulation",
    )(cc, w_mod, b_mod.reshape(depth, 1, n))


def _inproj_body(split, n_ctx_tiles, *refs):
    n_src = 2 if split else 1
    lat_ref = refs[0]
    ctx_ref = refs[1] if split else None
    (mod_ref, modc_ref, g_ref, wa_ref, wz_ref, wb_ref, cos_ref, sin_ref, qg_ref, kg_ref, ones_ref,
     waf_ref, baf_ref, wab_ref, bab_ref,
     conv_ref, gla_ref, ret_ref, q_out, k_out, vt_out) = refs[n_src:n_src + 21]
    xbuf = refs[n_src + 21] if split else None
    nb, tm, d = lat_ref.shape
    is_ctx = pl.program_id(1) < n_ctx_tiles

    x = _load_rows(split, is_ctx, lat_ref, ctx_ref, xbuf)
    wq = N_HEADS * HEAD_DV
    wk = KV_HEADS * HEAD_DV
    dk_all = N_HEADS * SCAN_DK

    def norm_rope(t, ss, gain, w):
        t = t * lax.rsqrt(ss * (1.0 / HEAD_DV) + EPS) * gain
        lane = lax.broadcasted_iota(jnp.int32, t.shape, 1)
        first_half = (lane % HEAD_DV) < HEAD_DV // 2
        rot = jnp.where(first_half, pltpu.roll(t, w - HEAD_DV // 2, 1), pltpu.roll(t, HEAD_DV // 2, 1))
        return t * cos_ref[:, :w] + rot * sin_ref[:, :w]

    low = lax.broadcasted_iota(jnp.int32, (tm, LANES), 1) < HEAD_DV
    ones = jnp.ones((VT_ROWS - HEAD_DV, tm), BF16)
    for bi in range(nb):
        mod = _mod_row(is_ctx, mod_ref, modc_ref, bi)
        shift, scale = mod[:, 0:d], mod[:, d:2 * d]
        hb = ((_rms(x[bi]) * g_ref[...]) * (1.0 + scale) + shift).astype(BF16)

        def proj(w_ref, lo, width):
            return jnp.dot(hb, w_ref[:, lo:lo + width], preferred_element_type=F32)

        att = proj(wb_ref, 0, W_ATT)
        q = att[:, :wq]
        k = att[:, wq:wq + wk]
        v = att[:, wq + wk:]
        z = proj(wz_ref, 0, LANES).astype(BF16)
        ag = proj(wa_ref, 0, W_CONV)
        conv_ref[bi] = ag[:, :GROUP_W] * jax.nn.sigmoid(ag[:, GROUP_W:])
        ss_q = _group_sumsq(q, ones_ref[...])
        ss_k = _group_sumsq(k, ones_ref[:wk, :wk])
        ret_ref[bi] = proj(wb_ref, W_ATT, W_RET)
        for n, (w_ref, bias_ref) in enumerate(((waf_ref, baf_ref), (wab_ref, bab_ref))):
            pre = jnp.dot(z, w_ref[...], preferred_element_type=F32) + bias_ref[...]
            log_sig = jnp.minimum(pre, 0.0) - jnp.log(1.0 + jnp.exp(-jnp.abs(pre)))
            lo = W_GLA_QKVR + n * dk_all
            gla_ref[bi, :, lo:lo + dk_all] = log_sig / GLA_TAU
        gla_ref[bi, :, 0:W_GLA_QKVR] = proj(wa_ref, W_CONV, W_GLA_QKVR)

        qn = norm_rope(q, ss_q, qg_ref[...], wq) * (HEAD_DV ** -0.5 * LOG2_E)
        kn = norm_rope(k, ss_k, kg_ref[...], wk)
        pair0 = qn[:, :LANES]
        pair1 = qn[:, LANES:]
        q_out[bi, 0] = jnp.where(low, pair0, 0.0).astype(BF16)
        q_out[bi, 1] = jnp.where(low, pltpu.roll(pair0, HEAD_DV, 1), 0.0).astype(BF16)
        q_out[bi, 2] = jnp.where(low, 0.0, pltpu.roll(pair1, HEAD_DV, 1)).astype(BF16)
        q_out[bi, 3] = jnp.where(low, 0.0, pair1).astype(BF16)
        k_out[bi] = kn.astype(BF16)
        vt = v.T.astype(BF16)
        for h in range(KV_HEADS):
            vt_out[bi, h, 0, 0:HEAD_DV, :] = vt[h * HEAD_DV:(h + 1) * HEAD_DV, :]
            vt_out[bi, h, 0, HEAD_DV:, :] = ones


def _in_proj(layer, x_lat, x_ctx, mods, norm1_g, w_layer, w_a, w_z, w_b, cos, sin, q_gain, k_gain,
             ones_blockdiag, w_a_f, b_a_f, w_a_b, b_a_b, n_ctx_tiles):
    split = x_ctx is not None
    bsz, _, d = x_lat.shape
    tm = ROW_TILE
    t_all = x_lat.shape[1] + (x_ctx.shape[1] if split else 0)
    n_tiles = t_all // tm
    ctx_row = bsz
    wq = N_HEADS * HEAD_DV
    nb = IN_PROJ_BATCH if bsz % IN_PROJ_BATCH == 0 else 1
    row_specs, row_args = _row_specs(split, x_lat, x_ctx, nb, tm, n_ctx_tiles, 0)
    tile = lambda b, j: (b, j, 0)
    return pl.pallas_call(
        functools.partial(_inproj_body, split, n_ctx_tiles),
        grid=(bsz // nb, n_tiles),
        in_specs=row_specs + _mod_specs(layer, nb, N_MOD * d, ctx_row) + [
            _layer_spec(norm1_g, layer), _layer_spec(w_a, w_layer), _layer_spec(w_z, w_layer),
            _layer_spec(w_b, w_layer),
            pl.BlockSpec((tm, wq), lambda b, j: (j, 0)), pl.BlockSpec((tm, wq), lambda b, j: (j, 0)),
            _layer_spec(q_gain, layer), _layer_spec(k_gain, layer), _const_spec(ones_blockdiag),
            _layer_spec(w_a_f, layer), _layer_spec(b_a_f, layer),
            _layer_spec(w_a_b, layer), _layer_spec(b_a_b, layer)],
        out_specs=[pl.BlockSpec((nb, tm, GROUP_W), tile), pl.BlockSpec((nb, tm, W_GLA), tile),
                   pl.BlockSpec((nb, tm, W_RET), tile),
                   pl.BlockSpec((nb, N_HEADS, tm, LANES), lambda b, j: (b, 0, j, 0)),
                   pl.BlockSpec((nb, tm, LANES), tile),
                   pl.BlockSpec((nb, KV_HEADS, 1, VT_ROWS, tm), lambda b, j: (b, 0, j, 0, 0))],
        out_shape=[jax.ShapeDtypeStruct((bsz, t_all, GROUP_W), F32),
                   jax.ShapeDtypeStruct((bsz, t_all, W_GLA), F32),
                   jax.ShapeDtypeStruct((bsz, t_all, W_RET), F32),
                   jax.ShapeDtypeStruct((bsz, N_HEADS, t_all, LANES), BF16),
                   jax.ShapeDtypeStruct((bsz, t_all, LANES), BF16),
                   jax.ShapeDtypeStruct((bsz, KV_HEADS, n_tiles, VT_ROWS, tm), BF16)],
        scratch_shapes=[pltpu.VMEM((nb, tm, d), F32)] if split else [],
        compiler_params=_params(("parallel", "parallel")),
        name="in_proj",
    )(*row_args, mods, mods, norm1_g, w_a, w_z, w_b, cos, sin, q_gain, k_gain, ones_blockdiag,
      w_a_f, b_a_f, w_a_b, b_a_b)


def _chunk_cumsum(la, tri_ref):
    w = la.shape[1]
    hi = la.astype(BF16)
    lo = (la - hi.astype(F32)).astype(BF16)
    both = jnp.dot(tri_ref[...], jnp.concatenate([hi, lo], axis=1), preferred_element_type=F32)
    return both[:, :w] + both[:, w:]


def _decay_factors(b, reverse):
    b_tot = b[0:1, :] if reverse else b[CHUNK - 1:CHUNK, :]
    dk_all = b.shape[1]
    dec_cols = jnp.broadcast_to(jnp.exp(b_tot), (dk_all, dk_all)).T
    dec_cols = jnp.concatenate([dec_cols] * (GROUP_W // dk_all), axis=1)
    return jnp.exp(b), jnp.exp(-b), jnp.exp(b_tot - b), dec_cols


def _scan_chunk_head(q, k, v, factors, st, key_mask, value_mask, head_mask):
    nt_dims = (((1,), (1,)), ((), ()))
    e_q, e_inv, e_end, dec_cols = factors
    q_dec = (q * e_q).astype(BF16)
    k_inv = (k * e_inv).astype(BF16)
    k_end = (k * e_end).astype(BF16)
    vb = v.astype(BF16)
    k_blk = jnp.concatenate([k_inv] * N_HEADS, axis=0) * key_mask
    scores = lax.dot_general(q_dec, k_blk, nt_dims, preferred_element_type=F32)
    o_inter = jnp.dot(q_dec, st.astype(BF16), preferred_element_type=F32)
    kv = lax.dot_general(k_end, vb, (((0,), (0,)), ((), ())), preferred_element_type=F32)
    st = st * dec_cols + jnp.where(head_mask, kv, 0.0)
    v_blk = jnp.concatenate([vb] * N_HEADS, axis=0) * value_mask
    return (scores, v_blk, o_inter), st


def _scan_chunk_tail(pending, tri):
    scores, v_blk, o_inter = pending
    scores = jnp.where(tri, scores, 0.0).astype(BF16)
    return jnp.dot(scores, v_blk, preferred_element_type=F32) + o_inter


def _scan_body(gf, gb, rf, rb, lg, cosf, sinf, cosb, sinb,
               key_mask_ref, value_mask_ref, cum_f_ref, cum_b_ref,
               gof, gob, rof, rob, st_all, b_all):
    n_batch = gf.shape[0]

    @pl.when(pl.program_id(1) == 0)
    def _():
        st_all[...] = jnp.zeros(st_all.shape, F32)

    tb = gf.shape[1]
    dk_all = N_HEADS * SCAN_DK
    dv_all = GROUP_W
    scale = SCAN_DK ** -0.5
    key_mask = key_mask_ref[...]
    value_mask = value_mask_ref[...]
    r = lax.broadcasted_iota(jnp.int32, (dk_all, dv_all), 0)
    c = lax.broadcasted_iota(jnp.int32, (dk_all, dv_all), 1)
    head_mask = (r // SCAN_DK) == (c // HEAD_DV)
    r = lax.broadcasted_iota(jnp.int32, (CHUNK, dv_all), 0)
    c = lax.broadcasted_iota(jnp.int32, (CHUNK, dv_all), 1) % CHUNK
    tri_f = c <= r
    tri_b = c >= r

    def qkv(ref, bi, lo):
        return (ref[bi, lo:lo + CHUNK, 0:dk_all], ref[bi, lo:lo + CHUNK, dk_all:2 * dk_all],
                ref[bi, lo:lo + CHUNK, 2 * dk_all:2 * dk_all + dv_all])

    def gla_chain(ref, bi, cum_ref, reverse):
        slot = int(reverse)
        lo = W_GLA_QKVR + slot * dk_all
        b_all[bi, slot] = _chunk_cumsum(ref[bi, :, lo:lo + dk_all], cum_ref)

        def load(lo):
            q, k, v = qkv(ref, bi, lo)
            return q * scale, k, v, _decay_factors(b_all[bi, slot, lo:lo + CHUNK, :], reverse)
        return load

    lane = lax.broadcasted_iota(jnp.int32, (CHUNK, dk_all), 1)
    first_half = (lane % SCAN_DK) < SCAN_DK // 2
    pos = lax.broadcasted_iota(jnp.int32, (CHUNK, dk_all), 0).astype(F32)

    ret_factors = [_decay_factors(((CHUNK - pos) if reverse else (pos + 1.0)) * lg[...], reverse)
                   for reverse in (False, True)]

    def ret_chain(ref, bi, cos_ref, sin_ref, reverse):
        def rope(t, lo):
            rot = jnp.where(first_half, pltpu.roll(t, dk_all - SCAN_DK // 2, 1),
                            pltpu.roll(t, SCAN_DK // 2, 1))
            return t * cos_ref[lo:lo + CHUNK, :] + rot * sin_ref[lo:lo + CHUNK, :]

        def load(lo):
            q, k, v = qkv(ref, bi, lo)
            return rope(q, lo), rope(k * scale, lo), v, ret_factors[int(reverse)]
        return load

    chains = []
    for bi in range(n_batch):
        chains += [
            (gla_chain(gf, bi, cum_f_ref, False), False, gof, bi, tri_f),
            (gla_chain(gb, bi, cum_b_ref, True), True, gob, bi, tri_b),
            (ret_chain(rf, bi, cosf, sinf, False), False, rof, bi, tri_f),
            (ret_chain(rb, bi, cosb, sinb, True), True, rob, bi, tri_b),
        ]
    st_vals = [st_all[n] for n in range(len(chains))]
    n_chunks = tb // CHUNK

    def heads(i):
        pending = []
        for n, (load, reverse, _, _, _) in enumerate(chains):
            lo = (n_chunks - 1 - i if reverse else i) * CHUNK
            q, k, v, factors = load(lo)
            part, st_vals[n] = _scan_chunk_head(q, k, v, factors, st_vals[n],
                                                key_mask, value_mask, head_mask)
            pending.append((lo, part))
        return pending

    pending = heads(0)
    for i in range(n_chunks):
        following = heads(i + 1) if i + 1 < n_chunks else None
        for (lo, part), (_, _, out_ref, bi, tri) in zip(pending, chains):
            out_ref[bi, lo:lo + CHUNK, :] = _scan_chunk_tail(part, tri)
        pending = following
    for n, val in enumerate(st_vals):
        st_all[n] = val


def _conv_tiles(j, n_ctx_tiles, n_tiles, main_ref, prev_ref, next_ref, wdw_ref, bdw_ref, lng_ref,
                lnb_ref, wpw_ref, bpw_ref, o_ref, ubuf, shifted, sbuf):
    n_batch, tt = main_ref.shape[0], main_ref.shape[1]
    c = o_ref.shape[2]

    has_prev = jnp.logical_and(j != 0, j != n_ctx_tiles)
    has_next = jnp.logical_and(j != n_ctx_tiles - 1, j != n_tiles - 1)

    @pl.when(has_prev)
    def _():
        ubuf[:, 0:HALO, :] = prev_ref[...]

    @pl.when(jnp.logical_not(has_prev))
    def _():
        ubuf[:, 0:HALO, :] = jnp.zeros((n_batch, HALO, c), F32)

    @pl.when(has_next)
    def _():
        ubuf[:, HALO + tt:, :] = next_ref[...]

    @pl.when(jnp.logical_not(has_next))
    def _():
        ubuf[:, HALO + tt:, :] = jnp.zeros((n_batch, HALO, c), F32)

    span = shifted.shape[2]
    rows = 128
    for bi in range(n_batch):
        ubuf[bi, HALO:HALO + tt, :] = main_ref[bi]
        for s in range(1, SUBLANES):
            shifted[bi, s - 1] = ubuf[bi, s:s + span, :]
        for r in range(0, tt, rows):
            acc = jnp.zeros((rows, c), F32)
            for k in range(CONV_KSIZE):
                off = k + HALO - CONV_PAD
                lo = r + off - off % SUBLANES
                if off % SUBLANES == 0:
                    tap = ubuf[bi, lo:lo + rows, :]
                else:
                    tap = shifted[bi, off % SUBLANES - 1, lo:lo + rows, :]
                acc = acc + tap * wdw_ref[k:k + 1, :]
            y = acc + bdw_ref[...]
            yc = y - jnp.mean(y, axis=-1, keepdims=True)
            yn = yc * lax.rsqrt(jnp.mean(yc * yc, axis=-1, keepdims=True) + EPS)
            yn = yn * lng_ref[...] + lnb_ref[...]
            sbuf[bi, r:r + rows, :] = _silu(yn).astype(BF16)
        o_ref[bi] = jnp.dot(sbuf[bi], wpw_ref[...], preferred_element_type=F32) + bpw_ref[...]


def _mixers_body(n_ctx_tiles, n_tiles, n_rid, *refs):
    scan_in, conv_in = refs[:13], refs[13:22]
    rider_in = refs[22:22 + n_rid]
    outs = refs[22 + n_rid:]
    scan_out, conv_out, rider_out = outs[:4], outs[4], outs[5:-5]
    scan_scratch, conv_scratch = outs[-5:-3], outs[-3:]
    _run_riders(rider_in, rider_out)
    _scan_body(*scan_in, *scan_out, *scan_scratch)
    _conv_tiles(pl.program_id(1), n_ctx_tiles, n_tiles, *conv_in, conv_out, *conv_scratch)


def _mixers(layer, gla_in, ret_in, conv_in, n_ctx_tiles, log_gamma, cos, sin,
            key_mask, value_mask, cum_f, cum_b, w_dw, b_dw, ln_g, ln_b, w_pw, b_pw, riders=()):
    bsz, t_all, _ = gla_in.shape
    tb = ROW_TILE
    n_tiles = t_all // tb
    dk_all = N_HEADS * SCAN_DK
    dv_all = GROUP_W
    c = GROUP_W
    nb = MIXER_BATCH if bsz % MIXER_BATCH == 0 else 1
    per = tb // HALO
    n_halo = t_all // HALO

    def bwd(j):
        return jnp.where(j < n_ctx_tiles, n_ctx_tiles - 1 - j, n_tiles - 1 - (j - n_ctx_tiles))

    fwd_blk = lambda b, j: (b, j, 0)
    bwd_blk = lambda b, j: (b, bwd(j), 0)
    n_steps = (bsz // nb) * n_tiles
    rid_in, rid_args, rid_out, rid_shapes = _rider_specs(riders, lambda b, j: b * n_tiles + j, n_steps)
    in_specs = [pl.BlockSpec((nb, tb, W_GLA), fwd_blk), pl.BlockSpec((nb, tb, W_GLA), bwd_blk),
                pl.BlockSpec((nb, tb, W_RET), fwd_blk), pl.BlockSpec((nb, tb, W_RET), bwd_blk),
                _const_spec(log_gamma),
                pl.BlockSpec((tb, dk_all), lambda b, j: (j, 0)),
                pl.BlockSpec((tb, dk_all), lambda b, j: (j, 0)),
                pl.BlockSpec((tb, dk_all), lambda b, j: (bwd(j), 0)),
                pl.BlockSpec((tb, dk_all), lambda b, j: (bwd(j), 0)),
                _const_spec(key_mask), _const_spec(value_mask), _const_spec(cum_f), _const_spec(cum_b),
                pl.BlockSpec((nb, tb, c), fwd_blk),
                pl.BlockSpec((nb, HALO, c), lambda b, j: (b, jnp.maximum(j * per - 1, 0), 0)),
                pl.BlockSpec((nb, HALO, c), lambda b, j: (b, jnp.minimum((j + 1) * per, n_halo - 1), 0)),
                _layer_spec(w_dw, layer), _layer_spec(b_dw, layer), _layer_spec(ln_g, layer),
                _layer_spec(ln_b, layer), _layer_spec(w_pw, layer), _layer_spec(b_pw, layer)] + rid_in
    return pl.pallas_call(
        functools.partial(_mixers_body, n_ctx_tiles, n_tiles, len(riders)),
        grid=(bsz // nb, n_tiles),
        in_specs=in_specs,
        out_specs=[pl.BlockSpec((nb, tb, dv_all), fwd_blk), pl.BlockSpec((nb, tb, dv_all), bwd_blk)] * 2
        + [pl.BlockSpec((nb, tb, c), fwd_blk)] + rid_out,
        out_shape=[jax.ShapeDtypeStruct((bsz, t_all, dv_all), F32)] * 5 + rid_shapes,
        scratch_shapes=[pltpu.VMEM((4 * nb, dk_all, dv_all), F32), pltpu.VMEM((nb, 2, tb, dk_all), F32),
                        pltpu.VMEM((nb, tb + 2 * HALO, c), F32),
                        pltpu.VMEM((nb, SUBLANES - 1, tb + 2 * HALO - SUBLANES, c), F32),
                        pltpu.VMEM((nb, tb, c), BF16)],
        compiler_params=_params(("arbitrary", "arbitrary")),
        name="mixers",
    )(gla_in, gla_in, ret_in, ret_in, log_gamma, cos, sin, cos, sin,
      key_mask, value_mask, cum_f, cum_b, conv_in, conv_in, conv_in, w_dw, b_dw, ln_g, ln_b, w_pw, b_pw,
      *rid_args)


def _att_body(n_ctx_tiles, n_tiles, tile0, q_ref, k_ref, vt_ref, o_ref):
    group = N_HEADS // KV_HEADS
    tq = q_ref.shape[2]
    tk = vt_ref.shape[4]
    nq = group * tq

    def attend_head(bi, h, n_blocks):
        q2 = q_ref[bi, h * group:(h + 1) * group].reshape(nq, LANES)

        def scores(i):
            kb = k_ref[bi, i * tk:(i + 1) * tk, :]
            return lax.dot_general(kb, q2, (((1,), (1,)), ((), ())), preferred_element_type=F32)

        m = acc = None
        ahead = [scores(i) for i in range(min(SCORE_LOOKAHEAD, n_blocks))]
        for i in range(n_blocks):
            s = ahead.pop(0)
            if i + SCORE_LOOKAHEAD < n_blocks:
                ahead.append(scores(i + SCORE_LOOKAHEAD))
            blk_max = jnp.max(s, axis=0, keepdims=True)
            m_new = blk_max if i == 0 else jnp.maximum(m, blk_max)
            p = jnp.exp2(s - m_new).astype(BF16)
            pv = jnp.dot(vt_ref[bi, h, i], p, preferred_element_type=F32)
            acc = pv if i == 0 else jnp.exp2(m - m_new) * acc + pv
            m = m_new
        o_t = acc[:HEAD_DV] / acc[HEAD_DV:HEAD_DV + 1]
        stacked = jnp.concatenate([o_t[:, g * tq:(g + 1) * tq] for g in range(group)], axis=0)
        o_ref[bi, :, h * group * HEAD_DV:(h + 1) * group * HEAD_DV] = stacked.T

    def attend(n_blocks):
        for bi in range(q_ref.shape[0]):
            for h in range(KV_HEADS):
                attend_head(bi, h, n_blocks)

    qi = pl.program_id(1) + tile0
    if tile0 < n_ctx_tiles:
        pl.when(qi < n_ctx_tiles)(lambda: attend(n_ctx_tiles))
    pl.when(qi >= n_ctx_tiles)(lambda: attend(n_tiles))


def _attention(q, k, vt, n_ctx_tiles, with_ctx):
    bsz, _, t_all, _ = q.shape
    tq = ROW_TILE
    n_tiles = t_all // tq
    tile0 = 0 if with_ctx else n_ctx_tiles
    nb = ATT_BATCH if bsz % ATT_BATCH == 0 else 1
    return pl.pallas_call(
        functools.partial(_att_body, n_ctx_tiles, n_tiles, tile0),
        grid=(bsz // nb, n_tiles - tile0),
        in_specs=[pl.BlockSpec((nb, N_HEADS, tq, LANES), lambda b, i: (b, 0, i + tile0, 0)),
                  pl.BlockSpec((nb, t_all, LANES), lambda b, i: (b, 0, 0)),
                  pl.BlockSpec((nb, KV_HEADS, n_tiles, VT_ROWS, tq), lambda b, i: (b, 0, 0, 0, 0))],
        out_specs=pl.BlockSpec((nb, tq, GROUP_W), lambda b, i: (b, i, 0)),
        out_shape=jax.ShapeDtypeStruct((bsz, t_all - tile0 * tq, GROUP_W), F32),
        compiler_params=_params(("parallel", "arbitrary")),
        name="attention",
    )(q, k, vt)


def _post_body(final, split, n_ctx_tiles, tile0, n_rid, *refs):
    (yconv, gla_f, gla_b, gla_gate, yatt, ret_f, ret_b, ret_gate, gla_gain, ret_gain, ones_ref) = refs[:11]
    n_src = 2 if split else 1
    lat_ref = refs[11]
    ctx_ref = refs[12] if split else None
    n_in = 18 + n_src
    mod_ref, modc_ref, n2g_ref, wout_ref, wup_ref, wdown_ref, fng_ref = refs[11 + n_src:n_in]
    o_ref = refs[n_in + n_rid]
    xbuf = refs[n_in + 2 * n_rid + 1] if split else None
    _run_riders(refs[n_in:n_in + n_rid], refs[n_in + n_rid + 1:n_in + 2 * n_rid + 1])
    nb, tm, d = lat_ref.shape
    rows = nb * tm
    d_ff = wup_ref.shape[1]
    is_ctx = pl.program_id(1) + tile0 < n_ctx_tiles

    def stacked(ref):
        return ref[...].reshape(rows, ref.shape[2])

    def finish(of_ref, ob_ref, gate_ref, gain_ref):
        o = stacked(of_ref) + stacked(ob_ref)
        ss = _group_sumsq(o, ones_ref[...])
        y = o * lax.rsqrt(ss * (1.0 / HEAD_DV) + EPS) * gain_ref[...]
        return (y * _silu(stacked(gate_ref))).astype(BF16)

    mixed = jnp.concatenate([stacked(yconv).astype(BF16), finish(gla_f, gla_b, gla_gate, gla_gain),
                             stacked(yatt).astype(BF16), finish(ret_f, ret_b, ret_gate, ret_gain)], axis=1)
    o = jnp.dot(mixed, wout_ref[...], preferred_element_type=F32)
    x = _load_rows(split, is_ctx, lat_ref, ctx_ref, xbuf)
    mods = [_mod_row(is_ctx, mod_ref, modc_ref, bi) for bi in range(nb)]
    x_mid, h2 = [], []
    for bi in range(nb):
        gate1 = mods[bi][:, 2 * d:3 * d]
        shift2 = mods[bi][:, 3 * d:4 * d]
        scale2 = mods[bi][:, 4 * d:5 * d]
        x_mid.append(x[bi] + gate1 * o[bi * tm:(bi + 1) * tm])
        h2.append(((_rms(x_mid[bi]) * n2g_ref[...]) * (1.0 + scale2) + shift2).astype(BF16))
    h2 = jnp.concatenate(h2, axis=0)
    acc = jnp.zeros((rows, d), F32)
    tf = 1024
    for f in range(0, d_ff, tf):
        u = jnp.dot(h2, wup_ref[:, f:f + tf], preferred_element_type=F32)
        a = jnp.square(jnp.maximum(u, 0.0)).astype(BF16)
        acc = acc + jnp.dot(a, wdown_ref[f:f + tf, :], preferred_element_type=F32)
    for bi in range(nb):
        gate2 = mods[bi][:, 5 * d:6 * d]
        x_out = x_mid[bi] + gate2 * acc[bi * tm:(bi + 1) * tm]
        if final:
            x_out = _rms(x_out) * fng_ref[...]
        o_ref[bi] = x_out


def _post(layer, final, yconv, gla_of, gla_ob, gla_in, yatt, ret_of, ret_ob, ret_in, gla_gain, ret_gain,
          ones_blockdiag, x_lat, x_ctx, mods, norm2_g, w_layer, w_out, w_up, w_down, final_g, n_ctx_tiles,
          with_ctx, riders=()):
    split = x_ctx is not None
    bsz, _, d = x_lat.shape
    tm = ROW_TILE
    t_all = gla_of.shape[1]
    ctx_row = bsz
    tile0 = 0 if with_ctx else n_ctx_tiles
    n_out = t_all // tm - tile0
    c = GROUP_W
    gate_blk = (2 * N_HEADS * SCAN_DK + c) // c
    full = lambda b, j: (b, j + tile0, 0)
    own = lambda b, j: (b, j, 0)
    assert not split or tile0 == 0
    nb = ROW_BATCH if bsz % ROW_BATCH == 0 else 1
    row_specs, row_args = _row_specs(split, x_lat, x_ctx, nb, tm, n_ctx_tiles, tile0)

    def resident(arr):
        rest = arr.shape[1:]
        return pl.BlockSpec((None,) + rest, lambda *_: (w_layer,) + (0,) * len(rest),
                            pipeline_mode=pl.Buffered(1))

    n_steps = (bsz // nb) * n_out
    rid_in, rid_args, rid_out, rid_shapes = _rider_specs(riders, lambda b, j: b * n_out + j, n_steps)
    return pl.pallas_call(
        functools.partial(_post_body, final, split, n_ctx_tiles, tile0, len(riders)),
        grid=(bsz // nb, n_out),
        in_specs=[pl.BlockSpec((nb, tm, c), full),
                  pl.BlockSpec((nb, tm, c), full), pl.BlockSpec((nb, tm, c), full),
                  pl.BlockSpec((nb, tm, c), lambda b, j: (b, j + tile0, gate_blk)),
                  pl.BlockSpec((nb, tm, c), own),
                  pl.BlockSpec((nb, tm, c), full), pl.BlockSpec((nb, tm, c), full),
                  pl.BlockSpec((nb, tm, c), lambda b, j: (b, j + tile0, gate_blk)),
                  _layer_spec(gla_gain, layer), _layer_spec(ret_gain, layer),
                  _const_spec(ones_blockdiag)] + row_specs + _mod_specs(layer, nb, N_MOD * d, ctx_row) + [
                  _layer_spec(norm2_g, layer),
                  resident(w_out), resident(w_up), resident(w_down),
                  _const_spec(final_g)] + rid_in,
        out_specs=[pl.BlockSpec((nb, tm, d), own)] + rid_out,
        out_shape=[jax.ShapeDtypeStruct((bsz, n_out * tm, d), F32)] + rid_shapes,
        scratch_shapes=[pltpu.VMEM((nb, tm, d), F32)] if split else [],
        compiler_params=_params(("arbitrary", "arbitrary") if riders else ("parallel", "parallel")),
        name="post",
    )(yconv, gla_of, gla_ob, gla_in, yatt, ret_of, ret_ob, ret_in, gla_gain, ret_gain,
      ones_blockdiag, *row_args, mods, mods, norm2_g, w_out, w_up, w_down, final_g, *rid_args)


def _rope_table(t_lat, n_ctx, head_dim, width):
    n_ax = head_dim // 4
    inv = np.float32(ROPE_THETA) ** (-np.arange(n_ax, dtype=np.float32) / np.float32(n_ax))
    pos = np.arange(t_lat)
    ang = np.concatenate([(pos // GRID_W).astype(np.float32)[:, None] * inv,
                          (pos % GRID_W).astype(np.float32)[:, None] * inv], axis=-1).astype(np.float32)
    cos, sin = np.cos(ang), np.sin(ang)
    cos = np.concatenate([np.ones((n_ctx, head_dim // 2), np.float32), cos], axis=0)
    sin = np.concatenate([np.zeros((n_ctx, head_dim // 2), np.float32), sin], axis=0)
    reps = width // head_dim
    return (jnp.asarray(np.tile(np.concatenate([cos, cos], axis=-1), (1, reps)), F32),
            jnp.asarray(np.tile(np.concatenate([-sin, sin], axis=-1), (1, reps)), F32))


def kernel(x, c, ctx, c_ctx, w_mod, b_mod, norm1_g, norm2_g, w_in, conv_w_dw, conv_b_dw, conv_ln_g,
           conv_ln_b, conv_w_pw, conv_b_pw, gla_w_a_f, gla_b_a_f, gla_w_a_b, gla_b_a_b, gla_norm_g,
           att_q_norm_g, att_k_norm_g, ret_norm_g, w_out, w_up, w_down, final_norm_g):
    bsz, t_lat, d = x.shape
    n_ctx = ctx.shape[1]
    depth = w_mod.shape[0]
    assert bsz + 1 <= 8 and n_ctx % ROW_TILE == 0 and t_lat % ROW_TILE == 0
    n_ctx_tiles = n_ctx // ROW_TILE
    dk_all = N_HEADS * SCAN_DK

    cc = jnp.concatenate([c, c_ctx[None, :], jnp.zeros((8 - bsz - 1, d), F32)], axis=0)
    mods = _modulation(cc, w_mod, b_mod)
    mods = mods.reshape(depth, 8, 1, N_MOD * d)

    stack_rows = lambda v: v.reshape(depth, 1, -1)
    w_in_b = w_in.astype(BF16)
    in_w = (w_in_b[:, :, :W_IN_A],
            jnp.pad(w_in_b[:, :, W_IN_A:W_IN_A + GLA_RANK], ((0, 0), (0, 0), (0, LANES - GLA_RANK))),
            w_in_b[:, :, W_IN_A + GLA_RANK:])
    pad_rank = lambda w: jnp.pad(w, ((0, 0), (0, LANES - GLA_RANK), (0, 0))).astype(BF16)
    w_a_f, w_a_b = pad_rank(gla_w_a_f), pad_rank(gla_w_a_b)
    w_pw = conv_w_pw.astype(BF16)
    q_gain = stack_rows(jnp.tile(att_q_norm_g, (1, N_HEADS)))
    k_gain = stack_rows(jnp.tile(att_k_norm_g, (1, KV_HEADS)))

    cos_att, sin_att = _rope_table(t_lat, n_ctx, HEAD_DV, N_HEADS * HEAD_DV)
    cos_ret, sin_ret = _rope_table(t_lat, n_ctx, SCAN_DK, dk_all)
    gamma = 1.0 - np.exp2(-5.0 - np.arange(N_HEADS, dtype=np.float64))
    log_gamma = jnp.asarray(np.repeat(np.log(gamma), SCAN_DK)[None, :], F32)
    gid = np.arange(GROUP_W) // HEAD_DV
    ones_blockdiag = jnp.asarray(gid[:, None] == gid[None, :], BF16)
    key_mask = jnp.asarray(gid[:, None] == (np.arange(dk_all) // SCAN_DK)[None, :], BF16)
    pos = np.arange(ROW_TILE)
    same_chunk = (pos[:, None] // CHUNK) == (pos[None, :] // CHUNK)
    cum_f = jnp.asarray(same_chunk & (pos[None, :] <= pos[:, None]), BF16)
    cum_b = jnp.asarray(same_chunk & (pos[None, :] >= pos[:, None]), BF16)

    x_lat, x_ctx = x, ctx
    for l in range(depth):
        with_ctx = l < depth - 1
        final = l == depth - 1
        conv_in, gla_in, ret_in, q, k, vt = _in_proj(
            l, x_lat, x_ctx, mods, stack_rows(norm1_g), l, *in_w, cos_att, sin_att, q_gain, k_gain,
            ones_blockdiag, w_a_f, stack_rows(gla_b_a_f), w_a_b, stack_rows(gla_b_a_b), n_ctx_tiles)
        mixer_riders = [(w, 0) for w in (w_out, w_up, w_down)] if l == 0 else []
        gla_of, gla_ob, ret_of, ret_ob, y_conv, *converted = _mixers(
            l, gla_in, ret_in, conv_in, n_ctx_tiles, log_gamma, cos_ret, sin_ret, key_mask,
            ones_blockdiag, cum_f, cum_b, conv_w_dw, stack_rows(conv_b_dw), stack_rows(conv_ln_g),
            stack_rows(conv_ln_b), w_pw, stack_rows(conv_b_pw), mixer_riders)
        if l == 0:
            post_w = tuple(w[None] for w in converted)
        y_att = _attention(q, k, vt, n_ctx_tiles, with_ctx)
        post_riders = [] if final else [(w, l + 1) for w in (w_out, w_up, w_down)]
        x_lat, *converted = _post(
            l, final, y_conv, gla_of, gla_ob, gla_in, y_att, ret_of, ret_ob, ret_in,
            stack_rows(gla_norm_g), stack_rows(ret_norm_g), ones_blockdiag, x_lat, x_ctx, mods,
            stack_rows(norm2_g), 0, *post_w, final_norm_g.reshape(1, -1), n_ctx_tiles, with_ctx,
            post_riders)
        if not final:
            post_w = tuple(w[None] for w in converted)
        x_ctx = None
    return x_lat
```

```python
import functools

import numpy as np
import jax
import jax.numpy as jnp
from jax import lax
from jax.experimental import pallas as pl
from jax.experimental.pallas import tpu as pltpu

F32 = jnp.float32
BF16 = jnp.bfloat16

EPS = 1e-6
GRID_W = 64
ROPE_THETA = 10000.0
N_MOD = 6
CONV_KSIZE = 31
CONV_PAD = (CONV_KSIZE - 1) // 2
HALO = 16
GLA_TAU = 16.0
GLA_RANK = 16
CHUNK = 64
N_HEADS = 4
KV_HEADS = 2
HEAD_DV = 64
SCAN_DK = 32
GROUP_W = 256
ROW_TILE = 256
LANES = 128
SUBLANES = 8
BF16_SUBLANES = 16
VT_ROWS = HEAD_DV + BF16_SUBLANES
LOG2_E = 1.4426950408889634
RIDER_CHUNKS = 64
IN_PROJ_BATCH = 4
ROW_BATCH = 2
ATT_BATCH = 2
MIXER_BATCH = 2
SCORE_LOOKAHEAD = 2
V7X_VMEM_BYTES = 64 * 1024 * 1024
VMEM_LIMIT = V7X_VMEM_BYTES * 3 // 4

W_CONV = 2 * GROUP_W
W_GLA_QKVR = 2 * N_HEADS * SCAN_DK + 2 * GROUP_W
W_GLA = W_GLA_QKVR + 2 * N_HEADS * SCAN_DK
W_ATT = GROUP_W + 2 * KV_HEADS * HEAD_DV
W_RET = 2 * N_HEADS * SCAN_DK + 2 * GROUP_W
W_IN_A = W_CONV + W_GLA_QKVR


def _params(semantics):
    return pltpu.CompilerParams(dimension_semantics=semantics, vmem_limit_bytes=VMEM_LIMIT)


def _layer_spec(arr, layer):
    rest = arr.shape[1:]
    return pl.BlockSpec((None,) + rest, lambda *_: (layer,) + (0,) * len(rest))


def _const_spec(arr):
    return pl.BlockSpec(arr.shape, lambda *_: (0,) * arr.ndim)


def _rider_specs(items, step_of, n_steps):
    in_specs, args, out_specs, out_shapes = [], [], [], []
    n_chunks = 1
    while 2 * n_chunks <= min(n_steps, RIDER_CHUNKS):
        n_chunks *= 2
    chunk = lambda *g: jnp.minimum(step_of(*g), n_chunks - 1)
    for arr, layer in items:
        _, rows, cols = arr.shape
        rpc = rows // n_chunks
        assert rows % n_chunks == 0 and rpc % BF16_SUBLANES == 0
        in_specs.append(pl.BlockSpec((None, rpc, cols), lambda *g, layer=layer: (layer, chunk(*g), 0)))
        args.append(arr)
        out_specs.append(pl.BlockSpec((rpc, cols), lambda *g: (chunk(*g), 0)))
        out_shapes.append(jax.ShapeDtypeStruct((rows, cols), BF16))
    return in_specs, args, out_specs, out_shapes


def _run_riders(in_refs, out_refs):
    for src, dst in zip(in_refs, out_refs):
        dst[...] = src[...].astype(BF16)


def _silu(x):
    return x * jax.nn.sigmoid(x)


def _rms(x):
    return x * lax.rsqrt(jnp.mean(x * x, axis=-1, keepdims=True) + EPS)


def _group_sumsq(t, ones_blockdiag):
    t2 = t * t
    hi = t2.astype(BF16)
    lo = (t2 - hi.astype(F32)).astype(BF16)
    return (jnp.dot(hi, ones_blockdiag, preferred_element_type=F32)
            + jnp.dot(lo, ones_blockdiag, preferred_element_type=F32))


def _load_rows(split, is_ctx, lat_ref, ctx_ref, buf):
    if not split:
        return lat_ref[...]

    @pl.when(is_ctx)
    def _():
        buf[...] = ctx_ref[...]

    @pl.when(jnp.logical_not(is_ctx))
    def _():
        buf[...] = lat_ref[...]

    return buf[...]


def _row_specs(split, x_lat, x_ctx, nb, tm, n_ctx_tiles, tile0):
    d = x_lat.shape[2]
    if not split:
        return [pl.BlockSpec((nb, tm, d), lambda b, j: (b, j + tile0, 0))], [x_lat]
    return ([pl.BlockSpec((nb, tm, d), lambda b, j: (b, jnp.maximum(j - n_ctx_tiles, 0), 0)),
             pl.BlockSpec((nb, tm, d), lambda b, j: (b, jnp.minimum(j, n_ctx_tiles - 1), 0))],
            [x_lat, x_ctx])


def _mod_specs(layer, nb, width, ctx_row):
    return [pl.BlockSpec((None, nb, 1, width), lambda b, j: (layer, b, 0, 0)),
            pl.BlockSpec((None, 1, 1, width), lambda b, j: (layer, ctx_row, 0, 0))]


def _mod_row(is_ctx, lat_mod_ref, ctx_mod_ref, bi):
    f = is_ctx.astype(F32)
    return f * ctx_mod_ref[0] + (1.0 - f) * lat_mod_ref[bi]


def _mod_body(cc_ref, w_ref, b_ref, o_ref):
    s = _silu(cc_ref[...])
    o_ref[0] = jnp.dot(s.astype(BF16), w_ref[0].astype(BF16),
                       preferred_element_type=F32) + b_ref[0]


def _modulation(cc, w_mod, b_mod):
    depth, d, n = w_mod.shape
    tn = 1024
    return pl.pallas_call(
        _mod_body,
        grid=(depth, n // tn),
        in_specs=[pl.BlockSpec((8, d), lambda l, i: (0, 0)),
                  pl.BlockSpec((1, d, tn), lambda l, i: (l, 0, i)),
                  pl.BlockSpec((1, 1, tn), lambda l, i: (l, 0, i))],
        out_specs=pl.BlockSpec((1, 8, tn), lambda l, i: (l, 0, i)),
        out_shape=jax.ShapeDtypeStruct((depth, 8, n), F32),
        compiler_params=_params(("parallel", "parallel")),
        name="modulation",
    )(cc, w_mod, b_mod.reshape(depth, 1, n))


def _inproj_body(split, n_ctx_tiles, *refs):
    n_src = 2 if split else 1
    lat_ref = refs[0]
    ctx_ref = refs[1] if split else None
    (mod_ref, modc_ref, g_ref, wa_ref, wz_ref, wb_ref, cos_ref, sin_ref, qg_ref, kg_ref, ones_ref,
     waf_ref, baf_ref, wab_ref, bab_ref,
     conv_ref, gla_ref, ret_ref, q_out, k_out, vt_out) = refs[n_src:n_src + 21]
    xbuf = refs[n_src + 21] if split else None
    nb, tm, d = lat_ref.shape
    is_ctx = pl.program_id(1) < n_ctx_tiles

    x = _load_rows(split, is_ctx, lat_ref, ctx_ref, xbuf)
    wq = N_HEADS * HEAD_DV
    wk = KV_HEADS * HEAD_DV
    dk_all = N_HEADS * SCAN_DK

    def norm_rope(t, ss, gain, w):
        t = t * lax.rsqrt(ss * (1.0 / HEAD_DV) + EPS) * gain
        lane = lax.broadcasted_iota(jnp.int32, t.shape, 1)
        first_half = (lane % HEAD_DV) < HEAD_DV // 2
        rot = jnp.where(first_half, pltpu.roll(t, w - HEAD_DV // 2, 1), pltpu.roll(t, HEAD_DV // 2, 1))
        return t * cos_ref[:, :w] + rot * sin_ref[:, :w]

    low = lax.broadcasted_iota(jnp.int32, (tm, LANES), 1) < HEAD_DV
    ones = jnp.ones((VT_ROWS - HEAD_DV, tm), BF16)
    for bi in range(nb):
        mod = _mod_row(is_ctx, mod_ref, modc_ref, bi)
        shift, scale = mod[:, 0:d], mod[:, d:2 * d]
        hb = ((_rms(x[bi]) * g_ref[...]) * (1.0 + scale) + shift).astype(BF16)

        def proj(w_ref, lo, width):
            return jnp.dot(hb, w_ref[:, lo:lo + width], preferred_element_type=F32)

        att = proj(wb_ref, 0, W_ATT)
        q = att[:, :wq]
        k = att[:, wq:wq + wk]
        v = att[:, wq + wk:]
        z = proj(wz_ref, 0, LANES).astype(BF16)
        ag = proj(wa_ref, 0, W_CONV)
        conv_ref[bi] = ag[:, :GROUP_W] * jax.nn.sigmoid(ag[:, GROUP_W:])
        ss_q = _group_sumsq(q, ones_ref[...])
        ss_k = _group_sumsq(k, ones_ref[:wk, :wk])
        ret_ref[bi] = proj(wb_ref, W_ATT, W_RET)
        for n, (w_ref, bias_ref) in enumerate(((waf_ref, baf_ref), (wab_ref, bab_ref))):
            pre = jnp.dot(z, w_ref[...], preferred_element_type=F32) + bias_ref[...]
            log_sig = jnp.minimum(pre, 0.0) - jnp.log(1.0 + jnp.exp(-jnp.abs(pre)))
            lo = W_GLA_QKVR + n * dk_all
            gla_ref[bi, :, lo:lo + dk_all] = log_sig / GLA_TAU
        gla_ref[bi, :, 0:W_GLA_QKVR] = proj(wa_ref, W_CONV, W_GLA_QKVR)

        qn = norm_rope(q, ss_q, qg_ref[...], wq) * (HEAD_DV ** -0.5 * LOG2_E)
        kn = norm_rope(k, ss_k, kg_ref[...], wk)
        pair0 = qn[:, :LANES]
        pair1 = qn[:, LANES:]
        q_out[bi, 0] = jnp.where(low, pair0, 0.0).astype(BF16)
        q_out[bi, 1] = jnp.where(low, pltpu.roll(pair0, HEAD_DV, 1), 0.0).astype(BF16)
        q_out[bi, 2] = jnp.where(low, 0.0, pltpu.roll(pair1, HEAD_DV, 1)).astype(BF16)
        q_out[bi, 3] = jnp.where(low, 0.0, pair1).astype(BF16)
        k_out[bi] = kn.astype(BF16)
        vt = v.T.astype(BF16)
        for h in range(KV_HEADS):
            vt_out[bi, h, 0, 0:HEAD_DV, :] = vt[h * HEAD_DV:(h + 1) * HEAD_DV, :]
            vt_out[bi, h, 0, HEAD_DV:, :] = ones


def _in_proj(layer, x_lat, x_ctx, mods, norm1_g, w_layer, w_a, w_z, w_b, cos, sin, q_gain, k_gain,
             ones_blockdiag, w_a_f, b_a_f, w_a_b, b_a_b, n_ctx_tiles):
    split = x_ctx is not None
    bsz, _, d = x_lat.shape
    tm = ROW_TILE
    t_all = x_lat.shape[1] + (x_ctx.shape[1] if split else 0)
    n_tiles = t_all // tm
    ctx_row = bsz
    wq = N_HEADS * HEAD_DV
    nb = IN_PROJ_BATCH if bsz % IN_PROJ_BATCH == 0 else 1
    row_specs, row_args = _row_specs(split, x_lat, x_ctx, nb, tm, n_ctx_tiles, 0)
    tile = lambda b, j: (b, j, 0)
    return pl.pallas_call(
        functools.partial(_inproj_body, split, n_ctx_tiles),
        grid=(bsz // nb, n_tiles),
        in_specs=row_specs + _mod_specs(layer, nb, N_MOD * d, ctx_row) + [
            _layer_spec(norm1_g, layer), _layer_spec(w_a, w_layer), _layer_spec(w_z, w_layer),
            _layer_spec(w_b, w_layer),
            pl.BlockSpec((tm, wq), lambda b, j: (j, 0)), pl.BlockSpec((tm, wq), lambda b, j: (j, 0)),
            _layer_spec(q_gain, layer), _layer_spec(k_gain, layer), _const_spec(ones_blockdiag),
            _layer_spec(w_a_f, layer), _layer_spec(b_a_f, layer),
            _layer_spec(w_a_b, layer), _layer_spec(b_a_b, layer)],
        out_specs=[pl.BlockSpec((nb, tm, GROUP_W), tile), pl.BlockSpec((nb, tm, W_GLA), tile),
                   pl.BlockSpec((nb, tm, W_RET), tile),
                   pl.BlockSpec((nb, N_HEADS, tm, LANES), lambda b, j: (b, 0, j, 0)),
                   pl.BlockSpec((nb, tm, LANES), tile),
                   pl.BlockSpec((nb, KV_HEADS, 1, VT_ROWS, tm), lambda b, j: (b, 0, j, 0, 0))],
        out_shape=[jax.ShapeDtypeStruct((bsz, t_all, GROUP_W), F32),
                   jax.ShapeDtypeStruct((bsz, t_all, W_GLA), F32),
                   jax.ShapeDtypeStruct((bsz, t_all, W_RET), F32),
                   jax.ShapeDtypeStruct((bsz, N_HEADS, t_all, LANES), BF16),
                   jax.ShapeDtypeStruct((bsz, t_all, LANES), BF16),
                   jax.ShapeDtypeStruct((bsz, KV_HEADS, n_tiles, VT_ROWS, tm), BF16)],
        scratch_shapes=[pltpu.VMEM((nb, tm, d), F32)] if split else [],
        compiler_params=_params(("parallel", "parallel")),
        name="in_proj",
    )(*row_args, mods, mods, norm1_g, w_a, w_z, w_b, cos, sin, q_gain, k_gain, ones_blockdiag,
      w_a_f, b_a_f, w_a_b, b_a_b)


def _chunk_cumsum(la, tri_ref):
    w = la.shape[1]
    hi = la.astype(BF16)
    lo = (la - hi.astype(F32)).astype(BF16)
    both = jnp.dot(tri_ref[...], jnp.concatenate([hi, lo], axis=1), preferred_element_type=F32)
    return both[:, :w] + both[:, w:]


def _decay_factors(b, reverse):
    b_tot = b[0:1, :] if reverse else b[CHUNK - 1:CHUNK, :]
    dk_all = b.shape[1]
    dec_cols = jnp.broadcast_to(jnp.exp(b_tot), (dk_all, dk_all)).T
    dec_cols = jnp.concatenate([dec_cols] * (GROUP_W // dk_all), axis=1)
    return jnp.exp(b), jnp.exp(-b), jnp.exp(b_tot - b), dec_cols


def _scan_chunk_head(q, k, v, factors, st, key_mask, value_mask, head_mask):
    nt_dims = (((1,), (1,)), ((), ()))
    e_q, e_inv, e_end, dec_cols = factors
    q_dec = (q * e_q).astype(BF16)
    k_inv = (k * e_inv).astype(BF16)
    k_end = (k * e_end).astype(BF16)
    vb = v.astype(BF16)
    k_blk = jnp.concatenate([k_inv] * N_HEADS, axis=0) * key_mask
    scores = lax.dot_general(q_dec, k_blk, nt_dims, preferred_element_type=F32)
    kv = lax.dot_general(k_end, vb, (((0,), (0,)), ((), ())), preferred_element_type=F32)
    o_inter = jnp.dot(q_dec, st.astype(BF16), preferred_element_type=F32)
    st = st * dec_cols + jnp.where(head_mask, kv, 0.0)
    v_blk = jnp.concatenate([vb] * N_HEADS, axis=0) * value_mask
    return (scores, v_blk, o_inter), st


def _scan_chunk_tail(pending, tri):
    scores, v_blk, o_inter = pending
    scores = jnp.where(tri, scores, 0.0).astype(BF16)
    return jnp.dot(scores, v_blk, preferred_element_type=F32) + o_inter


def _scan_body(gf, gb, rf, rb, lg, cosf, sinf, cosb, sinb,
               key_mask_ref, value_mask_ref, cum_f_ref, cum_b_ref,
               gof, gob, rof, rob, st_all, b_all):
    n_batch = gf.shape[0]

    @pl.when(pl.program_id(1) == 0)
    def _():
        st_all[...] = jnp.zeros(st_all.shape, F32)

    tb = gf.shape[1]
    dk_all = N_HEADS * SCAN_DK
    dv_all = GROUP_W
    scale = SCAN_DK ** -0.5
    key_mask = key_mask_ref[...]
    value_mask = value_mask_ref[...]
    r = lax.broadcasted_iota(jnp.int32, (dk_all, dv_all), 0)
    c = lax.broadcasted_iota(jnp.int32, (dk_all, dv_all), 1)
    head_mask = (r // SCAN_DK) == (c // HEAD_DV)
    r = lax.broadcasted_iota(jnp.int32, (CHUNK, dv_all), 0)
    c = lax.broadcasted_iota(jnp.int32, (CHUNK, dv_all), 1) % CHUNK
    tri_f = c <= r
    tri_b = c >= r

    def qkv(ref, bi, lo):
        return (ref[bi, lo:lo + CHUNK, 0:dk_all], ref[bi, lo:lo + CHUNK, dk_all:2 * dk_all],
                ref[bi, lo:lo + CHUNK, 2 * dk_all:2 * dk_all + dv_all])

    def gla_chain(ref, bi, cum_ref, reverse):
        slot = int(reverse)
        lo = W_GLA_QKVR + slot * dk_all
        b_all[bi, slot] = _chunk_cumsum(ref[bi, :, lo:lo + dk_all], cum_ref)

        def load(lo):
            q, k, v = qkv(ref, bi, lo)
            return q * scale, k, v, _decay_factors(b_all[bi, slot, lo:lo + CHUNK, :], reverse)
        return load

    lane = lax.broadcasted_iota(jnp.int32, (CHUNK, dk_all), 1)
    first_half = (lane % SCAN_DK) < SCAN_DK // 2
    pos = lax.broadcasted_iota(jnp.int32, (CHUNK, dk_all), 0).astype(F32)

    ret_factors = [_decay_factors(((CHUNK - pos) if reverse else (pos + 1.0)) * lg[...], reverse)
                   for reverse in (False, True)]

    def ret_chain(ref, bi, cos_ref, sin_ref, reverse):
        def rope(t, lo):
            rot = jnp.where(first_half, pltpu.roll(t, dk_all - SCAN_DK // 2, 1),
                            pltpu.roll(t, SCAN_DK // 2, 1))
            return t * cos_ref[lo:lo + CHUNK, :] + rot * sin_ref[lo:lo + CHUNK, :]

        def load(lo):
            q, k, v = qkv(ref, bi, lo)
            return rope(q, lo), rope(k * scale, lo), v, ret_factors[int(reverse)]
        return load

    chains = []
    for bi in range(n_batch):
        chains += [
            (gla_chain(gf, bi, cum_f_ref, False), False, gof, bi, tri_f),
            (gla_chain(gb, bi, cum_b_ref, True), True, gob, bi, tri_b),
            (ret_chain(rf, bi, cosf, sinf, False), False, rof, bi, tri_f),
            (ret_chain(rb, bi, cosb, sinb, True), True, rob, bi, tri_b),
        ]
    st_vals = [st_all[n] for n in range(len(chains))]
    n_chunks = tb // CHUNK

    def heads(i):
        pending = []
        for n, (load, reverse, _, _, _) in enumerate(chains):
            lo = (n_chunks - 1 - i if reverse else i) * CHUNK
            q, k, v, factors = load(lo)
            part, st_vals[n] = _scan_chunk_head(q, k, v, factors, st_vals[n],
                                                key_mask, value_mask, head_mask)
            pending.append((lo, part))
        return pending

    pending = heads(0)
    for i in range(n_chunks):
        following = heads(i + 1) if i + 1 < n_chunks else None
        for (lo, part), (_, _, out_ref, bi, tri) in zip(pending, chains):
            out_ref[bi, lo:lo + CHUNK, :] = _scan_chunk_tail(part, tri)
        pending = following
    for n, val in enumerate(st_vals):
        st_all[n] = val


def _conv_tiles(j, n_ctx_tiles, n_tiles, main_ref, prev_ref, next_ref, wdw_ref, bdw_ref, lng_ref,
                lnb_ref, wpw_ref, bpw_ref, o_ref, ubuf, shifted, sbuf):
    n_batch, tt = main_ref.shape[0], main_ref.shape[1]
    c = o_ref.shape[2]

    has_prev = jnp.logical_and(j != 0, j != n_ctx_tiles)
    has_next = jnp.logical_and(j != n_ctx_tiles - 1, j != n_tiles - 1)

    @pl.when(has_prev)
    def _():
        ubuf[:, 0:HALO, :] = prev_ref[...]

    @pl.when(jnp.logical_not(has_prev))
    def _():
        ubuf[:, 0:HALO, :] = jnp.zeros((n_batch, HALO, c), F32)

    @pl.when(has_next)
    def _():
        ubuf[:, HALO + tt:, :] = next_ref[...]

    @pl.when(jnp.logical_not(has_next))
    def _():
        ubuf[:, HALO + tt:, :] = jnp.zeros((n_batch, HALO, c), F32)

    span = shifted.shape[2]
    rows = 128
    for bi in range(n_batch):
        ubuf[bi, HALO:HALO + tt, :] = main_ref[bi]
        for s in range(1, SUBLANES):
            shifted[bi, s - 1] = ubuf[bi, s:s + span, :]
        for r in range(0, tt, rows):
            acc = jnp.zeros((rows, c), F32)
            for k in range(CONV_KSIZE):
                off = k + HALO - CONV_PAD
                lo = r + off - off % SUBLANES
                if off % SUBLANES == 0:
                    tap = ubuf[bi, lo:lo + rows, :]
                else:
                    tap = shifted[bi, off % SUBLANES - 1, lo:lo + rows, :]
                acc = acc + tap * wdw_ref[k:k + 1, :]
            y = acc + bdw_ref[...]
            yc = y - jnp.mean(y, axis=-1, keepdims=True)
            yn = yc * lax.rsqrt(jnp.mean(yc * yc, axis=-1, keepdims=True) + EPS)
            yn = yn * lng_ref[...] + lnb_ref[...]
            sbuf[bi, r:r + rows, :] = _silu(yn).astype(BF16)
        o_ref[bi] = jnp.dot(sbuf[bi], wpw_ref[...], preferred_element_type=F32) + bpw_ref[...]


def _mixers_body(n_ctx_tiles, n_tiles, n_rid, *refs):
    scan_in, conv_in = refs[:13], refs[13:22]
    rider_in = refs[22:22 + n_rid]
    outs = refs[22 + n_rid:]
    scan_out, conv_out, rider_out = outs[:4], outs[4], outs[5:-5]
    scan_scratch, conv_scratch = outs[-5:-3], outs[-3:]
    _run_riders(rider_in, rider_out)
    _scan_body(*scan_in, *scan_out, *scan_scratch)
    _conv_tiles(pl.program_id(1), n_ctx_tiles, n_tiles, *conv_in, conv_out, *conv_scratch)


def _mixers(layer, gla_in, ret_in, conv_in, n_ctx_tiles, log_gamma, cos, sin,
            key_mask, value_mask, cum_f, cum_b, w_dw, b_dw, ln_g, ln_b, w_pw, b_pw, riders=()):
    bsz, t_all, _ = gla_in.shape
    tb = ROW_TILE
    n_tiles = t_all // tb
    dk_all = N_HEADS * SCAN_DK
    dv_all = GROUP_W
    c = GROUP_W
    nb = MIXER_BATCH if bsz % MIXER_BATCH == 0 else 1
    per = tb // HALO
    n_halo = t_all // HALO

    def bwd(j):
        return jnp.where(j < n_ctx_tiles, n_ctx_tiles - 1 - j, n_tiles - 1 - (j - n_ctx_tiles))

    fwd_blk = lambda b, j: (b, j, 0)
    bwd_blk = lambda b, j: (b, bwd(j), 0)
    n_steps = (bsz // nb) * n_tiles
    rid_in, rid_args, rid_out, rid_shapes = _rider_specs(riders, lambda b, j: b * n_tiles + j, n_steps)
    in_specs = [pl.BlockSpec((nb, tb, W_GLA), fwd_blk), pl.BlockSpec((nb, tb, W_GLA), bwd_blk),
                pl.BlockSpec((nb, tb, W_RET), fwd_blk), pl.BlockSpec((nb, tb, W_RET), bwd_blk),
                _const_spec(log_gamma),
                pl.BlockSpec((tb, dk_all), lambda b, j: (j, 0)),
                pl.BlockSpec((tb, dk_all), lambda b, j: (j, 0)),
                pl.BlockSpec((tb, dk_all), lambda b, j: (bwd(j), 0)),
                pl.BlockSpec((tb, dk_all), lambda b, j: (bwd(j), 0)),
                _const_spec(key_mask), _const_spec(value_mask), _const_spec(cum_f), _const_spec(cum_b),
                pl.BlockSpec((nb, tb, c), fwd_blk),
                pl.BlockSpec((nb, HALO, c), lambda b, j: (b, jnp.maximum(j * per - 1, 0), 0)),
                pl.BlockSpec((nb, HALO, c), lambda b, j: (b, jnp.minimum((j + 1) * per, n_halo - 1), 0)),
                _layer_spec(w_dw, layer), _layer_spec(b_dw, layer), _layer_spec(ln_g, layer),
                _layer_spec(ln_b, layer), _layer_spec(w_pw, layer), _layer_spec(b_pw, layer)] + rid_in
    return pl.pallas_call(
        functools.partial(_mixers_body, n_ctx_tiles, n_tiles, len(riders)),
        grid=(bsz // nb, n_tiles),
        in_specs=in_specs,
        out_specs=[pl.BlockSpec((nb, tb, dv_all), fwd_blk), pl.BlockSpec((nb, tb, dv_all), bwd_blk)] * 2
        + [pl.BlockSpec((nb, tb, c), fwd_blk)] + rid_out,
        out_shape=[jax.ShapeDtypeStruct((bsz, t_all, dv_all), F32)] * 5 + rid_shapes,
        scratch_shapes=[pltpu.VMEM((4 * nb, dk_all, dv_all), F32), pltpu.VMEM((nb, 2, tb, dk_all), F32),
                        pltpu.VMEM((nb, tb + 2 * HALO, c), F32),
                        pltpu.VMEM((nb, SUBLANES - 1, tb + 2 * HALO - SUBLANES, c), F32),
                        pltpu.VMEM((nb, tb, c), BF16)],
        compiler_params=_params(("arbitrary", "arbitrary")),
        name="mixers",
    )(gla_in, gla_in, ret_in, ret_in, log_gamma, cos, sin, cos, sin,
      key_mask, value_mask, cum_f, cum_b, conv_in, conv_in, conv_in, w_dw, b_dw, ln_g, ln_b, w_pw, b_pw,
      *rid_args)


def _att_body(n_ctx_tiles, n_tiles, tile0, q_ref, k_ref, vt_ref, o_ref):
    group = N_HEADS // KV_HEADS
    tq = q_ref.shape[2]
    tk = vt_ref.shape[4]
    nq = group * tq

    def attend_head(bi, h, n_blocks):
        q2 = q_ref[bi, h * group:(h + 1) * group].reshape(nq, LANES)

        def scores(i):
            kb = k_ref[bi, i * tk:(i + 1) * tk, :]
            return lax.dot_general(kb, q2, (((1,), (1,)), ((), ())), preferred_element_type=F32)

        m = acc = None
        ahead = [scores(i) for i in range(min(SCORE_LOOKAHEAD, n_blocks))]
        for i in range(n_blocks):
            s = ahead.pop(0)
            if i + SCORE_LOOKAHEAD < n_blocks:
                ahead.append(scores(i + SCORE_LOOKAHEAD))
            blk_max = jnp.max(s, axis=0, keepdims=True)
            m_new = blk_max if i == 0 else jnp.maximum(m, blk_max)
            p = jnp.exp2(s - m_new).astype(BF16)
            pv = jnp.dot(vt_ref[bi, h, i], p, preferred_element_type=F32)
            acc = pv if i == 0 else jnp.exp2(m - m_new) * acc + pv
            m = m_new
        o_t = acc[:HEAD_DV] / acc[HEAD_DV:HEAD_DV + 1]
        stacked = jnp.concatenate([o_t[:, g * tq:(g + 1) * tq] for g in range(group)], axis=0)
        o_ref[bi, :, h * group * HEAD_DV:(h + 1) * group * HEAD_DV] = stacked.T

    def attend(n_blocks):
        for bi in range(q_ref.shape[0]):
            for h in range(KV_HEADS):
                attend_head(bi, h, n_blocks)

    qi = pl.program_id(1) + tile0
    if tile0 < n_ctx_tiles:
        pl.when(qi < n_ctx_tiles)(lambda: attend(n_ctx_tiles))
    pl.when(qi >= n_ctx_tiles)(lambda: attend(n_tiles))


def _attention(q, k, vt, n_ctx_tiles, with_ctx):
    bsz, _, t_all, _ = q.shape
    tq = ROW_TILE
    n_tiles = t_all // tq
    tile0 = 0 if with_ctx else n_ctx_tiles
    nb = ATT_BATCH if bsz % ATT_BATCH == 0 else 1
    return pl.pallas_call(
        functools.partial(_att_body, n_ctx_tiles, n_tiles, tile0),
        grid=(bsz // nb, n_tiles - tile0),
        in_specs=[pl.BlockSpec((nb, N_HEADS, tq, LANES), lambda b, i: (b, 0, i + tile0, 0)),
                  pl.BlockSpec((nb, t_all, LANES), lambda b, i: (b, 0, 0)),
                  pl.BlockSpec((nb, KV_HEADS, n_tiles, VT_ROWS, tq), lambda b, i: (b, 0, 0, 0, 0))],
        out_specs=pl.BlockSpec((nb, tq, GROUP_W), lambda b, i: (b, i, 0)),
        out_shape=jax.ShapeDtypeStruct((bsz, t_all - tile0 * tq, GROUP_W), F32),
        compiler_params=_params(("parallel", "arbitrary")),
        name="attention",
    )(q, k, vt)


def _post_body(final, split, n_ctx_tiles, tile0, n_rid, *refs):
    (yconv, gla_f, gla_b, gla_gate, yatt, ret_f, ret_b, ret_gate, gla_gain, ret_gain, ones_ref) = refs[:11]
    n_src = 2 if split else 1
    lat_ref = refs[11]
    ctx_ref = refs[12] if split else None
    n_in = 18 + n_src
    mod_ref, modc_ref, n2g_ref, wout_ref, wup_ref, wdown_ref, fng_ref = refs[11 + n_src:n_in]
    o_ref = refs[n_in + n_rid]
    xbuf = refs[n_in + 2 * n_rid + 1] if split else None
    _run_riders(refs[n_in:n_in + n_rid], refs[n_in + n_rid + 1:n_in + 2 * n_rid + 1])
    nb, tm, d = lat_ref.shape
    rows = nb * tm
    d_ff = wup_ref.shape[1]
    is_ctx = pl.program_id(1) + tile0 < n_ctx_tiles

    def stacked(ref):
        return ref[...].reshape(rows, ref.shape[2])

    def finish(of_ref, ob_ref, gate_ref, gain_ref):
        o = stacked(of_ref) + stacked(ob_ref)
        ss = _group_sumsq(o, ones_ref[...])
        y = o * lax.rsqrt(ss * (1.0 / HEAD_DV) + EPS) * gain_ref[...]
        return (y * _silu(stacked(gate_ref))).astype(BF16)

    mixed = jnp.concatenate([stacked(yconv).astype(BF16), finish(gla_f, gla_b, gla_gate, gla_gain),
                             stacked(yatt).astype(BF16), finish(ret_f, ret_b, ret_gate, ret_gain)], axis=1)
    o = jnp.dot(mixed, wout_ref[...], preferred_element_type=F32)
    x = _load_rows(split, is_ctx, lat_ref, ctx_ref, xbuf)
    mods = [_mod_row(is_ctx, mod_ref, modc_ref, bi) for bi in range(nb)]
    x_mid, h2 = [], []
    for bi in range(nb):
        gate1 = mods[bi][:, 2 * d:3 * d]
        shift2 = mods[bi][:, 3 * d:4 * d]
        scale2 = mods[bi][:, 4 * d:5 * d]
        x_mid.append(x[bi] + gate1 * o[bi * tm:(bi + 1) * tm])
        h2.append(((_rms(x_mid[bi]) * n2g_ref[...]) * (1.0 + scale2) + shift2).astype(BF16))
    h2 = jnp.concatenate(h2, axis=0)
    acc = jnp.zeros((rows, d), F32)
    tf = 1024
    for f in range(0, d_ff, tf):
        u = jnp.dot(h2, wup_ref[:, f:f + tf], preferred_element_type=F32)
        a = jnp.square(jnp.maximum(u, 0.0)).astype(BF16)
        acc = acc + jnp.dot(a, wdown_ref[f:f + tf, :], preferred_element_type=F32)
    for bi in range(nb):
        gate2 = mods[bi][:, 5 * d:6 * d]
        x_out = x_mid[bi] + gate2 * acc[bi * tm:(bi + 1) * tm]
        if final:
            x_out = _rms(x_out) * fng_ref[...]
        o_ref[bi] = x_out


def _post(layer, final, yconv, gla_of, gla_ob, gla_in, yatt, ret_of, ret_ob, ret_in, gla_gain, ret_gain,
          ones_blockdiag, x_lat, x_ctx, mods, norm2_g, w_layer, w_out, w_up, w_down, final_g, n_ctx_tiles,
          with_ctx, riders=()):
    split = x_ctx is not None
    bsz, _, d = x_lat.shape
    tm = ROW_TILE
    t_all = gla_of.shape[1]
    ctx_row = bsz
    tile0 = 0 if with_ctx else n_ctx_tiles
    n_out = t_all // tm - tile0
    c = GROUP_W
    gate_blk = (2 * N_HEADS * SCAN_DK + c) // c
    full = lambda b, j: (b, j + tile0, 0)
    own = lambda b, j: (b, j, 0)
    assert not split or tile0 == 0
    nb = ROW_BATCH if bsz % ROW_BATCH == 0 else 1
    row_specs, row_args = _row_specs(split, x_lat, x_ctx, nb, tm, n_ctx_tiles, tile0)

    def resident(arr):
        rest = arr.shape[1:]
        return pl.BlockSpec((None,) + rest, lambda *_: (w_layer,) + (0,) * len(rest),
                            pipeline_mode=pl.Buffered(1))

    n_steps = (bsz // nb) * n_out
    rid_in, rid_args, rid_out, rid_shapes = _rider_specs(riders, lambda b, j: b * n_out + j, n_steps)
    return pl.pallas_call(
        functools.partial(_post_body, final, split, n_ctx_tiles, tile0, len(riders)),
        grid=(bsz // nb, n_out),
        in_specs=[pl.BlockSpec((nb, tm, c), full),
                  pl.BlockSpec((nb, tm, c), full), pl.BlockSpec((nb, tm, c), full),
                  pl.BlockSpec((nb, tm, c), lambda b, j: (b, j + tile0, gate_blk)),
                  pl.BlockSpec((nb, tm, c), own),
                  pl.BlockSpec((nb, tm, c), full), pl.BlockSpec((nb, tm, c), full),
                  pl.BlockSpec((nb, tm, c), lambda b, j: (b, j + tile0, gate_blk)),
                  _layer_spec(gla_gain, layer), _layer_spec(ret_gain, layer),
                  _const_spec(ones_blockdiag)] + row_specs + _mod_specs(layer, nb, N_MOD * d, ctx_row) + [
                  _layer_spec(norm2_g, layer),
                  resident(w_out), resident(w_up), resident(w_down),
                  _const_spec(final_g)] + rid_in,
        out_specs=[pl.BlockSpec((nb, tm, d), own)] + rid_out,
        out_shape=[jax.ShapeDtypeStruct((bsz, n_out * tm, d), F32)] + rid_shapes,
        scratch_shapes=[pltpu.VMEM((nb, tm, d), F32)] if split else [],
        compiler_params=_params(("arbitrary", "arbitrary") if riders else ("parallel", "parallel")),
        name="post",
    )(yconv, gla_of, gla_ob, gla_in, yatt, ret_of, ret_ob, ret_in, gla_gain, ret_gain,
      ones_blockdiag, *row_args, mods, mods, norm2_g, w_out, w_up, w_down, final_g, *rid_args)


def _rope_table(t_lat, n_ctx, head_dim, width):
    n_ax = head_dim // 4
    inv = np.float32(ROPE_THETA) ** (-np.arange(n_ax, dtype=np.float32) / np.float32(n_ax))
    pos = np.arange(t_lat)
    ang = np.concatenate([(pos // GRID_W).astype(np.float32)[:, None] * inv,
                          (pos % GRID_W).astype(np.float32)[:, None] * inv], axis=-1).astype(np.float32)
    cos, sin = np.cos(ang), np.sin(ang)
    cos = np.concatenate([np.ones((n_ctx, head_dim // 2), np.float32), cos], axis=0)
    sin = np.concatenate([np.zeros((n_ctx, head_dim // 2), np.float32), sin], axis=0)
    reps = width // head_dim
    return (jnp.asarray(np.tile(np.concatenate([cos, cos], axis=-1), (1, reps)), F32),
            jnp.asarray(np.tile(np.concatenate([-sin, sin], axis=-1), (1, reps)), F32))


def kernel(x, c, ctx, c_ctx, w_mod, b_mod, norm1_g, norm2_g, w_in, conv_w_dw, conv_b_dw, conv_ln_g,
           conv_ln_b, conv_w_pw, conv_b_pw, gla_w_a_f, gla_b_a_f, gla_w_a_b, gla_b_a_b, gla_norm_g,
           att_q_norm_g, att_k_norm_g, ret_norm_g, w_out, w_up, w_down, final_norm_g):
    bsz, t_lat, d = x.shape
    n_ctx = ctx.shape[1]
    depth = w_mod.shape[0]
    assert bsz + 1 <= 8 and n_ctx % ROW_TILE == 0 and t_lat % ROW_TILE == 0
    n_ctx_tiles = n_ctx // ROW_TILE
    dk_all = N_HEADS * SCAN_DK

    cc = jnp.concatenate([c, c_ctx[None, :], jnp.zeros((8 - bsz - 1, d), F32)], axis=0)
    mods = _modulation(cc, w_mod, b_mod)
    mods = mods.reshape(depth, 8, 1, N_MOD * d)

    stack_rows = lambda v: v.reshape(depth, 1, -1)
    w_in_b = w_in.astype(BF16)
    in_w = (w_in_b[:, :, :W_IN_A],
            jnp.pad(w_in_b[:, :, W_IN_A:W_IN_A + GLA_RANK], ((0, 0), (0, 0), (0, LANES - GLA_RANK))),
            w_in_b[:, :, W_IN_A + GLA_RANK:])
    pad_rank = lambda w: jnp.pad(w, ((0, 0), (0, LANES - GLA_RANK), (0, 0))).astype(BF16)
    w_a_f, w_a_b = pad_rank(gla_w_a_f), pad_rank(gla_w_a_b)
    w_pw = conv_w_pw.astype(BF16)
    q_gain = stack_rows(jnp.tile(att_q_norm_g, (1, N_HEADS)))
    k_gain = stack_rows(jnp.tile(att_k_norm_g, (1, KV_HEADS)))

    cos_att, sin_att = _rope_table(t_lat, n_ctx, HEAD_DV, N_HEADS * HEAD_DV)
    cos_ret, sin_ret = _rope_table(t_lat, n_ctx, SCAN_DK, dk_all)
    gamma = 1.0 - np.exp2(-5.0 - np.arange(N_HEADS, dtype=np.float64))
    log_gamma = jnp.asarray(np.repeat(np.log(gamma), SCAN_DK)[None, :], F32)
    gid = np.arange(GROUP_W) // HEAD_DV
    ones_blockdiag = jnp.asarray(gid[:, None] == gid[None, :], BF16)
    key_mask = jnp.asarray(gid[:, None] == (np.arange(dk_all) // SCAN_DK)[None, :], BF16)
    pos = np.arange(ROW_TILE)
    same_chunk = (pos[:, None] // CHUNK) == (pos[None, :] // CHUNK)
    cum_f = jnp.asarray(same_chunk & (pos[None, :] <= pos[:, None]), BF16)
    cum_b = jnp.asarray(same_chunk & (pos[None, :] >= pos[:, None]), BF16)

    x_lat, x_ctx = x, ctx
    for l in range(depth):
        with_ctx = l < depth - 1
        final = l == depth - 1
        conv_in, gla_in, ret_in, q, k, vt = _in_proj(
            l, x_lat, x_ctx, mods, stack_rows(norm1_g), l, *in_w, cos_att, sin_att, q_gain, k_gain,
            ones_blockdiag, w_a_f, stack_rows(gla_b_a_f), w_a_b, stack_rows(gla_b_a_b), n_ctx_tiles)
        mixer_riders = [(w, 0) for w in (w_out, w_up, w_down)] if l == 0 else []
        gla_of, gla_ob, ret_of, ret_ob, y_conv, *converted = _mixers(
            l, gla_in, ret_in, conv_in, n_ctx_tiles, log_gamma, cos_ret, sin_ret, key_mask,
            ones_blockdiag, cum_f, cum_b, conv_w_dw, stack_rows(conv_b_dw), stack_rows(conv_ln_g),
            stack_rows(conv_ln_b), w_pw, stack_rows(conv_b_pw), mixer_riders)
        if l == 0:
            post_w = tuple(w[None] for w in converted)
        y_att = _attention(q, k, vt, n_ctx_tiles, with_ctx)
        post_riders = [] if final else [(w, l + 1) for w in (w_out, w_up, w_down)]
        x_lat, *converted = _post(
            l, final, y_conv, gla_of, gla_ob, gla_in, y_att, ret_of, ret_ob, ret_in,
            stack_rows(gla_norm_g), stack_rows(ret_norm_g), ones_blockdiag, x_lat, x_ctx, mods,
            stack_rows(norm2_g), 0, *post_w, final_norm_g.reshape(1, -1), n_ctx_tiles, with_ctx,
            post_riders)
        if not final:
            post_w = tuple(w[None] for w in converted)
        x_ctx = None
    return x_lat
```

```python
import functools

import numpy as np
import jax
import jax.numpy as jnp
from jax import lax
from jax.experimental import pallas as pl
from jax.experimental.pallas import tpu as pltpu

F32 = jnp.float32
BF16 = jnp.bfloat16

EPS = 1e-6
GRID_W = 64
ROPE_THETA = 10000.0
N_MOD = 6
CONV_KSIZE = 31
CONV_PAD = (CONV_KSIZE - 1) // 2
HALO = 16
GLA_TAU = 16.0
GLA_RANK = 16
CHUNK = 64
N_HEADS = 4
KV_HEADS = 2
HEAD_DV = 64
SCAN_DK = 32
GROUP_W = 256
ROW_TILE = 256
LANES = 128
SUBLANES = 8
BF16_SUBLANES = 16
VT_ROWS = HEAD_DV + BF16_SUBLANES
LOG2_E = 1.4426950408889634
RIDER_CHUNKS = 64
IN_PROJ_BATCH = 4
ROW_BATCH = 2
ATT_BATCH = 4
MIXER_BATCH = 2
SCORE_LOOKAHEAD = 2
V7X_VMEM_BYTES = 64 * 1024 * 1024
VMEM_LIMIT = V7X_VMEM_BYTES * 3 // 4

W_CONV = 2 * GROUP_W
W_GLA_QKVR = 2 * N_HEADS * SCAN_DK + 2 * GROUP_W
W_GLA = W_GLA_QKVR + 2 * N_HEADS * SCAN_DK
W_ATT = GROUP_W + 2 * KV_HEADS * HEAD_DV
W_RET = 2 * N_HEADS * SCAN_DK + 2 * GROUP_W
W_IN_A = W_CONV + W_GLA_QKVR


def _params(semantics):
    return pltpu.CompilerParams(dimension_semantics=semantics, vmem_limit_bytes=VMEM_LIMIT)


def _layer_spec(arr, layer):
    rest = arr.shape[1:]
    return pl.BlockSpec((None,) + rest, lambda *_: (layer,) + (0,) * len(rest))


def _const_spec(arr):
    return pl.BlockSpec(arr.shape, lambda *_: (0,) * arr.ndim)


def _rider_specs(items, step_of, n_steps):
    in_specs, args, out_specs, out_shapes = [], [], [], []
    n_chunks = 1
    while 2 * n_chunks <= min(n_steps, RIDER_CHUNKS):
        n_chunks *= 2
    chunk = lambda *g: jnp.minimum(step_of(*g), n_chunks - 1)
    for arr, layer in items:
        _, rows, cols = arr.shape
        rpc = rows // n_chunks
        assert rows % n_chunks == 0 and rpc % BF16_SUBLANES == 0
        in_specs.append(pl.BlockSpec((None, rpc, cols), lambda *g, layer=layer: (layer, chunk(*g), 0)))
        args.append(arr)
        out_specs.append(pl.BlockSpec((rpc, cols), lambda *g: (chunk(*g), 0)))
        out_shapes.append(jax.ShapeDtypeStruct((rows, cols), BF16))
    return in_specs, args, out_specs, out_shapes


def _run_riders(in_refs, out_refs):
    for src, dst in zip(in_refs, out_refs):
        dst[...] = src[...].astype(BF16)


def _silu(x):
    return x * jax.nn.sigmoid(x)


def _rms(x):
    return x * lax.rsqrt(jnp.mean(x * x, axis=-1, keepdims=True) + EPS)


def _group_sumsq(t, ones_blockdiag):
    t2 = t * t
    hi = t2.astype(BF16)
    lo = (t2 - hi.astype(F32)).astype(BF16)
    return (jnp.dot(hi, ones_blockdiag, preferred_element_type=F32)
            + jnp.dot(lo, ones_blockdiag, preferred_element_type=F32))


def _load_rows(split, is_ctx, lat_ref, ctx_ref, buf):
    if not split:
        return lat_ref[...]

    @pl.when(is_ctx)
    def _():
        buf[...] = ctx_ref[...]

    @pl.when(jnp.logical_not(is_ctx))
    def _():
        buf[...] = lat_ref[...]

    return buf[...]


def _row_specs(split, x_lat, x_ctx, nb, tm, n_ctx_tiles, tile0):
    d = x_lat.shape[2]
    if not split:
        return [pl.BlockSpec((nb, tm, d), lambda b, j: (b, j + tile0, 0))], [x_lat]
    return ([pl.BlockSpec((nb, tm, d), lambda b, j: (b, jnp.maximum(j - n_ctx_tiles, 0), 0)),
             pl.BlockSpec((nb, tm, d), lambda b, j: (b, jnp.minimum(j, n_ctx_tiles - 1), 0))],
            [x_lat, x_ctx])


def _mod_specs(layer, nb, width, ctx_row):
    return [pl.BlockSpec((None, nb, 1, width), lambda b, j: (layer, b, 0, 0)),
            pl.BlockSpec((None, 1, 1, width), lambda b, j: (layer, ctx_row, 0, 0))]


def _mod_row(is_ctx, lat_mod_ref, ctx_mod_ref, bi):
    f = is_ctx.astype(F32)
    return f * ctx_mod_ref[0] + (1.0 - f) * lat_mod_ref[bi]


def _mod_body(cc_ref, w_ref, b_ref, o_ref):
    s = _silu(cc_ref[...])
    o_ref[0] = jnp.dot(s.astype(BF16), w_ref[0].astype(BF16),
                       preferred_element_type=F32) + b_ref[0]


def _modulation(cc, w_mod, b_mod):
    depth, d, n = w_mod.shape
    tn = 1024
    return pl.pallas_call(
        _mod_body,
        grid=(depth, n // tn),
        in_specs=[pl.BlockSpec((8, d), lambda l, i: (0, 0)),
                  pl.BlockSpec((1, d, tn), lambda l, i: (l, 0, i)),
                  pl.BlockSpec((1, 1, tn), lambda l, i: (l, 0, i))],
        out_specs=pl.BlockSpec((1, 8, tn), lambda l, i: (l, 0, i)),
        out_shape=jax.ShapeDtypeStruct((depth, 8, n), F32),
        compiler_params=_params(("parallel", "parallel")),
        name="modulation",
    )(cc, w_mod, b_mod.reshape(depth, 1, n))


def _inproj_body(split, n_ctx_tiles, *refs):
    n_src = 2 if split else 1
    lat_ref = refs[0]
    ctx_ref = refs[1] if split else None
    (mod_ref, modc_ref, g_ref, wa_ref, wz_ref, wb_ref, cos_ref, sin_ref, qg_ref, kg_ref, ones_ref,
     waf_ref, baf_ref, wab_ref, bab_ref,
     conv_ref, gla_ref, ret_ref, q_out, k_out, vt_out) = refs[n_src:n_src + 21]
    xbuf = refs[n_src + 21] if split else None
    nb, tm, d = lat_ref.shape
    is_ctx = pl.program_id(1) < n_ctx_tiles

    x = _load_rows(split, is_ctx, lat_ref, ctx_ref, xbuf)
    wq = N_HEADS * HEAD_DV
    wk = KV_HEADS * HEAD_DV
    dk_all = N_HEADS * SCAN_DK

    def norm_rope(t, ss, gain, w):
        t = t * lax.rsqrt(ss * (1.0 / HEAD_DV) + EPS) * gain
        lane = lax.broadcasted_iota(jnp.int32, t.shape, 1)
        first_half = (lane % HEAD_DV) < HEAD_DV // 2
        rot = jnp.where(first_half, pltpu.roll(t, w - HEAD_DV // 2, 1), pltpu.roll(t, HEAD_DV // 2, 1))
        return t * cos_ref[:, :w] + rot * sin_ref[:, :w]

    low = lax.broadcasted_iota(jnp.int32, (tm, LANES), 1) < HEAD_DV
    ones = jnp.ones((VT_ROWS - HEAD_DV, tm), BF16)
    for bi in range(nb):
        mod = _mod_row(is_ctx, mod_ref, modc_ref, bi)
        shift, scale = mod[:, 0:d], mod[:, d:2 * d]
        hb = ((_rms(x[bi]) * g_ref[...]) * (1.0 + scale) + shift).astype(BF16)

        def proj(w_ref, lo, width):
            return jnp.dot(hb, w_ref[:, lo:lo + width], preferred_element_type=F32)

        att = proj(wb_ref, 0, W_ATT)
        q = att[:, :wq]
        k = att[:, wq:wq + wk]
        v = att[:, wq + wk:]
        z = proj(wz_ref, 0, LANES).astype(BF16)
        ag = proj(wa_ref, 0, W_CONV)
        conv_ref[bi] = ag[:, :GROUP_W] * jax.nn.sigmoid(ag[:, GROUP_W:])
        ss_q = _group_sumsq(q, ones_ref[...])
        ss_k = _group_sumsq(k, ones_ref[:wk, :wk])
        ret_ref[bi] = proj(wb_ref, W_ATT, W_RET)
        for n, (w_ref, bias_ref) in enumerate(((waf_ref, baf_ref), (wab_ref, bab_ref))):
            pre = jnp.dot(z, w_ref[...], preferred_element_type=F32) + bias_ref[...]
            log_sig = jnp.minimum(pre, 0.0) - jnp.log(1.0 + jnp.exp(-jnp.abs(pre)))
            lo = W_GLA_QKVR + n * dk_all
            gla_ref[bi, :, lo:lo + dk_all] = log_sig / GLA_TAU
        gla_ref[bi, :, 0:W_GLA_QKVR] = proj(wa_ref, W_CONV, W_GLA_QKVR)

        qn = norm_rope(q, ss_q, qg_ref[...], wq) * (HEAD_DV ** -0.5 * LOG2_E)
        kn = norm_rope(k, ss_k, kg_ref[...], wk)
        pair0 = qn[:, :LANES]
        pair1 = qn[:, LANES:]
        q_out[bi, 0] = jnp.where(low, pair0, 0.0).astype(BF16)
        q_out[bi, 1] = jnp.where(low, pltpu.roll(pair0, HEAD_DV, 1), 0.0).astype(BF16)
        q_out[bi, 2] = jnp.where(low, 0.0, pltpu.roll(pair1, HEAD_DV, 1)).astype(BF16)
        q_out[bi, 3] = jnp.where(low, 0.0, pair1).astype(BF16)
        k_out[bi] = kn.astype(BF16)
        vt = v.T.astype(BF16)
        for h in range(KV_HEADS):
            vt_out[bi, h, 0, 0:HEAD_DV, :] = vt[h * HEAD_DV:(h + 1) * HEAD_DV, :]
            vt_out[bi, h, 0, HEAD_DV:, :] = ones


def _in_proj(layer, x_lat, x_ctx, mods, norm1_g, w_layer, w_a, w_z, w_b, cos, sin, q_gain, k_gain,
             ones_blockdiag, w_a_f, b_a_f, w_a_b, b_a_b, n_ctx_tiles):
    split = x_ctx is not None
    bsz, _, d = x_lat.shape
    tm = ROW_TILE
    t_all = x_lat.shape[1] + (x_ctx.shape[1] if split else 0)
    n_tiles = t_all // tm
    ctx_row = bsz
    wq = N_HEADS * HEAD_DV
    nb = IN_PROJ_BATCH if bsz % IN_PROJ_BATCH == 0 else 1
    row_specs, row_args = _row_specs(split, x_lat, x_ctx, nb, tm, n_ctx_tiles, 0)
    tile = lambda b, j: (b, j, 0)
    return pl.pallas_call(
        functools.partial(_inproj_body, split, n_ctx_tiles),
        grid=(bsz // nb, n_tiles),
        in_specs=row_specs + _mod_specs(layer, nb, N_MOD * d, ctx_row) + [
            _layer_spec(norm1_g, layer), _layer_spec(w_a, w_layer), _layer_spec(w_z, w_layer),
            _layer_spec(w_b, w_layer),
            pl.BlockSpec((tm, wq), lambda b, j: (j, 0)), pl.BlockSpec((tm, wq), lambda b, j: (j, 0)),
            _layer_spec(q_gain, layer), _layer_spec(k_gain, layer), _const_spec(ones_blockdiag),
            _layer_spec(w_a_f, layer), _layer_spec(b_a_f, layer),
            _layer_spec(w_a_b, layer), _layer_spec(b_a_b, layer)],
        out_specs=[pl.BlockSpec((nb, tm, GROUP_W), tile), pl.BlockSpec((nb, tm, W_GLA), tile),
                   pl.BlockSpec((nb, tm, W_RET), tile),
                   pl.BlockSpec((nb, N_HEADS, tm, LANES), lambda b, j: (b, 0, j, 0)),
                   pl.BlockSpec((nb, tm, LANES), tile),
                   pl.BlockSpec((nb, KV_HEADS, 1, VT_ROWS, tm), lambda b, j: (b, 0, j, 0, 0))],
        out_shape=[jax.ShapeDtypeStruct((bsz, t_all, GROUP_W), F32),
                   jax.ShapeDtypeStruct((bsz, t_all, W_GLA), F32),
                   jax.ShapeDtypeStruct((bsz, t_all, W_RET), F32),
                   jax.ShapeDtypeStruct((bsz, N_HEADS, t_all, LANES), BF16),
                   jax.ShapeDtypeStruct((bsz, t_all, LANES), BF16),
                   jax.ShapeDtypeStruct((bsz, KV_HEADS, n_tiles, VT_ROWS, tm), BF16)],
        scratch_shapes=[pltpu.VMEM((nb, tm, d), F32)] if split else [],
        compiler_params=_params(("parallel", "parallel")),
        name="in_proj",
    )(*row_args, mods, mods, norm1_g, w_a, w_z, w_b, cos, sin, q_gain, k_gain, ones_blockdiag,
      w_a_f, b_a_f, w_a_b, b_a_b)


def _chunk_cumsum(la, tri_ref):
    w = la.shape[1]
    hi = la.astype(BF16)
    lo = (la - hi.astype(F32)).astype(BF16)
    both = jnp.dot(tri_ref[...], jnp.concatenate([hi, lo], axis=1), preferred_element_type=F32)
    return both[:, :w] + both[:, w:]


def _decay_factors(b, reverse):
    b_tot = b[0:1, :] if reverse else b[CHUNK - 1:CHUNK, :]
    dk_all = b.shape[1]
    dec_cols = jnp.broadcast_to(jnp.exp(b_tot), (dk_all, dk_all)).T
    dec_cols = jnp.concatenate([dec_cols] * (GROUP_W // dk_all), axis=1)
    return jnp.exp(b), jnp.exp(-b), jnp.exp(b_tot - b), dec_cols


def _scan_chunk_head(q, k, v, factors, st, key_mask, value_mask, head_mask):
    nt_dims = (((1,), (1,)), ((), ()))
    e_q, e_inv, e_end, dec_cols = factors
    q_dec = (q * e_q).astype(BF16)
    k_inv = (k * e_inv).astype(BF16)
    k_end = (k * e_end).astype(BF16)
    vb = v.astype(BF16)
    k_blk = jnp.concatenate([k_inv] * N_HEADS, axis=0) * key_mask
    scores = lax.dot_general(q_dec, k_blk, nt_dims, preferred_element_type=F32)
    kv = lax.dot_general(k_end, vb, (((0,), (0,)), ((), ())), preferred_element_type=F32)
    o_inter = jnp.dot(q_dec, st.astype(BF16), preferred_element_type=F32)
    st = st * dec_cols + jnp.where(head_mask, kv, 0.0)
    v_blk = jnp.concatenate([vb] * N_HEADS, axis=0) * value_mask
    return (scores, v_blk, o_inter), st


def _scan_chunk_tail(pending, tri):
    scores, v_blk, o_inter = pending
    scores = jnp.where(tri, scores, 0.0).astype(BF16)
    return jnp.dot(scores, v_blk, preferred_element_type=F32) + o_inter


def _scan_body(gf, gb, rf, rb, lg, cosf, sinf, cosb, sinb,
               key_mask_ref, value_mask_ref, cum_f_ref, cum_b_ref,
               gof, gob, rof, rob, st_all, b_all):
    n_batch = gf.shape[0]

    @pl.when(pl.program_id(1) == 0)
    def _():
        st_all[...] = jnp.zeros(st_all.shape, F32)

    tb = gf.shape[1]
    dk_all = N_HEADS * SCAN_DK
    dv_all = GROUP_W
    scale = SCAN_DK ** -0.5
    key_mask = key_mask_ref[...]
    value_mask = value_mask_ref[...]
    r = lax.broadcasted_iota(jnp.int32, (dk_all, dv_all), 0)
    c = lax.broadcasted_iota(jnp.int32, (dk_all, dv_all), 1)
    head_mask = (r // SCAN_DK) == (c // HEAD_DV)
    r = lax.broadcasted_iota(jnp.int32, (CHUNK, dv_all), 0)
    c = lax.broadcasted_iota(jnp.int32, (CHUNK, dv_all), 1) % CHUNK
    tri_f = c <= r
    tri_b = c >= r

    def qkv(ref, bi, lo):
        return (ref[bi, lo:lo + CHUNK, 0:dk_all], ref[bi, lo:lo + CHUNK, dk_all:2 * dk_all],
                ref[bi, lo:lo + CHUNK, 2 * dk_all:2 * dk_all + dv_all])

    def gla_chain(ref, bi, cum_ref, reverse):
        slot = int(reverse)
        lo = W_GLA_QKVR + slot * dk_all
        b_all[bi, slot] = _chunk_cumsum(ref[bi, :, lo:lo + dk_all], cum_ref)

        def load(lo):
            q, k, v = qkv(ref, bi, lo)
            return q * scale, k, v, _decay_factors(b_all[bi, slot, lo:lo + CHUNK, :], reverse)
        return load

    lane = lax.broadcasted_iota(jnp.int32, (CHUNK, dk_all), 1)
    first_half = (lane % SCAN_DK) < SCAN_DK // 2
    pos = lax.broadcasted_iota(jnp.int32, (CHUNK, dk_all), 0).astype(F32)

    ret_factors = [_decay_factors(((CHUNK - pos) if reverse else (pos + 1.0)) * lg[...], reverse)
                   for reverse in (False, True)]

    def ret_chain(ref, bi, cos_ref, sin_ref, reverse):
        def rope(t, lo):
            rot = jnp.where(first_half, pltpu.roll(t, dk_all - SCAN_DK // 2, 1),
                            pltpu.roll(t, SCAN_DK // 2, 1))
            return t * cos_ref[lo:lo + CHUNK, :] + rot * sin_ref[lo:lo + CHUNK, :]

        def load(lo):
            q, k, v = qkv(ref, bi, lo)
            return rope(q, lo), rope(k * scale, lo), v, ret_factors[int(reverse)]
        return load

    chains = []
    for bi in range(n_batch):
        chains += [
            (gla_chain(gf, bi, cum_f_ref, False), False, gof, bi, tri_f),
            (gla_chain(gb, bi, cum_b_ref, True), True, gob, bi, tri_b),
            (ret_chain(rf, bi, cosf, sinf, False), False, rof, bi, tri_f),
            (ret_chain(rb, bi, cosb, sinb, True), True, rob, bi, tri_b),
        ]
    st_vals = [st_all[n] for n in range(len(chains))]
    n_chunks = tb // CHUNK

    def heads(i):
        pending = []
        for n, (load, reverse, _, _, _) in enumerate(chains):
            lo = (n_chunks - 1 - i if reverse else i) * CHUNK
            q, k, v, factors = load(lo)
            part, st_vals[n] = _scan_chunk_head(q, k, v, factors, st_vals[n],
                                                key_mask, value_mask, head_mask)
            pending.append((lo, part))
        return pending

    pending = heads(0)
    for i in range(n_chunks):
        following = heads(i + 1) if i + 1 < n_chunks else None
        for (lo, part), (_, _, out_ref, bi, tri) in zip(pending, chains):
            out_ref[bi, lo:lo + CHUNK, :] = _scan_chunk_tail(part, tri)
        pending = following
    for n, val in enumerate(st_vals):
        st_all[n] = val


def _conv_tiles(j, n_ctx_tiles, n_tiles, main_ref, prev_ref, next_ref, wdw_ref, bdw_ref, lng_ref,
                lnb_ref, wpw_ref, bpw_ref, o_ref, ubuf, shifted, sbuf):
    n_batch, tt = main_ref.shape[0], main_ref.shape[1]
    c = o_ref.shape[2]

    has_prev = jnp.logical_and(j != 0, j != n_ctx_tiles)
    has_next = jnp.logical_and(j != n_ctx_tiles - 1, j != n_tiles - 1)

    @pl.when(has_prev)
    def _():
        ubuf[:, 0:HALO, :] = prev_ref[...]

    @pl.when(jnp.logical_not(has_prev))
    def _():
        ubuf[:, 0:HALO, :] = jnp.zeros((n_batch, HALO, c), F32)

    @pl.when(has_next)
    def _():
        ubuf[:, HALO + tt:, :] = next_ref[...]

    @pl.when(jnp.logical_not(has_next))
    def _():
        ubuf[:, HALO + tt:, :] = jnp.zeros((n_batch, HALO, c), F32)

    span = shifted.shape[2]
    rows = 128
    for bi in range(n_batch):
        ubuf[bi, HALO:HALO + tt, :] = main_ref[bi]
        for s in range(1, SUBLANES):
            shifted[bi, s - 1] = ubuf[bi, s:s + span, :]
        for r in range(0, tt, rows):
            acc = jnp.zeros((rows, c), F32)
            for k in range(CONV_KSIZE):
                off = k + HALO - CONV_PAD
                lo = r + off - off % SUBLANES
                if off % SUBLANES == 0:
                    tap = ubuf[bi, lo:lo + rows, :]
                else:
                    tap = shifted[bi, off % SUBLANES - 1, lo:lo + rows, :]
                acc = acc + tap * wdw_ref[k:k + 1, :]
            y = acc + bdw_ref[...]
            yc = y - jnp.mean(y, axis=-1, keepdims=True)
            yn = yc * lax.rsqrt(jnp.mean(yc * yc, axis=-1, keepdims=True) + EPS)
            yn = yn * lng_ref[...] + lnb_ref[...]
            sbuf[bi, r:r + rows, :] = _silu(yn).astype(BF16)
        o_ref[bi] = jnp.dot(sbuf[bi], wpw_ref[...], preferred_element_type=F32) + bpw_ref[...]


def _mixers_body(n_ctx_tiles, n_tiles, n_rid, *refs):
    scan_in, conv_in = refs[:13], refs[13:22]
    rider_in = refs[22:22 + n_rid]
    outs = refs[22 + n_rid:]
    scan_out, conv_out, rider_out = outs[:4], outs[4], outs[5:-5]
    scan_scratch, conv_scratch = outs[-5:-3], outs[-3:]
    _run_riders(rider_in, rider_out)
    _scan_body(*scan_in, *scan_out, *scan_scratch)
    _conv_tiles(pl.program_id(1), n_ctx_tiles, n_tiles, *conv_in, conv_out, *conv_scratch)


def _mixers(layer, gla_in, ret_in, conv_in, n_ctx_tiles, log_gamma, cos, sin,
            key_mask, value_mask, cum_f, cum_b, w_dw, b_dw, ln_g, ln_b, w_pw, b_pw, riders=()):
    bsz, t_all, _ = gla_in.shape
    tb = ROW_TILE
    n_tiles = t_all // tb
    dk_all = N_HEADS * SCAN_DK
    dv_all = GROUP_W
    c = GROUP_W
    nb = MIXER_BATCH if bsz % MIXER_BATCH == 0 else 1
    per = tb // HALO
    n_halo = t_all // HALO

    def bwd(j):
        return jnp.where(j < n_ctx_tiles, n_ctx_tiles - 1 - j, n_tiles - 1 - (j - n_ctx_tiles))

    fwd_blk = lambda b, j: (b, j, 0)
    bwd_blk = lambda b, j: (b, bwd(j), 0)
    n_steps = (bsz // nb) * n_tiles
    rid_in, rid_args, rid_out, rid_shapes = _rider_specs(riders, lambda b, j: b * n_tiles + j, n_steps)
    in_specs = [pl.BlockSpec((nb, tb, W_GLA), fwd_blk), pl.BlockSpec((nb, tb, W_GLA), bwd_blk),
                pl.BlockSpec((nb, tb, W_RET), fwd_blk), pl.BlockSpec((nb, tb, W_RET), bwd_blk),
                _const_spec(log_gamma),
                pl.BlockSpec((tb, dk_all), lambda b, j: (j, 0)),
                pl.BlockSpec((tb, dk_all), lambda b, j: (j, 0)),
                pl.BlockSpec((tb, dk_all), lambda b, j: (bwd(j), 0)),
                pl.BlockSpec((tb, dk_all), lambda b, j: (bwd(j), 0)),
                _const_spec(key_mask), _const_spec(value_mask), _const_spec(cum_f), _const_spec(cum_b),
                pl.BlockSpec((nb, tb, c), fwd_blk),
                pl.BlockSpec((nb, HALO, c), lambda b, j: (b, jnp.maximum(j * per - 1, 0), 0)),
                pl.BlockSpec((nb, HALO, c), lambda b, j: (b, jnp.minimum((j + 1) * per, n_halo - 1), 0)),
                _layer_spec(w_dw, layer), _layer_spec(b_dw, layer), _layer_spec(ln_g, layer),
                _layer_spec(ln_b, layer), _layer_spec(w_pw, layer), _layer_spec(b_pw, layer)] + rid_in
    return pl.pallas_call(
        functools.partial(_mixers_body, n_ctx_tiles, n_tiles, len(riders)),
        grid=(bsz // nb, n_tiles),
        in_specs=in_specs,
        out_specs=[pl.BlockSpec((nb, tb, dv_all), fwd_blk), pl.BlockSpec((nb, tb, dv_all), bwd_blk)] * 2
        + [pl.BlockSpec((nb, tb, c), fwd_blk)] + rid_out,
        out_shape=[jax.ShapeDtypeStruct((bsz, t_all, dv_all), F32)] * 5 + rid_shapes,
        scratch_shapes=[pltpu.VMEM((4 * nb, dk_all, dv_all), F32), pltpu.VMEM((nb, 2, tb, dk_all), F32),
                        pltpu.VMEM((nb, tb + 2 * HALO, c), F32),
                        pltpu.VMEM((nb, SUBLANES - 1, tb + 2 * HALO - SUBLANES, c), F32),
                        pltpu.VMEM((nb, tb, c), BF16)],
        compiler_params=_params(("arbitrary", "arbitrary")),
        name="mixers",
    )(gla_in, gla_in, ret_in, ret_in, log_gamma, cos, sin, cos, sin,
      key_mask, value_mask, cum_f, cum_b, conv_in, conv_in, conv_in, w_dw, b_dw, ln_g, ln_b, w_pw, b_pw,
      *rid_args)


def _att_body(n_ctx_tiles, n_tiles, tile0, q_ref, k_ref, vt_ref, o_ref):
    group = N_HEADS // KV_HEADS
    tq = q_ref.shape[2]
    tk = vt_ref.shape[4]
    nq = group * tq

    def attend_head(bi, h, n_blocks):
        q2 = q_ref[bi, h * group:(h + 1) * group].reshape(nq, LANES)

        def scores(i):
            kb = k_ref[bi, i * tk:(i + 1) * tk, :]
            return lax.dot_general(kb, q2, (((1,), (1,)), ((), ())), preferred_element_type=F32)

        m = acc = None
        ahead = [scores(i) for i in range(min(SCORE_LOOKAHEAD, n_blocks))]
        for i in range(n_blocks):
            s = ahead.pop(0)
            if i + SCORE_LOOKAHEAD < n_blocks:
                ahead.append(scores(i + SCORE_LOOKAHEAD))
            blk_max = jnp.max(s, axis=0, keepdims=True)
            m_new = blk_max if i == 0 else jnp.maximum(m, blk_max)
            p = jnp.exp2(s - m_new).astype(BF16)
            pv = jnp.dot(vt_ref[bi, h, i], p, preferred_element_type=F32)
            acc = pv if i == 0 else jnp.exp2(m - m_new) * acc + pv
            m = m_new
        o_t = acc[:HEAD_DV] / acc[HEAD_DV:HEAD_DV + 1]
        stacked = jnp.concatenate([o_t[:, g * tq:(g + 1) * tq] for g in range(group)], axis=0)
        o_ref[bi, :, h * group * HEAD_DV:(h + 1) * group * HEAD_DV] = stacked.T

    def attend(n_blocks):
        for bi in range(q_ref.shape[0]):
            for h in range(KV_HEADS):
                attend_head(bi, h, n_blocks)

    qi = pl.program_id(1) + tile0
    if tile0 < n_ctx_tiles:
        pl.when(qi < n_ctx_tiles)(lambda: attend(n_ctx_tiles))
    pl.when(qi >= n_ctx_tiles)(lambda: attend(n_tiles))


def _attention(q, k, vt, n_ctx_tiles, with_ctx):
    bsz, _, t_all, _ = q.shape
    tq = ROW_TILE
    n_tiles = t_all // tq
    tile0 = 0 if with_ctx else n_ctx_tiles
    nb = ATT_BATCH if bsz % ATT_BATCH == 0 else 1
    return pl.pallas_call(
        functools.partial(_att_body, n_ctx_tiles, n_tiles, tile0),
        grid=(bsz // nb, n_tiles - tile0),
        in_specs=[pl.BlockSpec((nb, N_HEADS, tq, LANES), lambda b, i: (b, 0, i + tile0, 0)),
                  pl.BlockSpec((nb, t_all, LANES), lambda b, i: (b, 0, 0)),
                  pl.BlockSpec((nb, KV_HEADS, n_tiles, VT_ROWS, tq), lambda b, i: (b, 0, 0, 0, 0))],
        out_specs=pl.BlockSpec((nb, tq, GROUP_W), lambda b, i: (b, i, 0)),
        out_shape=jax.ShapeDtypeStruct((bsz, t_all - tile0 * tq, GROUP_W), F32),
        compiler_params=_params(("parallel", "arbitrary")),
        name="attention",
    )(q, k, vt)


def _post_body(final, split, n_ctx_tiles, tile0, n_rid, *refs):
    (yconv, gla_f, gla_b, gla_gate, yatt, ret_f, ret_b, ret_gate, gla_gain, ret_gain, ones_ref) = refs[:11]
    n_src = 2 if split else 1
    lat_ref = refs[11]
    ctx_ref = refs[12] if split else None
    n_in = 18 + n_src
    mod_ref, modc_ref, n2g_ref, wout_ref, wup_ref, wdown_ref, fng_ref = refs[11 + n_src:n_in]
    o_ref = refs[n_in + n_rid]
    xbuf = refs[n_in + 2 * n_rid + 1] if split else None
    _run_riders(refs[n_in:n_in + n_rid], refs[n_in + n_rid + 1:n_in + 2 * n_rid + 1])
    nb, tm, d = lat_ref.shape
    rows = nb * tm
    d_ff = wup_ref.shape[1]
    is_ctx = pl.program_id(1) + tile0 < n_ctx_tiles

    def stacked(ref):
        return ref[...].reshape(rows, ref.shape[2])

    def finish(of_ref, ob_ref, gate_ref, gain_ref):
        o = stacked(of_ref) + stacked(ob_ref)
        ss = _group_sumsq(o, ones_ref[...])
        y = o * lax.rsqrt(ss * (1.0 / HEAD_DV) + EPS) * gain_ref[...]
        return (y * _silu(stacked(gate_ref))).astype(BF16)

    mixed = jnp.concatenate([stacked(yconv).astype(BF16), finish(gla_f, gla_b, gla_gate, gla_gain),
                             stacked(yatt).astype(BF16), finish(ret_f, ret_b, ret_gate, ret_gain)], axis=1)
    o = jnp.dot(mixed, wout_ref[...], preferred_element_type=F32)
    x = _load_rows(split, is_ctx, lat_ref, ctx_ref, xbuf)
    mods = [_mod_row(is_ctx, mod_ref, modc_ref, bi) for bi in range(nb)]
    x_mid, h2 = [], []
    for bi in range(nb):
        gate1 = mods[bi][:, 2 * d:3 * d]
        shift2 = mods[bi][:, 3 * d:4 * d]
        scale2 = mods[bi][:, 4 * d:5 * d]
        x_mid.append(x[bi] + gate1 * o[bi * tm:(bi + 1) * tm])
        h2.append(((_rms(x_mid[bi]) * n2g_ref[...]) * (1.0 + scale2) + shift2).astype(BF16))
    h2 = jnp.concatenate(h2, axis=0)
    acc = jnp.zeros((rows, d), F32)
    tf = 1024
    for f in range(0, d_ff, tf):
        u = jnp.dot(h2, wup_ref[:, f:f + tf], preferred_element_type=F32)
        a = jnp.square(jnp.maximum(u, 0.0)).astype(BF16)
        acc = acc + jnp.dot(a, wdown_ref[f:f + tf, :], preferred_element_type=F32)
    for bi in range(nb):
        gate2 = mods[bi][:, 5 * d:6 * d]
        x_out = x_mid[bi] + gate2 * acc[bi * tm:(bi + 1) * tm]
        if final:
            x_out = _rms(x_out) * fng_ref[...]
        o_ref[bi] = x_out


def _post(layer, final, yconv, gla_of, gla_ob, gla_in, yatt, ret_of, ret_ob, ret_in, gla_gain, ret_gain,
          ones_blockdiag, x_lat, x_ctx, mods, norm2_g, w_layer, w_out, w_up, w_down, final_g, n_ctx_tiles,
          with_ctx, riders=()):
    split = x_ctx is not None
    bsz, _, d = x_lat.shape
    tm = ROW_TILE
    t_all = gla_of.shape[1]
    ctx_row = bsz
    tile0 = 0 if with_ctx else n_ctx_tiles
    n_out = t_all // tm - tile0
    c = GROUP_W
    gate_blk = (2 * N_HEADS * SCAN_DK + c) // c
    full = lambda b, j: (b, j + tile0, 0)
    own = lambda b, j: (b, j, 0)
    assert not split or tile0 == 0
    nb = ROW_BATCH if bsz % ROW_BATCH == 0 else 1
    row_specs, row_args = _row_specs(split, x_lat, x_ctx, nb, tm, n_ctx_tiles, tile0)

    def resident(arr):
        rest = arr.shape[1:]
        return pl.BlockSpec((None,) + rest, lambda *_: (w_layer,) + (0,) * len(rest),
                            pipeline_mode=pl.Buffered(1))

    n_steps = (bsz // nb) * n_out
    rid_in, rid_args, rid_out, rid_shapes = _rider_specs(riders, lambda b, j: b * n_out + j, n_steps)
    return pl.pallas_call(
        functools.partial(_post_body, final, split, n_ctx_tiles, tile0, len(riders)),
        grid=(bsz // nb, n_out),
        in_specs=[pl.BlockSpec((nb, tm, c), full),
                  pl.BlockSpec((nb, tm, c), full), pl.BlockSpec((nb, tm, c), full),
                  pl.BlockSpec((nb, tm, c), lambda b, j: (b, j + tile0, gate_blk)),
                  pl.BlockSpec((nb, tm, c), own),
                  pl.BlockSpec((nb, tm, c), full), pl.BlockSpec((nb, tm, c), full),
                  pl.BlockSpec((nb, tm, c), lambda b, j: (b, j + tile0, gate_blk)),
                  _layer_spec(gla_gain, layer), _layer_spec(ret_gain, layer),
                  _const_spec(ones_blockdiag)] + row_specs + _mod_specs(layer, nb, N_MOD * d, ctx_row) + [
                  _layer_spec(norm2_g, layer),
                  resident(w_out), resident(w_up), resident(w_down),
                  _const_spec(final_g)] + rid_in,
        out_specs=[pl.BlockSpec((nb, tm, d), own)] + rid_out,
        out_shape=[jax.ShapeDtypeStruct((bsz, n_out * tm, d), F32)] + rid_shapes,
        scratch_shapes=[pltpu.VMEM((nb, tm, d), F32)] if split else [],
        compiler_params=_params(("arbitrary", "arbitrary") if riders else ("parallel", "parallel")),
        name="post",
    )(yconv, gla_of, gla_ob, gla_in, yatt, ret_of, ret_ob, ret_in, gla_gain, ret_gain,
      ones_blockdiag, *row_args, mods, mods, norm2_g, w_out, w_up, w_down, final_g, *rid_args)


def _rope_table(t_lat, n_ctx, head_dim, width):
    n_ax = head_dim // 4
    inv = np.float32(ROPE_THETA) ** (-np.arange(n_ax, dtype=np.float32) / np.float32(n_ax))
    pos = np.arange(t_lat)
    ang = np.concatenate([(pos // GRID_W).astype(np.float32)[:, None] * inv,
                          (pos % GRID_W).astype(np.float32)[:, None] * inv], axis=-1).astype(np.float32)
    cos, sin = np.cos(ang), np.sin(ang)
    cos = np.concatenate([np.ones((n_ctx, head_dim // 2), np.float32), cos], axis=0)
    sin = np.concatenate([np.zeros((n_ctx, head_dim // 2), np.float32), sin], axis=0)
    reps = width // head_dim
    return (jnp.asarray(np.tile(np.concatenate([cos, cos], axis=-1), (1, reps)), F32),
            jnp.asarray(np.tile(np.concatenate([-sin, sin], axis=-1), (1, reps)), F32))


def kernel(x, c, ctx, c_ctx, w_mod, b_mod, norm1_g, norm2_g, w_in, conv_w_dw, conv_b_dw, conv_ln_g,
           conv_ln_b, conv_w_pw, conv_b_pw, gla_w_a_f, gla_b_a_f, gla_w_a_b, gla_b_a_b, gla_norm_g,
           att_q_norm_g, att_k_norm_g, ret_norm_g, w_out, w_up, w_down, final_norm_g):
    bsz, t_lat, d = x.shape
    n_ctx = ctx.shape[1]
    depth = w_mod.shape[0]
    assert bsz + 1 <= 8 and n_ctx % ROW_TILE == 0 and t_lat % ROW_TILE == 0
    n_ctx_tiles = n_ctx // ROW_TILE
    dk_all = N_HEADS * SCAN_DK

    cc = jnp.concatenate([c, c_ctx[None, :], jnp.zeros((8 - bsz - 1, d), F32)], axis=0)
    mods = _modulation(cc, w_mod, b_mod)
    mods = mods.reshape(depth, 8, 1, N_MOD * d)

    stack_rows = lambda v: v.reshape(depth, 1, -1)
    w_in_b = w_in.astype(BF16)
    in_w = (w_in_b[:, :, :W_IN_A],
            jnp.pad(w_in_b[:, :, W_IN_A:W_IN_A + GLA_RANK], ((0, 0), (0, 0), (0, LANES - GLA_RANK))),
            w_in_b[:, :, W_IN_A + GLA_RANK:])
    pad_rank = lambda w: jnp.pad(w, ((0, 0), (0, LANES - GLA_RANK), (0, 0))).astype(BF16)
    w_a_f, w_a_b = pad_rank(gla_w_a_f), pad_rank(gla_w_a_b)
    w_pw = conv_w_pw.astype(BF16)
    q_gain = stack_rows(jnp.tile(att_q_norm_g, (1, N_HEADS)))
    k_gain = stack_rows(jnp.tile(att_k_norm_g, (1, KV_HEADS)))

    cos_att, sin_att = _rope_table(t_lat, n_ctx, HEAD_DV, N_HEADS * HEAD_DV)
    cos_ret, sin_ret = _rope_table(t_lat, n_ctx, SCAN_DK, dk_all)
    gamma = 1.0 - np.exp2(-5.0 - np.arange(N_HEADS, dtype=np.float64))
    log_gamma = jnp.asarray(np.repeat(np.log(gamma), SCAN_DK)[None, :], F32)
    gid = np.arange(GROUP_W) // HEAD_DV
    ones_blockdiag = jnp.asarray(gid[:, None] == gid[None, :], BF16)
    key_mask = jnp.asarray(gid[:, None] == (np.arange(dk_all) // SCAN_DK)[None, :], BF16)
    pos = np.arange(ROW_TILE)
    same_chunk = (pos[:, None] // CHUNK) == (pos[None, :] // CHUNK)
    cum_f = jnp.asarray(same_chunk & (pos[None, :] <= pos[:, None]), BF16)
    cum_b = jnp.asarray(same_chunk & (pos[None, :] >= pos[:, None]), BF16)

    x_lat, x_ctx = x, ctx
    for l in range(depth):
        with_ctx = l < depth - 1
        final = l == depth - 1
        conv_in, gla_in, ret_in, q, k, vt = _in_proj(
            l, x_lat, x_ctx, mods, stack_rows(norm1_g), l, *in_w, cos_att, sin_att, q_gain, k_gain,
            ones_blockdiag, w_a_f, stack_rows(gla_b_a_f), w_a_b, stack_rows(gla_b_a_b), n_ctx_tiles)
        mixer_riders = [(w, 0) for w in (w_out, w_up, w_down)] if l == 0 else []
        gla_of, gla_ob, ret_of, ret_ob, y_conv, *converted = _mixers(
            l, gla_in, ret_in, conv_in, n_ctx_tiles, log_gamma, cos_ret, sin_ret, key_mask,
            ones_blockdiag, cum_f, cum_b, conv_w_dw, stack_rows(conv_b_dw), stack_rows(conv_ln_g),
            stack_rows(conv_ln_b), w_pw, stack_rows(conv_b_pw), mixer_riders)
        if l == 0:
            post_w = tuple(w[None] for w in converted)
        y_att = _attention(q, k, vt, n_ctx_tiles, with_ctx)
        post_riders = [] if final else [(w, l + 1) for w in (w_out, w_up, w_down)]
        x_lat, *converted = _post(
            l, final, y_conv, gla_of, gla_ob, gla_in, y_att, ret_of, ret_ob, ret_in,
            stack_rows(gla_norm_g), stack_rows(ret_norm_g), ones_blockdiag, x_lat, x_ctx, mods,
            stack_rows(norm2_g), 0, *post_w, final_norm_g.reshape(1, -1), n_ctx_tiles, with_ctx,
            post_riders)
        if not final:
            post_w = tuple(w[None] for w in converted)
        x_ctx = None
    return x_lat
```
